```python
import jax, jax.numpy as jnp
from jax import lax
import numpy as np

D_MODEL = 2048
BATCH = 8
SEQ = 4096
DEPTH = 4

N_MIXERS = 2
N_A = (DEPTH + 1) // 2
N_B = DEPTH // 2
HEAD_DIM = 128
N_HEADS = D_MODEL // HEAD_DIM
Q_BLOCK = 128
CHUNK = 128
D_GM = D_MODEL
GM_GROUP = 128
N_GM_GROUPS = D_GM // GM_GROUP
D_FF = 5504
CONV_W = 3
EPS = 1e-6

kernel_name = "hybrid_fox_gmlp_convffn_adaln"


def rms_norm(x, g):
    xf = x.astype(jnp.float32)
    y = xf * lax.rsqrt(jnp.mean(xf * xf, axis=-1, keepdims=True) + EPS)
    return (y * g.astype(jnp.float32)).astype(x.dtype)


def modulate(h, shift, scale):
    return h * (1 + scale[:, None, :]) + shift[:, None, :]


def fox_attention(h, w_in, b_f, w_o):
    B, S, D = h.shape
    proj = h @ w_in
    q = proj[..., :D].reshape(B, S, N_HEADS, HEAD_DIM).transpose(0, 2, 1, 3)
    k = proj[..., D:2 * D].reshape(B, S, N_HEADS, HEAD_DIM).transpose(0, 2, 1, 3)
    v = proj[..., 2 * D:3 * D].reshape(B, S, N_HEADS, HEAD_DIM).transpose(0, 2, 1, 3)
    f_logit = (proj[..., 3 * D:] + b_f).astype(jnp.float32)
    log_f = jax.nn.log_sigmoid(f_logit)
    F = jnp.cumsum(log_f, axis=1).transpose(0, 2, 1)
    scale = HEAD_DIM ** -0.5
    outs = []
    for i in range(S // Q_BLOCK):
        lo, hi = i * Q_BLOCK, (i + 1) * Q_BLOCK
        qb = q[:, :, lo:hi]
        kb = k[:, :, :hi]
        vb = v[:, :, :hi]
        s = jnp.einsum('bhqd,bhkd->bhqk', qb, kb).astype(jnp.float32) * scale
        s = s + F[:, :, lo:hi, None] - F[:, :, None, :hi]
        q_pos = lo + jnp.arange(Q_BLOCK)
        k_pos = jnp.arange(hi)
        mask = k_pos[None, :] <= q_pos[:, None]
        s = jnp.where(mask, s, -jnp.inf)
        p = jax.nn.softmax(s, axis=-1).astype(v.dtype)
        outs.append(jnp.einsum('bhqk,bhkd->bhqd', p, vb))
    o = jnp.concatenate(outs, axis=2)
    o = o.transpose(0, 2, 1, 3).reshape(B, S, D)
    return o @ w_o


def chunked_gmlp(h, w_in, v_g, w_s, b_s, w_o):
    B, S, _ = h.shape
    z = jax.nn.gelu(h @ w_in)
    u, v = z[..., :D_GM], z[..., D_GM:]
    v = rms_norm(v, v_g)
    vc = v.reshape(B, S // CHUNK, CHUNK, N_GM_GROUPS, GM_GROUP)
    causal = jnp.tril(jnp.ones((CHUNK, CHUNK), dtype=w_s.dtype))
    w = w_s * causal[None]
    sv = jnp.einsum('gts,bnsgd->bntgd', w, vc)
    sv = sv + b_s.T[None, None, :, :, None]
    gated = u * sv.reshape(B, S, D_GM)
    return gated @ w_o


def conv_ffn(h, w_in, conv_w, conv_b, w_out):
    a = h @ w_in
    S = a.shape[1]
    ap = jnp.pad(a, ((0, 0), (CONV_W - 1, 0), (0, 0)))
    a = (conv_w[0] * ap[:, 0:S] + conv_w[1] * ap[:, 1:S + 1]
         + conv_w[2] * ap[:, 2:S + 2] + conv_b)
    gate, up = a[..., :D_FF], a[..., D_FF:]
    return (jax.nn.silu(gate) * up) @ w_out


def _fwd_setup_inputs(seed: int = 0) -> dict:
    key = jax.random.key(seed)
    ks = jax.random.split(key, 24)
    f32 = jnp.float32
    nrm = lambda k, shape, s: jax.random.normal(k, shape, f32) * s
    D = D_MODEL
    return {
        "x": nrm(ks[0], (BATCH, SEQ, D), 1.0),
        "c": nrm(ks[1], (BATCH, D), 1.0),
        "mod_w": nrm(ks[2], (DEPTH, D, 6 * D), 0.5 * D ** -0.5),
        "mod_b": nrm(ks[3], (DEPTH, 6 * D), 0.02),
        "mix_norm_g": 1.0 + nrm(ks[4], (DEPTH, D), 0.02),
        "ffn_norm_g": 1.0 + nrm(ks[5], (DEPTH, D), 0.02),
        "attn_w_in": nrm(ks[6], (N_A, D, 3 * D + N_HEADS), D ** -0.5),
        "attn_b_f": jax.random.uniform(ks[7], (N_A, N_HEADS), f32, 1.0, 6.0),
        "attn_w_o": nrm(ks[8], (N_A, D, D), D ** -0.5),
        "gm_w_in": nrm(ks[9], (N_B, D, 2 * D_GM), D ** -0.5),
        "gm_v_g": 1.0 + nrm(ks[10], (N_B, D_GM), 0.02),
        "gm_w_s": nrm(ks[11], (N_B, N_GM_GROUPS, CHUNK, CHUNK), CHUNK ** -0.5),
        "gm_b_s": 1.0 + nrm(ks[12], (N_B, N_GM_GROUPS, CHUNK), 0.1),
        "gm_w_o": nrm(ks[13], (N_B, D_GM, D), D_GM ** -0.5),
        "ffn_w_in": nrm(ks[14], (DEPTH, D, 2 * D_FF), D ** -0.5),
        "ffn_conv_w": nrm(ks[15], (DEPTH, CONV_W, 2 * D_FF), CONV_W ** -0.5),
        "ffn_conv_b": nrm(ks[16], (DEPTH, 2 * D_FF), 0.02),
        "ffn_w_out": nrm(ks[17], (DEPTH, D_FF, D), D_FF ** -0.5),
        "final_g": 1.0 + nrm(ks[18], (D,), 0.02),
    }


def _fwd_reference(x, c, mod_w, mod_b, mix_norm_g, ffn_norm_g, attn_w_in, attn_b_f, attn_w_o,
              gm_w_in, gm_v_g, gm_w_s, gm_b_s, gm_w_o, ffn_w_in, ffn_conv_w, ffn_conv_b,
              ffn_w_out, final_g):
    c_act = jax.nn.silu(c)
    for i in range(DEPTH):
        mod = c_act @ mod_w[i] + mod_b[i]
        sh1, sc1, g1, sh2, sc2, g2 = jnp.split(mod, 6, axis=-1)
        h = modulate(rms_norm(x, mix_norm_g[i]), sh1, sc1)
        j = i // N_MIXERS
        if i % N_MIXERS == 0:
            y = fox_attention(h, attn_w_in[j], attn_b_f[j], attn_w_o[j])
        else:
            y = chunked_gmlp(h, gm_w_in[j], gm_v_g[j], gm_w_s[j], gm_b_s[j], gm_w_o[j])
        x = x + g1[:, None, :] * y
        h = modulate(rms_norm(x, ffn_norm_g[i]), sh2, sc2)
        x = x + g2[:, None, :] * conv_ffn(h, ffn_w_in[i], ffn_conv_w[i], ffn_conv_b[i], ffn_w_out[i])
    return rms_norm(x, final_g)


import jax as _jax
import jax.numpy as _jnp

TWIN_FORMAT = 'train_step'
FWD_PARAMS = ['x', 'c', 'mod_w', 'mod_b', 'mix_norm_g', 'ffn_norm_g', 'attn_w_in', 'attn_b_f', 'attn_w_o', 'gm_w_in', 'gm_v_g', 'gm_w_s', 'gm_b_s', 'gm_w_o', 'ffn_w_in', 'ffn_conv_w', 'ffn_conv_b', 'ffn_w_out', 'final_g']
TWIN_WEIGHTS = ['mod_w', 'mod_b', 'mix_norm_g', 'ffn_norm_g', 'attn_w_in', 'attn_b_f', 'attn_w_o', 'gm_w_in', 'gm_v_g', 'gm_w_s', 'gm_b_s', 'gm_w_o', 'ffn_w_in', 'ffn_conv_w', 'ffn_conv_b', 'ffn_w_out', 'final_g']
TWIN_DIFF_INPUT = 'x'
TWIN_INPUTS = ['x', 'c', 'mod_w', 'mod_b', 'mix_norm_g', 'ffn_norm_g', 'attn_w_in', 'attn_b_f', 'attn_w_o', 'gm_w_in', 'gm_v_g', 'gm_w_s', 'gm_b_s', 'gm_w_o', 'ffn_w_in', 'ffn_conv_w', 'ffn_conv_b', 'ffn_w_out', 'final_g', 'loss_target', 'm_mod_w', 'm_mod_b', 'm_mix_norm_g', 'm_ffn_norm_g', 'm_attn_w_in', 'm_attn_b_f', 'm_attn_w_o', 'm_gm_w_in', 'm_gm_v_g', 'm_gm_w_s', 'm_gm_b_s', 'm_gm_w_o', 'm_ffn_w_in', 'm_ffn_conv_w', 'm_ffn_conv_b', 'm_ffn_w_out', 'm_final_g', 'v_mod_w', 'v_mod_b', 'v_mix_norm_g', 'v_ffn_norm_g', 'v_attn_w_in', 'v_attn_b_f', 'v_attn_w_o', 'v_gm_w_in', 'v_gm_v_g', 'v_gm_w_s', 'v_gm_b_s', 'v_gm_w_o', 'v_ffn_w_in', 'v_ffn_conv_w', 'v_ffn_conv_b', 'v_ffn_w_out', 'v_final_g']
TWIN_OUTPUTS = ['loss', 'grad_x', 'grad_mod_w', 'grad_mod_b', 'grad_mix_norm_g', 'grad_ffn_norm_g', 'grad_attn_w_in', 'grad_attn_b_f', 'grad_attn_w_o', 'grad_gm_w_in', 'grad_gm_v_g', 'grad_gm_w_s', 'grad_gm_b_s', 'grad_gm_w_o', 'grad_ffn_w_in', 'grad_ffn_conv_w', 'grad_ffn_conv_b', 'grad_ffn_w_out', 'grad_final_g', 'delta_mod_w', 'delta_mod_b', 'delta_mix_norm_g', 'delta_ffn_norm_g', 'delta_attn_w_in', 'delta_attn_b_f', 'delta_attn_w_o', 'delta_gm_w_in', 'delta_gm_v_g', 'delta_gm_w_s', 'delta_gm_b_s', 'delta_gm_w_o', 'delta_ffn_w_in', 'delta_ffn_conv_w', 'delta_ffn_conv_b', 'delta_ffn_w_out', 'delta_final_g', 'new_m_mod_w', 'new_m_mod_b', 'new_m_mix_norm_g', 'new_m_ffn_norm_g', 'new_m_attn_w_in', 'new_m_attn_b_f', 'new_m_attn_w_o', 'new_m_gm_w_in', 'new_m_gm_v_g', 'new_m_gm_w_s', 'new_m_gm_b_s', 'new_m_gm_w_o', 'new_m_ffn_w_in', 'new_m_ffn_conv_w', 'new_m_ffn_conv_b', 'new_m_ffn_w_out', 'new_m_final_g', 'new_v_mod_w', 'new_v_mod_b', 'new_v_mix_norm_g', 'new_v_ffn_norm_g', 'new_v_attn_w_in', 'new_v_attn_b_f', 'new_v_attn_w_o', 'new_v_gm_w_in', 'new_v_gm_v_g', 'new_v_gm_w_s', 'new_v_gm_b_s', 'new_v_gm_w_o', 'new_v_ffn_w_in', 'new_v_ffn_conv_w', 'new_v_ffn_conv_b', 'new_v_ffn_w_out', 'new_v_final_g']
TWIN_LEAF_KINDS = {'loss': 'loss', 'grad_x': 'grad_x', 'grad_mod_w': 'grad_w', 'grad_mod_b': 'grad_w', 'grad_mix_norm_g': 'grad_w', 'grad_ffn_norm_g': 'grad_w', 'grad_attn_w_in': 'grad_w', 'grad_attn_b_f': 'grad_w', 'grad_attn_w_o': 'grad_w', 'grad_gm_w_in': 'grad_w', 'grad_gm_v_g': 'grad_w', 'grad_gm_w_s': 'grad_w', 'grad_gm_b_s': 'grad_w', 'grad_gm_w_o': 'grad_w', 'grad_ffn_w_in': 'grad_w', 'grad_ffn_conv_w': 'grad_w', 'grad_ffn_conv_b': 'grad_w', 'grad_ffn_w_out': 'grad_w', 'grad_final_g': 'grad_w', 'delta_mod_w': 'delta_w', 'delta_mod_b': 'delta_w', 'delta_mix_norm_g': 'delta_w', 'delta_ffn_norm_g': 'delta_w', 'delta_attn_w_in': 'delta_w', 'delta_attn_b_f': 'delta_w', 'delta_attn_w_o': 'delta_w', 'delta_gm_w_in': 'delta_w', 'delta_gm_v_g': 'delta_w', 'delta_gm_w_s': 'delta_w', 'delta_gm_b_s': 'delta_w', 'delta_gm_w_o': 'delta_w', 'delta_ffn_w_in': 'delta_w', 'delta_ffn_conv_w': 'delta_w', 'delta_ffn_conv_b': 'delta_w', 'delta_ffn_w_out': 'delta_w', 'delta_final_g': 'delta_w', 'new_m_mod_w': 'new_m', 'new_m_mod_b': 'new_m', 'new_m_mix_norm_g': 'new_m', 'new_m_ffn_norm_g': 'new_m', 'new_m_attn_w_in': 'new_m', 'new_m_attn_b_f': 'new_m', 'new_m_attn_w_o': 'new_m', 'new_m_gm_w_in': 'new_m', 'new_m_gm_v_g': 'new_m', 'new_m_gm_w_s': 'new_m', 'new_m_gm_b_s': 'new_m', 'new_m_gm_w_o': 'new_m', 'new_m_ffn_w_in': 'new_m', 'new_m_ffn_conv_w': 'new_m', 'new_m_ffn_conv_b': 'new_m', 'new_m_ffn_w_out': 'new_m', 'new_m_final_g': 'new_m', 'new_v_mod_w': 'new_v', 'new_v_mod_b': 'new_v', 'new_v_mix_norm_g': 'new_v', 'new_v_ffn_norm_g': 'new_v', 'new_v_attn_w_in': 'new_v', 'new_v_attn_b_f': 'new_v', 'new_v_attn_w_o': 'new_v', 'new_v_gm_w_in': 'new_v', 'new_v_gm_v_g': 'new_v', 'new_v_gm_w_s': 'new_v', 'new_v_gm_b_s': 'new_v', 'new_v_gm_w_o': 'new_v', 'new_v_ffn_w_in': 'new_v', 'new_v_ffn_conv_w': 'new_v', 'new_v_ffn_conv_b': 'new_v', 'new_v_ffn_w_out': 'new_v', 'new_v_final_g': 'new_v'}


def _forward(args):
    return _fwd_reference(*[args[k] for k in FWD_PARAMS])


def _output_shape():
    out = _jax.eval_shape(lambda: _forward(_fwd_setup_inputs(0)))
    return out.shape, out.dtype

N_MICROBATCH = 1
ADAM_LR = 0.001
ADAM_B1 = 0.9
ADAM_B2 = 0.999
ADAM_EPS = 1e-08
ADAM_WD = 0.01
ADAM_STEP = 10
PER_EXAMPLE_BATCH_AXIS = {'x': 0, 'c': 0, 'loss_target': 0}
SHARED_INPUTS = []
_WEIGHT_DTYPES = {'mod_w': _jnp.float32, 'mod_b': _jnp.float32, 'mix_norm_g': _jnp.float32, 'ffn_norm_g': _jnp.float32, 'attn_w_in': _jnp.float32, 'attn_b_f': _jnp.float32, 'attn_w_o': _jnp.float32, 'gm_w_in': _jnp.float32, 'gm_v_g': _jnp.float32, 'gm_w_s': _jnp.float32, 'gm_b_s': _jnp.float32, 'gm_w_o': _jnp.float32, 'ffn_w_in': _jnp.float32, 'ffn_conv_w': _jnp.float32, 'ffn_conv_b': _jnp.float32, 'ffn_w_out': _jnp.float32, 'final_g': _jnp.float32}
MOMENT_SCALE = {'mod_w': 2.547100e-02, 'mod_b': 4.331082e-02, 'mix_norm_g': 2.120239e-02, 'ffn_norm_g': 2.627395e-02, 'attn_w_in': 1.066770e-02, 'attn_b_f': 5.654100e-02, 'attn_w_o': 1.370594e-02, 'gm_w_in': 1.879105e-02, 'gm_v_g': 1.282720e-02, 'gm_w_s': 1.280282e-02, 'gm_b_s': 1.821994e-02, 'gm_w_o': 2.243856e-02, 'ffn_w_in': 1.159478e-02, 'ffn_conv_w': 1.158761e-02, 'ffn_conv_b': 1.055445e-02, 'ffn_w_out': 1.867109e-02, 'final_g': 1.603501e+01}


def _to_microbatches(a, axis):
    t = _jnp.moveaxis(a, axis, 0)
    t = t.reshape((N_MICROBATCH, t.shape[0] // N_MICROBATCH) + t.shape[1:])
    return _jnp.moveaxis(t, 1, axis + 1)


def setup_inputs(seed: int = 0) -> dict:
    inp = _fwd_setup_inputs(seed)
    key = _jax.random.fold_in(_jax.random.key(seed), 7919)
    shape, _ = _output_shape()
    out = dict(inp)
    out["loss_target"] = _jax.random.normal(_jax.random.fold_in(key, 0), shape, _jnp.float32)
    for i, name in enumerate(TWIN_WEIGHTS):
        w = inp[name].astype(_jnp.float32)
        if MOMENT_SCALE is None:
            s = _jnp.sqrt(_jnp.mean(_jnp.square(w)) + 1e-30)
        else:
            s = MOMENT_SCALE[name]
        km, kv = _jax.random.split(_jax.random.fold_in(key, i + 1))
        out[name] = w
        out["m_" + name] = s * _jax.random.normal(km, w.shape, _jnp.float32)
        out["v_" + name] = (s * s) * _jax.random.uniform(kv, w.shape, _jnp.float32, 0.5, 1.5)
    if N_MICROBATCH > 1:
        for name, axis in PER_EXAMPLE_BATCH_AXIS.items():
            out[name] = _to_microbatches(out[name], axis)
    return {'x': out['x'], 'c': out['c'], 'mod_w': out['mod_w'], 'mod_b': out['mod_b'], 'mix_norm_g': out['mix_norm_g'], 'ffn_norm_g': out['ffn_norm_g'], 'attn_w_in': out['attn_w_in'], 'attn_b_f': out['attn_b_f'], 'attn_w_o': out['attn_w_o'], 'gm_w_in': out['gm_w_in'], 'gm_v_g': out['gm_v_g'], 'gm_w_s': out['gm_w_s'], 'gm_b_s': out['gm_b_s'], 'gm_w_o': out['gm_w_o'], 'ffn_w_in': out['ffn_w_in'], 'ffn_conv_w': out['ffn_conv_w'], 'ffn_conv_b': out['ffn_conv_b'], 'ffn_w_out': out['ffn_w_out'], 'final_g': out['final_g'], 'loss_target': out['loss_target'], 'm_mod_w': out['m_mod_w'], 'm_mod_b': out['m_mod_b'], 'm_mix_norm_g': out['m_mix_norm_g'], 'm_ffn_norm_g': out['m_ffn_norm_g'], 'm_attn_w_in': out['m_attn_w_in'], 'm_attn_b_f': out['m_attn_b_f'], 'm_attn_w_o': out['m_attn_w_o'], 'm_gm_w_in': out['m_gm_w_in'], 'm_gm_v_g': out['m_gm_v_g'], 'm_gm_w_s': out['m_gm_w_s'], 'm_gm_b_s': out['m_gm_b_s'], 'm_gm_w_o': out['m_gm_w_o'], 'm_ffn_w_in': out['m_ffn_w_in'], 'm_ffn_conv_w': out['m_ffn_conv_w'], 'm_ffn_conv_b': out['m_ffn_conv_b'], 'm_ffn_w_out': out['m_ffn_w_out'], 'm_final_g': out['m_final_g'], 'v_mod_w': out['v_mod_w'], 'v_mod_b': out['v_mod_b'], 'v_mix_norm_g': out['v_mix_norm_g'], 'v_ffn_norm_g': out['v_ffn_norm_g'], 'v_attn_w_in': out['v_attn_w_in'], 'v_attn_b_f': out['v_attn_b_f'], 'v_attn_w_o': out['v_attn_w_o'], 'v_gm_w_in': out['v_gm_w_in'], 'v_gm_v_g': out['v_gm_v_g'], 'v_gm_w_s': out['v_gm_w_s'], 'v_gm_b_s': out['v_gm_b_s'], 'v_gm_w_o': out['v_gm_w_o'], 'v_ffn_w_in': out['v_ffn_w_in'], 'v_ffn_conv_w': out['v_ffn_conv_w'], 'v_ffn_conv_b': out['v_ffn_conv_b'], 'v_ffn_w_out': out['v_ffn_w_out'], 'v_final_g': out['v_final_g']}


def _loss(weights, diff, rest, loss_target):
    with _jax.named_scope("forward"):
        args = {**rest, TWIN_DIFF_INPUT: diff, **{k: w.astype(_WEIGHT_DTYPES[k]) for k, w in weights.items()}}
        y = _forward(args)
    with _jax.named_scope("loss_head"):
        err = _jnp.square(y.astype(_jnp.float32) - loss_target)
        return 0.5 * _jnp.sum(_jnp.mean(err, axis=-1)) if err.ndim else 0.5 * err


def _adamw(w, g, m, v):
    m = ADAM_B1 * m + (1.0 - ADAM_B1) * g
    v = ADAM_B2 * v + (1.0 - ADAM_B2) * _jnp.square(g)
    m_hat = m / (1.0 - ADAM_B1 ** ADAM_STEP)
    v_hat = v / (1.0 - ADAM_B2 ** ADAM_STEP)
    delta = -ADAM_LR * (m_hat / (_jnp.sqrt(v_hat) + ADAM_EPS) + ADAM_WD * w)
    return delta, m, v


def reference(x, c, mod_w, mod_b, mix_norm_g, ffn_norm_g, attn_w_in, attn_b_f, attn_w_o, gm_w_in, gm_v_g, gm_w_s, gm_b_s, gm_w_o, ffn_w_in, ffn_conv_w, ffn_conv_b, ffn_w_out, final_g, loss_target, m_mod_w, m_mod_b, m_mix_norm_g, m_ffn_norm_g, m_attn_w_in, m_attn_b_f, m_attn_w_o, m_gm_w_in, m_gm_v_g, m_gm_w_s, m_gm_b_s, m_gm_w_o, m_ffn_w_in, m_ffn_conv_w, m_ffn_conv_b, m_ffn_w_out, m_final_g, v_mod_w, v_mod_b, v_mix_norm_g, v_ffn_norm_g, v_attn_w_in, v_attn_b_f, v_attn_w_o, v_gm_w_in, v_gm_v_g, v_gm_w_s, v_gm_b_s, v_gm_w_o, v_ffn_w_in, v_ffn_conv_w, v_ffn_conv_b, v_ffn_w_out, v_final_g):
    given = dict(x=x, c=c, mod_w=mod_w, mod_b=mod_b, mix_norm_g=mix_norm_g, ffn_norm_g=ffn_norm_g, attn_w_in=attn_w_in, attn_b_f=attn_b_f, attn_w_o=attn_w_o, gm_w_in=gm_w_in, gm_v_g=gm_v_g, gm_w_s=gm_w_s, gm_b_s=gm_b_s, gm_w_o=gm_w_o, ffn_w_in=ffn_w_in, ffn_conv_w=ffn_conv_w, ffn_conv_b=ffn_conv_b, ffn_w_out=ffn_w_out, final_g=final_g, loss_target=loss_target, m_mod_w=m_mod_w, m_mod_b=m_mod_b, m_mix_norm_g=m_mix_norm_g, m_ffn_norm_g=m_ffn_norm_g, m_attn_w_in=m_attn_w_in, m_attn_b_f=m_attn_b_f, m_attn_w_o=m_attn_w_o, m_gm_w_in=m_gm_w_in, m_gm_v_g=m_gm_v_g, m_gm_w_s=m_gm_w_s, m_gm_b_s=m_gm_b_s, m_gm_w_o=m_gm_w_o, m_ffn_w_in=m_ffn_w_in, m_ffn_conv_w=m_ffn_conv_w, m_ffn_conv_b=m_ffn_conv_b, m_ffn_w_out=m_ffn_w_out, m_final_g=m_final_g, v_mod_w=v_mod_w, v_mod_b=v_mod_b, v_mix_norm_g=v_mix_norm_g, v_ffn_norm_g=v_ffn_norm_g, v_attn_w_in=v_attn_w_in, v_attn_b_f=v_attn_b_f, v_attn_w_o=v_attn_w_o, v_gm_w_in=v_gm_w_in, v_gm_v_g=v_gm_v_g, v_gm_w_s=v_gm_w_s, v_gm_b_s=v_gm_b_s, v_gm_w_o=v_gm_w_o, v_ffn_w_in=v_ffn_w_in, v_ffn_conv_w=v_ffn_conv_w, v_ffn_conv_b=v_ffn_conv_b, v_ffn_w_out=v_ffn_w_out, v_final_g=v_final_g)
    weights = {n: given[n] for n in TWIN_WEIGHTS}
    shared = {n: given[n] for n in SHARED_INPUTS}
    per_example = {n: given[n] for n in ['x', 'c']}
    grad_fn = _jax.value_and_grad(_loss, argnums=(0, 1))

    def one_microbatch(ex, loss_target):
        ex = dict(ex)
        diff = ex.pop(TWIN_DIFF_INPUT)
        return grad_fn(weights, diff, {**shared, **ex}, loss_target)

    if N_MICROBATCH == 1:
        loss, (grad_w, grad_x) = one_microbatch(per_example, given["loss_target"])
    else:
        def body(carry, xs):
            loss_sum, grad_sum = carry
            l_k, (gw_k, gx_k) = one_microbatch(xs[0], xs[1])
            with _jax.named_scope("update"):
                return (loss_sum + l_k, _jax.tree.map(_jnp.add, grad_sum, gw_k)), gx_k

        init = (_jnp.zeros((), _jnp.float32), _jax.tree.map(_jnp.zeros_like, weights))
        (loss, grad_w), grad_x = _jax.lax.scan(body, init, (per_example, given["loss_target"]))
    with _jax.named_scope("update"):
        delta_w, new_m, new_v = {}, {}, {}
        for n in TWIN_WEIGHTS:
            delta_w[n], new_m[n], new_v[n] = _adamw(weights[n], grad_w[n], given["m_" + n], given["v_" + n])
    return (loss, grad_x, *[grad_w[n] for n in TWIN_WEIGHTS], *[delta_w[n] for n in TWIN_WEIGHTS],
            *[new_m[n] for n in TWIN_WEIGHTS], *[new_v[n] for n in TWIN_WEIGHTS])
```

```python
import jax
import jax.numpy as jnp
from jax import lax
from jax.experimental import pallas as pl
from jax.experimental.pallas import tpu as pltpu

f32 = jnp.float32
bf16 = jnp.bfloat16

AXES = ("x", "y", "c")
N_DEV = 8
N_CHIPS = 4
LANES = 128
SUBLANES = 8
HEAD_DIM = 128
CHUNK = 128
GM_GROUP = 128
CONV_W = 3
EPS = 1e-6
NEG = -1e30
VMEM_LIMIT_BYTES = 56 * 1024 * 1024
MAX_FULL_K = 2048
SLOT_ROWS = 512
LOG2E = 1.4426950408889634
ATT_TILE = 512

ADAM_LR = 0.001
ADAM_B1 = 0.9
ADAM_B2 = 0.999
ADAM_EPS = 1e-08
ADAM_WD = 0.01
ADAM_STEP = 10

MESH = pl.DeviceIdType.MESH
ANY = pl.BlockSpec(memory_space=pl.ANY)


def _cp(sem):
    return pltpu.CompilerParams(dimension_semantics=sem, vmem_limit_bytes=VMEM_LIMIT_BYTES)


def _pick(n, prefs):
    for p in prefs:
        if n % p == 0:
            return p
    return n


def _round_up(n, m):
    return (n + m - 1) // m * m


def all_gather(xl, name):
    def body(x_ref, out_ref, send_sems, recv_sems, local_sem):
        x, y, c = lax.axis_index("x"), lax.axis_index("y"), lax.axis_index("c")
        me, sibling = (x, y, c), (x, y, 1 - c)
        chips = [(1 - x, y), (x, 1 - y), (1 - x, 1 - y)]

        def slot(px, py, pc):
            return out_ref.at[4 * px + 2 * py + pc]

        def copy(k, block, to, src=None):
            return pltpu.make_async_remote_copy(
                src_ref=slot(*block) if src is None else src, dst_ref=slot(*block),
                send_sem=send_sems.at[k], recv_sem=recv_sems.at[k],
                device_id=to, device_id_type=MESH)

        mine = pltpu.make_async_copy(x_ref, slot(*me), local_sem)
        mine.start()
        first = [copy(0, me, sibling, src=x_ref)]
        first += [copy(1 + j, me, (*chip, c), src=x_ref) for j, chip in enumerate(chips)]
        for cp in first:
            cp.start()
        passed = [copy(4 + j, (*chip, c), sibling) for j, chip in enumerate(chips)]
        for j, chip in enumerate(chips):
            copy(1 + j, (*chip, c), me).wait_recv()
            passed[j].start()
        copy(0, sibling, me).wait_recv()
        for j, chip in enumerate(chips):
            copy(4 + j, (*chip, 1 - c), me).wait_recv()
        for cp in first + passed:
            cp.wait_send()
        mine.wait()

    return pl.pallas_call(
        body, name=name,
        out_shape=jax.ShapeDtypeStruct((N_DEV,) + xl.shape, xl.dtype),
        in_specs=[ANY], out_specs=ANY,
        scratch_shapes=[pltpu.SemaphoreType.DMA((7,)), pltpu.SemaphoreType.DMA((7,)),
                        pltpu.SemaphoreType.DMA],
    )(xl)


def pair_exchange(g8, name):
    _, R, W = g8.shape

    def body(g_ref, out_ref, send_sems, recv_sems):
        x, y, c = lax.axis_index("x"), lax.axis_index("y"), lax.axis_index("c")
        copies = []
        for q in range(N_CHIPS):
            copies.append(pltpu.make_async_remote_copy(
                src_ref=g_ref.at[2 * q + (1 - c)], dst_ref=out_ref.at[q],
                send_sem=send_sems.at[q], recv_sem=recv_sems.at[q],
                device_id=(x, y, 1 - c), device_id_type=MESH))
        for cp in copies:
            cp.start()
        for cp in copies:
            cp.wait()

    return pl.pallas_call(
        body, name=name,
        out_shape=jax.ShapeDtypeStruct((N_CHIPS, R, W), g8.dtype),
        in_specs=[ANY], out_specs=ANY,
        scratch_shapes=[pltpu.SemaphoreType.DMA((N_CHIPS,)), pltpu.SemaphoreType.DMA((N_CHIPS,))],
    )(g8)


def chip_exchange(p4, name):
    def body(p_ref, out_ref, send_sems, recv_sems, local_sem):
        x, y, c = lax.axis_index("x"), lax.axis_index("y"), lax.axis_index("c")
        my_q = 2 * x + y
        chips = [(1 - x, y), (x, 1 - y), (1 - x, 1 - y)]
        mine = pltpu.make_async_copy(p_ref.at[my_q], out_ref.at[my_q], local_sem)
        mine.start()
        copies = []
        for k, (px, py) in enumerate(chips):
            copies.append(pltpu.make_async_remote_copy(
                src_ref=p_ref.at[2 * px + py], dst_ref=out_ref.at[my_q],
                send_sem=send_sems.at[k], recv_sem=recv_sems.at[k],
                device_id=(px, py, c), device_id_type=MESH))
        for cp in copies:
            cp.start()
        for k, (px, py) in enumerate(chips):
            pltpu.make_async_remote_copy(
                src_ref=p_ref.at[my_q], dst_ref=out_ref.at[2 * px + py],
                send_sem=send_sems.at[k], recv_sem=recv_sems.at[k],
                device_id=(px, py, c), device_id_type=MESH).wait_recv()
        for cp in copies:
            cp.wait_send()
        mine.wait()

    return pl.pallas_call(
        body, name=name,
        out_shape=jax.ShapeDtypeStruct(p4.shape, p4.dtype),
        in_specs=[ANY], out_specs=ANY,
        scratch_shapes=[pltpu.SemaphoreType.DMA((3,)), pltpu.SemaphoreType.DMA((3,)),
                        pltpu.SemaphoreType.DMA],
    )(p4)


def pair_sum(g8, got4, c_idx, name):
    _, R, W = g8.shape
    tr = _pick(R, (512, 256, 128, 64, 32, 16))
    g5 = g8.reshape(N_CHIPS, 2, R, W)

    def body(c_ref, a_ref, b_ref, o_ref):
        o_ref[...] = (a_ref[...].astype(f32) + b_ref[...].astype(f32)).astype(o_ref.dtype)

    grid_spec = pltpu.PrefetchScalarGridSpec(
        num_scalar_prefetch=1, grid=(N_CHIPS, R // tr),
        in_specs=[pl.BlockSpec((None, None, tr, W), lambda q, r, cr: (q, cr[0], r, 0)),
                  pl.BlockSpec((None, tr, W), lambda q, r, cr: (q, r, 0))],
        out_specs=pl.BlockSpec((None, tr, W), lambda q, r, cr: (q, r, 0)))
    return pl.pallas_call(
        body, name=name, grid_spec=grid_spec,
        out_shape=jax.ShapeDtypeStruct((N_CHIPS, R, W), g8.dtype),
        compiler_params=_cp(("parallel", "parallel")),
    )(c_idx, g5, got4)


def sum_slots(xs, name, out_dtype=f32):
    S, R, W = xs.shape
    tr = _pick(R, (512, 256, 128, 64, 32, 16, 8))

    def body(x_ref, o_ref):
        acc = x_ref[0].astype(f32)
        for s in range(1, S):
            acc = acc + x_ref[s].astype(f32)
        o_ref[...] = acc.astype(o_ref.dtype)

    return pl.pallas_call(
        body, name=name, grid=(R // tr,),
        in_specs=[pl.BlockSpec((S, tr, W), lambda r: (0, r, 0))],
        out_specs=pl.BlockSpec((tr, W), lambda r: (r, 0)),
        out_shape=jax.ShapeDtypeStruct((R, W), out_dtype),
        compiler_params=_cp(("parallel",)),
    )(xs)


def matmul(a, b, *, name, ta=False, tb=False, a_split=False, b_split=False, out_split=False,
           out_dtype=f32, resid=None, gvec=None, emit_acc=False):
    if a_split:
        rows, cols = a.shape[1], 2 * a.shape[2]
        M, K = (cols, rows) if ta else (rows, cols)
    else:
        M, K = (a.shape[1], a.shape[0]) if ta else a.shape
    if b_split:
        assert not tb
        N = 2 * b.shape[2]
        assert b.shape[1] == K
    else:
        N = b.shape[0] if tb else b.shape[1]
        assert (b.shape[1] if tb else b.shape[0]) == K, (a.shape, b.shape, name)

    m_split = a_split and ta
    k_split = a_split and not ta
    n_split = b_split or out_split
    tm = _pick(M // 2 if m_split else M, (1024, 512, 256, 128, 64, 32, 16, 8))
    tn = _pick(N // 2 if n_split else N, (1024, 512, 256, 128))
    if K <= MAX_FULL_K and not k_split:
        tk = K
    else:
        tk = _pick(K // 2 if k_split else K, (1024, 512, 256, 128))
    nk = K // tk
    n_half = (N // 2) // tn if n_split else 0
    k_half = (K // 2) // tk if k_split else 0
    m_half = (M // 2) // tm if m_split else 0

    if m_split:
        a_spec = pl.BlockSpec((None, tk, tm), lambda i, j, k: (i // m_half, k, i % m_half))
    elif k_split:
        a_spec = pl.BlockSpec((None, tm, tk), lambda i, j, k: (k // k_half, i, k % k_half))
    elif ta:
        a_spec = pl.BlockSpec((tk, tm), lambda i, j, k: (k, i))
    else:
        a_spec = pl.BlockSpec((tm, tk), lambda i, j, k: (i, k))
    if b_split:
        b_spec = pl.BlockSpec((None, tk, tn), lambda i, j, k: (j // n_half, k, j % n_half))
    elif tb:
        b_spec = pl.BlockSpec((tn, tk), lambda i, j, k: (j, k))
    else:
        b_spec = pl.BlockSpec((tk, tn), lambda i, j, k: (k, j))
    if out_split:
        o_spec = pl.BlockSpec((None, tm, tn), lambda i, j, k: (j // n_half, i, j % n_half))
        o_shape = (2, M, N // 2)
    else:
        o_spec = pl.BlockSpec((tm, tn), lambda i, j, k: (i, j))
        o_shape = (M, N)

    in_specs = [a_spec, b_spec]
    args = [a, b]
    if resid is not None:
        in_specs.append(pl.BlockSpec((tm, tn), lambda i, j, k: (i, j)))
        args.append(resid)
    if gvec is not None:
        in_specs.append(pl.BlockSpec((1, tn), lambda i, j, k: (0, j)))
        args.append(gvec)
    out_specs = [o_spec]
    out_shape = [jax.ShapeDtypeStruct(o_shape, out_dtype)]
    if emit_acc:
        out_specs.append(pl.BlockSpec((tm, tn), lambda i, j, k: (i, j)))
        out_shape.append(jax.ShapeDtypeStruct((M, N), bf16))
    dims = (((0 if ta else 1,), (1 if tb else 0,)), ((), ()))
    has_r, has_g = resid is not None, gvec is not None

    def body(*refs):
        a_ref, b_ref = refs[0], refs[1]
        pos = 2
        r_ref = g_ref = y_ref = None
        if has_r:
            r_ref = refs[pos]
            pos += 1
        if has_g:
            g_ref = refs[pos]
            pos += 1
        o_ref = refs[pos]
        pos += 1
        if emit_acc:
            y_ref = refs[pos]
            pos += 1

        def finish(acc):
            if emit_acc:
                y_ref[...] = acc.astype(bf16)
            if has_g:
                acc = acc * g_ref[...]
            if has_r:
                acc = r_ref[...] + acc
            o_ref[...] = acc.astype(o_ref.dtype)

        part = lax.dot_general(a_ref[...].astype(bf16), b_ref[...].astype(bf16), dims,
                               preferred_element_type=f32)
        if nk == 1:
            finish(part)
            return
        acc_ref = refs[pos]
        k = pl.program_id(2)

        @pl.when(k == 0)
        def _():
            acc_ref[...] = part

        @pl.when(k > 0)
        def _():
            acc_ref[...] += part

        @pl.when(k == nk - 1)
        def _():
            finish(acc_ref[...])

    outs = pl.pallas_call(
        body, name=name, grid=(M // tm, N // tn, nk),
        in_specs=in_specs, out_specs=out_specs, out_shape=out_shape,
        scratch_shapes=[pltpu.VMEM((tm, tn), f32)] if nk > 1 else [],
        compiler_params=_cp(("parallel", "parallel", "arbitrary")),
    )(*args)
    return outs if emit_acc else outs[0]


def _rows(T):
    return _pick(T, (256, 128, 64, 32, 16, 8))


def norm_mod_fwd(x, gn, sc, sh, name):
    T, D = x.shape
    tr = _rows(T)

    def body(x_ref, gn_ref, sc_ref, sh_ref, h_ref):
        xv = x_ref[...]
        r = lax.rsqrt(jnp.mean(xv * xv, axis=-1, keepdims=True) + EPS)
        y = (xv * r) * gn_ref[...]
        h_ref[...] = (y * (1.0 + sc_ref[...]) + sh_ref[...]).astype(bf16)

    vec = pl.BlockSpec((1, D), lambda i: (0, 0))
    row = pl.BlockSpec((tr, D), lambda i: (i, 0))
    return pl.pallas_call(
        body, name=name, grid=(T // tr,), in_specs=[row, vec, vec, vec], out_specs=row,
        out_shape=jax.ShapeDtypeStruct((T, D), bf16), compiler_params=_cp(("parallel",)),
    )(x, gn, sc, sh)


def norm_mod_bwd(x, gn, sc, dh, dx_res, name):
    T, D = x.shape
    tr = _rows(T)

    def body(x_ref, gn_ref, sc_ref, dh_ref, dr_ref, dx_ref, dsh_ref, dsc_ref, dgn_ref):
        @pl.when(pl.program_id(0) == 0)
        def _():
            dsh_ref[...] = jnp.zeros_like(dsh_ref)
            dsc_ref[...] = jnp.zeros_like(dsc_ref)
            dgn_ref[...] = jnp.zeros_like(dgn_ref)

        xv = x_ref[...]
        r = lax.rsqrt(jnp.mean(xv * xv, axis=-1, keepdims=True) + EPS)
        xn = xv * r
        gn_v = gn_ref[...]
        dh_v = dh_ref[...]
        dsh_ref[...] += jnp.sum(dh_v, axis=0, keepdims=True)
        dsc_ref[...] += jnp.sum(dh_v * (xn * gn_v), axis=0, keepdims=True)
        dy = dh_v * (1.0 + sc_ref[...])
        dgn_ref[...] += jnp.sum(dy * xn, axis=0, keepdims=True)
        dxn = dy * gn_v
        dx = r * (dxn - xn * jnp.mean(dxn * xn, axis=-1, keepdims=True))
        dx_ref[...] = dr_ref[...] + dx

    vec = pl.BlockSpec((1, D), lambda i: (0, 0))
    row = pl.BlockSpec((tr, D), lambda i: (i, 0))
    vshape = jax.ShapeDtypeStruct((1, D), f32)
    return pl.pallas_call(
        body, name=name, grid=(T // tr,), in_specs=[row, vec, vec, row, row],
        out_specs=[row, vec, vec, vec],
        out_shape=[jax.ShapeDtypeStruct((T, D), f32), vshape, vshape, vshape],
        compiler_params=_cp(("arbitrary",)),
    )(x, gn, sc, dh, dx_res)


def gate_bwd(dx, y, g, name):
    T, D = dx.shape
    tr = _rows(T)

    def body(dx_ref, y_ref, g_ref, dy_ref, dg_ref):
        @pl.when(pl.program_id(0) == 0)
        def _():
            dg_ref[...] = jnp.zeros_like(dg_ref)

        dxv = dx_ref[...]
        dy_ref[...] = (dxv * g_ref[...]).astype(bf16)
        dg_ref[...] += jnp.sum(dxv * y_ref[...].astype(f32), axis=0, keepdims=True)

    vec = pl.BlockSpec((1, D), lambda i: (0, 0))
    row = pl.BlockSpec((tr, D), lambda i: (i, 0))
    return pl.pallas_call(
        body, name=name, grid=(T // tr,), in_specs=[row, row, vec], out_specs=[row, vec],
        out_shape=[jax.ShapeDtypeStruct((T, D), bf16), jax.ShapeDtypeStruct((1, D), f32)],
        compiler_params=_cp(("arbitrary",)),
    )(dx, y, g)


def loss_head(x, fg, tgt, name):
    T, D = x.shape
    tr = _rows(T)

    def body(x_ref, fg_ref, t_ref, loss_ref, dx_ref, dfg_ref):
        @pl.when(pl.program_id(0) == 0)
        def _():
            loss_ref[...] = jnp.zeros_like(loss_ref)
            dfg_ref[...] = jnp.zeros_like(dfg_ref)

        xv = x_ref[...]
        r = lax.rsqrt(jnp.mean(xv * xv, axis=-1, keepdims=True) + EPS)
        xn = xv * r
        fg_v = fg_ref[...]
        err = xn * fg_v - t_ref[...]
        per_tok = jnp.mean(err * err, axis=-1, keepdims=True)
        loss_ref[...] += 0.5 * jnp.sum(per_tok, axis=0, keepdims=True)
        dy = err * (1.0 / D)
        dfg_ref[...] += jnp.sum(dy * xn, axis=0, keepdims=True)
        dxn = dy * fg_v
        dx_ref[...] = r * (dxn - xn * jnp.mean(dxn * xn, axis=-1, keepdims=True))

    vec = pl.BlockSpec((1, D), lambda i: (0, 0))
    row = pl.BlockSpec((tr, D), lambda i: (i, 0))
    one = pl.BlockSpec((1, 1), lambda i: (0, 0))
    return pl.pallas_call(
        body, name=name, grid=(T // tr,), in_specs=[row, vec, row], out_specs=[one, row, vec],
        out_shape=[jax.ShapeDtypeStruct((1, 1), f32), jax.ShapeDtypeStruct((T, D), f32),
                   jax.ShapeDtypeStruct((1, D), f32)],
        compiler_params=_cp(("arbitrary",)),
    )(x, fg, tgt)


def _conv_tiles(T, FP):
    return _pick(T, (512, 256, 128, 64, 32, 16, 8)), _pick(FP, (512, 256, 128))


def _conv_specs(tr, tc, T, FP):
    nj = FP // tc
    r8 = tr // SUBLANES
    last8 = T // SUBLANES - 1
    main = pl.BlockSpec((2, tr, tc), lambda j, i: (0, i, j))
    prev = pl.BlockSpec((2, SUBLANES, tc), lambda j, i: (0, jnp.maximum(i * r8 - 1, 0), j))
    nxt = pl.BlockSpec((2, SUBLANES, tc), lambda j, i: (0, jnp.minimum((i + 1) * r8, last8), j))
    wg = pl.BlockSpec((CONV_W, tc), lambda j, i: (0, j))
    wu = pl.BlockSpec((CONV_W, tc), lambda j, i: (0, j + nj))
    bg = pl.BlockSpec((1, tc), lambda j, i: (0, j))
    bu = pl.BlockSpec((1, tc), lambda j, i: (0, j + nj))
    return main, prev, nxt, wg, wu, bg, bu


def _causal_taps(av, hp_ref, s, has_prev, row):
    h7 = jnp.where(has_prev, hp_ref[s, 7:8, :], 0.0)
    h6 = jnp.where(has_prev, hp_ref[s, 6:7, :], 0.0)
    m1 = jnp.where(row == 0, h7, pltpu.roll(av, 1, 0))
    m2 = jnp.where(row == 0, h6, jnp.where(row == 1, h7, pltpu.roll(av, 2, 0)))
    return m1, m2


def conv_glu_fwd(a3, conv_w, conv_b, name):
    _, T, FP = a3.shape
    tr, tc = _conv_tiles(T, FP)
    main, prev, _, wg, wu, bg, bu = _conv_specs(tr, tc, T, FP)

    def body(a_ref, hp_ref, wg_ref, wu_ref, bg_ref, bu_ref, act_ref):
        has_prev = pl.program_id(1) > 0
        row = lax.broadcasted_iota(jnp.int32, (tr, tc), 0)

        def conv(s, w_ref, b_ref):
            av = a_ref[s]
            m1, m2 = _causal_taps(av, hp_ref, s, has_prev, row)
            return w_ref[0:1, :] * m2 + w_ref[1:2, :] * m1 + w_ref[2:3, :] * av + b_ref[...]

        gate = conv(0, wg_ref, bg_ref)
        up = conv(1, wu_ref, bu_ref)
        act_ref[...] = ((gate * jax.nn.sigmoid(gate)) * up).astype(bf16)

    return pl.pallas_call(
        body, name=name, grid=(FP // tc, T // tr),
        in_specs=[main, prev, wg, wu, bg, bu],
        out_specs=pl.BlockSpec((tr, tc), lambda j, i: (i, j)),
        out_shape=jax.ShapeDtypeStruct((T, FP), bf16),
        compiler_params=_cp(("parallel", "parallel")),
    )(a3, a3, conv_w, conv_w, conv_b, conv_b)


def conv_glu_bwd(a3, conv_w, conv_b, dact, name):
    _, T, FP = a3.shape
    tr, tc = _conv_tiles(T, FP)
    ni = T // tr
    main, prev, nxt, wg, wu, bg, bu = _conv_specs(tr, tc, T, FP)
    r8 = tr // SUBLANES
    last8 = T // SUBLANES - 1
    d_main = pl.BlockSpec((tr, tc), lambda j, i: (i, j))
    d_next = pl.BlockSpec((SUBLANES, tc), lambda j, i: (jnp.minimum((i + 1) * r8, last8), j))

    def body(a_ref, hp_ref, hn_ref, d_ref, dn_ref, wg_ref, wu_ref, bg_ref, bu_ref,
             da_ref, dwg_ref, dwu_ref, dbg_ref, dbu_ref):
        i = pl.program_id(1)
        has_prev = i > 0
        has_next = i < ni - 1
        row = lax.broadcasted_iota(jnp.int32, (tr, tc), 0)
        row8 = lax.broadcasted_iota(jnp.int32, (SUBLANES, tc), 0)

        @pl.when(i == 0)
        def _():
            dwg_ref[...] = jnp.zeros_like(dwg_ref)
            dwu_ref[...] = jnp.zeros_like(dwu_ref)
            dbg_ref[...] = jnp.zeros_like(dbg_ref)
            dbu_ref[...] = jnp.zeros_like(dbu_ref)

        def prep(s, w_ref, b_ref):
            av = a_ref[s]
            m1, m2 = _causal_taps(av, hp_ref, s, has_prev, row)
            w0, w1, w2, bv = w_ref[0:1, :], w_ref[1:2, :], w_ref[2:3, :], b_ref[...]
            pre = w0 * m2 + w1 * m1 + w2 * av + bv
            an = hn_ref[s]
            l1 = a_ref[s, tr - 1:tr, :]
            l2 = a_ref[s, tr - 2:tr - 1, :]
            n1 = jnp.where(row8 == 0, l1, pltpu.roll(an, 1, 0))
            n2 = jnp.where(row8 == 0, l2, jnp.where(row8 == 1, l1, pltpu.roll(an, 2, 0)))
            pre_n = w0 * n2 + w1 * n1 + w2 * an + bv
            return av, m1, m2, pre, pre_n

        def glu_bwd(gate, up, d):
            sg = jax.nn.sigmoid(gate)
            dgate = d * up * (sg * (1.0 + gate * (1.0 - sg)))
            dup = d * (gate * sg)
            return dgate, dup

        def row_of(v8, r):
            return jnp.sum(jnp.where(row8 == r, v8, 0.0), axis=0, keepdims=True)

        def back(dc, dc_n, w_ref):
            n0, n1 = row_of(dc_n, 0), row_of(dc_n, 1)
            p1 = jnp.where(row == tr - 1, n0, pltpu.roll(dc, tr - 1, 0))
            p2 = jnp.where(row == tr - 1, n1, jnp.where(row == tr - 2, n0, pltpu.roll(dc, tr - 2, 0)))
            return w_ref[2:3, :] * dc + w_ref[1:2, :] * p1 + w_ref[0:1, :] * p2

        def tok_sum(v):
            return jnp.sum(v, axis=0, keepdims=True)

        ag, g1, g2, gate, gate_n = prep(0, wg_ref, bg_ref)
        au, u1, u2, up, up_n = prep(1, wu_ref, bu_ref)
        dcg, dcu = glu_bwd(gate, up, d_ref[...])
        dn = jnp.where(has_next, dn_ref[...], 0.0)
        dcg_n, dcu_n = glu_bwd(gate_n, up_n, dn)
        da_ref[0] = back(dcg, dcg_n, wg_ref).astype(bf16)
        da_ref[1] = back(dcu, dcu_n, wu_ref).astype(bf16)
        dwg_ref[0:1, :] += tok_sum(dcg * g2)
        dwg_ref[1:2, :] += tok_sum(dcg * g1)
        dwg_ref[2:3, :] += tok_sum(dcg * ag)
        dwu_ref[0:1, :] += tok_sum(dcu * u2)
        dwu_ref[1:2, :] += tok_sum(dcu * u1)
        dwu_ref[2:3, :] += tok_sum(dcu * au)
        dbg_ref[...] += tok_sum(dcg)
        dbu_ref[...] += tok_sum(dcu)

    w_out = pl.BlockSpec((CONV_W, tc), lambda j, i: (0, j))
    b_out = pl.BlockSpec((1, tc), lambda j, i: (0, j))
    return pl.pallas_call(
        body, name=name, grid=(FP // tc, ni),
        in_specs=[main, prev, nxt, d_main, d_next, wg, wu, bg, bu],
        out_specs=[main, w_out, w_out, b_out, b_out],
        out_shape=[jax.ShapeDtypeStruct((2, T, FP), bf16),
                   jax.ShapeDtypeStruct((CONV_W, FP), f32), jax.ShapeDtypeStruct((CONV_W, FP), f32),
                   jax.ShapeDtypeStruct((1, FP), f32), jax.ShapeDtypeStruct((1, FP), f32)],
        compiler_params=_cp(("parallel", "arbitrary")),
    )(a3, a3, a3, dact, dact, conv_w, conv_w, conv_b, conv_b)


_GELU_C = 0.7978845608028654
_GELU_A = 0.044715


def _gelu(x):
    return 0.5 * x * (1.0 + jnp.tanh(_GELU_C * (x + _GELU_A * (x * x * x))))


def _gelu_and_grad(x):
    t = jnp.tanh(_GELU_C * (x + _GELU_A * (x * x * x)))
    g = 0.5 * x * (1.0 + t)
    dg = 0.5 * (1.0 + t) + 0.5 * x * (1.0 - t * t) * (_GELU_C * (1.0 + 3.0 * _GELU_A * (x * x)))
    return g, dg


def _tril_bf16(w):
    r = lax.broadcasted_iota(jnp.int32, w.shape, 0)
    c = lax.broadcasted_iota(jnp.int32, w.shape, 1)
    return jnp.where(r >= c, w, 0.0).astype(bf16)


def gm_gate_fwd(z, vg, ws, bs_t, name):
    T, D2 = z.shape
    D = D2 // 2
    G = D // GM_GROUP
    tr = _pick(T, (256, 128))
    nc = tr // CHUNK

    def body(z_ref, vg_ref, ws_ref, bs_ref, o_ref):
        u = _gelu(z_ref[:, :D])
        v = _gelu(z_ref[:, D:])
        rv = lax.rsqrt(jnp.mean(v * v, axis=-1, keepdims=True) + EPS)
        vn = ((v * rv) * vg_ref[...]).astype(bf16)
        for g in range(G):
            wg = _tril_bf16(ws_ref[g])
            bg = bs_ref[:, g:g + 1]
            cs = slice(g * GM_GROUP, (g + 1) * GM_GROUP)
            for c in range(nc):
                rs = slice(c * CHUNK, (c + 1) * CHUNK)
                sv = jnp.dot(wg, vn[rs, cs], preferred_element_type=f32) + bg
                o_ref[rs, cs] = (u[rs, cs] * sv).astype(bf16)

    return pl.pallas_call(
        body, name=name, grid=(T // tr,),
        in_specs=[pl.BlockSpec((tr, D2), lambda i: (i, 0)),
                  pl.BlockSpec((1, D), lambda i: (0, 0)),
                  pl.BlockSpec((G, CHUNK, CHUNK), lambda i: (0, 0, 0)),
                  pl.BlockSpec((CHUNK, G), lambda i: (0, 0))],
        out_specs=pl.BlockSpec((tr, D), lambda i: (i, 0)),
        out_shape=jax.ShapeDtypeStruct((T, D), bf16),
        compiler_params=_cp(("parallel",)),
    )(z, vg, ws, bs_t)


def gm_gate_bwd(z, vg, ws, bs_t, dgated, name):
    T, D2 = z.shape
    D = D2 // 2
    G = D // GM_GROUP
    tr = _pick(T, (256, 128))
    nc = tr // CHUNK

    def body(z_ref, vg_ref, ws_ref, bs_ref, dg_ref, dz_ref, dws_ref, dbs_ref, dvg_ref,
             du_s, dvn_s):
        @pl.when(pl.program_id(0) == 0)
        def _():
            dws_ref[...] = jnp.zeros_like(dws_ref)
            dbs_ref[...] = jnp.zeros_like(dbs_ref)
            dvg_ref[...] = jnp.zeros_like(dvg_ref)

        u, du_dz = _gelu_and_grad(z_ref[:, :D])
        v, dv_dz = _gelu_and_grad(z_ref[:, D:])
        rv = lax.rsqrt(jnp.mean(v * v, axis=-1, keepdims=True) + EPS)
        vhat = v * rv
        vg_v = vg_ref[...]
        vn = (vhat * vg_v).astype(bf16)
        rr = lax.broadcasted_iota(jnp.int32, (CHUNK, CHUNK), 0)
        cc = lax.broadcasted_iota(jnp.int32, (CHUNK, CHUNK), 1)
        for g in range(G):
            wg = _tril_bf16(ws_ref[g])
            bg = bs_ref[:, g:g + 1]
            cs = slice(g * GM_GROUP, (g + 1) * GM_GROUP)
            dw_acc = jnp.zeros((CHUNK, CHUNK), f32)
            db_acc = jnp.zeros((CHUNK, 1), f32)
            for c in range(nc):
                rs = slice(c * CHUNK, (c + 1) * CHUNK)
                vb = vn[rs, cs]
                sv = jnp.dot(wg, vb, preferred_element_type=f32) + bg
                dgb = dg_ref[rs, cs]
                du_s[rs, cs] = dgb * sv
                dsv = dgb * u[rs, cs]
                dsv_b = dsv.astype(bf16)
                dw_acc += lax.dot_general(dsv_b, vb, (((1,), (1,)), ((), ())),
                                          preferred_element_type=f32)
                db_acc += jnp.sum(dsv, axis=1, keepdims=True)
                dvn_s[rs, cs] = lax.dot_general(wg, dsv_b, (((0,), (0,)), ((), ())),
                                                preferred_element_type=f32)
            dws_ref[g] += jnp.where(rr >= cc, dw_acc, 0.0)
            dbs_ref[:, g:g + 1] += db_acc
        dz_ref[:, :D] = (du_s[...] * du_dz).astype(bf16)
        dvn = dvn_s[...]
        dvg_ref[...] += jnp.sum(dvn * vhat, axis=0, keepdims=True)
        dvh = dvn * vg_v
        dv = rv * (dvh - vhat * jnp.mean(dvh * vhat, axis=-1, keepdims=True))
        dz_ref[:, D:] = (dv * dv_dz).astype(bf16)

    return pl.pallas_call(
        body, name=name, grid=(T // tr,),
        in_specs=[pl.BlockSpec((tr, D2), lambda i: (i, 0)),
                  pl.BlockSpec((1, D), lambda i: (0, 0)),
                  pl.BlockSpec((G, CHUNK, CHUNK), lambda i: (0, 0, 0)),
                  pl.BlockSpec((CHUNK, G), lambda i: (0, 0)),
                  pl.BlockSpec((tr, D), lambda i: (i, 0))],
        out_specs=[pl.BlockSpec((tr, D2), lambda i: (i, 0)),
                   pl.BlockSpec((G, CHUNK, CHUNK), lambda i: (0, 0, 0)),
                   pl.BlockSpec((CHUNK, G), lambda i: (0, 0)),
                   pl.BlockSpec((1, D), lambda i: (0, 0))],
        out_shape=[jax.ShapeDtypeStruct((T, D2), bf16),
                   jax.ShapeDtypeStruct((G, CHUNK, CHUNK), f32),
                   jax.ShapeDtypeStruct((CHUNK, G), f32),
                   jax.ShapeDtypeStruct((1, D), f32)],
        scratch_shapes=[pltpu.VMEM((tr, D), f32), pltpu.VMEM((tr, D), f32)],
        compiler_params=_cp(("arbitrary",)),
    )(z, vg, ws, bs_t, dgated)


def fox_gates_fwd(flog_t, b_col, name):
    H, T = flog_t.shape

    def body(fl_ref, b_ref, o_ref):
        xv = fl_ref[...] + b_ref[...]
        lf = jnp.minimum(xv, 0.0) - jnp.log1p(jnp.exp(-jnp.abs(xv)))
        lane = lax.broadcasted_iota(jnp.int32, (H, T), 1)
        s = 1
        while s < T:
            lf = lf + jnp.where(lane >= s, pltpu.roll(lf, s, 1), 0.0)
            s *= 2
        o_ref[...] = lf * LOG2E

    return pl.pallas_call(
        body, name=name, out_shape=jax.ShapeDtypeStruct((H, T), f32),
        compiler_params=pltpu.CompilerParams(vmem_limit_bytes=VMEM_LIMIT_BYTES),
    )(flog_t, b_col)


def fox_gates_bwd(flog_t, b_col, dF, name):
    H, T = flog_t.shape

    def body(fl_ref, b_ref, d_ref, o_ref, db_ref):
        xv = fl_ref[...] + b_ref[...]
        g = d_ref[...]
        lane = lax.broadcasted_iota(jnp.int32, (H, T), 1)
        s = 1
        while s < T:
            g = g + jnp.where(lane < T - s, pltpu.roll(g, T - s, 1), 0.0)
            s *= 2
        dfl = g * jax.nn.sigmoid(-xv)
        o_ref[...] = dfl
        db_ref[...] = jnp.sum(dfl, axis=1, keepdims=True)

    return pl.pallas_call(
        body, name=name,
        out_shape=[jax.ShapeDtypeStruct((H, T), f32), jax.ShapeDtypeStruct((H, 1), f32)],
        compiler_params=pltpu.CompilerParams(vmem_limit_bytes=VMEM_LIMIT_BYTES),
    )(flog_t, b_col, dF)


_NT = (((1,), (1,)), ((), ()))
_TN = (((0,), (0,)), ((), ()))


def _scores(q, k, fq, fk, i, j, t):
    s = lax.dot_general(q, k, _NT, preferred_element_type=f32) * (HEAD_DIM ** -0.5 * LOG2E)
    s = s + fq - fk
    rows = i * t + lax.broadcasted_iota(jnp.int32, (t, t), 0)
    cols = j * t + lax.broadcasted_iota(jnp.int32, (t, t), 1)
    return jnp.where(cols <= rows, s, NEG)


def fox_attn_fwd(qkv, f_row, f_col, name):
    T, D3 = qkv.shape
    D = D3 // 3
    H = D // HEAD_DIM
    tq = _pick(T, (ATT_TILE, 256, 128))
    nq = T // tq

    def body(q_ref, k_ref, v_ref, fq_ref, fk_ref, o_ref, lse_ref, m_s, l_s, acc_s):
        i, j = pl.program_id(1), pl.program_id(2)

        @pl.when(j == 0)
        def _():
            m_s[...] = jnp.full_like(m_s, NEG)
            l_s[...] = jnp.zeros_like(l_s)
            acc_s[...] = jnp.zeros_like(acc_s)

        @pl.when(j <= i)
        def _():
            s = _scores(q_ref[...], k_ref[...], fq_ref[0], fk_ref[0], i, j, tq)
            m_prev = m_s[...]
            m_new = jnp.maximum(m_prev, jnp.max(s, axis=1, keepdims=True))
            alpha = jnp.exp2(m_prev - m_new)
            p = jnp.exp2(s - m_new)
            l_s[...] = alpha * l_s[...] + jnp.sum(p, axis=1, keepdims=True)
            acc_s[...] = alpha * acc_s[...] + jnp.dot(p.astype(bf16), v_ref[...],
                                                      preferred_element_type=f32)
            m_s[...] = m_new

        @pl.when(j == nq - 1)
        def _():
            o_ref[...] = (acc_s[...] / l_s[...]).astype(bf16)
            lse_ref[0] = m_s[...] + jnp.log2(l_s[...])

    blk = (tq, HEAD_DIM)
    return pl.pallas_call(
        body, name=name, grid=(H, nq, nq),
        in_specs=[pl.BlockSpec(blk, lambda h, i, j: (i, h)),
                  pl.BlockSpec(blk, lambda h, i, j: (jnp.minimum(j, i), H + h)),
                  pl.BlockSpec(blk, lambda h, i, j: (jnp.minimum(j, i), 2 * H + h)),
                  pl.BlockSpec((1, tq, 1), lambda h, i, j: (h, i, 0)),
                  pl.BlockSpec((1, 1, tq), lambda h, i, j: (h, 0, jnp.minimum(j, i)))],
        out_specs=[pl.BlockSpec(blk, lambda h, i, j: (i, h)),
                   pl.BlockSpec((1, tq, 1), lambda h, i, j: (h, i, 0))],
        out_shape=[jax.ShapeDtypeStruct((T, D), bf16), jax.ShapeDtypeStruct((H, T, 1), f32)],
        scratch_shapes=[pltpu.VMEM((tq, 1), f32), pltpu.VMEM((tq, 1), f32),
                        pltpu.VMEM((tq, HEAD_DIM), f32)],
        compiler_params=_cp(("parallel", "parallel", "arbitrary")),
    )(qkv, qkv, qkv, f_col, f_row)


def fox_attn_bwd(qkv, o, do, lse, f_row, f_col, name):
    T, D3 = qkv.shape
    D = D3 // 3
    H = D // HEAD_DIM
    t = _pick(T, (ATT_TILE, 256, 128))
    n = T // t
    scale = HEAD_DIM ** -0.5

    def body(q_ref, k_ref, v_ref, o_ref, do_ref, lse_ref, fq_ref, fk_ref,
             dq_ref, dk_ref, dv_ref, cs_ref, rs_ref, dk_s, dv_s, dq_s, di_s):
        j, i = pl.program_id(1), pl.program_id(2)
        rows = pl.ds(pl.multiple_of(i * t, t), t)

        @pl.when(jnp.logical_and(j == 0, i == 0))
        def _():
            dq_s[...] = jnp.zeros_like(dq_s)
            rs_ref[...] = jnp.zeros_like(rs_ref)

        @pl.when(j == 0)
        def _():
            di_s[rows, :] = jnp.sum(do_ref[...] * o_ref[...].astype(f32), axis=1, keepdims=True)

        @pl.when(i == 0)
        def _():
            dk_s[...] = jnp.zeros_like(dk_s)
            dv_s[...] = jnp.zeros_like(dv_s)
            cs_ref[...] = jnp.zeros_like(cs_ref)

        @pl.when(i >= j)
        def _():
            q = q_ref[...]
            k = k_ref[...]
            s = _scores(q, k, fq_ref[0], fk_ref[0], i, j, t)
            p = jnp.exp2(s - lse_ref[0])
            do_b = do_ref[...].astype(bf16)
            dp = lax.dot_general(do_b, v_ref[...], _NT, preferred_element_type=f32)
            ds = p * (dp - di_s[rows, :])
            ds_b = (ds * scale).astype(bf16)
            cs_ref[0] += jnp.sum(ds, axis=0, keepdims=True)
            rs_ref[0, rows, :] += jnp.sum(ds, axis=1, keepdims=True)
            dv_s[...] += lax.dot_general(p.astype(bf16), do_b, _TN, preferred_element_type=f32)
            dk_s[...] += lax.dot_general(ds_b, q, _TN, preferred_element_type=f32)
            dq_s[rows, :] += jnp.dot(ds_b, k, preferred_element_type=f32)

        @pl.when(i == n - 1)
        def _():
            dk_ref[...] = dk_s[...].astype(bf16)
            dv_ref[...] = dv_s[...].astype(bf16)

        @pl.when(jnp.logical_and(j == n - 1, i == n - 1))
        def _():
            dq_ref[...] = dq_s[...].astype(bf16)

    blk = (t, HEAD_DIM)
    at_q = lambda h, j, i: (jnp.maximum(i, j), h)
    col_q = pl.BlockSpec((1, t, 1), lambda h, j, i: (h, jnp.maximum(i, j), 0))
    row_k = pl.BlockSpec((1, 1, t), lambda h, j, i: (h, 0, j))
    return pl.pallas_call(
        body, name=name, grid=(H, n, n),
        in_specs=[pl.BlockSpec(blk, at_q),
                  pl.BlockSpec(blk, lambda h, j, i: (j, H + h)),
                  pl.BlockSpec(blk, lambda h, j, i: (j, 2 * H + h)),
                  pl.BlockSpec(blk, at_q), pl.BlockSpec(blk, at_q), col_q, col_q, row_k],
        out_specs=[pl.BlockSpec((T, HEAD_DIM), lambda h, j, i: (0, h)),
                   pl.BlockSpec(blk, lambda h, j, i: (j, h)),
                   pl.BlockSpec(blk, lambda h, j, i: (j, h)),
                   row_k,
                   pl.BlockSpec((1, T, 1), lambda h, j, i: (h, 0, 0))],
        out_shape=[jax.ShapeDtypeStruct((T, D), bf16), jax.ShapeDtypeStruct((T, D), bf16),
                   jax.ShapeDtypeStruct((T, D), bf16), jax.ShapeDtypeStruct((H, 1, T), f32),
                   jax.ShapeDtypeStruct((H, T, 1), f32)],
        scratch_shapes=[pltpu.VMEM(blk, f32), pltpu.VMEM(blk, f32),
                        pltpu.VMEM((T, HEAD_DIM), f32), pltpu.VMEM((T, 1), f32)],
        compiler_params=_cp(("parallel", "arbitrary", "arbitrary")),
    )(qkv, qkv, qkv, o, do, lse, f_col, f_row)


def mod_fwd(c16, mod_w, mod_b_loc, name):
    L, D, MW = mod_w.shape
    tn = _pick(MW, (512, 256, 128))

    def body(c_ref, w_ref, b_ref, o_ref):
        cv = c_ref[...]
        ca = (cv * jax.nn.sigmoid(cv)).astype(bf16)
        o_ref[...] = jnp.dot(ca, w_ref[...].astype(bf16), preferred_element_type=f32) + b_ref[...]

    return pl.pallas_call(
        body, name=name, grid=(L, MW // tn),
        in_specs=[pl.BlockSpec((16, D), lambda l, j: (0, 0)),
                  pl.BlockSpec((None, D, tn), lambda l, j: (l, 0, j)),
                  pl.BlockSpec((None, 1, tn), lambda l, j: (l, 0, j))],
        out_specs=pl.BlockSpec((None, 16, tn), lambda l, j: (l, 0, j)),
        out_shape=jax.ShapeDtypeStruct((L, 16, MW), f32),
        compiler_params=_cp(("parallel", "parallel")),
    )(c16, mod_w, mod_b_loc)


def mod_w_bwd(c_t, dmod, name):
    D = c_t.shape[0]
    L, _, MW = dmod.shape
    tn = _pick(MW, (512, 256, 128))

    def body(c_ref, d_ref, o_ref):
        cv = c_ref[...]
        ca = (cv * jax.nn.sigmoid(cv)).astype(bf16)
        o_ref[...] = jnp.dot(ca, d_ref[...].astype(bf16), preferred_element_type=f32)

    return pl.pallas_call(
        body, name=name, grid=(L, MW // tn),
        in_specs=[pl.BlockSpec((D, LANES), lambda l, j: (0, 0)),
                  pl.BlockSpec((None, LANES, tn), lambda l, j: (l, 0, j))],
        out_specs=pl.BlockSpec((None, D, tn), lambda l, j: (l, 0, j)),
        out_shape=jax.ShapeDtypeStruct((L, D, MW), f32),
        compiler_params=_cp(("parallel", "parallel")),
    )(c_t, dmod)


def adamw(w, g, m, v, name):
    shape = w.shape
    C = shape[-1] if w.ndim >= 1 else 1
    R = max(w.size // C, 1)
    w2, g2, m2, v2 = (t.reshape(R, C) for t in (w, g, m, v))
    tr = R
    for cand in (2048, 1024, 512, 256, 128, 64, 32, 16, 8):
        if R % cand == 0 and cand * _round_up(C, LANES) <= 256 * 1024:
            tr = cand
            break

    def body(w_ref, g_ref, m_ref, v_ref, d_ref, mo_ref, vo_ref):
        gv = g_ref[...]
        mn = ADAM_B1 * m_ref[...] + (1.0 - ADAM_B1) * gv
        vn = ADAM_B2 * v_ref[...] + (1.0 - ADAM_B2) * (gv * gv)
        m_hat = mn / (1.0 - ADAM_B1 ** ADAM_STEP)
        v_hat = vn / (1.0 - ADAM_B2 ** ADAM_STEP)
        d_ref[...] = -ADAM_LR * (m_hat / (jnp.sqrt(v_hat) + ADAM_EPS) + ADAM_WD * w_ref[...])
        mo_ref[...] = mn
        vo_ref[...] = vn

    spec = pl.BlockSpec((tr, C), lambda i: (i, 0))
    sds = jax.ShapeDtypeStruct((R, C), f32)
    d, mn, vn = pl.pallas_call(
        body, name=name, grid=(R // tr,), in_specs=[spec] * 4, out_specs=[spec] * 3,
        out_shape=[sds, sds, sds], compiler_params=_cp(("parallel",)),
    )(w2, g2, m2, v2)
    return d.reshape(shape), mn.reshape(shape), vn.reshape(shape)


def reduce_scatter(g8, c_idx, tag):
    got = pair_exchange(g8, "rs_pair_exchange_" + tag)
    pair = pair_sum(g8, got, c_idx, "rs_pair_sum_" + tag)
    quad = chip_exchange(pair, "rs_chip_exchange_" + tag)
    return sum_slots(quad, "rs_final_sum_" + tag)


def kernel(x, c, mod_w, mod_b, mix_norm_g, ffn_norm_g, attn_w_in, attn_b_f, attn_w_o, gm_w_in, gm_v_g, gm_w_s, gm_b_s, gm_w_o, ffn_w_in, ffn_conv_w, ffn_conv_b, ffn_w_out, final_g, loss_target, m_mod_w, m_mod_b, m_mix_norm_g, m_ffn_norm_g, m_attn_w_in, m_attn_b_f, m_attn_w_o, m_gm_w_in, m_gm_v_g, m_gm_w_s, m_gm_b_s, m_gm_w_o, m_ffn_w_in, m_ffn_conv_w, m_ffn_conv_b, m_ffn_w_out, m_final_g, v_mod_w, v_mod_b, v_mix_norm_g, v_ffn_norm_g, v_attn_w_in, v_attn_b_f, v_attn_w_o, v_gm_w_in, v_gm_v_g, v_gm_w_s, v_gm_b_s, v_gm_w_o, v_ffn_w_in, v_ffn_conv_w, v_ffn_conv_b, v_ffn_w_out, v_final_g):
    xi, yi, ci = lax.axis_index("x"), lax.axis_index("y"), lax.axis_index("c")
    me = 4 * xi + 2 * yi + ci

    _, T, D = x.shape
    L = mod_w.shape[0]
    MW = mod_w.shape[2]
    NA, _, QW = attn_w_in.shape
    NB = gm_w_in.shape[0]
    H = D // HEAD_DIM
    G = D // GM_GROUP
    DR = attn_w_o.shape[1]
    GW = gm_w_in.shape[2]
    FW = ffn_w_in.shape[2]
    FR = ffn_w_out.shape[1]
    FRP = _round_up(FR, LANES // 2)
    FWP = 2 * FRP
    FP = N_CHIPS * FWP
    DFF2 = N_DEV * FW
    assert 2 * FR == FW and N_DEV * QW == 3 * D + H and N_DEV * GW == 2 * D
    c_idx = ci.reshape(1).astype(jnp.int32)

    def pad_ff(t, axis, blocks):
        ax = axis % t.ndim
        t = t.reshape(t.shape[:ax] + (blocks, FR) + t.shape[ax + 1:])
        pad = [(0, 0)] * t.ndim
        pad[ax + 1] = (0, FRP - FR)
        t = jnp.pad(t, pad)
        return t.reshape(t.shape[:ax] + (blocks * FRP,) + t.shape[ax + 2:])

    def unpad_ff(t, axis, blocks):
        ax = axis % t.ndim
        t = t.reshape(t.shape[:ax] + (blocks, FRP) + t.shape[ax + 1:])
        t = lax.slice_in_dim(t, 0, FR, axis=ax + 1)
        return t.reshape(t.shape[:ax] + (blocks * FR,) + t.shape[ax + 2:])

    x0 = x[0]
    tgt = loss_target[0]

    c_all = all_gather(c, "gather_c").reshape(N_DEV, D)
    cw_loc = pad_ff(ffn_conv_w, 2, 2).reshape(L * CONV_W, FWP)
    conv_w_full = all_gather(cw_loc, "gather_conv_w").transpose(1, 0, 2).reshape(L, CONV_W, 2 * FP)
    vg_full = all_gather(gm_v_g, "gather_vg").transpose(1, 0, 2).reshape(NB, 1, D)
    conv_b_full = pad_ff(ffn_conv_b, 1, 2 * N_DEV).reshape(L, 1, 2 * FP)

    c16 = jnp.pad(c_all, ((0, 16 - N_DEV), (0, 0)))
    mod_b_loc = lax.dynamic_slice_in_dim(mod_b, me * MW, MW, axis=1).reshape(L, 1, MW)
    mod_part = mod_fwd(c16, mod_w, mod_b_loc, "mod_fwd")[:, :N_DEV]
    mod_all = all_gather(mod_part, "gather_mod")
    mod_me = lax.dynamic_index_in_dim(mod_all, me, axis=2, keepdims=False)
    mod_me = mod_me.transpose(1, 0, 2).reshape(L, 6, 1, D)

    w_ai_t = jnp.swapaxes(attn_w_in, 1, 2).astype(bf16)
    w_gi_t = jnp.swapaxes(gm_w_in, 1, 2).astype(bf16)
    w_fi_t = pad_ff(jnp.swapaxes(ffn_w_in, 1, 2).astype(bf16), 1, 2)
    w_ao_l = attn_w_o.astype(bf16)
    w_go_l = gm_w_o.astype(bf16)
    w_fo_l = jnp.pad(ffn_w_out.astype(bf16), ((0, 0), (0, FRP - FR), (0, 0)))

    stash = []
    xc = x0
    for i in range(L):
        sh1, sc1, g1, sh2, sc2, g2 = (mod_me[i, k] for k in range(6))
        jm = i // 2
        st = {"x_in": xc}
        h = norm_mod_fwd(xc, mix_norm_g[i][None], sc1, sh1, "norm_mod_fwd")
        st["h"] = h
        if i % 2 == 0:
            w_in_t = all_gather(w_ai_t[jm], "gather_attn_in").reshape(N_DEV * QW, D)
            w_qkv_t = w_in_t[:3 * D]
            w_f_t = jnp.pad(w_in_t[3 * D:], ((0, LANES - H), (0, 0)))
            w_mo = all_gather(w_ao_l[jm], "gather_mix_out").reshape(D, D)
            qkv = matmul(h, w_qkv_t, name="fox_qkv", tb=True, out_dtype=bf16)
            flog = matmul(h, w_f_t, name="fox_flog", tb=True)
            flog_t = flog[:, :H].T
            b_col = attn_b_f[jm][:, None]
            F = fox_gates_fwd(flog_t, b_col, "fox_gates_fwd")
            f_row, f_col = F[:, None, :], F[:, :, None]
            o, lse = fox_attn_fwd(qkv, f_row, f_col, "fox_attn_fwd")
            x1, y = matmul(o, w_mo, name="mix_out", resid=xc, gvec=g1, emit_acc=True)
            st.update(qkv=qkv, flog_t=flog_t, b_col=b_col, f_row=f_row, f_col=f_col, o=o, lse=lse,
                      w_qkv_t=w_qkv_t, w_f_t=w_f_t, w_mo=w_mo)
        else:
            w_gi_full = all_gather(w_gi_t[jm], "gather_gm_in").reshape(2 * D, D)
            w_mo = all_gather(w_go_l[jm], "gather_mix_out").reshape(D, D)
            z = matmul(h, w_gi_full, name="gm_in", tb=True)
            bs_t = gm_b_s[jm].T
            gated = gm_gate_fwd(z, vg_full[jm], gm_w_s[jm], bs_t, "gm_gate_fwd")
            x1, y = matmul(gated, w_mo, name="mix_out", resid=xc, gvec=g1, emit_acc=True)
            st.update(z=z, bs_t=bs_t, gated=gated, w_gi_full=w_gi_full, w_mo=w_mo)
        st.update(y=y, x1=x1)
        h2 = norm_mod_fwd(x1, ffn_norm_g[i][None], sc2, sh2, "norm_mod_fwd")
        w_fi_full = all_gather(w_fi_t[i], "gather_ffn_in").reshape(2 * FP, D)
        w_fo_full = all_gather(w_fo_l[i], "gather_ffn_out").reshape(FP, D)
        a3 = matmul(h2, w_fi_full, name="ffn_up", tb=True, out_split=True)
        act = conv_glu_fwd(a3, conv_w_full[i], conv_b_full[i], "conv_glu_fwd")
        xc, f_out = matmul(act, w_fo_full, name="ffn_down", resid=x1, gvec=g2, emit_acc=True)
        st.update(h2=h2, a3=a3, act=act, f=f_out, w_fi_full=w_fi_full, w_fo_full=w_fo_full)
        stash.append(st)

    loss_part, dx, d_final_g = loss_head(xc, final_g[None], tgt, "loss_head")
    loss = lax.psum(loss_part[0, 0], AXES)

    d_mod = [None] * L
    d_mix_g = [None] * L
    d_ffn_g = [None] * L
    d_conv_w = [None] * L
    d_conv_b = [None] * L
    g_wfi = [None] * L
    g_wfo = [None] * L
    g_wai = [None] * NA
    g_wao = [None] * NA
    d_bf = [None] * NA
    g_wgi = [None] * NB
    g_wgo = [None] * NB
    d_ws = [None] * NB
    d_bs = [None] * NB
    d_vg = [None] * NB

    for i in reversed(range(L)):
        st = stash[i]
        sh1, sc1, g1, sh2, sc2, g2 = (mod_me[i, k] for k in range(6))
        jm = i // 2
        dy, dg2 = gate_bwd(dx, st["f"], g2, "gate_bwd")
        dact = matmul(dy, st["w_fo_full"], name="ffn_down_dx", tb=True)
        dw_fo = matmul(st["act"], dy, name="ffn_down_dw", ta=True, out_dtype=bf16)
        g_wfo[i] = reduce_scatter(dw_fo.reshape(N_DEV, FRP, D), c_idx, "ffn_out")[:FR]
        da3, dwg, dwu, dbg, dbu = conv_glu_bwd(st["a3"], conv_w_full[i], conv_b_full[i], dact, "conv_glu_bwd")
        d_conv_w[i] = jnp.concatenate([dwg, dwu], axis=1)
        d_conv_b[i] = jnp.concatenate([dbg, dbu], axis=1)
        dw_fi_t = matmul(da3, st["h2"], name="ffn_up_dw", ta=True, a_split=True, out_dtype=bf16)
        g_fi_t = reduce_scatter(dw_fi_t.reshape(N_DEV, FWP, D), c_idx, "ffn_in")
        g_wfi[i] = unpad_ff(g_fi_t, 0, 2).T
        dh2 = matmul(da3, st["w_fi_full"], name="ffn_up_dx", a_split=True)
        dx, dsh2, dsc2, d_ffn_g[i] = norm_mod_bwd(st["x1"], ffn_norm_g[i][None], sc2, dh2, dx, "norm_mod_bwd")
        dy, dg1 = gate_bwd(dx, st["y"], g1, "gate_bwd")
        if i % 2 == 0:
            do = matmul(dy, st["w_mo"], name="mix_out_dx", tb=True)
            dw_mo = matmul(st["o"], dy, name="mix_out_dw", ta=True, out_dtype=bf16)
            g_wao[jm] = reduce_scatter(dw_mo.reshape(N_DEV, DR, D), c_idx, "mix_out")
            dq, dk, dv, cs, rs = fox_attn_bwd(st["qkv"], st["o"], do, st["lse"], st["f_row"], st["f_col"], "fox_attn_bwd")
            dF = rs[:, :, 0] - cs[:, 0, :]
            dflog_t, d_bf[jm] = fox_gates_bwd(st["flog_t"], st["b_col"], dF, "fox_gates_bwd")
            dflog = jnp.pad(dflog_t.T, ((0, 0), (0, LANES - H))).astype(bf16)
            dqkv = jnp.concatenate([dq, dk, dv], axis=1)
            dw_qkv_t = matmul(dqkv, st["h"], name="fox_qkv_dw", ta=True, out_dtype=bf16)
            dw_f_t = matmul(dflog, st["h"], name="fox_flog_dw", ta=True, out_dtype=bf16)
            dw_in_t = jnp.concatenate([dw_qkv_t, dw_f_t[:H]], axis=0)
            g_wai[jm] = reduce_scatter(dw_in_t.reshape(N_DEV, QW, D), c_idx, "attn_in").T
            dh = matmul(dqkv, st["w_qkv_t"], name="fox_qkv_dx")
            dh = matmul(dflog, st["w_f_t"], name="fox_flog_dx", resid=dh)
        else:
            dgated = matmul(dy, st["w_mo"], name="mix_out_dx", tb=True)
            dw_mo = matmul(st["gated"], dy, name="mix_out_dw", ta=True, out_dtype=bf16)
            g_wgo[jm] = reduce_scatter(dw_mo.reshape(N_DEV, DR, D), c_idx, "mix_out")
            dz, d_ws[jm], dbs_t, d_vg[jm] = gm_gate_bwd(st["z"], vg_full[jm], gm_w_s[jm], st["bs_t"], dgated, "gm_gate_bwd")
            d_bs[jm] = dbs_t.T
            dw_gi_t = matmul(dz, st["h"], name="gm_in_dw", ta=True, out_dtype=bf16)
            g_wgi[jm] = reduce_scatter(dw_gi_t.reshape(N_DEV, GW, D), c_idx, "gm_in").T
            dh = matmul(dz, st["w_gi_full"], name="gm_in_dx")
        dx, dsh1, dsc1, d_mix_g[i] = norm_mod_bwd(st["x_in"], mix_norm_g[i][None], sc1, dh, dx, "norm_mod_bwd")
        d_mod[i] = jnp.concatenate([dsh1, dsc1, dg1, dsh2, dsc2, dg2], axis=0)

    grad_x = dx[None]
    grad_attn_w_in, grad_attn_w_o = jnp.stack(g_wai), jnp.stack(g_wao)
    grad_gm_w_in, grad_gm_w_o = jnp.stack(g_wgi), jnp.stack(g_wgo)
    grad_ffn_w_in, grad_ffn_w_out = jnp.stack(g_wfi), jnp.stack(g_wfo)

    def gathered_sum(rows, tag, mult=SUBLANES):
        n = rows.shape[0]
        rows = jnp.pad(rows, ((0, _round_up(n, mult) - n), (0, 0)))
        every = all_gather(rows, "gather_small_grads_" + tag)
        return every, sum_slots(every, "sum_small_grads_" + tag)

    rows_d = jnp.concatenate([jnp.concatenate(d_mod, axis=0), jnp.concatenate(d_mix_g, axis=0),
                              jnp.concatenate(d_ffn_g, axis=0), jnp.concatenate(d_vg, axis=0), d_final_g], axis=0)
    every_d, sum_d = gathered_sum(rows_d, "d")
    r0 = L * 6
    grad_mod_b = sum_d[:r0].reshape(L, 6 * D)
    grad_mix_g, grad_ffn_g = sum_d[r0:r0 + L], sum_d[r0 + L:r0 + 2 * L]
    grad_vg_full = sum_d[r0 + 2 * L:r0 + 2 * L + NB]
    grad_final_g = sum_d[r0 + 2 * L + NB]
    grad_vg = lax.dynamic_slice_in_dim(grad_vg_full, me * DR, DR, axis=1)

    rows_f = jnp.concatenate([jnp.concatenate(d_conv_w, axis=0), jnp.concatenate(d_conv_b, axis=0)], axis=0)
    _, sum_f = gathered_sum(rows_f, "f")
    sum_f = unpad_ff(sum_f, 1, 2 * N_DEV)
    grad_conv_w = lax.dynamic_slice_in_dim(sum_f[:L * CONV_W].reshape(L, CONV_W, DFF2), me * FW, FW, axis=2)
    grad_conv_b = sum_f[L * CONV_W:L * CONV_W + L]

    rows_c = jnp.concatenate([jnp.stack(d_ws).reshape(NB * G * CHUNK, CHUNK), jnp.stack(d_bs).reshape(NB * G, CHUNK),
                              jnp.pad(jnp.stack(d_bf).reshape(NA, H), ((0, 0), (0, LANES - H)))], axis=0)
    _, sum_c = gathered_sum(rows_c, "c", mult=SLOT_ROWS)
    n_ws = NB * G * CHUNK
    grad_ws = sum_c[:n_ws].reshape(NB, G, CHUNK, CHUNK)
    grad_bs = sum_c[n_ws:n_ws + NB * G].reshape(NB, G, CHUNK)
    grad_bf = sum_c[n_ws + NB * G:n_ws + NB * G + NA, :H]

    dmod_all = every_d[:, :r0].reshape(N_DEV, L, 6 * D)
    dmod_loc = lax.dynamic_slice_in_dim(dmod_all, me * MW, MW, axis=2).transpose(1, 0, 2)
    dmod_loc = jnp.pad(dmod_loc, ((0, 0), (0, LANES - N_DEV), (0, 0)))
    c_t = jnp.pad(c_all.T, ((0, 0), (0, LANES - N_DEV)))
    grad_mod_w = mod_w_bwd(c_t, dmod_loc, "mod_w_bwd")

    weights = [mod_w, mod_b, mix_norm_g, ffn_norm_g, attn_w_in, attn_b_f, attn_w_o, gm_w_in, gm_v_g, gm_w_s,
               gm_b_s, gm_w_o, ffn_w_in, ffn_conv_w, ffn_conv_b, ffn_w_out, final_g]
    grads = [grad_mod_w, grad_mod_b, grad_mix_g, grad_ffn_g, grad_attn_w_in, grad_bf, grad_attn_w_o,
             grad_gm_w_in, grad_vg, grad_ws, grad_bs, grad_gm_w_o, grad_ffn_w_in, grad_conv_w, grad_conv_b,
             grad_ffn_w_out, grad_final_g]
    ms = [m_mod_w, m_mod_b, m_mix_norm_g, m_ffn_norm_g, m_attn_w_in, m_attn_b_f, m_attn_w_o, m_gm_w_in, m_gm_v_g,
          m_gm_w_s, m_gm_b_s, m_gm_w_o, m_ffn_w_in, m_ffn_conv_w, m_ffn_conv_b, m_ffn_w_out, m_final_g]
    vs = [v_mod_w, v_mod_b, v_mix_norm_g, v_ffn_norm_g, v_attn_w_in, v_attn_b_f, v_attn_w_o, v_gm_w_in, v_gm_v_g,
          v_gm_w_s, v_gm_b_s, v_gm_w_o, v_ffn_w_in, v_ffn_conv_w, v_ffn_conv_b, v_ffn_w_out, v_final_g]
    deltas, new_ms, new_vs = [], [], []
    for w, g, m_, v_ in zip(weights, grads, ms, vs):
        d_, mn_, vn_ = adamw(w, g, m_, v_, "adamw")
        deltas.append(d_)
        new_ms.append(mn_)
        new_vs.append(vn_)

    return (loss, grad_x, *grads, *deltas, *new_ms, *new_vs)
```

```python
import jax
import jax.numpy as jnp
from jax import lax
from jax.experimental import pallas as pl
from jax.experimental.pallas import tpu as pltpu

f32 = jnp.float32
bf16 = jnp.bfloat16

AXES = ("x", "y", "c")
N_DEV = 8
N_CHIPS = 4
LANES = 128
SUBLANES = 8
HEAD_DIM = 128
CHUNK = 128
GM_GROUP = 128
CONV_W = 3
EPS = 1e-6
NEG = -1e30
VMEM_LIMIT_BYTES = 56 * 1024 * 1024
MAX_FULL_K = 2048
SLOT_ROWS = 512
LOG2E = 1.4426950408889634
ATT_TILE = 512

ADAM_LR = 0.001
ADAM_B1 = 0.9
ADAM_B2 = 0.999
ADAM_EPS = 1e-08
ADAM_WD = 0.01
ADAM_STEP = 10

MESH = pl.DeviceIdType.MESH
ANY = pl.BlockSpec(memory_space=pl.ANY)


def _cp(sem):
    return pltpu.CompilerParams(dimension_semantics=sem, vmem_limit_bytes=VMEM_LIMIT_BYTES)


def _pick(n, prefs):
    for p in prefs:
        if n % p == 0:
            return p
    return n


def _round_up(n, m):
    return (n + m - 1) // m * m


class Exchange:
    def __init__(self, src, out_shape, scratch, phases):
        self.src = src
        self.out_shape = out_shape
        self.scratch = scratch
        self.phases = phases


def gather_exchange(xl):
    def phases(x_ref, out_ref, send_sems, recv_sems, local_sem):
        x, y, c = lax.axis_index("x"), lax.axis_index("y"), lax.axis_index("c")
        me, sibling = (x, y, c), (x, y, 1 - c)
        chips = [(1 - x, y), (x, 1 - y), (1 - x, 1 - y)]

        def slot(px, py, pc):
            return out_ref.at[4 * px + 2 * py + pc]

        def copy(k, block, to, src=None):
            return pltpu.make_async_remote_copy(
                src_ref=slot(*block) if src is None else src, dst_ref=slot(*block),
                send_sem=send_sems.at[k], recv_sem=recv_sems.at[k],
                device_id=to, device_id_type=MESH)

        mine = pltpu.make_async_copy(x_ref, slot(*me), local_sem)
        first = [copy(0, me, sibling, src=x_ref)]
        first += [copy(1 + j, me, (*chip, c), src=x_ref) for j, chip in enumerate(chips)]
        passed = [copy(4 + j, (*chip, c), sibling) for j, chip in enumerate(chips)]

        def start():
            mine.start()
            for cp in first:
                cp.start()

        def hand_on():
            for j, chip in enumerate(chips):
                copy(1 + j, (*chip, c), me).wait_recv()
                passed[j].start()

        def finish():
            copy(0, sibling, me).wait_recv()
            for j, chip in enumerate(chips):
                copy(4 + j, (*chip, 1 - c), me).wait_recv()
            for cp in first + passed:
                cp.wait_send()
            mine.wait()

        return start, hand_on, finish

    return Exchange(xl, jax.ShapeDtypeStruct((N_DEV,) + xl.shape, xl.dtype),
                    [pltpu.SemaphoreType.DMA((7,)), pltpu.SemaphoreType.DMA((7,)),
                     pltpu.SemaphoreType.DMA], phases)


def pair_exchange(g8):
    _, R, W = g8.shape

    def phases(g_ref, out_ref, send_sems, recv_sems):
        x, y, c = lax.axis_index("x"), lax.axis_index("y"), lax.axis_index("c")
        copies = [pltpu.make_async_remote_copy(
            src_ref=g_ref.at[2 * q + (1 - c)], dst_ref=out_ref.at[q],
            send_sem=send_sems.at[q], recv_sem=recv_sems.at[q],
            device_id=(x, y, 1 - c), device_id_type=MESH) for q in range(N_CHIPS)]

        def start():
            for cp in copies:
                cp.start()

        def finish():
            for cp in copies:
                cp.wait()

        return start, None, finish

    return Exchange(g8, jax.ShapeDtypeStruct((N_CHIPS, R, W), g8.dtype),
                    [pltpu.SemaphoreType.DMA((N_CHIPS,)), pltpu.SemaphoreType.DMA((N_CHIPS,))], phases)


def chip_exchange(p4):
    def phases(p_ref, out_ref, send_sems, recv_sems, local_sem):
        x, y, c = lax.axis_index("x"), lax.axis_index("y"), lax.axis_index("c")
        my_q = 2 * x + y
        chips = [(1 - x, y), (x, 1 - y), (1 - x, 1 - y)]
        mine = pltpu.make_async_copy(p_ref.at[my_q], out_ref.at[my_q], local_sem)
        copies = [pltpu.make_async_remote_copy(
            src_ref=p_ref.at[2 * px + py], dst_ref=out_ref.at[my_q],
            send_sem=send_sems.at[k], recv_sem=recv_sems.at[k],
            device_id=(px, py, c), device_id_type=MESH) for k, (px, py) in enumerate(chips)]

        def start():
            mine.start()
            for cp in copies:
                cp.start()

        def finish():
            for k, (px, py) in enumerate(chips):
                pltpu.make_async_remote_copy(
                    src_ref=p_ref.at[my_q], dst_ref=out_ref.at[2 * px + py],
                    send_sem=send_sems.at[k], recv_sem=recv_sems.at[k],
                    device_id=(px, py, c), device_id_type=MESH).wait_recv()
            for cp in copies:
                cp.wait_send()
            mine.wait()

        return start, None, finish

    return Exchange(p4, jax.ShapeDtypeStruct(p4.shape, p4.dtype),
                    [pltpu.SemaphoreType.DMA((3,)), pltpu.SemaphoreType.DMA((3,)),
                     pltpu.SemaphoreType.DMA], phases)


def run_exchange(ex, name):
    def body(src_ref, out_ref, *sems):
        start, hand_on, finish = ex.phases(src_ref, out_ref, *sems)
        start()
        if hand_on is not None:
            hand_on()
        finish()

    return pl.pallas_call(body, name=name, out_shape=ex.out_shape, in_specs=[ANY], out_specs=ANY,
                          scratch_shapes=ex.scratch)(ex.src)


def all_gather(xl, name):
    return run_exchange(gather_exchange(xl), name)


class Riders:
    def __init__(self, exchanges):
        self.exs = list(exchanges or [])
        self.in_specs = [ANY] * len(self.exs)
        self.args = [ex.src for ex in self.exs]
        self.out_specs = [ANY] * len(self.exs)
        self.out_shape = [ex.out_shape for ex in self.exs]
        self.scratch = [s for ex in self.exs for s in ex.scratch]

    def split(self, in_refs, out_refs, scratch_refs):
        n = len(self.exs)
        self.refs = []
        pos = len(scratch_refs) - len(self.scratch)
        for k, ex in enumerate(self.exs):
            sems = scratch_refs[pos:pos + len(ex.scratch)]
            pos += len(ex.scratch)
            self.refs.append((in_refs[len(in_refs) - n + k], out_refs[len(out_refs) - n + k], sems))

    def _parts(self):
        return [ex.phases(src, out, *sems) for ex, (src, out, sems) in zip(self.exs, self.refs)]

    def before(self, step):
        if not self.exs:
            return
        parts = self._parts()

        @pl.when(step == 0)
        def _():
            for start, _, _ in parts:
                start()

    def after(self, step, n_steps):
        if not self.exs:
            return
        parts = self._parts()
        mid = (3 * n_steps) // 4

        if any(h is not None for _, h, _ in parts):
            @pl.when(step == mid)
            def _():
                for _, hand_on, _ in parts:
                    if hand_on is not None:
                        hand_on()

        @pl.when(step == n_steps - 1)
        def _():
            for _, _, finish in parts:
                finish()


def pair_sum(g8, got4, c_idx, name):
    _, R, W = g8.shape
    tr = _pick(R, (512, 256, 128, 64, 32, 16))
    g5 = g8.reshape(N_CHIPS, 2, R, W)

    def body(c_ref, a_ref, b_ref, o_ref):
        o_ref[...] = (a_ref[...].astype(f32) + b_ref[...].astype(f32)).astype(o_ref.dtype)

    grid_spec = pltpu.PrefetchScalarGridSpec(
        num_scalar_prefetch=1, grid=(N_CHIPS, R // tr),
        in_specs=[pl.BlockSpec((None, None, tr, W), lambda q, r, cr: (q, cr[0], r, 0)),
                  pl.BlockSpec((None, tr, W), lambda q, r, cr: (q, r, 0))],
        out_specs=pl.BlockSpec((None, tr, W), lambda q, r, cr: (q, r, 0)))
    return pl.pallas_call(
        body, name=name, grid_spec=grid_spec,
        out_shape=jax.ShapeDtypeStruct((N_CHIPS, R, W), g8.dtype),
        compiler_params=_cp(("parallel", "parallel")),
    )(c_idx, g5, got4)


def sum_slots(xs, name, out_dtype=f32):
    S, R, W = xs.shape
    tr = _pick(R, (512, 256, 128, 64, 32, 16, 8))

    def body(x_ref, o_ref):
        acc = x_ref[0].astype(f32)
        for s in range(1, S):
            acc = acc + x_ref[s].astype(f32)
        o_ref[...] = acc.astype(o_ref.dtype)

    return pl.pallas_call(
        body, name=name, grid=(R // tr,),
        in_specs=[pl.BlockSpec((S, tr, W), lambda r: (0, r, 0))],
        out_specs=pl.BlockSpec((tr, W), lambda r: (r, 0)),
        out_shape=jax.ShapeDtypeStruct((R, W), out_dtype),
        compiler_params=_cp(("parallel",)),
    )(xs)


def matmul(a, b, *, name, ta=False, tb=False, a_split=False, b_split=False, out_split=False,
           out_dtype=f32, resid=None, gvec=None, emit_acc=False, riders=None):
    rd = Riders(riders)
    if a_split:
        rows, cols = a.shape[1], 2 * a.shape[2]
        M, K = (cols, rows) if ta else (rows, cols)
    else:
        M, K = (a.shape[1], a.shape[0]) if ta else a.shape
    if b_split:
        assert not tb
        N = 2 * b.shape[2]
        assert b.shape[1] == K
    else:
        N = b.shape[0] if tb else b.shape[1]
        assert (b.shape[1] if tb else b.shape[0]) == K, (a.shape, b.shape, name)

    m_split = a_split and ta
    k_split = a_split and not ta
    n_split = b_split or out_split
    tm = _pick(M // 2 if m_split else M, (1024, 512, 256, 128, 64, 32, 16, 8))
    tn = _pick(N // 2 if n_split else N, (1024, 512, 256, 128))
    if K <= MAX_FULL_K and not k_split:
        tk = K
    else:
        tk = _pick(K // 2 if k_split else K, (1024, 512, 256, 128))
    nk = K // tk
    n_half = (N // 2) // tn if n_split else 0
    k_half = (K // 2) // tk if k_split else 0
    m_half = (M // 2) // tm if m_split else 0

    if m_split:
        a_spec = pl.BlockSpec((None, tk, tm), lambda i, j, k: (i // m_half, k, i % m_half))
    elif k_split:
        a_spec = pl.BlockSpec((None, tm, tk), lambda i, j, k: (k // k_half, i, k % k_half))
    elif ta:
        a_spec = pl.BlockSpec((tk, tm), lambda i, j, k: (k, i))
    else:
        a_spec = pl.BlockSpec((tm, tk), lambda i, j, k: (i, k))
    if b_split:
        b_spec = pl.BlockSpec((None, tk, tn), lambda i, j, k: (j // n_half, k, j % n_half))
    elif tb:
        b_spec = pl.BlockSpec((tn, tk), lambda i, j, k: (j, k))
    else:
        b_spec = pl.BlockSpec((tk, tn), lambda i, j, k: (k, j))
    if out_split:
        o_spec = pl.BlockSpec((None, tm, tn), lambda i, j, k: (j // n_half, i, j % n_half))
        o_shape = (2, M, N // 2)
    else:
        o_spec = pl.BlockSpec((tm, tn), lambda i, j, k: (i, j))
        o_shape = (M, N)

    in_specs = [a_spec, b_spec]
    args = [a, b]
    if resid is not None:
        in_specs.append(pl.BlockSpec((tm, tn), lambda i, j, k: (i, j)))
        args.append(resid)
    if gvec is not None:
        in_specs.append(pl.BlockSpec((1, tn), lambda i, j, k: (0, j)))
        args.append(gvec)
    out_specs = [o_spec]
    out_shape = [jax.ShapeDtypeStruct(o_shape, out_dtype)]
    if emit_acc:
        out_specs.append(pl.BlockSpec((tm, tn), lambda i, j, k: (i, j)))
        out_shape.append(jax.ShapeDtypeStruct((M, N), bf16))
    dims = (((0 if ta else 1,), (1 if tb else 0,)), ((), ()))
    has_r, has_g = resid is not None, gvec is not None
    n_in, n_out = len(in_specs) + len(rd.exs), len(out_specs) + len(rd.exs)
    grid = (M // tm, N // tn, nk)
    n_steps = grid[0] * grid[1] * grid[2]

    def body(*refs):
        in_refs, out_refs, scratch_refs = refs[:n_in], refs[n_in:n_in + n_out], refs[n_in + n_out:]
        rd.split(in_refs, out_refs, scratch_refs)
        a_ref, b_ref = in_refs[0], in_refs[1]
        pos = 2
        r_ref = g_ref = None
        if has_r:
            r_ref = in_refs[pos]
            pos += 1
        if has_g:
            g_ref = in_refs[pos]
        o_ref = out_refs[0]
        y_ref = out_refs[1] if emit_acc else None
        step = (pl.program_id(0) * grid[1] + pl.program_id(1)) * nk + pl.program_id(2)
        rd.before(step)

        def finish(acc):
            if emit_acc:
                y_ref[...] = acc.astype(bf16)
            if has_g:
                acc = acc * g_ref[...]
            if has_r:
                acc = r_ref[...] + acc
            o_ref[...] = acc.astype(o_ref.dtype)

        part = lax.dot_general(a_ref[...].astype(bf16), b_ref[...].astype(bf16), dims,
                               preferred_element_type=f32)
        if nk == 1:
            finish(part)
        else:
            acc_ref = scratch_refs[0]
            k = pl.program_id(2)

            @pl.when(k == 0)
            def _():
                acc_ref[...] = part

            @pl.when(k > 0)
            def _():
                acc_ref[...] += part

            @pl.when(k == nk - 1)
            def _():
                finish(acc_ref[...])

        rd.after(step, n_steps)

    outs = pl.pallas_call(
        body, name=name, grid=grid,
        in_specs=in_specs + rd.in_specs, out_specs=out_specs + rd.out_specs,
        out_shape=out_shape + rd.out_shape,
        scratch_shapes=([pltpu.VMEM((tm, tn), f32)] if nk > 1 else []) + rd.scratch,
        compiler_params=_cp(("arbitrary",) * 3 if rd.exs else ("parallel", "parallel", "arbitrary")),
    )(*args, *rd.args)
    return outs if (emit_acc or rd.exs) else outs[0]


def _rows(T):
    return _pick(T, (256, 128, 64, 32, 16, 8))


def norm_mod_fwd(x, gn, sc, sh, name):
    T, D = x.shape
    tr = _rows(T)

    def body(x_ref, gn_ref, sc_ref, sh_ref, h_ref):
        xv = x_ref[...]
        r = lax.rsqrt(jnp.mean(xv * xv, axis=-1, keepdims=True) + EPS)
        y = (xv * r) * gn_ref[...]
        h_ref[...] = (y * (1.0 + sc_ref[...]) + sh_ref[...]).astype(bf16)

    vec = pl.BlockSpec((1, D), lambda i: (0, 0))
    row = pl.BlockSpec((tr, D), lambda i: (i, 0))
    return pl.pallas_call(
        body, name=name, grid=(T // tr,), in_specs=[row, vec, vec, vec], out_specs=row,
        out_shape=jax.ShapeDtypeStruct((T, D), bf16), compiler_params=_cp(("parallel",)),
    )(x, gn, sc, sh)


def norm_mod_bwd(x, gn, sc, dh, dx_res, name):
    T, D = x.shape
    tr = _rows(T)

    def body(x_ref, gn_ref, sc_ref, dh_ref, dr_ref, dx_ref, dsh_ref, dsc_ref, dgn_ref):
        @pl.when(pl.program_id(0) == 0)
        def _():
            dsh_ref[...] = jnp.zeros_like(dsh_ref)
            dsc_ref[...] = jnp.zeros_like(dsc_ref)
            dgn_ref[...] = jnp.zeros_like(dgn_ref)

        xv = x_ref[...]
        r = lax.rsqrt(jnp.mean(xv * xv, axis=-1, keepdims=True) + EPS)
        xn = xv * r
        gn_v = gn_ref[...]
        dh_v = dh_ref[...]
        dsh_ref[...] += jnp.sum(dh_v, axis=0, keepdims=True)
        dsc_ref[...] += jnp.sum(dh_v * (xn * gn_v), axis=0, keepdims=True)
        dy = dh_v * (1.0 + sc_ref[...])
        dgn_ref[...] += jnp.sum(dy * xn, axis=0, keepdims=True)
        dxn = dy * gn_v
        dx = r * (dxn - xn * jnp.mean(dxn * xn, axis=-1, keepdims=True))
        dx_ref[...] = dr_ref[...] + dx

    vec = pl.BlockSpec((1, D), lambda i: (0, 0))
    row = pl.BlockSpec((tr, D), lambda i: (i, 0))
    vshape = jax.ShapeDtypeStruct((1, D), f32)
    return pl.pallas_call(
        body, name=name, grid=(T // tr,), in_specs=[row, vec, vec, row, row],
        out_specs=[row, vec, vec, vec],
        out_shape=[jax.ShapeDtypeStruct((T, D), f32), vshape, vshape, vshape],
        compiler_params=_cp(("arbitrary",)),
    )(x, gn, sc, dh, dx_res)


def gate_bwd(dx, y, g, name):
    T, D = dx.shape
    tr = _rows(T)

    def body(dx_ref, y_ref, g_ref, dy_ref, dg_ref):
        @pl.when(pl.program_id(0) == 0)
        def _():
            dg_ref[...] = jnp.zeros_like(dg_ref)

        dxv = dx_ref[...]
        dy_ref[...] = (dxv * g_ref[...]).astype(bf16)
        dg_ref[...] += jnp.sum(dxv * y_ref[...].astype(f32), axis=0, keepdims=True)

    vec = pl.BlockSpec((1, D), lambda i: (0, 0))
    row = pl.BlockSpec((tr, D), lambda i: (i, 0))
    return pl.pallas_call(
        body, name=name, grid=(T // tr,), in_specs=[row, row, vec], out_specs=[row, vec],
        out_shape=[jax.ShapeDtypeStruct((T, D), bf16), jax.ShapeDtypeStruct((1, D), f32)],
        compiler_params=_cp(("arbitrary",)),
    )(dx, y, g)


def loss_head(x, fg, tgt, name):
    T, D = x.shape
    tr = _rows(T)

    def body(x_ref, fg_ref, t_ref, loss_ref, dx_ref, dfg_ref):
        @pl.when(pl.program_id(0) == 0)
        def _():
            loss_ref[...] = jnp.zeros_like(loss_ref)
            dfg_ref[...] = jnp.zeros_like(dfg_ref)

        xv = x_ref[...]
        r = lax.rsqrt(jnp.mean(xv * xv, axis=-1, keepdims=True) + EPS)
        xn = xv * r
        fg_v = fg_ref[...]
        err = xn * fg_v - t_ref[...]
        per_tok = jnp.mean(err * err, axis=-1, keepdims=True)
        loss_ref[...] += 0.5 * jnp.sum(per_tok, axis=0, keepdims=True)
        dy = err * (1.0 / D)
        dfg_ref[...] += jnp.sum(dy * xn, axis=0, keepdims=True)
        dxn = dy * fg_v
        dx_ref[...] = r * (dxn - xn * jnp.mean(dxn * xn, axis=-1, keepdims=True))

    vec = pl.BlockSpec((1, D), lambda i: (0, 0))
    row = pl.BlockSpec((tr, D), lambda i: (i, 0))
    one = pl.BlockSpec((1, 1), lambda i: (0, 0))
    return pl.pallas_call(
        body, name=name, grid=(T // tr,), in_specs=[row, vec, row], out_specs=[one, row, vec],
        out_shape=[jax.ShapeDtypeStruct((1, 1), f32), jax.ShapeDtypeStruct((T, D), f32),
                   jax.ShapeDtypeStruct((1, D), f32)],
        compiler_params=_cp(("arbitrary",)),
    )(x, fg, tgt)


def _conv_tiles(T, FP):
    return _pick(T, (512, 256, 128, 64, 32, 16, 8)), _pick(FP, (512, 256, 128))


def _conv_specs(tr, tc, T, FP):
    nj = FP // tc
    r8 = tr // SUBLANES
    last8 = T // SUBLANES - 1
    main = pl.BlockSpec((2, tr, tc), lambda j, i: (0, i, j))
    prev = pl.BlockSpec((2, SUBLANES, tc), lambda j, i: (0, jnp.maximum(i * r8 - 1, 0), j))
    nxt = pl.BlockSpec((2, SUBLANES, tc), lambda j, i: (0, jnp.minimum((i + 1) * r8, last8), j))
    wg = pl.BlockSpec((CONV_W, tc), lambda j, i: (0, j))
    wu = pl.BlockSpec((CONV_W, tc), lambda j, i: (0, j + nj))
    bg = pl.BlockSpec((1, tc), lambda j, i: (0, j))
    bu = pl.BlockSpec((1, tc), lambda j, i: (0, j + nj))
    return main, prev, nxt, wg, wu, bg, bu


def _causal_taps(av, hp_ref, s, has_prev, row):
    h7 = jnp.where(has_prev, hp_ref[s, 7:8, :], 0.0)
    h6 = jnp.where(has_prev, hp_ref[s, 6:7, :], 0.0)
    m1 = jnp.where(row == 0, h7, pltpu.roll(av, 1, 0))
    m2 = jnp.where(row == 0, h6, jnp.where(row == 1, h7, pltpu.roll(av, 2, 0)))
    return m1, m2


def conv_glu_fwd(a3, conv_w, conv_b, name, riders=None):
    _, T, FP = a3.shape
    tr, tc = _conv_tiles(T, FP)
    main, prev, _, wg, wu, bg, bu = _conv_specs(tr, tc, T, FP)
    rd = Riders(riders)
    n_ex = len(rd.exs)
    ni = T // tr

    def body(*refs):
        in_refs, out_refs, scratch_refs = refs[:6 + n_ex], refs[6 + n_ex:7 + 2 * n_ex], refs[7 + 2 * n_ex:]
        rd.split(in_refs, out_refs, scratch_refs)
        a_ref, hp_ref, wg_ref, wu_ref, bg_ref, bu_ref = in_refs[:6]
        act_ref = out_refs[0]
        step = pl.program_id(0) * ni + pl.program_id(1)
        rd.before(step)
        has_prev = pl.program_id(1) > 0
        row = lax.broadcasted_iota(jnp.int32, (tr, tc), 0)

        def conv(s, w_ref, b_ref):
            av = a_ref[s]
            m1, m2 = _causal_taps(av, hp_ref, s, has_prev, row)
            return w_ref[0:1, :] * m2 + w_ref[1:2, :] * m1 + w_ref[2:3, :] * av + b_ref[...]

        gate = conv(0, wg_ref, bg_ref)
        up = conv(1, wu_ref, bu_ref)
        act_ref[...] = ((gate * jax.nn.sigmoid(gate)) * up).astype(bf16)
        rd.after(step, (FP // tc) * ni)

    outs = pl.pallas_call(
        body, name=name, grid=(FP // tc, ni),
        in_specs=[main, prev, wg, wu, bg, bu] + rd.in_specs,
        out_specs=[pl.BlockSpec((tr, tc), lambda j, i: (i, j))] + rd.out_specs,
        out_shape=[jax.ShapeDtypeStruct((T, FP), bf16)] + rd.out_shape,
        scratch_shapes=rd.scratch,
        compiler_params=_cp(("arbitrary", "arbitrary") if rd.exs else ("parallel", "parallel")),
    )(a3, a3, conv_w, conv_w, conv_b, conv_b, *rd.args)
    return outs if rd.exs else outs[0]


def conv_glu_bwd(a3, conv_w, conv_b, dact, name, riders=None):
    _, T, FP = a3.shape
    tr, tc = _conv_tiles(T, FP)
    ni = T // tr
    main, prev, nxt, wg, wu, bg, bu = _conv_specs(tr, tc, T, FP)
    r8 = tr // SUBLANES
    last8 = T // SUBLANES - 1
    d_main = pl.BlockSpec((tr, tc), lambda j, i: (i, j))
    d_next = pl.BlockSpec((SUBLANES, tc), lambda j, i: (jnp.minimum((i + 1) * r8, last8), j))
    rd = Riders(riders)
    n_ex = len(rd.exs)

    def body(*refs):
        in_refs, out_refs, scratch_refs = refs[:9 + n_ex], refs[9 + n_ex:14 + 2 * n_ex], refs[14 + 2 * n_ex:]
        rd.split(in_refs, out_refs, scratch_refs)
        a_ref, hp_ref, hn_ref, d_ref, dn_ref, wg_ref, wu_ref, bg_ref, bu_ref = in_refs[:9]
        da_ref, dwg_ref, dwu_ref, dbg_ref, dbu_ref = out_refs[:5]
        i = pl.program_id(1)
        step = pl.program_id(0) * ni + i
        rd.before(step)
        has_prev = i > 0
        has_next = i < ni - 1
        row = lax.broadcasted_iota(jnp.int32, (tr, tc), 0)
        row8 = lax.broadcasted_iota(jnp.int32, (SUBLANES, tc), 0)

        @pl.when(i == 0)
        def _():
            dwg_ref[...] = jnp.zeros_like(dwg_ref)
            dwu_ref[...] = jnp.zeros_like(dwu_ref)
            dbg_ref[...] = jnp.zeros_like(dbg_ref)
            dbu_ref[...] = jnp.zeros_like(dbu_ref)

        def prep(s, w_ref, b_ref):
            av = a_ref[s]
            m1, m2 = _causal_taps(av, hp_ref, s, has_prev, row)
            w0, w1, w2, bv = w_ref[0:1, :], w_ref[1:2, :], w_ref[2:3, :], b_ref[...]
            pre = w0 * m2 + w1 * m1 + w2 * av + bv
            an = hn_ref[s]
            l1 = a_ref[s, tr - 1:tr, :]
            l2 = a_ref[s, tr - 2:tr - 1, :]
            n1 = jnp.where(row8 == 0, l1, pltpu.roll(an, 1, 0))
            n2 = jnp.where(row8 == 0, l2, jnp.where(row8 == 1, l1, pltpu.roll(an, 2, 0)))
            pre_n = w0 * n2 + w1 * n1 + w2 * an + bv
            return av, m1, m2, pre, pre_n

        def glu_bwd(gate, up, d):
            sg = jax.nn.sigmoid(gate)
            dgate = d * up * (sg * (1.0 + gate * (1.0 - sg)))
            dup = d * (gate * sg)
            return dgate, dup

        def row_of(v8, r):
            return jnp.sum(jnp.where(row8 == r, v8, 0.0), axis=0, keepdims=True)

        def back(dc, dc_n, w_ref):
            n0, n1 = row_of(dc_n, 0), row_of(dc_n, 1)
            p1 = jnp.where(row == tr - 1, n0, pltpu.roll(dc, tr - 1, 0))
            p2 = jnp.where(row == tr - 1, n1, jnp.where(row == tr - 2, n0, pltpu.roll(dc, tr - 2, 0)))
            return w_ref[2:3, :] * dc + w_ref[1:2, :] * p1 + w_ref[0:1, :] * p2

        def tok_sum(v):
            return jnp.sum(v, axis=0, keepdims=True)

        ag, g1, g2, gate, gate_n = prep(0, wg_ref, bg_ref)
        au, u1, u2, up, up_n = prep(1, wu_ref, bu_ref)
        dcg, dcu = glu_bwd(gate, up, d_ref[...])
        dn = jnp.where(has_next, dn_ref[...], 0.0)
        dcg_n, dcu_n = glu_bwd(gate_n, up_n, dn)
        da_ref[0] = back(dcg, dcg_n, wg_ref).astype(bf16)
        da_ref[1] = back(dcu, dcu_n, wu_ref).astype(bf16)
        dwg_ref[0:1, :] += tok_sum(dcg * g2)
        dwg_ref[1:2, :] += tok_sum(dcg * g1)
        dwg_ref[2:3, :] += tok_sum(dcg * ag)
        dwu_ref[0:1, :] += tok_sum(dcu * u2)
        dwu_ref[1:2, :] += tok_sum(dcu * u1)
        dwu_ref[2:3, :] += tok_sum(dcu * au)
        dbg_ref[...] += tok_sum(dcg)
        dbu_ref[...] += tok_sum(dcu)
        rd.after(step, (FP // tc) * ni)

    w_out = pl.BlockSpec((CONV_W, tc), lambda j, i: (0, j))
    b_out = pl.BlockSpec((1, tc), lambda j, i: (0, j))
    return pl.pallas_call(
        body, name=name, grid=(FP // tc, ni),
        in_specs=[main, prev, nxt, d_main, d_next, wg, wu, bg, bu] + rd.in_specs,
        out_specs=[main, w_out, w_out, b_out, b_out] + rd.out_specs,
        out_shape=[jax.ShapeDtypeStruct((2, T, FP), bf16),
                   jax.ShapeDtypeStruct((CONV_W, FP), f32), jax.ShapeDtypeStruct((CONV_W, FP), f32),
                   jax.ShapeDtypeStruct((1, FP), f32), jax.ShapeDtypeStruct((1, FP), f32)] + rd.out_shape,
        scratch_shapes=rd.scratch,
        compiler_params=_cp(("arbitrary", "arbitrary") if rd.exs else ("parallel", "arbitrary")),
    )(a3, a3, a3, dact, dact, conv_w, conv_w, conv_b, conv_b, *rd.args)


_GELU_C = 0.7978845608028654
_GELU_A = 0.044715


def _gelu(x):
    return 0.5 * x * (1.0 + jnp.tanh(_GELU_C * (x + _GELU_A * (x * x * x))))


def _gelu_and_grad(x):
    t = jnp.tanh(_GELU_C * (x + _GELU_A * (x * x * x)))
    g = 0.5 * x * (1.0 + t)
    dg = 0.5 * (1.0 + t) + 0.5 * x * (1.0 - t * t) * (_GELU_C * (1.0 + 3.0 * _GELU_A * (x * x)))
    return g, dg


def _tril_bf16(w):
    r = lax.broadcasted_iota(jnp.int32, w.shape, 0)
    c = lax.broadcasted_iota(jnp.int32, w.shape, 1)
    return jnp.where(r >= c, w, 0.0).astype(bf16)


def gm_gate_fwd(z, vg, ws, bs_t, name):
    T, D2 = z.shape
    D = D2 // 2
    G = D // GM_GROUP
    tr = _pick(T, (256, 128))
    nc = tr // CHUNK

    def body(z_ref, vg_ref, ws_ref, bs_ref, o_ref):
        u = _gelu(z_ref[:, :D])
        v = _gelu(z_ref[:, D:])
        rv = lax.rsqrt(jnp.mean(v * v, axis=-1, keepdims=True) + EPS)
        vn = ((v * rv) * vg_ref[...]).astype(bf16)
        for g in range(G):
            wg = _tril_bf16(ws_ref[g])
            bg = bs_ref[:, g:g + 1]
            cs = slice(g * GM_GROUP, (g + 1) * GM_GROUP)
            for c in range(nc):
                rs = slice(c * CHUNK, (c + 1) * CHUNK)
                sv = jnp.dot(wg, vn[rs, cs], preferred_element_type=f32) + bg
                o_ref[rs, cs] = (u[rs, cs] * sv).astype(bf16)

    return pl.pallas_call(
        body, name=name, grid=(T // tr,),
        in_specs=[pl.BlockSpec((tr, D2), lambda i: (i, 0)),
                  pl.BlockSpec((1, D), lambda i: (0, 0)),
                  pl.BlockSpec((G, CHUNK, CHUNK), lambda i: (0, 0, 0)),
                  pl.BlockSpec((CHUNK, G), lambda i: (0, 0))],
        out_specs=pl.BlockSpec((tr, D), lambda i: (i, 0)),
        out_shape=jax.ShapeDtypeStruct((T, D), bf16),
        compiler_params=_cp(("parallel",)),
    )(z, vg, ws, bs_t)


def gm_gate_bwd(z, vg, ws, bs_t, dgated, name):
    T, D2 = z.shape
    D = D2 // 2
    G = D // GM_GROUP
    tr = _pick(T, (256, 128))
    nc = tr // CHUNK

    def body(z_ref, vg_ref, ws_ref, bs_ref, dg_ref, dz_ref, dws_ref, dbs_ref, dvg_ref,
             du_s, dvn_s):
        @pl.when(pl.program_id(0) == 0)
        def _():
            dws_ref[...] = jnp.zeros_like(dws_ref)
            dbs_ref[...] = jnp.zeros_like(dbs_ref)
            dvg_ref[...] = jnp.zeros_like(dvg_ref)

        u, du_dz = _gelu_and_grad(z_ref[:, :D])
        v, dv_dz = _gelu_and_grad(z_ref[:, D:])
        rv = lax.rsqrt(jnp.mean(v * v, axis=-1, keepdims=True) + EPS)
        vhat = v * rv
        vg_v = vg_ref[...]
        vn = (vhat * vg_v).astype(bf16)
        rr = lax.broadcasted_iota(jnp.int32, (CHUNK, CHUNK), 0)
        cc = lax.broadcasted_iota(jnp.int32, (CHUNK, CHUNK), 1)
        for g in range(G):
            wg = _tril_bf16(ws_ref[g])
            bg = bs_ref[:, g:g + 1]
            cs = slice(g * GM_GROUP, (g + 1) * GM_GROUP)
            dw_acc = jnp.zeros((CHUNK, CHUNK), f32)
            db_acc = jnp.zeros((CHUNK, 1), f32)
            for c in range(nc):
                rs = slice(c * CHUNK, (c + 1) * CHUNK)
                vb = vn[rs, cs]
                sv = jnp.dot(wg, vb, preferred_element_type=f32) + bg
                dgb = dg_ref[rs, cs]
                du_s[rs, cs] = dgb * sv
                dsv = dgb * u[rs, cs]
                dsv_b = dsv.astype(bf16)
                dw_acc += lax.dot_general(dsv_b, vb, (((1,), (1,)), ((), ())),
                                          preferred_element_type=f32)
                db_acc += jnp.sum(dsv, axis=1, keepdims=True)
                dvn_s[rs, cs] = lax.dot_general(wg, dsv_b, (((0,), (0,)), ((), ())),
                                                preferred_element_type=f32)
            dws_ref[g] += jnp.where(rr >= cc, dw_acc, 0.0)
            dbs_ref[:, g:g + 1] += db_acc
        dz_ref[:, :D] = (du_s[...] * du_dz).astype(bf16)
        dvn = dvn_s[...]
        dvg_ref[...] += jnp.sum(dvn * vhat, axis=0, keepdims=True)
        dvh = dvn * vg_v
        dv = rv * (dvh - vhat * jnp.mean(dvh * vhat, axis=-1, keepdims=True))
        dz_ref[:, D:] = (dv * dv_dz).astype(bf16)

    return pl.pallas_call(
        body, name=name, grid=(T // tr,),
        in_specs=[pl.BlockSpec((tr, D2), lambda i: (i, 0)),
                  pl.BlockSpec((1, D), lambda i: (0, 0)),
                  pl.BlockSpec((G, CHUNK, CHUNK), lambda i: (0, 0, 0)),
                  pl.BlockSpec((CHUNK, G), lambda i: (0, 0)),
                  pl.BlockSpec((tr, D), lambda i: (i, 0))],
        out_specs=[pl.BlockSpec((tr, D2), lambda i: (i, 0)),
                   pl.BlockSpec((G, CHUNK, CHUNK), lambda i: (0, 0, 0)),
                   pl.BlockSpec((CHUNK, G), lambda i: (0, 0)),
                   pl.BlockSpec((1, D), lambda i: (0, 0))],
        out_shape=[jax.ShapeDtypeStruct((T, D2), bf16),
                   jax.ShapeDtypeStruct((G, CHUNK, CHUNK), f32),
                   jax.ShapeDtypeStruct((CHUNK, G), f32),
                   jax.ShapeDtypeStruct((1, D), f32)],
        scratch_shapes=[pltpu.VMEM((tr, D), f32), pltpu.VMEM((tr, D), f32)],
        compiler_params=_cp(("arbitrary",)),
    )(z, vg, ws, bs_t, dgated)


def fox_gates_fwd(flog_t, b_col, name):
    H, T = flog_t.shape

    def body(fl_ref, b_ref, o_ref):
        xv = fl_ref[...] + b_ref[...]
        lf = jnp.minimum(xv, 0.0) - jnp.log1p(jnp.exp(-jnp.abs(xv)))
        lane = lax.broadcasted_iota(jnp.int32, (H, T), 1)
        s = 1
        while s < T:
            lf = lf + jnp.where(lane >= s, pltpu.roll(lf, s, 1), 0.0)
            s *= 2
        o_ref[...] = lf * LOG2E

    return pl.pallas_call(
        body, name=name, out_shape=jax.ShapeDtypeStruct((H, T), f32),
        compiler_params=pltpu.CompilerParams(vmem_limit_bytes=VMEM_LIMIT_BYTES),
    )(flog_t, b_col)


def fox_gates_bwd(flog_t, b_col, dF, name):
    H, T = flog_t.shape

    def body(fl_ref, b_ref, d_ref, o_ref, db_ref):
        xv = fl_ref[...] + b_ref[...]
        g = d_ref[...]
        lane = lax.broadcasted_iota(jnp.int32, (H, T), 1)
        s = 1
        while s < T:
            g = g + jnp.where(lane < T - s, pltpu.roll(g, T - s, 1), 0.0)
            s *= 2
        dfl = g * jax.nn.sigmoid(-xv)
        o_ref[...] = dfl
        db_ref[...] = jnp.sum(dfl, axis=1, keepdims=True)

    return pl.pallas_call(
        body, name=name,
        out_shape=[jax.ShapeDtypeStruct((H, T), f32), jax.ShapeDtypeStruct((H, 1), f32)],
        compiler_params=pltpu.CompilerParams(vmem_limit_bytes=VMEM_LIMIT_BYTES),
    )(flog_t, b_col, dF)


_NT = (((1,), (1,)), ((), ()))
_TN = (((0,), (0,)), ((), ()))


def _scores(q, k, fq, fk, i, j, t):
    s = lax.dot_general(q, k, _NT, preferred_element_type=f32) * (HEAD_DIM ** -0.5 * LOG2E)
    s = s + fq - fk
    rows = i * t + lax.broadcasted_iota(jnp.int32, (t, t), 0)
    cols = j * t + lax.broadcasted_iota(jnp.int32, (t, t), 1)
    return jnp.where(cols <= rows, s, NEG)


def fox_attn_fwd(qkv, f_row, f_col, name, riders=None):
    T, D3 = qkv.shape
    D = D3 // 3
    H = D // HEAD_DIM
    tq = _pick(T, (ATT_TILE, 256, 128))
    nq = T // tq
    rd = Riders(riders)

    def body(*refs):
        in_refs, out_refs, scratch_refs = refs[:5 + len(rd.exs)], refs[5 + len(rd.exs):7 + 2 * len(rd.exs)], refs[7 + 2 * len(rd.exs):]
        rd.split(in_refs, out_refs, scratch_refs)
        q_ref, k_ref, v_ref, fq_ref, fk_ref = in_refs[:5]
        o_ref, lse_ref = out_refs[:2]
        m_s, l_s, acc_s = scratch_refs[:3]
        i, j = pl.program_id(1), pl.program_id(2)
        step = (pl.program_id(0) * nq + i) * nq + j
        rd.before(step)

        @pl.when(j == 0)
        def _():
            m_s[...] = jnp.full_like(m_s, NEG)
            l_s[...] = jnp.zeros_like(l_s)
            acc_s[...] = jnp.zeros_like(acc_s)

        @pl.when(j <= i)
        def _():
            s = _scores(q_ref[...], k_ref[...], fq_ref[0], fk_ref[0], i, j, tq)
            m_prev = m_s[...]
            m_new = jnp.maximum(m_prev, jnp.max(s, axis=1, keepdims=True))
            alpha = jnp.exp2(m_prev - m_new)
            p = jnp.exp2(s - m_new)
            l_s[...] = alpha * l_s[...] + jnp.sum(p, axis=1, keepdims=True)
            acc_s[...] = alpha * acc_s[...] + jnp.dot(p.astype(bf16), v_ref[...],
                                                      preferred_element_type=f32)
            m_s[...] = m_new

        @pl.when(j == nq - 1)
        def _():
            o_ref[...] = (acc_s[...] / l_s[...]).astype(bf16)
            lse_ref[0] = m_s[...] + jnp.log2(l_s[...])

        rd.after(step, H * nq * nq)

    blk = (tq, HEAD_DIM)
    return pl.pallas_call(
        body, name=name, grid=(H, nq, nq),
        in_specs=[pl.BlockSpec(blk, lambda h, i, j: (i, h)),
                  pl.BlockSpec(blk, lambda h, i, j: (jnp.minimum(j, i), H + h)),
                  pl.BlockSpec(blk, lambda h, i, j: (jnp.minimum(j, i), 2 * H + h)),
                  pl.BlockSpec((1, tq, 1), lambda h, i, j: (h, i, 0)),
                  pl.BlockSpec((1, 1, tq), lambda h, i, j: (h, 0, jnp.minimum(j, i)))] + rd.in_specs,
        out_specs=[pl.BlockSpec(blk, lambda h, i, j: (i, h)),
                   pl.BlockSpec((1, tq, 1), lambda h, i, j: (h, i, 0))] + rd.out_specs,
        out_shape=[jax.ShapeDtypeStruct((T, D), bf16), jax.ShapeDtypeStruct((H, T, 1), f32)] + rd.out_shape,
        scratch_shapes=[pltpu.VMEM((tq, 1), f32), pltpu.VMEM((tq, 1), f32),
                        pltpu.VMEM((tq, HEAD_DIM), f32)] + rd.scratch,
        compiler_params=_cp(("arbitrary",) * 3 if rd.exs else ("parallel", "parallel", "arbitrary")),
    )(qkv, qkv, qkv, f_col, f_row, *rd.args)


def fox_attn_bwd(qkv, o, do, lse, f_row, f_col, name):
    T, D3 = qkv.shape
    D = D3 // 3
    H = D // HEAD_DIM
    t = _pick(T, (ATT_TILE, 256, 128))
    n = T // t
    scale = HEAD_DIM ** -0.5

    def body(q_ref, k_ref, v_ref, o_ref, do_ref, lse_ref, fq_ref, fk_ref,
             dq_ref, dk_ref, dv_ref, cs_ref, rs_ref, dk_s, dv_s, dq_s, di_s):
        j, i = pl.program_id(1), pl.program_id(2)
        rows = pl.ds(pl.multiple_of(i * t, t), t)

        @pl.when(jnp.logical_and(j == 0, i == 0))
        def _():
            dq_s[...] = jnp.zeros_like(dq_s)
            rs_ref[...] = jnp.zeros_like(rs_ref)

        @pl.when(j == 0)
        def _():
            di_s[rows, :] = jnp.sum(do_ref[...] * o_ref[...].astype(f32), axis=1, keepdims=True)

        @pl.when(i == 0)
        def _():
            dk_s[...] = jnp.zeros_like(dk_s)
            dv_s[...] = jnp.zeros_like(dv_s)
            cs_ref[...] = jnp.zeros_like(cs_ref)

        @pl.when(i >= j)
        def _():
            q = q_ref[...]
            k = k_ref[...]
            s = _scores(q, k, fq_ref[0], fk_ref[0], i, j, t)
            p = jnp.exp2(s - lse_ref[0])
            do_b = do_ref[...].astype(bf16)
            dp = lax.dot_general(do_b, v_ref[...], _NT, preferred_element_type=f32)
            ds = p * (dp - di_s[rows, :])
            ds_b = (ds * scale).astype(bf16)
            cs_ref[0] += jnp.sum(ds, axis=0, keepdims=True)
            rs_ref[0, rows, :] += jnp.sum(ds, axis=1, keepdims=True)
            dv_s[...] += lax.dot_general(p.astype(bf16), do_b, _TN, preferred_element_type=f32)
            dk_s[...] += lax.dot_general(ds_b, q, _TN, preferred_element_type=f32)
            dq_s[rows, :] += jnp.dot(ds_b, k, preferred_element_type=f32)

        @pl.when(i == n - 1)
        def _():
            dk_ref[...] = dk_s[...].astype(bf16)
            dv_ref[...] = dv_s[...].astype(bf16)

        @pl.when(jnp.logical_and(j == n - 1, i == n - 1))
        def _():
            dq_ref[...] = dq_s[...].astype(bf16)

    blk = (t, HEAD_DIM)
    at_q = lambda h, j, i: (jnp.maximum(i, j), h)
    col_q = pl.BlockSpec((1, t, 1), lambda h, j, i: (h, jnp.maximum(i, j), 0))
    row_k = pl.BlockSpec((1, 1, t), lambda h, j, i: (h, 0, j))
    return pl.pallas_call(
        body, name=name, grid=(H, n, n),
        in_specs=[pl.BlockSpec(blk, at_q),
                  pl.BlockSpec(blk, lambda h, j, i: (j, H + h)),
                  pl.BlockSpec(blk, lambda h, j, i: (j, 2 * H + h)),
                  pl.BlockSpec(blk, at_q), pl.BlockSpec(blk, at_q), col_q, col_q, row_k],
        out_specs=[pl.BlockSpec((T, HEAD_DIM), lambda h, j, i: (0, h)),
                   pl.BlockSpec(blk, lambda h, j, i: (j, h)),
                   pl.BlockSpec(blk, lambda h, j, i: (j, h)),
                   row_k,
                   pl.BlockSpec((1, T, 1), lambda h, j, i: (h, 0, 0))],
        out_shape=[jax.ShapeDtypeStruct((T, D), bf16), jax.ShapeDtypeStruct((T, D), bf16),
                   jax.ShapeDtypeStruct((T, D), bf16), jax.ShapeDtypeStruct((H, 1, T), f32),
                   jax.ShapeDtypeStruct((H, T, 1), f32)],
        scratch_shapes=[pltpu.VMEM(blk, f32), pltpu.VMEM(blk, f32),
                        pltpu.VMEM((T, HEAD_DIM), f32), pltpu.VMEM((T, 1), f32)],
        compiler_params=_cp(("parallel", "arbitrary", "arbitrary")),
    )(qkv, qkv, qkv, o, do, lse, f_col, f_row)


def mod_fwd(c16, mod_w, mod_b_loc, name):
    L, D, MW = mod_w.shape
    tn = _pick(MW, (512, 256, 128))

    def body(c_ref, w_ref, b_ref, o_ref):
        cv = c_ref[...]
        ca = (cv * jax.nn.sigmoid(cv)).astype(bf16)
        o_ref[...] = jnp.dot(ca, w_ref[...].astype(bf16), preferred_element_type=f32) + b_ref[...]

    return pl.pallas_call(
        body, name=name, grid=(L, MW // tn),
        in_specs=[pl.BlockSpec((16, D), lambda l, j: (0, 0)),
                  pl.BlockSpec((None, D, tn), lambda l, j: (l, 0, j)),
                  pl.BlockSpec((None, 1, tn), lambda l, j: (l, 0, j))],
        out_specs=pl.BlockSpec((None, 16, tn), lambda l, j: (l, 0, j)),
        out_shape=jax.ShapeDtypeStruct((L, 16, MW), f32),
        compiler_params=_cp(("parallel", "parallel")),
    )(c16, mod_w, mod_b_loc)


def mod_w_bwd(c_t, dmod, name):
    D = c_t.shape[0]
    L, _, MW = dmod.shape
    tn = _pick(MW, (512, 256, 128))

    def body(c_ref, d_ref, o_ref):
        cv = c_ref[...]
        ca = (cv * jax.nn.sigmoid(cv)).astype(bf16)
        o_ref[...] = jnp.dot(ca, d_ref[...].astype(bf16), preferred_element_type=f32)

    return pl.pallas_call(
        body, name=name, grid=(L, MW // tn),
        in_specs=[pl.BlockSpec((D, LANES), lambda l, j: (0, 0)),
                  pl.BlockSpec((None, LANES, tn), lambda l, j: (l, 0, j))],
        out_specs=pl.BlockSpec((None, D, tn), lambda l, j: (l, 0, j)),
        out_shape=jax.ShapeDtypeStruct((L, D, MW), f32),
        compiler_params=_cp(("parallel", "parallel")),
    )(c_t, dmod)


def adamw(w, g, m, v, name):
    shape = w.shape
    C = shape[-1] if w.ndim >= 1 else 1
    R = max(w.size // C, 1)
    w2, g2, m2, v2 = (t.reshape(R, C) for t in (w, g, m, v))
    tr = R
    for cand in (2048, 1024, 512, 256, 128, 64, 32, 16, 8):
        if R % cand == 0 and cand * _round_up(C, LANES) <= 256 * 1024:
            tr = cand
            break

    def body(w_ref, g_ref, m_ref, v_ref, d_ref, mo_ref, vo_ref):
        gv = g_ref[...]
        mn = ADAM_B1 * m_ref[...] + (1.0 - ADAM_B1) * gv
        vn = ADAM_B2 * v_ref[...] + (1.0 - ADAM_B2) * (gv * gv)
        m_hat = mn / (1.0 - ADAM_B1 ** ADAM_STEP)
        v_hat = vn / (1.0 - ADAM_B2 ** ADAM_STEP)
        d_ref[...] = -ADAM_LR * (m_hat / (jnp.sqrt(v_hat) + ADAM_EPS) + ADAM_WD * w_ref[...])
        mo_ref[...] = mn
        vo_ref[...] = vn

    spec = pl.BlockSpec((tr, C), lambda i: (i, 0))
    sds = jax.ShapeDtypeStruct((R, C), f32)
    d, mn, vn = pl.pallas_call(
        body, name=name, grid=(R // tr,), in_specs=[spec] * 4, out_specs=[spec] * 3,
        out_shape=[sds, sds, sds], compiler_params=_cp(("parallel",)),
    )(w2, g2, m2, v2)
    return d.reshape(shape), mn.reshape(shape), vn.reshape(shape)


def reduce_scatter_tail(pair, tag):
    quad = run_exchange(chip_exchange(pair), "rs_chip_exchange_" + tag)
    return sum_slots(quad, "rs_final_sum_" + tag)


def kernel(x, c, mod_w, mod_b, mix_norm_g, ffn_norm_g, attn_w_in, attn_b_f, attn_w_o, gm_w_in, gm_v_g, gm_w_s, gm_b_s, gm_w_o, ffn_w_in, ffn_conv_w, ffn_conv_b, ffn_w_out, final_g, loss_target, m_mod_w, m_mod_b, m_mix_norm_g, m_ffn_norm_g, m_attn_w_in, m_attn_b_f, m_attn_w_o, m_gm_w_in, m_gm_v_g, m_gm_w_s, m_gm_b_s, m_gm_w_o, m_ffn_w_in, m_ffn_conv_w, m_ffn_conv_b, m_ffn_w_out, m_final_g, v_mod_w, v_mod_b, v_mix_norm_g, v_ffn_norm_g, v_attn_w_in, v_attn_b_f, v_attn_w_o, v_gm_w_in, v_gm_v_g, v_gm_w_s, v_gm_b_s, v_gm_w_o, v_ffn_w_in, v_ffn_conv_w, v_ffn_conv_b, v_ffn_w_out, v_final_g):
    xi, yi, ci = lax.axis_index("x"), lax.axis_index("y"), lax.axis_index("c")
    me = 4 * xi + 2 * yi + ci

    _, T, D = x.shape
    L = mod_w.shape[0]
    MW = mod_w.shape[2]
    NA, _, QW = attn_w_in.shape
    NB = gm_w_in.shape[0]
    H = D // HEAD_DIM
    G = D // GM_GROUP
    DR = attn_w_o.shape[1]
    GW = gm_w_in.shape[2]
    FW = ffn_w_in.shape[2]
    FR = ffn_w_out.shape[1]
    FRP = _round_up(FR, LANES // 2)
    FWP = 2 * FRP
    FP = N_CHIPS * FWP
    DFF2 = N_DEV * FW
    assert 2 * FR == FW and N_DEV * QW == 3 * D + H and N_DEV * GW == 2 * D
    c_idx = ci.reshape(1).astype(jnp.int32)

    def pad_ff(t, axis, blocks):
        ax = axis % t.ndim
        t = t.reshape(t.shape[:ax] + (blocks, FR) + t.shape[ax + 1:])
        pad = [(0, 0)] * t.ndim
        pad[ax + 1] = (0, FRP - FR)
        t = jnp.pad(t, pad)
        return t.reshape(t.shape[:ax] + (blocks * FRP,) + t.shape[ax + 2:])

    def unpad_ff(t, axis, blocks):
        ax = axis % t.ndim
        t = t.reshape(t.shape[:ax] + (blocks, FRP) + t.shape[ax + 1:])
        t = lax.slice_in_dim(t, 0, FR, axis=ax + 1)
        return t.reshape(t.shape[:ax] + (blocks * FR,) + t.shape[ax + 2:])

    x0 = x[0]
    tgt = loss_target[0]

    c_all = all_gather(c, "gather_c").reshape(N_DEV, D)
    cw_loc = pad_ff(ffn_conv_w, 2, 2).reshape(L * CONV_W, FWP)
    conv_w_full = all_gather(cw_loc, "gather_conv_w").transpose(1, 0, 2).reshape(L, CONV_W, 2 * FP)
    vg_full = all_gather(gm_v_g, "gather_vg").transpose(1, 0, 2).reshape(NB, 1, D)
    conv_b_full = pad_ff(ffn_conv_b, 1, 2 * N_DEV).reshape(L, 1, 2 * FP)

    c16 = jnp.pad(c_all, ((0, 16 - N_DEV), (0, 0)))
    mod_b_loc = lax.dynamic_slice_in_dim(mod_b, me * MW, MW, axis=1).reshape(L, 1, MW)
    mod_part = mod_fwd(c16, mod_w, mod_b_loc, "mod_fwd")[:, :N_DEV]
    mod_all = all_gather(mod_part, "gather_mod")
    mod_me = lax.dynamic_index_in_dim(mod_all, me, axis=2, keepdims=False)
    mod_me = mod_me.transpose(1, 0, 2).reshape(L, 6, 1, D)

    w_ai_t = jnp.swapaxes(attn_w_in, 1, 2).astype(bf16)
    w_gi_t = jnp.swapaxes(gm_w_in, 1, 2).astype(bf16)
    w_fi_t = pad_ff(jnp.swapaxes(ffn_w_in, 1, 2).astype(bf16), 1, 2)
    w_ao_l = attn_w_o.astype(bf16)
    w_go_l = gm_w_o.astype(bf16)
    w_fo_l = jnp.pad(ffn_w_out.astype(bf16), ((0, 0), (0, FRP - FR), (0, 0)))

    stash = []
    ffn_w = {}
    ahead_w = {}
    xc = x0
    for i in range(L):
        sh1, sc1, g1, sh2, sc2, g2 = (mod_me[i, k] for k in range(6))
        jm = i // 2
        st = {"x_in": xc}
        h = norm_mod_fwd(xc, mix_norm_g[i][None], sc1, sh1, "norm_mod_fwd")
        st["h"] = h
        mixer_w = ahead_w.pop(i, None)
        if mixer_w is None:
            mixer_w = {"mix_in": all_gather((w_ai_t if i % 2 == 0 else w_gi_t)[jm], "gather_mix_in"),
                       "mix_out": all_gather((w_ao_l if i % 2 == 0 else w_go_l)[jm], "gather_mix_out")}
        w_mo = mixer_w["mix_out"].reshape(D, D)
        if i % 2 == 0:
            w_in_t = mixer_w["mix_in"].reshape(N_DEV * QW, D)
            w_qkv_t = w_in_t[:3 * D]
            w_f_t = jnp.pad(w_in_t[3 * D:], ((0, LANES - H), (0, 0)))
            qkv = matmul(h, w_qkv_t, name="fox_qkv", tb=True, out_dtype=bf16)
            flog = matmul(h, w_f_t, name="fox_flog", tb=True)
            flog_t = flog[:, :H].T
            b_col = attn_b_f[jm][:, None]
            F = fox_gates_fwd(flog_t, b_col, "fox_gates_fwd")
            f_row, f_col = F[:, None, :], F[:, :, None]
            o, lse, g_fi, g_fo = fox_attn_fwd(qkv, f_row, f_col, "fox_attn_fwd",
                                              riders=[gather_exchange(w_fi_t[i]), gather_exchange(w_fo_l[i])])
            ffn_w[i] = (g_fi.reshape(2 * FP, D), g_fo.reshape(FP, D))
            x1, y = matmul(o, w_mo, name="mix_out", resid=xc, gvec=g1, emit_acc=True)
            st.update(qkv=qkv, flog_t=flog_t, b_col=b_col, f_row=f_row, f_col=f_col, o=o, lse=lse,
                      w_qkv_t=w_qkv_t, w_f_t=w_f_t, w_mo=w_mo)
        else:
            w_gi_full = mixer_w["mix_in"].reshape(2 * D, D)
            z = matmul(h, w_gi_full, name="gm_in", tb=True)
            bs_t = gm_b_s[jm].T
            gated = gm_gate_fwd(z, vg_full[jm], gm_w_s[jm], bs_t, "gm_gate_fwd")
            x1, y = matmul(gated, w_mo, name="mix_out", resid=xc, gvec=g1, emit_acc=True)
            st.update(z=z, bs_t=bs_t, gated=gated, w_gi_full=w_gi_full, w_mo=w_mo)
        st.update(y=y, x1=x1)
        h2 = norm_mod_fwd(x1, ffn_norm_g[i][None], sc2, sh2, "norm_mod_fwd")
        if i not in ffn_w:
            ffn_w[i] = (all_gather(w_fi_t[i], "gather_ffn_in").reshape(2 * FP, D),
                        all_gather(w_fo_l[i], "gather_ffn_out").reshape(FP, D))
        w_fi_full, w_fo_full = ffn_w[i]
        up_keys, conv_keys, down_keys = [], [], []
        nxt = i + 1
        if nxt < L and nxt % 2 == 1:
            up_keys = [("ffn_in", w_fi_t[nxt])]
            conv_keys = [("mix_in", w_gi_t[nxt // 2])]
            down_keys = [("ffn_out", w_fo_l[nxt]), ("mix_out", w_go_l[nxt // 2])]
        elif nxt < L:
            up_keys = [("mix_in", w_ai_t[nxt // 2])]
            down_keys = [("mix_out", w_ao_l[nxt // 2])]
        got = {}

        def hosting(keys, n_own, fn, *args, **kw):
            outs = fn(*args, riders=[gather_exchange(w) for _, w in keys], **kw)
            outs = list(outs) if isinstance(outs, (list, tuple)) else [outs]
            got.update({k: r for (k, _), r in zip(keys, outs[n_own:])})
            return outs[:n_own]

        a3, = hosting(up_keys, 1, matmul, h2, w_fi_full, name="ffn_up", tb=True, out_split=True)
        act, = hosting(conv_keys, 1, conv_glu_fwd, a3, conv_w_full[i], conv_b_full[i], "conv_glu_fwd")
        xc, f_out = hosting(down_keys, 2, matmul, act, w_fo_full, name="ffn_down", resid=x1, gvec=g2,
                            emit_acc=True)
        if "ffn_in" in got:
            ffn_w[nxt] = (got["ffn_in"].reshape(2 * FP, D), got["ffn_out"].reshape(FP, D))
        if "mix_in" in got:
            ahead_w[nxt] = got
        st.update(h2=h2, a3=a3, act=act, f=f_out, w_fi_full=w_fi_full, w_fo_full=w_fo_full)
        stash.append(st)

    loss_part, dx, d_final_g = loss_head(xc, final_g[None], tgt, "loss_head")
    loss = lax.psum(loss_part[0, 0], AXES)

    d_mod = [None] * L
    d_mix_g = [None] * L
    d_ffn_g = [None] * L
    d_conv_w = [None] * L
    d_conv_b = [None] * L
    g_wfi = [None] * L
    g_wfo = [None] * L
    g_wai = [None] * NA
    g_wao = [None] * NA
    d_bf = [None] * NA
    g_wgi = [None] * NB
    g_wgo = [None] * NB
    d_ws = [None] * NB
    d_bs = [None] * NB
    d_vg = [None] * NB
    ffn_pairs = None
    mix_pairs = None

    def half_reduce(g8, tag):
        got = run_exchange(pair_exchange(g8), "rs_pair_exchange_" + tag)
        return pair_sum(g8, got, c_idx, "rs_pair_sum_" + tag)

    def finish_mixer(layer, quad_mo, quad_in):
        g_mo = sum_slots(quad_mo, "rs_final_sum_mix_out")
        g_in = sum_slots(quad_in, "rs_final_sum_mix_in").T
        if layer % 2 == 0:
            g_wao[layer // 2], g_wai[layer // 2] = g_mo, g_in
        else:
            g_wgo[layer // 2], g_wgi[layer // 2] = g_mo, g_in

    for i in reversed(range(L)):
        st = stash[i]
        sh1, sc1, g1, sh2, sc2, g2 = (mod_me[i, k] for k in range(6))
        jm = i // 2
        dy, dg2 = gate_bwd(dx, st["f"], g2, "gate_bwd")
        if mix_pairs is None:
            dact = matmul(dy, st["w_fo_full"], name="ffn_down_dx", tb=True)
        else:
            dact, quad_mo = matmul(dy, st["w_fo_full"], name="ffn_down_dx", tb=True,
                                   riders=[chip_exchange(mix_pairs[1])])
        if ffn_pairs is None:
            dw_fo = matmul(st["act"], dy, name="ffn_down_dw", ta=True, out_dtype=bf16)
        else:
            dw_fo, quad = matmul(st["act"], dy, name="ffn_down_dw", ta=True, out_dtype=bf16,
                                 riders=[chip_exchange(ffn_pairs[0])])
            g_wfo[i + 1] = sum_slots(quad, "rs_final_sum_ffn_out")[:FR]
        g8_fo = dw_fo.reshape(N_DEV, FRP, D)
        da3, dwg, dwu, dbg, dbu, got_fo = conv_glu_bwd(st["a3"], conv_w_full[i], conv_b_full[i], dact,
                                                       "conv_glu_bwd", riders=[pair_exchange(g8_fo)])
        pair_fo = pair_sum(g8_fo, got_fo, c_idx, "rs_pair_sum_ffn_out")
        d_conv_w[i] = jnp.concatenate([dwg, dwu], axis=1)
        d_conv_b[i] = jnp.concatenate([dbg, dbu], axis=1)
        if ffn_pairs is None:
            dw_fi_t = matmul(da3, st["h2"], name="ffn_up_dw", ta=True, a_split=True, out_dtype=bf16)
        else:
            dw_fi_t, quad = matmul(da3, st["h2"], name="ffn_up_dw", ta=True, a_split=True, out_dtype=bf16,
                                   riders=[chip_exchange(ffn_pairs[1])])
            g_wfi[i + 1] = unpad_ff(sum_slots(quad, "rs_final_sum_ffn_in"), 0, 2).T
        g8_fi = dw_fi_t.reshape(N_DEV, FWP, D)
        if mix_pairs is None:
            dh2, got_fi = matmul(da3, st["w_fi_full"], name="ffn_up_dx", a_split=True,
                                 riders=[pair_exchange(g8_fi)])
        else:
            dh2, got_fi, quad_in = matmul(da3, st["w_fi_full"], name="ffn_up_dx", a_split=True,
                                          riders=[pair_exchange(g8_fi), chip_exchange(mix_pairs[2])])
            finish_mixer(mix_pairs[0], quad_mo, quad_in)
        ffn_pairs = (pair_fo, pair_sum(g8_fi, got_fi, c_idx, "rs_pair_sum_ffn_in"))
        dx, dsh2, dsc2, d_ffn_g[i] = norm_mod_bwd(st["x1"], ffn_norm_g[i][None], sc2, dh2, dx, "norm_mod_bwd")
        dy, dg1 = gate_bwd(dx, st["y"], g1, "gate_bwd")
        if i % 2 == 0:
            do = matmul(dy, st["w_mo"], name="mix_out_dx", tb=True)
            dw_mo = matmul(st["o"], dy, name="mix_out_dw", ta=True, out_dtype=bf16)
            dq, dk, dv, cs, rs = fox_attn_bwd(st["qkv"], st["o"], do, st["lse"], st["f_row"], st["f_col"], "fox_attn_bwd")
            dF = rs[:, :, 0] - cs[:, 0, :]
            dflog_t, d_bf[jm] = fox_gates_bwd(st["flog_t"], st["b_col"], dF, "fox_gates_bwd")
            dflog = jnp.pad(dflog_t.T, ((0, 0), (0, LANES - H))).astype(bf16)
            dqkv = jnp.concatenate([dq, dk, dv], axis=1)
            dw_qkv_t = matmul(dqkv, st["h"], name="fox_qkv_dw", ta=True, out_dtype=bf16)
            dw_f_t = matmul(dflog, st["h"], name="fox_flog_dw", ta=True, out_dtype=bf16)
            dw_in_t = jnp.concatenate([dw_qkv_t, dw_f_t[:H]], axis=0)
            dh = matmul(dqkv, st["w_qkv_t"], name="fox_qkv_dx")
            dh = matmul(dflog, st["w_f_t"], name="fox_flog_dx", resid=dh)
        else:
            dgated = matmul(dy, st["w_mo"], name="mix_out_dx", tb=True)
            dw_mo = matmul(st["gated"], dy, name="mix_out_dw", ta=True, out_dtype=bf16)
            dz, d_ws[jm], dbs_t, d_vg[jm] = gm_gate_bwd(st["z"], vg_full[jm], gm_w_s[jm], st["bs_t"], dgated, "gm_gate_bwd")
            d_bs[jm] = dbs_t.T
            dw_in_t = matmul(dz, st["h"], name="gm_in_dw", ta=True, out_dtype=bf16)
            dh = matmul(dz, st["w_gi_full"], name="gm_in_dx")
        mix_pairs = (i, half_reduce(dw_mo.reshape(N_DEV, DR, D), "mix_out"),
                     half_reduce(dw_in_t.reshape(N_DEV, -1, D), "mix_in"))
        dx, dsh1, dsc1, d_mix_g[i] = norm_mod_bwd(st["x_in"], mix_norm_g[i][None], sc1, dh, dx, "norm_mod_bwd")
        d_mod[i] = jnp.concatenate([dsh1, dsc1, dg1, dsh2, dsc2, dg2], axis=0)

    grad_x = dx[None]
    g_wfo[0] = reduce_scatter_tail(ffn_pairs[0], "ffn_out")[:FR]
    g_wfi[0] = unpad_ff(reduce_scatter_tail(ffn_pairs[1], "ffn_in"), 0, 2).T
    finish_mixer(mix_pairs[0], run_exchange(chip_exchange(mix_pairs[1]), "rs_chip_exchange_mix_out"),
                 run_exchange(chip_exchange(mix_pairs[2]), "rs_chip_exchange_mix_in"))
    grad_attn_w_in, grad_attn_w_o = jnp.stack(g_wai), jnp.stack(g_wao)
    grad_gm_w_in, grad_gm_w_o = jnp.stack(g_wgi), jnp.stack(g_wgo)
    grad_ffn_w_in, grad_ffn_w_out = jnp.stack(g_wfi), jnp.stack(g_wfo)

    def gathered_sum(rows, tag, mult=SUBLANES):
        n = rows.shape[0]
        rows = jnp.pad(rows, ((0, _round_up(n, mult) - n), (0, 0)))
        every = all_gather(rows, "gather_small_grads_" + tag)
        return every, sum_slots(every, "sum_small_grads_" + tag)

    rows_d = jnp.concatenate([jnp.concatenate(d_mod, axis=0), jnp.concatenate(d_mix_g, axis=0),
                              jnp.concatenate(d_ffn_g, axis=0), jnp.concatenate(d_vg, axis=0), d_final_g], axis=0)
    every_d, sum_d = gathered_sum(rows_d, "d")
    r0 = L * 6
    grad_mod_b = sum_d[:r0].reshape(L, 6 * D)
    grad_mix_g, grad_ffn_g = sum_d[r0:r0 + L], sum_d[r0 + L:r0 + 2 * L]
    grad_vg_full = sum_d[r0 + 2 * L:r0 + 2 * L + NB]
    grad_final_g = sum_d[r0 + 2 * L + NB]
    grad_vg = lax.dynamic_slice_in_dim(grad_vg_full, me * DR, DR, axis=1)

    rows_f = jnp.concatenate([jnp.concatenate(d_conv_w, axis=0), jnp.concatenate(d_conv_b, axis=0)], axis=0)
    _, sum_f = gathered_sum(rows_f, "f")
    sum_f = unpad_ff(sum_f, 1, 2 * N_DEV)
    grad_conv_w = lax.dynamic_slice_in_dim(sum_f[:L * CONV_W].reshape(L, CONV_W, DFF2), me * FW, FW, axis=2)
    grad_conv_b = sum_f[L * CONV_W:L * CONV_W + L]

    rows_c = jnp.concatenate([jnp.stack(d_ws).reshape(NB * G * CHUNK, CHUNK), jnp.stack(d_bs).reshape(NB * G, CHUNK),
                              jnp.pad(jnp.stack(d_bf).reshape(NA, H), ((0, 0), (0, LANES - H)))], axis=0)
    _, sum_c = gathered_sum(rows_c, "c", mult=SLOT_ROWS)
    n_ws = NB * G * CHUNK
    grad_ws = sum_c[:n_ws].reshape(NB, G, CHUNK, CHUNK)
    grad_bs = sum_c[n_ws:n_ws + NB * G].reshape(NB, G, CHUNK)
    grad_bf = sum_c[n_ws + NB * G:n_ws + NB * G + NA, :H]

    dmod_all = every_d[:, :r0].reshape(N_DEV, L, 6 * D)
    dmod_loc = lax.dynamic_slice_in_dim(dmod_all, me * MW, MW, axis=2).transpose(1, 0, 2)
    dmod_loc = jnp.pad(dmod_loc, ((0, 0), (0, LANES - N_DEV), (0, 0)))
    c_t = jnp.pad(c_all.T, ((0, 0), (0, LANES - N_DEV)))
    grad_mod_w = mod_w_bwd(c_t, dmod_loc, "mod_w_bwd")

    weights = [mod_w, mod_b, mix_norm_g, ffn_norm_g, attn_w_in, attn_b_f, attn_w_o, gm_w_in, gm_v_g, gm_w_s,
               gm_b_s, gm_w_o, ffn_w_in, ffn_conv_w, ffn_conv_b, ffn_w_out, final_g]
    grads = [grad_mod_w, grad_mod_b, grad_mix_g, grad_ffn_g, grad_attn_w_in, grad_bf, grad_attn_w_o,
             grad_gm_w_in, grad_vg, grad_ws, grad_bs, grad_gm_w_o, grad_ffn_w_in, grad_conv_w, grad_conv_b,
             grad_ffn_w_out, grad_final_g]
    ms = [m_mod_w, m_mod_b, m_mix_norm_g, m_ffn_norm_g, m_attn_w_in, m_attn_b_f, m_attn_w_o, m_gm_w_in, m_gm_v_g,
          m_gm_w_s, m_gm_b_s, m_gm_w_o, m_ffn_w_in, m_ffn_conv_w, m_ffn_conv_b, m_ffn_w_out, m_final_g]
    vs = [v_mod_w, v_mod_b, v_mix_norm_g, v_ffn_norm_g, v_attn_w_in, v_attn_b_f, v_attn_w_o, v_gm_w_in, v_gm_v_g,
          v_gm_w_s, v_gm_b_s, v_gm_w_o, v_ffn_w_in, v_ffn_conv_w, v_ffn_conv_b, v_ffn_w_out, v_final_g]
    deltas, new_ms, new_vs = [], [], []
    for w, g, m_, v_ in zip(weights, grads, ms, vs):
        d_, mn_, vn_ = adamw(w, g, m_, v_, "adamw")
        deltas.append(d_)
        new_ms.append(mn_)
        new_vs.append(vn_)

    return (loss, grad_x, *grads, *deltas, *new_ms, *new_vs)
```

```python
import numpy as np
import jax
import jax.numpy as jnp
from jax import lax
from jax.experimental import pallas as pl
from jax.experimental.pallas import tpu as pltpu

f32 = jnp.float32
bf16 = jnp.bfloat16

AXES = ("x", "y", "c")
N_DEV = 8
N_CHIPS = 4
LANES = 128
SUBLANES = 8
HEAD_DIM = 128
CHUNK = 128
GM_GROUP = 128
CONV_W = 3
EPS = 1e-6
NEG = -1e30
VMEM_LIMIT_BYTES = 56 * 1024 * 1024
MATMUL_VMEM_BYTES = 40 * 1024 * 1024
SLOT_ROWS = 512
LOG2E = 1.4426950408889634
ATT_TILE = 512

ADAM_LR = 0.001
ADAM_B1 = 0.9
ADAM_B2 = 0.999
ADAM_EPS = 1e-08
ADAM_WD = 0.01
ADAM_STEP = 10

MESH = pl.DeviceIdType.MESH
ANY = pl.BlockSpec(memory_space=pl.ANY)


def _cp(sem):
    return pltpu.CompilerParams(dimension_semantics=sem, vmem_limit_bytes=VMEM_LIMIT_BYTES)


def _pick(n, prefs):
    for p in prefs:
        if n % p == 0:
            return p
    return n


def _round_up(n, m):
    return (n + m - 1) // m * m


class Exchange:
    def __init__(self, src, out_shape, scratch, phases):
        self.src = src
        self.out_shape = out_shape
        self.scratch = scratch
        self.phases = phases


def gather_exchange(xl):
    def phases(x_ref, out_ref, send_sems, recv_sems, local_sem):
        x, y, c = lax.axis_index("x"), lax.axis_index("y"), lax.axis_index("c")
        me, sibling = (x, y, c), (x, y, 1 - c)
        chips = [(1 - x, y), (x, 1 - y), (1 - x, 1 - y)]

        def slot(px, py, pc):
            return out_ref.at[4 * px + 2 * py + pc]

        def copy(k, block, to, src=None):
            return pltpu.make_async_remote_copy(
                src_ref=slot(*block) if src is None else src, dst_ref=slot(*block),
                send_sem=send_sems.at[k], recv_sem=recv_sems.at[k],
                device_id=to, device_id_type=MESH)

        mine = pltpu.make_async_copy(x_ref, slot(*me), local_sem)
        first = [copy(0, me, sibling, src=x_ref)]
        first += [copy(1 + j, me, (*chip, c), src=x_ref) for j, chip in enumerate(chips)]
        passed = [copy(4 + j, (*chip, c), sibling) for j, chip in enumerate(chips)]

        def start():
            mine.start()
            for cp in first:
                cp.start()

        def hand_on():
            for j, chip in enumerate(chips):
                copy(1 + j, (*chip, c), me).wait_recv()
                passed[j].start()

        def finish():
            copy(0, sibling, me).wait_recv()
            for j, chip in enumerate(chips):
                copy(4 + j, (*chip, 1 - c), me).wait_recv()
            for cp in first + passed:
                cp.wait_send()
            mine.wait()

        return start, hand_on, finish

    return Exchange(xl, jax.ShapeDtypeStruct((N_DEV,) + xl.shape, xl.dtype),
                    [pltpu.SemaphoreType.DMA((7,)), pltpu.SemaphoreType.DMA((7,)),
                     pltpu.SemaphoreType.DMA], phases)


def pair_exchange(g8):
    _, R, W = g8.shape

    def phases(g_ref, out_ref, send_sems, recv_sems):
        x, y, c = lax.axis_index("x"), lax.axis_index("y"), lax.axis_index("c")
        copies = [pltpu.make_async_remote_copy(
            src_ref=g_ref.at[2 * q + (1 - c)], dst_ref=out_ref.at[q],
            send_sem=send_sems.at[q], recv_sem=recv_sems.at[q],
            device_id=(x, y, 1 - c), device_id_type=MESH) for q in range(N_CHIPS)]

        def start():
            for cp in copies:
                cp.start()

        def finish():
            for cp in copies:
                cp.wait()

        return start, None, finish

    return Exchange(g8, jax.ShapeDtypeStruct((N_CHIPS, R, W), g8.dtype),
                    [pltpu.SemaphoreType.DMA((N_CHIPS,)), pltpu.SemaphoreType.DMA((N_CHIPS,))], phases)


def chip_exchange(p4):
    def phases(p_ref, out_ref, send_sems, recv_sems, local_sem):
        x, y, c = lax.axis_index("x"), lax.axis_index("y"), lax.axis_index("c")
        my_q = 2 * x + y
        chips = [(1 - x, y), (x, 1 - y), (1 - x, 1 - y)]
        mine = pltpu.make_async_copy(p_ref.at[my_q], out_ref.at[my_q], local_sem)
        copies = [pltpu.make_async_remote_copy(
            src_ref=p_ref.at[2 * px + py], dst_ref=out_ref.at[my_q],
            send_sem=send_sems.at[k], recv_sem=recv_sems.at[k],
            device_id=(px, py, c), device_id_type=MESH) for k, (px, py) in enumerate(chips)]

        def start():
            mine.start()
            for cp in copies:
                cp.start()

        def finish():
            for k, (px, py) in enumerate(chips):
                pltpu.make_async_remote_copy(
                    src_ref=p_ref.at[my_q], dst_ref=out_ref.at[2 * px + py],
                    send_sem=send_sems.at[k], recv_sem=recv_sems.at[k],
                    device_id=(px, py, c), device_id_type=MESH).wait_recv()
            for cp in copies:
                cp.wait_send()
            mine.wait()

        return start, None, finish

    return Exchange(p4, jax.ShapeDtypeStruct(p4.shape, p4.dtype),
                    [pltpu.SemaphoreType.DMA((3,)), pltpu.SemaphoreType.DMA((3,)),
                     pltpu.SemaphoreType.DMA], phases)


def run_exchange(ex, name):
    def body(src_ref, out_ref, *sems):
        start, hand_on, finish = ex.phases(src_ref, out_ref, *sems)
        start()
        if hand_on is not None:
            hand_on()
        finish()

    return pl.pallas_call(body, name=name, out_shape=ex.out_shape, in_specs=[ANY], out_specs=ANY,
                          scratch_shapes=ex.scratch)(ex.src)


def all_gather(xl, name):
    return run_exchange(gather_exchange(xl), name)


class Riders:
    def __init__(self, exchanges):
        self.exs = list(exchanges or [])
        self.in_specs = [ANY] * len(self.exs)
        self.args = [ex.src for ex in self.exs]
        self.out_specs = [ANY] * len(self.exs)
        self.out_shape = [ex.out_shape for ex in self.exs]
        self.scratch = [s for ex in self.exs for s in ex.scratch]

    def split(self, in_refs, out_refs, scratch_refs):
        n = len(self.exs)
        self.refs = []
        pos = len(scratch_refs) - len(self.scratch)
        for k, ex in enumerate(self.exs):
            sems = scratch_refs[pos:pos + len(ex.scratch)]
            pos += len(ex.scratch)
            self.refs.append((in_refs[len(in_refs) - n + k], out_refs[len(out_refs) - n + k], sems))

    def _parts(self):
        return [ex.phases(src, out, *sems) for ex, (src, out, sems) in zip(self.exs, self.refs)]

    def before(self, step):
        if not self.exs:
            return
        parts = self._parts()

        @pl.when(step == 0)
        def _():
            for start, _, _ in parts:
                start()

    def after(self, step, n_steps):
        if not self.exs:
            return
        parts = self._parts()
        mid = (3 * n_steps) // 4

        if any(h is not None for _, h, _ in parts):
            @pl.when(step == mid)
            def _():
                for _, hand_on, _ in parts:
                    if hand_on is not None:
                        hand_on()

        @pl.when(step == n_steps - 1)
        def _():
            for _, _, finish in parts:
                finish()


def pair_sum(g8, got4, c_idx, name):
    _, R, W = g8.shape
    tr = _pick(R, (512, 256, 128, 64, 32, 16))
    g5 = g8.reshape(N_CHIPS, 2, R, W)

    def body(c_ref, a_ref, b_ref, o_ref):
        o_ref[...] = (a_ref[...].astype(f32) + b_ref[...].astype(f32)).astype(o_ref.dtype)

    grid_spec = pltpu.PrefetchScalarGridSpec(
        num_scalar_prefetch=1, grid=(N_CHIPS, R // tr),
        in_specs=[pl.BlockSpec((None, None, tr, W), lambda q, r, cr: (q, cr[0], r, 0)),
                  pl.BlockSpec((None, tr, W), lambda q, r, cr: (q, r, 0))],
        out_specs=pl.BlockSpec((None, tr, W), lambda q, r, cr: (q, r, 0)))
    return pl.pallas_call(
        body, name=name, grid_spec=grid_spec,
        out_shape=jax.ShapeDtypeStruct((N_CHIPS, R, W), g8.dtype),
        compiler_params=_cp(("parallel", "parallel")),
    )(c_idx, g5, got4)


def sum_slots(xs, name, out_dtype=f32):
    S, R, W = xs.shape
    tr = _pick(R, (512, 256, 128, 64, 32, 16, 8))

    def body(x_ref, o_ref):
        acc = x_ref[0].astype(f32)
        for s in range(1, S):
            acc = acc + x_ref[s].astype(f32)
        o_ref[...] = acc.astype(o_ref.dtype)

    return pl.pallas_call(
        body, name=name, grid=(R // tr,),
        in_specs=[pl.BlockSpec((S, tr, W), lambda r: (0, r, 0))],
        out_specs=pl.BlockSpec((tr, W), lambda r: (r, 0)),
        out_shape=jax.ShapeDtypeStruct((R, W), out_dtype),
        compiler_params=_cp(("parallel",)),
    )(xs)


def matmul(a, b, *, name, ta=False, tb=False, a_split=False, b_split=False, out_split=False,
           out_dtype=f32, resid=None, gvec=None, emit_acc=False, riders=None):
    rd = Riders(riders)
    if a_split:
        rows, cols = a.shape[1], 2 * a.shape[2]
        M, K = (cols, rows) if ta else (rows, cols)
    else:
        M, K = (a.shape[1], a.shape[0]) if ta else a.shape
    if b_split:
        assert not tb
        N = 2 * b.shape[2]
        assert b.shape[1] == K
    else:
        N = b.shape[0] if tb else b.shape[1]
        assert (b.shape[1] if tb else b.shape[0]) == K, (a.shape, b.shape, name)

    m_split = a_split and ta
    k_split = a_split and not ta
    n_split = b_split or out_split
    tm = _pick(M // 2 if m_split else M, (1024, 512, 256, 128, 64, 32, 16, 8))
    k_len, n_len = (K // 2 if k_split else K), (N // 2 if n_split else N)
    out_bytes = jnp.dtype(out_dtype).itemsize + (4 if resid is not None else 0) + (2 if emit_acc else 0)

    def fits(tk_, tn_):
        operands = 2 * (tm * tk_ * a.dtype.itemsize + tk_ * tn_ * b.dtype.itemsize)
        acc = tm * tn_ * 4 if tk_ < K else 0
        return operands + acc + 2 * tm * tn_ * out_bytes <= MATMUL_VMEM_BYTES

    tk_options = [d for d in range(k_len, 0, -LANES) if k_len % d == 0 and d % LANES == 0]
    tn_options = [t for t in (1024, 512, 256, 128) if n_len % t == 0]
    tk, tn = next(((tk_, tn_) for tn_min in (512, 128) for tk_ in tk_options for tn_ in tn_options
                   if tn_ >= tn_min and fits(tk_, tn_)), (tk_options[-1], tn_options[-1]))
    nk = K // tk
    n_half = (N // 2) // tn if n_split else 0
    k_half = (K // 2) // tk if k_split else 0
    m_half = (M // 2) // tm if m_split else 0

    if m_split:
        a_spec = pl.BlockSpec((None, tk, tm), lambda i, j, k: (i // m_half, k, i % m_half))
    elif k_split:
        a_spec = pl.BlockSpec((None, tm, tk), lambda i, j, k: (k // k_half, i, k % k_half))
    elif ta:
        a_spec = pl.BlockSpec((tk, tm), lambda i, j, k: (k, i))
    else:
        a_spec = pl.BlockSpec((tm, tk), lambda i, j, k: (i, k))
    if b_split:
        b_spec = pl.BlockSpec((None, tk, tn), lambda i, j, k: (j // n_half, k, j % n_half))
    elif tb:
        b_spec = pl.BlockSpec((tn, tk), lambda i, j, k: (j, k))
    else:
        b_spec = pl.BlockSpec((tk, tn), lambda i, j, k: (k, j))
    if out_split:
        o_spec = pl.BlockSpec((None, tm, tn), lambda i, j, k: (j // n_half, i, j % n_half))
        o_shape = (2, M, N // 2)
    else:
        o_spec = pl.BlockSpec((tm, tn), lambda i, j, k: (i, j))
        o_shape = (M, N)

    in_specs = [a_spec, b_spec]
    args = [a, b]
    if resid is not None:
        in_specs.append(pl.BlockSpec((tm, tn), lambda i, j, k: (i, j)))
        args.append(resid)
    if gvec is not None:
        in_specs.append(pl.BlockSpec((1, tn), lambda i, j, k: (0, j)))
        args.append(gvec)
    out_specs = [o_spec]
    out_shape = [jax.ShapeDtypeStruct(o_shape, out_dtype)]
    if emit_acc:
        out_specs.append(pl.BlockSpec((tm, tn), lambda i, j, k: (i, j)))
        out_shape.append(jax.ShapeDtypeStruct((M, N), bf16))
    dims = (((0 if ta else 1,), (1 if tb else 0,)), ((), ()))
    has_r, has_g = resid is not None, gvec is not None
    n_in, n_out = len(in_specs) + len(rd.exs), len(out_specs) + len(rd.exs)
    grid = (M // tm, N // tn, nk)
    n_steps = grid[0] * grid[1] * grid[2]

    def body(*refs):
        in_refs, out_refs, scratch_refs = refs[:n_in], refs[n_in:n_in + n_out], refs[n_in + n_out:]
        rd.split(in_refs, out_refs, scratch_refs)
        a_ref, b_ref = in_refs[0], in_refs[1]
        pos = 2
        r_ref = g_ref = None
        if has_r:
            r_ref = in_refs[pos]
            pos += 1
        if has_g:
            g_ref = in_refs[pos]
        o_ref = out_refs[0]
        y_ref = out_refs[1] if emit_acc else None
        step = (pl.program_id(0) * grid[1] + pl.program_id(1)) * nk + pl.program_id(2)
        rd.before(step)

        def finish(acc):
            if emit_acc:
                y_ref[...] = acc.astype(bf16)
            if has_g:
                acc = acc * g_ref[...]
            if has_r:
                acc = r_ref[...] + acc
            o_ref[...] = acc.astype(o_ref.dtype)

        part = lax.dot_general(a_ref[...].astype(bf16), b_ref[...].astype(bf16), dims,
                               preferred_element_type=f32)
        if nk == 1:
            finish(part)
        else:
            acc_ref = scratch_refs[0]
            k = pl.program_id(2)

            @pl.when(k == 0)
            def _():
                acc_ref[...] = part

            @pl.when(k > 0)
            def _():
                acc_ref[...] += part

            @pl.when(k == nk - 1)
            def _():
                finish(acc_ref[...])

        rd.after(step, n_steps)

    outs = pl.pallas_call(
        body, name=name, grid=grid,
        in_specs=in_specs + rd.in_specs, out_specs=out_specs + rd.out_specs,
        out_shape=out_shape + rd.out_shape,
        scratch_shapes=([pltpu.VMEM((tm, tn), f32)] if nk > 1 else []) + rd.scratch,
        compiler_params=_cp(("arbitrary",) * 3 if rd.exs else ("parallel", "parallel", "arbitrary")),
    )(*args, *rd.args)
    return outs if (emit_acc or rd.exs) else outs[0]


def _rows(T):
    return _pick(T, (256, 128, 64, 32, 16, 8))


def norm_mod_fwd(x, gn, sc, sh, name):
    T, D = x.shape
    tr = _rows(T)

    def body(x_ref, gn_ref, sc_ref, sh_ref, h_ref):
        xv = x_ref[...]
        r = lax.rsqrt(jnp.mean(xv * xv, axis=-1, keepdims=True) + EPS)
        y = (xv * r) * gn_ref[...]
        h_ref[...] = (y * (1.0 + sc_ref[...]) + sh_ref[...]).astype(bf16)

    vec = pl.BlockSpec((1, D), lambda i: (0, 0))
    row = pl.BlockSpec((tr, D), lambda i: (i, 0))
    return pl.pallas_call(
        body, name=name, grid=(T // tr,), in_specs=[row, vec, vec, vec], out_specs=row,
        out_shape=jax.ShapeDtypeStruct((T, D), bf16), compiler_params=_cp(("parallel",)),
    )(x, gn, sc, sh)


def norm_mod_bwd(x, gn, sc, dh, dx_res, name):
    T, D = x.shape
    tr = _rows(T)

    def body(x_ref, gn_ref, sc_ref, dh_ref, dr_ref, dx_ref, dsh_ref, dsc_ref, dgn_ref):
        @pl.when(pl.program_id(0) == 0)
        def _():
            dsh_ref[...] = jnp.zeros_like(dsh_ref)
            dsc_ref[...] = jnp.zeros_like(dsc_ref)
            dgn_ref[...] = jnp.zeros_like(dgn_ref)

        xv = x_ref[...]
        r = lax.rsqrt(jnp.mean(xv * xv, axis=-1, keepdims=True) + EPS)
        xn = xv * r
        gn_v = gn_ref[...]
        dh_v = dh_ref[...]
        dsh_ref[...] += jnp.sum(dh_v, axis=0, keepdims=True)
        dsc_ref[...] += jnp.sum(dh_v * (xn * gn_v), axis=0, keepdims=True)
        dy = dh_v * (1.0 + sc_ref[...])
        dgn_ref[...] += jnp.sum(dy * xn, axis=0, keepdims=True)
        dxn = dy * gn_v
        dx = r * (dxn - xn * jnp.mean(dxn * xn, axis=-1, keepdims=True))
        dx_ref[...] = dr_ref[...] + dx

    vec = pl.BlockSpec((1, D), lambda i: (0, 0))
    row = pl.BlockSpec((tr, D), lambda i: (i, 0))
    vshape = jax.ShapeDtypeStruct((1, D), f32)
    return pl.pallas_call(
        body, name=name, grid=(T // tr,), in_specs=[row, vec, vec, row, row],
        out_specs=[row, vec, vec, vec],
        out_shape=[jax.ShapeDtypeStruct((T, D), f32), vshape, vshape, vshape],
        compiler_params=_cp(("arbitrary",)),
    )(x, gn, sc, dh, dx_res)


def gate_bwd(dx, y, g, name):
    T, D = dx.shape
    tr = _rows(T)

    def body(dx_ref, y_ref, g_ref, dy_ref, dg_ref):
        @pl.when(pl.program_id(0) == 0)
        def _():
            dg_ref[...] = jnp.zeros_like(dg_ref)

        dxv = dx_ref[...]
        dy_ref[...] = (dxv * g_ref[...]).astype(bf16)
        dg_ref[...] += jnp.sum(dxv * y_ref[...].astype(f32), axis=0, keepdims=True)

    vec = pl.BlockSpec((1, D), lambda i: (0, 0))
    row = pl.BlockSpec((tr, D), lambda i: (i, 0))
    return pl.pallas_call(
        body, name=name, grid=(T // tr,), in_specs=[row, row, vec], out_specs=[row, vec],
        out_shape=[jax.ShapeDtypeStruct((T, D), bf16), jax.ShapeDtypeStruct((1, D), f32)],
        compiler_params=_cp(("arbitrary",)),
    )(dx, y, g)


def loss_head(x, fg, tgt, name):
    T, D = x.shape
    tr = _rows(T)

    def body(x_ref, fg_ref, t_ref, loss_ref, dx_ref, dfg_ref):
        @pl.when(pl.program_id(0) == 0)
        def _():
            loss_ref[...] = jnp.zeros_like(loss_ref)
            dfg_ref[...] = jnp.zeros_like(dfg_ref)

        xv = x_ref[...]
        r = lax.rsqrt(jnp.mean(xv * xv, axis=-1, keepdims=True) + EPS)
        xn = xv * r
        fg_v = fg_ref[...]
        err = xn * fg_v - t_ref[...]
        per_tok = jnp.mean(err * err, axis=-1, keepdims=True)
        loss_ref[...] += 0.5 * jnp.sum(per_tok, axis=0, keepdims=True)
        dy = err * (1.0 / D)
        dfg_ref[...] += jnp.sum(dy * xn, axis=0, keepdims=True)
        dxn = dy * fg_v
        dx_ref[...] = r * (dxn - xn * jnp.mean(dxn * xn, axis=-1, keepdims=True))

    vec = pl.BlockSpec((1, D), lambda i: (0, 0))
    row = pl.BlockSpec((tr, D), lambda i: (i, 0))
    one = pl.BlockSpec((1, 1), lambda i: (0, 0))
    return pl.pallas_call(
        body, name=name, grid=(T // tr,), in_specs=[row, vec, row], out_specs=[one, row, vec],
        out_shape=[jax.ShapeDtypeStruct((1, 1), f32), jax.ShapeDtypeStruct((T, D), f32),
                   jax.ShapeDtypeStruct((1, D), f32)],
        compiler_params=_cp(("arbitrary",)),
    )(x, fg, tgt)


def _conv_tiles(T, FP):
    return _pick(T, (512, 256, 128, 64, 32, 16, 8)), _pick(FP, (512, 256, 128))


def _conv_specs(tr, tc, T, FP):
    nj = FP // tc
    r8 = tr // SUBLANES
    last8 = T // SUBLANES - 1
    main = pl.BlockSpec((2, tr, tc), lambda j, i: (0, i, j))
    prev = pl.BlockSpec((2, SUBLANES, tc), lambda j, i: (0, jnp.maximum(i * r8 - 1, 0), j))
    nxt = pl.BlockSpec((2, SUBLANES, tc), lambda j, i: (0, jnp.minimum((i + 1) * r8, last8), j))
    wg = pl.BlockSpec((CONV_W, tc), lambda j, i: (0, j))
    wu = pl.BlockSpec((CONV_W, tc), lambda j, i: (0, j + nj))
    bg = pl.BlockSpec((1, tc), lambda j, i: (0, j))
    bu = pl.BlockSpec((1, tc), lambda j, i: (0, j + nj))
    return main, prev, nxt, wg, wu, bg, bu


def _causal_taps(av, hp_ref, s, has_prev, row):
    h7 = jnp.where(has_prev, hp_ref[s, 7:8, :], 0.0)
    h6 = jnp.where(has_prev, hp_ref[s, 6:7, :], 0.0)
    m1 = jnp.where(row == 0, h7, pltpu.roll(av, 1, 0))
    m2 = jnp.where(row == 0, h6, jnp.where(row == 1, h7, pltpu.roll(av, 2, 0)))
    return m1, m2


def conv_glu_fwd(a3, conv_w, conv_b, name, riders=None):
    _, T, FP = a3.shape
    tr, tc = _conv_tiles(T, FP)
    main, prev, _, wg, wu, bg, bu = _conv_specs(tr, tc, T, FP)
    rd = Riders(riders)
    n_ex = len(rd.exs)
    ni = T // tr

    def body(*refs):
        in_refs, out_refs, scratch_refs = refs[:6 + n_ex], refs[6 + n_ex:7 + 2 * n_ex], refs[7 + 2 * n_ex:]
        rd.split(in_refs, out_refs, scratch_refs)
        a_ref, hp_ref, wg_ref, wu_ref, bg_ref, bu_ref = in_refs[:6]
        act_ref = out_refs[0]
        step = pl.program_id(0) * ni + pl.program_id(1)
        rd.before(step)
        has_prev = pl.program_id(1) > 0
        row = lax.broadcasted_iota(jnp.int32, (tr, tc), 0)

        def conv(s, w_ref, b_ref):
            av = a_ref[s]
            m1, m2 = _causal_taps(av, hp_ref, s, has_prev, row)
            return w_ref[0:1, :] * m2 + w_ref[1:2, :] * m1 + w_ref[2:3, :] * av + b_ref[...]

        gate = conv(0, wg_ref, bg_ref)
        up = conv(1, wu_ref, bu_ref)
        act_ref[...] = ((gate * jax.nn.sigmoid(gate)) * up).astype(bf16)
        rd.after(step, (FP // tc) * ni)

    outs = pl.pallas_call(
        body, name=name, grid=(FP // tc, ni),
        in_specs=[main, prev, wg, wu, bg, bu] + rd.in_specs,
        out_specs=[pl.BlockSpec((tr, tc), lambda j, i: (i, j))] + rd.out_specs,
        out_shape=[jax.ShapeDtypeStruct((T, FP), bf16)] + rd.out_shape,
        scratch_shapes=rd.scratch,
        compiler_params=_cp(("arbitrary", "arbitrary") if rd.exs else ("parallel", "parallel")),
    )(a3, a3, conv_w, conv_w, conv_b, conv_b, *rd.args)
    return outs if rd.exs else outs[0]


def conv_glu_bwd(a3, conv_w, conv_b, dact, name, riders=None):
    _, T, FP = a3.shape
    tr, tc = _conv_tiles(T, FP)
    ni = T // tr
    main, prev, nxt, wg, wu, bg, bu = _conv_specs(tr, tc, T, FP)
    r8 = tr // SUBLANES
    last8 = T // SUBLANES - 1
    d_main = pl.BlockSpec((tr, tc), lambda j, i: (i, j))
    d_next = pl.BlockSpec((SUBLANES, tc), lambda j, i: (jnp.minimum((i + 1) * r8, last8), j))
    rd = Riders(riders)
    n_ex = len(rd.exs)

    def body(*refs):
        in_refs, out_refs, scratch_refs = refs[:9 + n_ex], refs[9 + n_ex:14 + 2 * n_ex], refs[14 + 2 * n_ex:]
        rd.split(in_refs, out_refs, scratch_refs)
        a_ref, hp_ref, hn_ref, d_ref, dn_ref, wg_ref, wu_ref, bg_ref, bu_ref = in_refs[:9]
        da_ref, dwg_ref, dwu_ref, dbg_ref, dbu_ref = out_refs[:5]
        i = pl.program_id(1)
        step = pl.program_id(0) * ni + i
        rd.before(step)
        has_prev = i > 0
        has_next = i < ni - 1
        row = lax.broadcasted_iota(jnp.int32, (tr, tc), 0)
        row8 = lax.broadcasted_iota(jnp.int32, (SUBLANES, tc), 0)

        @pl.when(i == 0)
        def _():
            dwg_ref[...] = jnp.zeros_like(dwg_ref)
            dwu_ref[...] = jnp.zeros_like(dwu_ref)
            dbg_ref[...] = jnp.zeros_like(dbg_ref)
            dbu_ref[...] = jnp.zeros_like(dbu_ref)

        def prep(s, w_ref, b_ref):
            av = a_ref[s]
            m1, m2 = _causal_taps(av, hp_ref, s, has_prev, row)
            w0, w1, w2, bv = w_ref[0:1, :], w_ref[1:2, :], w_ref[2:3, :], b_ref[...]
            pre = w0 * m2 + w1 * m1 + w2 * av + bv
            an = hn_ref[s]
            l1 = a_ref[s, tr - 1:tr, :]
            l2 = a_ref[s, tr - 2:tr - 1, :]
            n1 = jnp.where(row8 == 0, l1, pltpu.roll(an, 1, 0))
            n2 = jnp.where(row8 == 0, l2, jnp.where(row8 == 1, l1, pltpu.roll(an, 2, 0)))
            pre_n = w0 * n2 + w1 * n1 + w2 * an + bv
            return av, m1, m2, pre, pre_n

        def glu_bwd(gate, up, d):
            sg = jax.nn.sigmoid(gate)
            dgate = d * up * (sg * (1.0 + gate * (1.0 - sg)))
            dup = d * (gate * sg)
            return dgate, dup

        def row_of(v8, r):
            return jnp.sum(jnp.where(row8 == r, v8, 0.0), axis=0, keepdims=True)

        def back(dc, dc_n, w_ref):
            n0, n1 = row_of(dc_n, 0), row_of(dc_n, 1)
            p1 = jnp.where(row == tr - 1, n0, pltpu.roll(dc, tr - 1, 0))
            p2 = jnp.where(row == tr - 1, n1, jnp.where(row == tr - 2, n0, pltpu.roll(dc, tr - 2, 0)))
            return w_ref[2:3, :] * dc + w_ref[1:2, :] * p1 + w_ref[0:1, :] * p2

        def tok_sum(v):
            return jnp.sum(v, axis=0, keepdims=True)

        ag, g1, g2, gate, gate_n = prep(0, wg_ref, bg_ref)
        au, u1, u2, up, up_n = prep(1, wu_ref, bu_ref)
        dcg, dcu = glu_bwd(gate, up, d_ref[...])
        dn = jnp.where(has_next, dn_ref[...], 0.0)
        dcg_n, dcu_n = glu_bwd(gate_n, up_n, dn)
        da_ref[0] = back(dcg, dcg_n, wg_ref).astype(bf16)
        da_ref[1] = back(dcu, dcu_n, wu_ref).astype(bf16)
        dwg_ref[0:1, :] += tok_sum(dcg * g2)
        dwg_ref[1:2, :] += tok_sum(dcg * g1)
        dwg_ref[2:3, :] += tok_sum(dcg * ag)
        dwu_ref[0:1, :] += tok_sum(dcu * u2)
        dwu_ref[1:2, :] += tok_sum(dcu * u1)
        dwu_ref[2:3, :] += tok_sum(dcu * au)
        dbg_ref[...] += tok_sum(dcg)
        dbu_ref[...] += tok_sum(dcu)
        rd.after(step, (FP // tc) * ni)

    w_out = pl.BlockSpec((CONV_W, tc), lambda j, i: (0, j))
    b_out = pl.BlockSpec((1, tc), lambda j, i: (0, j))
    return pl.pallas_call(
        body, name=name, grid=(FP // tc, ni),
        in_specs=[main, prev, nxt, d_main, d_next, wg, wu, bg, bu] + rd.in_specs,
        out_specs=[main, w_out, w_out, b_out, b_out] + rd.out_specs,
        out_shape=[jax.ShapeDtypeStruct((2, T, FP), bf16),
                   jax.ShapeDtypeStruct((CONV_W, FP), f32), jax.ShapeDtypeStruct((CONV_W, FP), f32),
                   jax.ShapeDtypeStruct((1, FP), f32), jax.ShapeDtypeStruct((1, FP), f32)] + rd.out_shape,
        scratch_shapes=rd.scratch,
        compiler_params=_cp(("arbitrary", "arbitrary") if rd.exs else ("parallel", "arbitrary")),
    )(a3, a3, a3, dact, dact, conv_w, conv_w, conv_b, conv_b, *rd.args)


_GELU_C = 0.7978845608028654
_GELU_A = 0.044715


def _gelu(x):
    return 0.5 * x * (1.0 + jnp.tanh(_GELU_C * (x + _GELU_A * (x * x * x))))


def _gelu_and_grad(x):
    t = jnp.tanh(_GELU_C * (x + _GELU_A * (x * x * x)))
    g = 0.5 * x * (1.0 + t)
    dg = 0.5 * (1.0 + t) + 0.5 * x * (1.0 - t * t) * (_GELU_C * (1.0 + 3.0 * _GELU_A * (x * x)))
    return g, dg


def _tril_bf16(w):
    r = lax.broadcasted_iota(jnp.int32, w.shape, 0)
    c = lax.broadcasted_iota(jnp.int32, w.shape, 1)
    return jnp.where(r >= c, w, 0.0).astype(bf16)


def gm_gate_fwd(z, vg, ws, bs_t, name):
    T, D2 = z.shape
    D = D2 // 2
    G = D // GM_GROUP
    tr = _pick(T, (256, 128))
    nc = tr // CHUNK

    def body(z_ref, vg_ref, ws_ref, bs_ref, o_ref):
        u = _gelu(z_ref[:, :D])
        v = _gelu(z_ref[:, D:])
        rv = lax.rsqrt(jnp.mean(v * v, axis=-1, keepdims=True) + EPS)
        vn = ((v * rv) * vg_ref[...]).astype(bf16)
        for g in range(G):
            wg = _tril_bf16(ws_ref[g])
            bg = bs_ref[:, g:g + 1]
            cs = slice(g * GM_GROUP, (g + 1) * GM_GROUP)
            for c in range(nc):
                rs = slice(c * CHUNK, (c + 1) * CHUNK)
                sv = jnp.dot(wg, vn[rs, cs], preferred_element_type=f32) + bg
                o_ref[rs, cs] = (u[rs, cs] * sv).astype(bf16)

    return pl.pallas_call(
        body, name=name, grid=(T // tr,),
        in_specs=[pl.BlockSpec((tr, D2), lambda i: (i, 0)),
                  pl.BlockSpec((1, D), lambda i: (0, 0)),
                  pl.BlockSpec((G, CHUNK, CHUNK), lambda i: (0, 0, 0)),
                  pl.BlockSpec((CHUNK, G), lambda i: (0, 0))],
        out_specs=pl.BlockSpec((tr, D), lambda i: (i, 0)),
        out_shape=jax.ShapeDtypeStruct((T, D), bf16),
        compiler_params=_cp(("parallel",)),
    )(z, vg, ws, bs_t)


def gm_gate_bwd(z, vg, ws, bs_t, dgated, name):
    T, D2 = z.shape
    D = D2 // 2
    G = D // GM_GROUP
    tr = _pick(T, (256, 128))
    nc = tr // CHUNK

    def body(z_ref, vg_ref, ws_ref, bs_ref, dg_ref, dz_ref, dws_ref, dbs_ref, dvg_ref,
             du_s, dvn_s):
        @pl.when(pl.program_id(0) == 0)
        def _():
            dws_ref[...] = jnp.zeros_like(dws_ref)
            dbs_ref[...] = jnp.zeros_like(dbs_ref)
            dvg_ref[...] = jnp.zeros_like(dvg_ref)

        u, du_dz = _gelu_and_grad(z_ref[:, :D])
        v, dv_dz = _gelu_and_grad(z_ref[:, D:])
        rv = lax.rsqrt(jnp.mean(v * v, axis=-1, keepdims=True) + EPS)
        vhat = v * rv
        vg_v = vg_ref[...]
        vn = (vhat * vg_v).astype(bf16)
        rr = lax.broadcasted_iota(jnp.int32, (CHUNK, CHUNK), 0)
        cc = lax.broadcasted_iota(jnp.int32, (CHUNK, CHUNK), 1)
        for g in range(G):
            wg = _tril_bf16(ws_ref[g])
            bg = bs_ref[:, g:g + 1]
            cs = slice(g * GM_GROUP, (g + 1) * GM_GROUP)
            dw_acc = jnp.zeros((CHUNK, CHUNK), f32)
            db_acc = jnp.zeros((CHUNK, 1), f32)
            for c in range(nc):
                rs = slice(c * CHUNK, (c + 1) * CHUNK)
                vb = vn[rs, cs]
                sv = jnp.dot(wg, vb, preferred_element_type=f32) + bg
                dgb = dg_ref[rs, cs]
                du_s[rs, cs] = dgb * sv
                dsv = dgb * u[rs, cs]
                dsv_b = dsv.astype(bf16)
                dw_acc += lax.dot_general(dsv_b, vb, (((1,), (1,)), ((), ())),
                                          preferred_element_type=f32)
                db_acc += jnp.sum(dsv, axis=1, keepdims=True)
                dvn_s[rs, cs] = lax.dot_general(wg, dsv_b, (((0,), (0,)), ((), ())),
                                                preferred_element_type=f32)
            dws_ref[g] += jnp.where(rr >= cc, dw_acc, 0.0)
            dbs_ref[:, g:g + 1] += db_acc
        dz_ref[:, :D] = (du_s[...] * du_dz).astype(bf16)
        dvn = dvn_s[...]
        dvg_ref[...] += jnp.sum(dvn * vhat, axis=0, keepdims=True)
        dvh = dvn * vg_v
        dv = rv * (dvh - vhat * jnp.mean(dvh * vhat, axis=-1, keepdims=True))
        dz_ref[:, D:] = (dv * dv_dz).astype(bf16)

    return pl.pallas_call(
        body, name=name, grid=(T // tr,),
        in_specs=[pl.BlockSpec((tr, D2), lambda i: (i, 0)),
                  pl.BlockSpec((1, D), lambda i: (0, 0)),
                  pl.BlockSpec((G, CHUNK, CHUNK), lambda i: (0, 0, 0)),
                  pl.BlockSpec((CHUNK, G), lambda i: (0, 0)),
                  pl.BlockSpec((tr, D), lambda i: (i, 0))],
        out_specs=[pl.BlockSpec((tr, D2), lambda i: (i, 0)),
                   pl.BlockSpec((G, CHUNK, CHUNK), lambda i: (0, 0, 0)),
                   pl.BlockSpec((CHUNK, G), lambda i: (0, 0)),
                   pl.BlockSpec((1, D), lambda i: (0, 0))],
        out_shape=[jax.ShapeDtypeStruct((T, D2), bf16),
                   jax.ShapeDtypeStruct((G, CHUNK, CHUNK), f32),
                   jax.ShapeDtypeStruct((CHUNK, G), f32),
                   jax.ShapeDtypeStruct((1, D), f32)],
        scratch_shapes=[pltpu.VMEM((tr, D), f32), pltpu.VMEM((tr, D), f32)],
        compiler_params=_cp(("arbitrary",)),
    )(z, vg, ws, bs_t, dgated)


def fox_gates_fwd(flog_t, b_col, name):
    H, T = flog_t.shape

    def body(fl_ref, b_ref, o_ref):
        xv = fl_ref[...] + b_ref[...]
        lf = jnp.minimum(xv, 0.0) - jnp.log1p(jnp.exp(-jnp.abs(xv)))
        lane = lax.broadcasted_iota(jnp.int32, (H, T), 1)
        s = 1
        while s < T:
            lf = lf + jnp.where(lane >= s, pltpu.roll(lf, s, 1), 0.0)
            s *= 2
        o_ref[...] = lf * LOG2E

    return pl.pallas_call(
        body, name=name, out_shape=jax.ShapeDtypeStruct((H, T), f32),
        compiler_params=pltpu.CompilerParams(vmem_limit_bytes=VMEM_LIMIT_BYTES),
    )(flog_t, b_col)


def fox_gates_bwd(flog_t, b_col, dF, name):
    H, T = flog_t.shape

    def body(fl_ref, b_ref, d_ref, o_ref, db_ref):
        xv = fl_ref[...] + b_ref[...]
        g = d_ref[...]
        lane = lax.broadcasted_iota(jnp.int32, (H, T), 1)
        s = 1
        while s < T:
            g = g + jnp.where(lane < T - s, pltpu.roll(g, T - s, 1), 0.0)
            s *= 2
        dfl = g * jax.nn.sigmoid(-xv)
        o_ref[...] = dfl
        db_ref[...] = jnp.sum(dfl, axis=1, keepdims=True)

    return pl.pallas_call(
        body, name=name,
        out_shape=[jax.ShapeDtypeStruct((H, T), f32), jax.ShapeDtypeStruct((H, 1), f32)],
        compiler_params=pltpu.CompilerParams(vmem_limit_bytes=VMEM_LIMIT_BYTES),
    )(flog_t, b_col, dF)


_NT = (((1,), (1,)), ((), ()))
_TN = (((0,), (0,)), ((), ()))


def _scores(q, k, fq, fk, diagonal):
    s = lax.dot_general(q, k, _NT, preferred_element_type=f32) * (HEAD_DIM ** -0.5 * LOG2E)
    s = s + fq - fk
    if diagonal:
        rows = lax.broadcasted_iota(jnp.int32, s.shape, 0)
        cols = lax.broadcasted_iota(jnp.int32, s.shape, 1)
        s = jnp.where(cols <= rows, s, NEG)
    return s


def fox_attn_fwd(qkv, f_row, f_col, name, riders=None):
    T, D3 = qkv.shape
    D = D3 // 3
    H = D // HEAD_DIM
    tq = _pick(T, (ATT_TILE, 256, 128))
    nq = T // tq
    rd = Riders(riders)
    pairs = [(i, j) for i in range(nq) for j in range(i + 1)]
    i_tab = np.array([p[0] for p in pairs], np.int32)
    j_tab = np.array([p[1] for p in pairs], np.int32)

    def body(i_ref, j_ref, *refs):
        in_refs, out_refs, scratch_refs = refs[:5 + len(rd.exs)], refs[5 + len(rd.exs):7 + 2 * len(rd.exs)], refs[7 + 2 * len(rd.exs):]
        rd.split(in_refs, out_refs, scratch_refs)
        q_ref, k_ref, v_ref, fq_ref, fk_ref = in_refs[:5]
        o_ref, lse_ref = out_refs[:2]
        m_s, l_s, acc_s = scratch_refs[:3]
        t = pl.program_id(1)
        i, j = i_ref[t], j_ref[t]
        step = pl.program_id(0) * len(pairs) + t
        rd.before(step)

        @pl.when(j == 0)
        def _():
            m_s[...] = jnp.full_like(m_s, NEG)
            l_s[...] = jnp.zeros_like(l_s)
            acc_s[...] = jnp.zeros_like(acc_s)

        def update(masked):
            s = _scores(q_ref[...], k_ref[...], fq_ref[0], fk_ref[0], masked)
            m_prev = m_s[...]
            m_new = jnp.maximum(m_prev, jnp.max(s, axis=1, keepdims=True))
            alpha = jnp.exp2(m_prev - m_new)
            p = jnp.exp2(s - m_new)
            l_s[...] = alpha * l_s[...] + jnp.sum(p, axis=1, keepdims=True)
            acc_s[...] = alpha * acc_s[...] + jnp.dot(p.astype(bf16), v_ref[...],
                                                      preferred_element_type=f32)
            m_s[...] = m_new

        @pl.when(j < i)
        def _():
            update(False)

        @pl.when(j == i)
        def _():
            update(True)
            o_ref[...] = (acc_s[...] / l_s[...]).astype(bf16)
            lse_ref[0] = m_s[...] + jnp.log2(l_s[...])

        rd.after(step, H * len(pairs))

    blk = (tq, HEAD_DIM)
    grid_spec = pltpu.PrefetchScalarGridSpec(
        num_scalar_prefetch=2, grid=(H, len(pairs)),
        in_specs=[pl.BlockSpec(blk, lambda h, t, it, jt: (it[t], h)),
                  pl.BlockSpec(blk, lambda h, t, it, jt: (jt[t], H + h)),
                  pl.BlockSpec(blk, lambda h, t, it, jt: (jt[t], 2 * H + h)),
                  pl.BlockSpec((1, tq, 1), lambda h, t, it, jt: (h, it[t], 0)),
                  pl.BlockSpec((1, 1, tq), lambda h, t, it, jt: (h, 0, jt[t]))] + rd.in_specs,
        out_specs=[pl.BlockSpec(blk, lambda h, t, it, jt: (it[t], h)),
                   pl.BlockSpec((1, tq, 1), lambda h, t, it, jt: (h, it[t], 0))] + rd.out_specs,
        scratch_shapes=[pltpu.VMEM((tq, 1), f32), pltpu.VMEM((tq, 1), f32),
                        pltpu.VMEM((tq, HEAD_DIM), f32)] + rd.scratch)
    return pl.pallas_call(
        body, name=name, grid_spec=grid_spec,
        out_shape=[jax.ShapeDtypeStruct((T, D), bf16), jax.ShapeDtypeStruct((H, T, 1), f32)] + rd.out_shape,
        compiler_params=_cp(("arbitrary", "arbitrary") if rd.exs else ("parallel", "arbitrary")),
    )(i_tab, j_tab, qkv, qkv, qkv, f_col, f_row, *rd.args)


def fox_attn_bwd(qkv, o, do, lse, f_row, f_col, name):
    T, D3 = qkv.shape
    D = D3 // 3
    H = D // HEAD_DIM
    t = _pick(T, (ATT_TILE, 256, 128))
    n = T // t
    scale = HEAD_DIM ** -0.5

    pairs = [(j, i) for j in range(n) for i in range(j, n)]
    j_tab = np.array([p[0] for p in pairs], np.int32)
    i_tab = np.array([p[1] for p in pairs], np.int32)

    def body(j_ref, i_ref, q_ref, k_ref, v_ref, o_ref, do_ref, lse_ref, fq_ref, fk_ref,
             dq_ref, dk_ref, dv_ref, cs_ref, rs_ref, dk_s, dv_s, dq_s, di_s):
        step = pl.program_id(1)
        j, i = j_ref[step], i_ref[step]
        rows = pl.ds(pl.multiple_of(i * t, t), t)

        @pl.when(step == 0)
        def _():
            dq_s[...] = jnp.zeros_like(dq_s)
            rs_ref[...] = jnp.zeros_like(rs_ref)

        @pl.when(j == 0)
        def _():
            di_s[rows, :] = jnp.sum(do_ref[...] * o_ref[...].astype(f32), axis=1, keepdims=True)

        @pl.when(i == j)
        def _():
            dk_s[...] = jnp.zeros_like(dk_s)
            dv_s[...] = jnp.zeros_like(dv_s)
            cs_ref[...] = jnp.zeros_like(cs_ref)

        def accumulate(diagonal):
            q = q_ref[...]
            k = k_ref[...]
            s = _scores(q, k, fq_ref[0], fk_ref[0], diagonal)
            p = jnp.exp2(s - lse_ref[0])
            do_b = do_ref[...].astype(bf16)
            dp = lax.dot_general(do_b, v_ref[...], _NT, preferred_element_type=f32)
            ds = p * (dp - di_s[rows, :])
            ds_b = (ds * scale).astype(bf16)
            cs_ref[0] += jnp.sum(ds, axis=0, keepdims=True)
            rs_ref[0, rows, :] += jnp.sum(ds, axis=1, keepdims=True)
            dv_s[...] += lax.dot_general(p.astype(bf16), do_b, _TN, preferred_element_type=f32)
            dk_s[...] += lax.dot_general(ds_b, q, _TN, preferred_element_type=f32)
            dq_s[rows, :] += jnp.dot(ds_b, k, preferred_element_type=f32)

        @pl.when(i == j)
        def _():
            accumulate(True)

        @pl.when(i > j)
        def _():
            accumulate(False)

        @pl.when(i == n - 1)
        def _():
            dk_ref[...] = dk_s[...].astype(bf16)
            dv_ref[...] = dv_s[...].astype(bf16)

        @pl.when(step == len(pairs) - 1)
        def _():
            dq_ref[...] = dq_s[...].astype(bf16)

    blk = (t, HEAD_DIM)
    at_q = lambda h, s, jt, it: (it[s], h)
    col_q = pl.BlockSpec((1, t, 1), lambda h, s, jt, it: (h, it[s], 0))
    row_k = pl.BlockSpec((1, 1, t), lambda h, s, jt, it: (h, 0, jt[s]))
    grid_spec = pltpu.PrefetchScalarGridSpec(
        num_scalar_prefetch=2, grid=(H, len(pairs)),
        in_specs=[pl.BlockSpec(blk, at_q),
                  pl.BlockSpec(blk, lambda h, s, jt, it: (jt[s], H + h)),
                  pl.BlockSpec(blk, lambda h, s, jt, it: (jt[s], 2 * H + h)),
                  pl.BlockSpec(blk, at_q), pl.BlockSpec(blk, at_q), col_q, col_q, row_k],
        out_specs=[pl.BlockSpec((T, HEAD_DIM), lambda h, s, jt, it: (0, h)),
                   pl.BlockSpec(blk, lambda h, s, jt, it: (jt[s], h)),
                   pl.BlockSpec(blk, lambda h, s, jt, it: (jt[s], h)),
                   row_k,
                   pl.BlockSpec((1, T, 1), lambda h, s, jt, it: (h, 0, 0))],
        scratch_shapes=[pltpu.VMEM(blk, f32), pltpu.VMEM(blk, f32),
                        pltpu.VMEM((T, HEAD_DIM), f32), pltpu.VMEM((T, 1), f32)])
    return pl.pallas_call(
        body, name=name, grid_spec=grid_spec,
        out_shape=[jax.ShapeDtypeStruct((T, D), bf16), jax.ShapeDtypeStruct((T, D), bf16),
                   jax.ShapeDtypeStruct((T, D), bf16), jax.ShapeDtypeStruct((H, 1, T), f32),
                   jax.ShapeDtypeStruct((H, T, 1), f32)],
        compiler_params=_cp(("parallel", "arbitrary")),
    )(j_tab, i_tab, qkv, qkv, qkv, o, do, lse, f_col, f_row)


def mod_fwd(c16, mod_w, mod_b_loc, name):
    L, D, MW = mod_w.shape
    tn = _pick(MW, (512, 256, 128))

    def body(c_ref, w_ref, b_ref, o_ref):
        cv = c_ref[...]
        ca = (cv * jax.nn.sigmoid(cv)).astype(bf16)
        o_ref[...] = jnp.dot(ca, w_ref[...].astype(bf16), preferred_element_type=f32) + b_ref[...]

    return pl.pallas_call(
        body, name=name, grid=(L, MW // tn),
        in_specs=[pl.BlockSpec((16, D), lambda l, j: (0, 0)),
                  pl.BlockSpec((None, D, tn), lambda l, j: (l, 0, j)),
                  pl.BlockSpec((None, 1, tn), lambda l, j: (l, 0, j))],
        out_specs=pl.BlockSpec((None, 16, tn), lambda l, j: (l, 0, j)),
        out_shape=jax.ShapeDtypeStruct((L, 16, MW), f32),
        compiler_params=_cp(("parallel", "parallel")),
    )(c16, mod_w, mod_b_loc)


def mod_w_bwd(c_t, dmod, name):
    D = c_t.shape[0]
    L, _, MW = dmod.shape
    tn = _pick(MW, (512, 256, 128))

    def body(c_ref, d_ref, o_ref):
        cv = c_ref[...]
        ca = (cv * jax.nn.sigmoid(cv)).astype(bf16)
        o_ref[...] = jnp.dot(ca, d_ref[...].astype(bf16), preferred_element_type=f32)

    return pl.pallas_call(
        body, name=name, grid=(L, MW // tn),
        in_specs=[pl.BlockSpec((D, LANES), lambda l, j: (0, 0)),
                  pl.BlockSpec((None, LANES, tn), lambda l, j: (l, 0, j))],
        out_specs=pl.BlockSpec((None, D, tn), lambda l, j: (l, 0, j)),
        out_shape=jax.ShapeDtypeStruct((L, D, MW), f32),
        compiler_params=_cp(("parallel", "parallel")),
    )(c_t, dmod)


def adamw(w, g, m, v, name):
    shape = w.shape
    C = shape[-1] if w.ndim >= 1 else 1
    R = max(w.size // C, 1)
    w2, g2, m2, v2 = (t.reshape(R, C) for t in (w, g, m, v))
    tr = R
    for cand in (2048, 1024, 512, 256, 128, 64, 32, 16, 8):
        if R % cand == 0 and cand * _round_up(C, LANES) <= 256 * 1024:
            tr = cand
            break

    def body(w_ref, g_ref, m_ref, v_ref, d_ref, mo_ref, vo_ref):
        gv = g_ref[...]
        mn = ADAM_B1 * m_ref[...] + (1.0 - ADAM_B1) * gv
        vn = ADAM_B2 * v_ref[...] + (1.0 - ADAM_B2) * (gv * gv)
        m_hat = mn / (1.0 - ADAM_B1 ** ADAM_STEP)
        v_hat = vn / (1.0 - ADAM_B2 ** ADAM_STEP)
        d_ref[...] = -ADAM_LR * (m_hat / (jnp.sqrt(v_hat) + ADAM_EPS) + ADAM_WD * w_ref[...])
        mo_ref[...] = mn
        vo_ref[...] = vn

    spec = pl.BlockSpec((tr, C), lambda i: (i, 0))
    sds = jax.ShapeDtypeStruct((R, C), f32)
    d, mn, vn = pl.pallas_call(
        body, name=name, grid=(R // tr,), in_specs=[spec] * 4, out_specs=[spec] * 3,
        out_shape=[sds, sds, sds], compiler_params=_cp(("parallel",)),
    )(w2, g2, m2, v2)
    return d.reshape(shape), mn.reshape(shape), vn.reshape(shape)


def reduce_scatter_tail(pair, tag):
    quad = run_exchange(chip_exchange(pair), "rs_chip_exchange_" + tag)
    return sum_slots(quad, "rs_final_sum_" + tag)


def kernel(x, c, mod_w, mod_b, mix_norm_g, ffn_norm_g, attn_w_in, attn_b_f, attn_w_o, gm_w_in, gm_v_g, gm_w_s, gm_b_s, gm_w_o, ffn_w_in, ffn_conv_w, ffn_conv_b, ffn_w_out, final_g, loss_target, m_mod_w, m_mod_b, m_mix_norm_g, m_ffn_norm_g, m_attn_w_in, m_attn_b_f, m_attn_w_o, m_gm_w_in, m_gm_v_g, m_gm_w_s, m_gm_b_s, m_gm_w_o, m_ffn_w_in, m_ffn_conv_w, m_ffn_conv_b, m_ffn_w_out, m_final_g, v_mod_w, v_mod_b, v_mix_norm_g, v_ffn_norm_g, v_attn_w_in, v_attn_b_f, v_attn_w_o, v_gm_w_in, v_gm_v_g, v_gm_w_s, v_gm_b_s, v_gm_w_o, v_ffn_w_in, v_ffn_conv_w, v_ffn_conv_b, v_ffn_w_out, v_final_g):
    xi, yi, ci = lax.axis_index("x"), lax.axis_index("y"), lax.axis_index("c")
    me = 4 * xi + 2 * yi + ci

    _, T, D = x.shape
    L = mod_w.shape[0]
    MW = mod_w.shape[2]
    NA, _, QW = attn_w_in.shape
    NB = gm_w_in.shape[0]
    H = D // HEAD_DIM
    G = D // GM_GROUP
    DR = attn_w_o.shape[1]
    GW = gm_w_in.shape[2]
    FW = ffn_w_in.shape[2]
    FR = ffn_w_out.shape[1]
    FRP = _round_up(FR, LANES // 2)
    FWP = 2 * FRP
    FP = N_CHIPS * FWP
    DFF2 = N_DEV * FW
    assert 2 * FR == FW and N_DEV * QW == 3 * D + H and N_DEV * GW == 2 * D
    c_idx = ci.reshape(1).astype(jnp.int32)

    def pad_ff(t, axis, blocks):
        ax = axis % t.ndim
        t = t.reshape(t.shape[:ax] + (blocks, FR) + t.shape[ax + 1:])
        pad = [(0, 0)] * t.ndim
        pad[ax + 1] = (0, FRP - FR)
        t = jnp.pad(t, pad)
        return t.reshape(t.shape[:ax] + (blocks * FRP,) + t.shape[ax + 2:])

    def unpad_ff(t, axis, blocks):
        ax = axis % t.ndim
        t = t.reshape(t.shape[:ax] + (blocks, FRP) + t.shape[ax + 1:])
        t = lax.slice_in_dim(t, 0, FR, axis=ax + 1)
        return t.reshape(t.shape[:ax] + (blocks * FR,) + t.shape[ax + 2:])

    x0 = x[0]
    tgt = loss_target[0]

    c_all = all_gather(c, "gather_c").reshape(N_DEV, D)
    cw_loc = pad_ff(ffn_conv_w, 2, 2).reshape(L * CONV_W, FWP)
    conv_w_full = all_gather(cw_loc, "gather_conv_w").transpose(1, 0, 2).reshape(L, CONV_W, 2 * FP)
    vg_full = all_gather(gm_v_g, "gather_vg").transpose(1, 0, 2).reshape(NB, 1, D)
    conv_b_full = pad_ff(ffn_conv_b, 1, 2 * N_DEV).reshape(L, 1, 2 * FP)

    c16 = jnp.pad(c_all, ((0, 16 - N_DEV), (0, 0)))
    mod_b_loc = lax.dynamic_slice_in_dim(mod_b, me * MW, MW, axis=1).reshape(L, 1, MW)
    mod_part = mod_fwd(c16, mod_w, mod_b_loc, "mod_fwd")[:, :N_DEV]
    mod_all = all_gather(mod_part, "gather_mod")
    mod_me = lax.dynamic_index_in_dim(mod_all, me, axis=2, keepdims=False)
    mod_me = mod_me.transpose(1, 0, 2).reshape(L, 6, 1, D)

    w_ai_t = jnp.swapaxes(attn_w_in, 1, 2).astype(bf16)
    w_gi_t = jnp.swapaxes(gm_w_in, 1, 2).astype(bf16)
    w_fi_t = pad_ff(jnp.swapaxes(ffn_w_in, 1, 2).astype(bf16), 1, 2)
    w_ao_l = attn_w_o.astype(bf16)
    w_go_l = gm_w_o.astype(bf16)
    w_fo_l = jnp.pad(ffn_w_out.astype(bf16), ((0, 0), (0, FRP - FR), (0, 0)))

    stash = []
    ffn_w = {}
    ahead_w = {}
    xc = x0
    for i in range(L):
        sh1, sc1, g1, sh2, sc2, g2 = (mod_me[i, k] for k in range(6))
        jm = i // 2
        st = {"x_in": xc}
        h = norm_mod_fwd(xc, mix_norm_g[i][None], sc1, sh1, "norm_mod_fwd")
        st["h"] = h
        mixer_w = ahead_w.pop(i, None)
        if mixer_w is None:
            mixer_w = {"mix_in": all_gather((w_ai_t if i % 2 == 0 else w_gi_t)[jm], "gather_mix_in"),
                       "mix_out": all_gather((w_ao_l if i % 2 == 0 else w_go_l)[jm], "gather_mix_out")}
        w_mo = mixer_w["mix_out"].reshape(D, D)
        if i % 2 == 0:
            w_in_t = mixer_w["mix_in"].reshape(N_DEV * QW, D)
            w_qkv_t = w_in_t[:3 * D]
            w_f_t = jnp.pad(w_in_t[3 * D:], ((0, LANES - H), (0, 0)))
            qkv = matmul(h, w_qkv_t, name="fox_qkv", tb=True, out_dtype=bf16)
            flog = matmul(h, w_f_t, name="fox_flog", tb=True)
            flog_t = flog[:, :H].T
            b_col = attn_b_f[jm][:, None]
            F = fox_gates_fwd(flog_t, b_col, "fox_gates_fwd")
            f_row, f_col = F[:, None, :], F[:, :, None]
            o, lse, g_fi, g_fo = fox_attn_fwd(qkv, f_row, f_col, "fox_attn_fwd",
                                              riders=[gather_exchange(w_fi_t[i]), gather_exchange(w_fo_l[i])])
            ffn_w[i] = (g_fi.reshape(2 * FP, D), g_fo.reshape(FP, D))
            x1, y = matmul(o, w_mo, name="mix_out", resid=xc, gvec=g1, emit_acc=True)
            st.update(qkv=qkv, flog_t=flog_t, b_col=b_col, f_row=f_row, f_col=f_col, o=o, lse=lse,
                      w_qkv_t=w_qkv_t, w_f_t=w_f_t, w_mo=w_mo)
        else:
            w_gi_full = mixer_w["mix_in"].reshape(2 * D, D)
            z = matmul(h, w_gi_full, name="gm_in", tb=True)
            bs_t = gm_b_s[jm].T
            gated = gm_gate_fwd(z, vg_full[jm], gm_w_s[jm], bs_t, "gm_gate_fwd")
            x1, y = matmul(gated, w_mo, name="mix_out", resid=xc, gvec=g1, emit_acc=True)
            st.update(z=z, bs_t=bs_t, gated=gated, w_gi_full=w_gi_full, w_mo=w_mo)
        st.update(y=y, x1=x1)
        h2 = norm_mod_fwd(x1, ffn_norm_g[i][None], sc2, sh2, "norm_mod_fwd")
        if i not in ffn_w:
            ffn_w[i] = (all_gather(w_fi_t[i], "gather_ffn_in").reshape(2 * FP, D),
                        all_gather(w_fo_l[i], "gather_ffn_out").reshape(FP, D))
        w_fi_full, w_fo_full = ffn_w[i]
        up_keys, conv_keys, down_keys = [], [], []
        nxt = i + 1
        if nxt < L and nxt % 2 == 1:
            up_keys = [("ffn_in", w_fi_t[nxt])]
            conv_keys = [("mix_in", w_gi_t[nxt // 2])]
            down_keys = [("ffn_out", w_fo_l[nxt]), ("mix_out", w_go_l[nxt // 2])]
        elif nxt < L:
            up_keys = [("mix_in", w_ai_t[nxt // 2])]
            down_keys = [("mix_out", w_ao_l[nxt // 2])]
        got = {}

        def hosting(keys, n_own, fn, *args, **kw):
            outs = fn(*args, riders=[gather_exchange(w) for _, w in keys], **kw)
            outs = list(outs) if isinstance(outs, (list, tuple)) else [outs]
            got.update({k: r for (k, _), r in zip(keys, outs[n_own:])})
            return outs[:n_own]

        a3, = hosting(up_keys, 1, matmul, h2, w_fi_full, name="ffn_up", tb=True, out_split=True)
        act, = hosting(conv_keys, 1, conv_glu_fwd, a3, conv_w_full[i], conv_b_full[i], "conv_glu_fwd")
        xc, f_out = hosting(down_keys, 2, matmul, act, w_fo_full, name="ffn_down", resid=x1, gvec=g2,
                            emit_acc=True)
        if "ffn_in" in got:
            ffn_w[nxt] = (got["ffn_in"].reshape(2 * FP, D), got["ffn_out"].reshape(FP, D))
        if "mix_in" in got:
            ahead_w[nxt] = got
        st.update(h2=h2, a3=a3, act=act, f=f_out, w_fi_full=w_fi_full, w_fo_full=w_fo_full)
        stash.append(st)

    loss_part, dx, d_final_g = loss_head(xc, final_g[None], tgt, "loss_head")
    loss = lax.psum(loss_part[0, 0], AXES)

    d_mod = [None] * L
    d_mix_g = [None] * L
    d_ffn_g = [None] * L
    d_conv_w = [None] * L
    d_conv_b = [None] * L
    g_wfi = [None] * L
    g_wfo = [None] * L
    g_wai = [None] * NA
    g_wao = [None] * NA
    d_bf = [None] * NA
    g_wgi = [None] * NB
    g_wgo = [None] * NB
    d_ws = [None] * NB
    d_bs = [None] * NB
    d_vg = [None] * NB
    ffn_pairs = None
    mix_pairs = None

    def half_reduce(g8, tag):
        got = run_exchange(pair_exchange(g8), "rs_pair_exchange_" + tag)
        return pair_sum(g8, got, c_idx, "rs_pair_sum_" + tag)

    def finish_mixer(layer, quad_mo, quad_in):
        g_mo = sum_slots(quad_mo, "rs_final_sum_mix_out")
        g_in = sum_slots(quad_in, "rs_final_sum_mix_in").T
        if layer % 2 == 0:
            g_wao[layer // 2], g_wai[layer // 2] = g_mo, g_in
        else:
            g_wgo[layer // 2], g_wgi[layer // 2] = g_mo, g_in

    for i in reversed(range(L)):
        st = stash[i]
        sh1, sc1, g1, sh2, sc2, g2 = (mod_me[i, k] for k in range(6))
        jm = i // 2
        dy, dg2 = gate_bwd(dx, st["f"], g2, "gate_bwd")
        if mix_pairs is None:
            dact = matmul(dy, st["w_fo_full"], name="ffn_down_dx", tb=True)
        else:
            dact, quad_mo = matmul(dy, st["w_fo_full"], name="ffn_down_dx", tb=True,
                                   riders=[chip_exchange(mix_pairs[1])])
        if ffn_pairs is None:
            dw_fo = matmul(st["act"], dy, name="ffn_down_dw", ta=True, out_dtype=bf16)
        else:
            dw_fo, quad = matmul(st["act"], dy, name="ffn_down_dw", ta=True, out_dtype=bf16,
                                 riders=[chip_exchange(ffn_pairs[0])])
            g_wfo[i + 1] = sum_slots(quad, "rs_final_sum_ffn_out")[:FR]
        g8_fo = dw_fo.reshape(N_DEV, FRP, D)
        da3, dwg, dwu, dbg, dbu, got_fo = conv_glu_bwd(st["a3"], conv_w_full[i], conv_b_full[i], dact,
                                                       "conv_glu_bwd", riders=[pair_exchange(g8_fo)])
        pair_fo = pair_sum(g8_fo, got_fo, c_idx, "rs_pair_sum_ffn_out")
        d_conv_w[i] = jnp.concatenate([dwg, dwu], axis=1)
        d_conv_b[i] = jnp.concatenate([dbg, dbu], axis=1)
        if ffn_pairs is None:
            dw_fi_t = matmul(da3, st["h2"], name="ffn_up_dw", ta=True, a_split=True, out_dtype=bf16)
        else:
            dw_fi_t, quad = matmul(da3, st["h2"], name="ffn_up_dw", ta=True, a_split=True, out_dtype=bf16,
                                   riders=[chip_exchange(ffn_pairs[1])])
            g_wfi[i + 1] = unpad_ff(sum_slots(quad, "rs_final_sum_ffn_in"), 0, 2).T
        g8_fi = dw_fi_t.reshape(N_DEV, FWP, D)
        if mix_pairs is None:
            dh2, got_fi = matmul(da3, st["w_fi_full"], name="ffn_up_dx", a_split=True,
                                 riders=[pair_exchange(g8_fi)])
        else:
            dh2, got_fi, quad_in = matmul(da3, st["w_fi_full"], name="ffn_up_dx", a_split=True,
                                          riders=[pair_exchange(g8_fi), chip_exchange(mix_pairs[2])])
            finish_mixer(mix_pairs[0], quad_mo, quad_in)
        ffn_pairs = (pair_fo, pair_sum(g8_fi, got_fi, c_idx, "rs_pair_sum_ffn_in"))
        dx, dsh2, dsc2, d_ffn_g[i] = norm_mod_bwd(st["x1"], ffn_norm_g[i][None], sc2, dh2, dx, "norm_mod_bwd")
        dy, dg1 = gate_bwd(dx, st["y"], g1, "gate_bwd")
        if i % 2 == 0:
            do = matmul(dy, st["w_mo"], name="mix_out_dx", tb=True)
            dw_mo = matmul(st["o"], dy, name="mix_out_dw", ta=True, out_dtype=bf16)
            dq, dk, dv, cs, rs = fox_attn_bwd(st["qkv"], st["o"], do, st["lse"], st["f_row"], st["f_col"], "fox_attn_bwd")
            dF = rs[:, :, 0] - cs[:, 0, :]
            dflog_t, d_bf[jm] = fox_gates_bwd(st["flog_t"], st["b_col"], dF, "fox_gates_bwd")
            dflog = jnp.pad(dflog_t.T, ((0, 0), (0, LANES - H))).astype(bf16)
            dqkv = jnp.concatenate([dq, dk, dv], axis=1)
            dw_qkv_t = matmul(dqkv, st["h"], name="fox_qkv_dw", ta=True, out_dtype=bf16)
            dw_f_t = matmul(dflog, st["h"], name="fox_flog_dw", ta=True, out_dtype=bf16)
            dw_in_t = jnp.concatenate([dw_qkv_t, dw_f_t[:H]], axis=0)
            dh = matmul(dqkv, st["w_qkv_t"], name="fox_qkv_dx")
            dh = matmul(dflog, st["w_f_t"], name="fox_flog_dx", resid=dh)
        else:
            dgated = matmul(dy, st["w_mo"], name="mix_out_dx", tb=True)
            dw_mo = matmul(st["gated"], dy, name="mix_out_dw", ta=True, out_dtype=bf16)
            dz, d_ws[jm], dbs_t, d_vg[jm] = gm_gate_bwd(st["z"], vg_full[jm], gm_w_s[jm], st["bs_t"], dgated, "gm_gate_bwd")
            d_bs[jm] = dbs_t.T
            dw_in_t = matmul(dz, st["h"], name="gm_in_dw", ta=True, out_dtype=bf16)
            dh = matmul(dz, st["w_gi_full"], name="gm_in_dx")
        mix_pairs = (i, half_reduce(dw_mo.reshape(N_DEV, DR, D), "mix_out"),
                     half_reduce(dw_in_t.reshape(N_DEV, -1, D), "mix_in"))
        dx, dsh1, dsc1, d_mix_g[i] = norm_mod_bwd(st["x_in"], mix_norm_g[i][None], sc1, dh, dx, "norm_mod_bwd")
        d_mod[i] = jnp.concatenate([dsh1, dsc1, dg1, dsh2, dsc2, dg2], axis=0)

    grad_x = dx[None]
    g_wfo[0] = reduce_scatter_tail(ffn_pairs[0], "ffn_out")[:FR]
    g_wfi[0] = unpad_ff(reduce_scatter_tail(ffn_pairs[1], "ffn_in"), 0, 2).T
    finish_mixer(mix_pairs[0], run_exchange(chip_exchange(mix_pairs[1]), "rs_chip_exchange_mix_out"),
                 run_exchange(chip_exchange(mix_pairs[2]), "rs_chip_exchange_mix_in"))
    grad_attn_w_in, grad_attn_w_o = jnp.stack(g_wai), jnp.stack(g_wao)
    grad_gm_w_in, grad_gm_w_o = jnp.stack(g_wgi), jnp.stack(g_wgo)
    grad_ffn_w_in, grad_ffn_w_out = jnp.stack(g_wfi), jnp.stack(g_wfo)

    def gathered_sum(rows, tag, mult=SUBLANES):
        n = rows.shape[0]
        rows = jnp.pad(rows, ((0, _round_up(n, mult) - n), (0, 0)))
        every = all_gather(rows, "gather_small_grads_" + tag)
        return every, sum_slots(every, "sum_small_grads_" + tag)

    rows_d = jnp.concatenate([jnp.concatenate(d_mod, axis=0), jnp.concatenate(d_mix_g, axis=0),
                              jnp.concatenate(d_ffn_g, axis=0), jnp.concatenate(d_vg, axis=0), d_final_g], axis=0)
    every_d, sum_d = gathered_sum(rows_d, "d")
    r0 = L * 6
    grad_mod_b = sum_d[:r0].reshape(L, 6 * D)
    grad_mix_g, grad_ffn_g = sum_d[r0:r0 + L], sum_d[r0 + L:r0 + 2 * L]
    grad_vg_full = sum_d[r0 + 2 * L:r0 + 2 * L + NB]
    grad_final_g = sum_d[r0 + 2 * L + NB]
    grad_vg = lax.dynamic_slice_in_dim(grad_vg_full, me * DR, DR, axis=1)

    rows_f = jnp.concatenate([jnp.concatenate(d_conv_w, axis=0), jnp.concatenate(d_conv_b, axis=0)], axis=0)
    _, sum_f = gathered_sum(rows_f, "f")
    sum_f = unpad_ff(sum_f, 1, 2 * N_DEV)
    grad_conv_w = lax.dynamic_slice_in_dim(sum_f[:L * CONV_W].reshape(L, CONV_W, DFF2), me * FW, FW, axis=2)
    grad_conv_b = sum_f[L * CONV_W:L * CONV_W + L]

    rows_c = jnp.concatenate([jnp.stack(d_ws).reshape(NB * G * CHUNK, CHUNK), jnp.stack(d_bs).reshape(NB * G, CHUNK),
                              jnp.pad(jnp.stack(d_bf).reshape(NA, H), ((0, 0), (0, LANES - H)))], axis=0)
    _, sum_c = gathered_sum(rows_c, "c", mult=SLOT_ROWS)
    n_ws = NB * G * CHUNK
    grad_ws = sum_c[:n_ws].reshape(NB, G, CHUNK, CHUNK)
    grad_bs = sum_c[n_ws:n_ws + NB * G].reshape(NB, G, CHUNK)
    grad_bf = sum_c[n_ws + NB * G:n_ws + NB * G + NA, :H]

    dmod_all = every_d[:, :r0].reshape(N_DEV, L, 6 * D)
    dmod_loc = lax.dynamic_slice_in_dim(dmod_all, me * MW, MW, axis=2).transpose(1, 0, 2)
    dmod_loc = jnp.pad(dmod_loc, ((0, 0), (0, LANES - N_DEV), (0, 0)))
    c_t = jnp.pad(c_all.T, ((0, 0), (0, LANES - N_DEV)))
    grad_mod_w = mod_w_bwd(c_t, dmod_loc, "mod_w_bwd")

    weights = [mod_w, mod_b, mix_norm_g, ffn_norm_g, attn_w_in, attn_b_f, attn_w_o, gm_w_in, gm_v_g, gm_w_s,
               gm_b_s, gm_w_o, ffn_w_in, ffn_conv_w, ffn_conv_b, ffn_w_out, final_g]
    grads = [grad_mod_w, grad_mod_b, grad_mix_g, grad_ffn_g, grad_attn_w_in, grad_bf, grad_attn_w_o,
             grad_gm_w_in, grad_vg, grad_ws, grad_bs, grad_gm_w_o, grad_ffn_w_in, grad_conv_w, grad_conv_b,
             grad_ffn_w_out, grad_final_g]
    ms = [m_mod_w, m_mod_b, m_mix_norm_g, m_ffn_norm_g, m_attn_w_in, m_attn_b_f, m_attn_w_o, m_gm_w_in, m_gm_v_g,
          m_gm_w_s, m_gm_b_s, m_gm_w_o, m_ffn_w_in, m_ffn_conv_w, m_ffn_conv_b, m_ffn_w_out, m_final_g]
    vs = [v_mod_w, v_mod_b, v_mix_norm_g, v_ffn_norm_g, v_attn_w_in, v_attn_b_f, v_attn_w_o, v_gm_w_in, v_gm_v_g,
          v_gm_w_s, v_gm_b_s, v_gm_w_o, v_ffn_w_in, v_ffn_conv_w, v_ffn_conv_b, v_ffn_w_out, v_final_g]
    deltas, new_ms, new_vs = [], [], []
    for w, g, m_, v_ in zip(weights, grads, ms, vs):
        d_, mn_, vn_ = adamw(w, g, m_, v_, "adamw")
        deltas.append(d_)
        new_ms.append(mn_)
        new_vs.append(vn_)

    return (loss, grad_x, *grads, *deltas, *new_ms, *new_vs)
```

```python
import numpy as np
import jax
import jax.numpy as jnp
from jax import lax
from jax.experimental import pallas as pl
from jax.experimental.pallas import tpu as pltpu

f32 = jnp.float32
bf16 = jnp.bfloat16

AXES = ("x", "y", "c")
N_DEV = 8
N_CHIPS = 4
LANES = 128
SUBLANES = 8
HEAD_DIM = 128
CHUNK = 128
GM_GROUP = 128
CONV_W = 3
EPS = 1e-6
NEG = -1e30
VMEM_LIMIT_BYTES = 56 * 1024 * 1024
MATMUL_VMEM_BYTES = 40 * 1024 * 1024
SLOT_ROWS = 512
LOG2E = 1.4426950408889634
ATT_TILE = 512

ADAM_LR = 0.001
ADAM_B1 = 0.9
ADAM_B2 = 0.999
ADAM_EPS = 1e-08
ADAM_WD = 0.01
ADAM_STEP = 10

MESH = pl.DeviceIdType.MESH
ANY = pl.BlockSpec(memory_space=pl.ANY)


def _cp(sem):
    return pltpu.CompilerParams(dimension_semantics=sem, vmem_limit_bytes=VMEM_LIMIT_BYTES)


def _pick(n, prefs):
    for p in prefs:
        if n % p == 0:
            return p
    return n


def _round_up(n, m):
    return (n + m - 1) // m * m


class Exchange:
    def __init__(self, src, out_shape, scratch, phases):
        self.src = src
        self.out_shape = out_shape
        self.scratch = scratch
        self.phases = phases


def gather_exchange(xl):
    def phases(x_ref, out_ref, send_sems, recv_sems, local_sem):
        x, y, c = lax.axis_index("x"), lax.axis_index("y"), lax.axis_index("c")
        me, sibling = (x, y, c), (x, y, 1 - c)
        chips = [(1 - x, y), (x, 1 - y), (1 - x, 1 - y)]

        def slot(px, py, pc):
            return out_ref.at[4 * px + 2 * py + pc]

        def copy(k, block, to, src=None):
            return pltpu.make_async_remote_copy(
                src_ref=slot(*block) if src is None else src, dst_ref=slot(*block),
                send_sem=send_sems.at[k], recv_sem=recv_sems.at[k],
                device_id=to, device_id_type=MESH)

        mine = pltpu.make_async_copy(x_ref, slot(*me), local_sem)
        first = [copy(0, me, sibling, src=x_ref)]
        first += [copy(1 + j, me, (*chip, c), src=x_ref) for j, chip in enumerate(chips)]
        passed = [copy(4 + j, (*chip, c), sibling) for j, chip in enumerate(chips)]

        def start():
            mine.start()
            for cp in first:
                cp.start()

        def hand_on():
            for j, chip in enumerate(chips):
                copy(1 + j, (*chip, c), me).wait_recv()
                passed[j].start()

        def finish():
            copy(0, sibling, me).wait_recv()
            for j, chip in enumerate(chips):
                copy(4 + j, (*chip, 1 - c), me).wait_recv()
            for cp in first + passed:
                cp.wait_send()
            mine.wait()

        return start, hand_on, finish

    return Exchange(xl, jax.ShapeDtypeStruct((N_DEV,) + xl.shape, xl.dtype),
                    [pltpu.SemaphoreType.DMA((7,)), pltpu.SemaphoreType.DMA((7,)),
                     pltpu.SemaphoreType.DMA], phases)


def pair_exchange(g8):
    _, R, W = g8.shape

    def phases(g_ref, out_ref, send_sems, recv_sems):
        x, y, c = lax.axis_index("x"), lax.axis_index("y"), lax.axis_index("c")
        copies = [pltpu.make_async_remote_copy(
            src_ref=g_ref.at[2 * q + (1 - c)], dst_ref=out_ref.at[q],
            send_sem=send_sems.at[q], recv_sem=recv_sems.at[q],
            device_id=(x, y, 1 - c), device_id_type=MESH) for q in range(N_CHIPS)]

        def start():
            for cp in copies:
                cp.start()

        def finish():
            for cp in copies:
                cp.wait()

        return start, None, finish

    return Exchange(g8, jax.ShapeDtypeStruct((N_CHIPS, R, W), g8.dtype),
                    [pltpu.SemaphoreType.DMA((N_CHIPS,)), pltpu.SemaphoreType.DMA((N_CHIPS,))], phases)


def chip_exchange(p4):
    def phases(p_ref, out_ref, send_sems, recv_sems, local_sem):
        x, y, c = lax.axis_index("x"), lax.axis_index("y"), lax.axis_index("c")
        my_q = 2 * x + y
        chips = [(1 - x, y), (x, 1 - y), (1 - x, 1 - y)]
        mine = pltpu.make_async_copy(p_ref.at[my_q], out_ref.at[my_q], local_sem)
        copies = [pltpu.make_async_remote_copy(
            src_ref=p_ref.at[2 * px + py], dst_ref=out_ref.at[my_q],
            send_sem=send_sems.at[k], recv_sem=recv_sems.at[k],
            device_id=(px, py, c), device_id_type=MESH) for k, (px, py) in enumerate(chips)]

        def start():
            mine.start()
            for cp in copies:
                cp.start()

        def finish():
            for k, (px, py) in enumerate(chips):
                pltpu.make_async_remote_copy(
                    src_ref=p_ref.at[my_q], dst_ref=out_ref.at[2 * px + py],
                    send_sem=send_sems.at[k], recv_sem=recv_sems.at[k],
                    device_id=(px, py, c), device_id_type=MESH).wait_recv()
            for cp in copies:
                cp.wait_send()
            mine.wait()

        return start, None, finish

    return Exchange(p4, jax.ShapeDtypeStruct(p4.shape, p4.dtype),
                    [pltpu.SemaphoreType.DMA((3,)), pltpu.SemaphoreType.DMA((3,)),
                     pltpu.SemaphoreType.DMA], phases)


def run_exchange(ex, name):
    def body(src_ref, out_ref, *sems):
        start, hand_on, finish = ex.phases(src_ref, out_ref, *sems)
        start()
        if hand_on is not None:
            hand_on()
        finish()

    return pl.pallas_call(body, name=name, out_shape=ex.out_shape, in_specs=[ANY], out_specs=ANY,
                          scratch_shapes=ex.scratch)(ex.src)


def all_gather(xl, name):
    return run_exchange(gather_exchange(xl), name)


class Riders:
    def __init__(self, exchanges):
        self.exs = list(exchanges or [])
        self.in_specs = [ANY] * len(self.exs)
        self.args = [ex.src for ex in self.exs]
        self.out_specs = [ANY] * len(self.exs)
        self.out_shape = [ex.out_shape for ex in self.exs]
        self.scratch = [s for ex in self.exs for s in ex.scratch]

    def split(self, in_refs, out_refs, scratch_refs):
        n = len(self.exs)
        self.refs = []
        pos = len(scratch_refs) - len(self.scratch)
        for k, ex in enumerate(self.exs):
            sems = scratch_refs[pos:pos + len(ex.scratch)]
            pos += len(ex.scratch)
            self.refs.append((in_refs[len(in_refs) - n + k], out_refs[len(out_refs) - n + k], sems))

    def _parts(self):
        return [ex.phases(src, out, *sems) for ex, (src, out, sems) in zip(self.exs, self.refs)]

    def before(self, step):
        if not self.exs:
            return
        parts = self._parts()

        @pl.when(step == 0)
        def _():
            for start, _, _ in parts:
                start()

    def after(self, step, n_steps):
        if not self.exs:
            return
        parts = self._parts()
        mid = (3 * n_steps) // 4

        if any(h is not None for _, h, _ in parts):
            @pl.when(step == mid)
            def _():
                for _, hand_on, _ in parts:
                    if hand_on is not None:
                        hand_on()

        @pl.when(step == n_steps - 1)
        def _():
            for _, _, finish in parts:
                finish()


def pair_sum(g8, got4, c_idx, name):
    _, R, W = g8.shape
    tr = _pick(R, (512, 256, 128, 64, 32, 16))
    g5 = g8.reshape(N_CHIPS, 2, R, W)

    def body(c_ref, a_ref, b_ref, o_ref):
        o_ref[...] = (a_ref[...].astype(f32) + b_ref[...].astype(f32)).astype(o_ref.dtype)

    grid_spec = pltpu.PrefetchScalarGridSpec(
        num_scalar_prefetch=1, grid=(N_CHIPS, R // tr),
        in_specs=[pl.BlockSpec((None, None, tr, W), lambda q, r, cr: (q, cr[0], r, 0)),
                  pl.BlockSpec((None, tr, W), lambda q, r, cr: (q, r, 0))],
        out_specs=pl.BlockSpec((None, tr, W), lambda q, r, cr: (q, r, 0)))
    return pl.pallas_call(
        body, name=name, grid_spec=grid_spec,
        out_shape=jax.ShapeDtypeStruct((N_CHIPS, R, W), g8.dtype),
        compiler_params=_cp(("parallel", "parallel")),
    )(c_idx, g5, got4)


def sum_slots(xs, name, out_dtype=f32):
    S, R, W = xs.shape
    tr = _pick(R, (512, 256, 128, 64, 32, 16, 8))

    def body(x_ref, o_ref):
        acc = x_ref[0].astype(f32)
        for s in range(1, S):
            acc = acc + x_ref[s].astype(f32)
        o_ref[...] = acc.astype(o_ref.dtype)

    return pl.pallas_call(
        body, name=name, grid=(R // tr,),
        in_specs=[pl.BlockSpec((S, tr, W), lambda r: (0, r, 0))],
        out_specs=pl.BlockSpec((tr, W), lambda r: (r, 0)),
        out_shape=jax.ShapeDtypeStruct((R, W), out_dtype),
        compiler_params=_cp(("parallel",)),
    )(xs)


def matmul(a, b, *, name, ta=False, tb=False, a_split=False, b_split=False, out_split=False,
           out_dtype=f32, resid=None, gvec=None, emit_acc=False, riders=None):
    rd = Riders(riders)
    if a_split:
        rows, cols = a.shape[1], 2 * a.shape[2]
        M, K = (cols, rows) if ta else (rows, cols)
    else:
        M, K = (a.shape[1], a.shape[0]) if ta else a.shape
    if b_split:
        assert not tb
        N = 2 * b.shape[2]
        assert b.shape[1] == K
    else:
        N = b.shape[0] if tb else b.shape[1]
        assert (b.shape[1] if tb else b.shape[0]) == K, (a.shape, b.shape, name)

    m_split = a_split and ta
    k_split = a_split and not ta
    n_split = b_split or out_split
    tm = _pick(M // 2 if m_split else M, (1024, 512, 256, 128, 64, 32, 16, 8))
    k_len, n_len = (K // 2 if k_split else K), (N // 2 if n_split else N)
    out_bytes = jnp.dtype(out_dtype).itemsize + (4 if resid is not None else 0) + (2 if emit_acc else 0)

    def fits(tk_, tn_):
        operands = 2 * (tm * tk_ * a.dtype.itemsize + tk_ * tn_ * b.dtype.itemsize)
        acc = tm * tn_ * 4 if tk_ < K else 0
        return operands + acc + 2 * tm * tn_ * out_bytes <= MATMUL_VMEM_BYTES

    tk_options = [d for d in range(k_len, 0, -LANES) if k_len % d == 0 and d % LANES == 0]
    tn_options = [t for t in (1024, 512, 256, 128) if n_len % t == 0]
    tk, tn = next(((tk_, tn_) for tn_min in (512, 128) for tk_ in tk_options for tn_ in tn_options
                   if tn_ >= tn_min and fits(tk_, tn_)), (tk_options[-1], tn_options[-1]))
    nk = K // tk
    n_half = (N // 2) // tn if n_split else 0
    k_half = (K // 2) // tk if k_split else 0
    m_half = (M // 2) // tm if m_split else 0

    if m_split:
        a_spec = pl.BlockSpec((None, tk, tm), lambda i, j, k: (i // m_half, k, i % m_half))
    elif k_split:
        a_spec = pl.BlockSpec((None, tm, tk), lambda i, j, k: (k // k_half, i, k % k_half))
    elif ta:
        a_spec = pl.BlockSpec((tk, tm), lambda i, j, k: (k, i))
    else:
        a_spec = pl.BlockSpec((tm, tk), lambda i, j, k: (i, k))
    if b_split:
        b_spec = pl.BlockSpec((None, tk, tn), lambda i, j, k: (j // n_half, k, j % n_half))
    elif tb:
        b_spec = pl.BlockSpec((tn, tk), lambda i, j, k: (j, k))
    else:
        b_spec = pl.BlockSpec((tk, tn), lambda i, j, k: (k, j))
    if out_split:
        o_spec = pl.BlockSpec((None, tm, tn), lambda i, j, k: (j // n_half, i, j % n_half))
        o_shape = (2, M, N // 2)
    else:
        o_spec = pl.BlockSpec((tm, tn), lambda i, j, k: (i, j))
        o_shape = (M, N)

    in_specs = [a_spec, b_spec]
    args = [a, b]
    if resid is not None:
        in_specs.append(pl.BlockSpec((tm, tn), lambda i, j, k: (i, j)))
        args.append(resid)
    if gvec is not None:
        in_specs.append(pl.BlockSpec((1, tn), lambda i, j, k: (0, j)))
        args.append(gvec)
    out_specs = [o_spec]
    out_shape = [jax.ShapeDtypeStruct(o_shape, out_dtype)]
    if emit_acc:
        out_specs.append(pl.BlockSpec((tm, tn), lambda i, j, k: (i, j)))
        out_shape.append(jax.ShapeDtypeStruct((M, N), bf16))
    dims = (((0 if ta else 1,), (1 if tb else 0,)), ((), ()))
    has_r, has_g = resid is not None, gvec is not None
    n_in, n_out = len(in_specs) + len(rd.exs), len(out_specs) + len(rd.exs)
    grid = (M // tm, N // tn, nk)
    n_steps = grid[0] * grid[1] * grid[2]

    def body(*refs):
        in_refs, out_refs, scratch_refs = refs[:n_in], refs[n_in:n_in + n_out], refs[n_in + n_out:]
        rd.split(in_refs, out_refs, scratch_refs)
        a_ref, b_ref = in_refs[0], in_refs[1]
        pos = 2
        r_ref = g_ref = None
        if has_r:
            r_ref = in_refs[pos]
            pos += 1
        if has_g:
            g_ref = in_refs[pos]
        o_ref = out_refs[0]
        y_ref = out_refs[1] if emit_acc else None
        step = (pl.program_id(0) * grid[1] + pl.program_id(1)) * nk + pl.program_id(2)
        rd.before(step)

        def finish(acc):
            if emit_acc:
                y_ref[...] = acc.astype(bf16)
            if has_g:
                acc = acc * g_ref[...]
            if has_r:
                acc = r_ref[...] + acc
            o_ref[...] = acc.astype(o_ref.dtype)

        part = lax.dot_general(a_ref[...].astype(bf16), b_ref[...].astype(bf16), dims,
                               preferred_element_type=f32)
        if nk == 1:
            finish(part)
        else:
            acc_ref = scratch_refs[0]
            k = pl.program_id(2)

            @pl.when(k == 0)
            def _():
                acc_ref[...] = part

            @pl.when(k > 0)
            def _():
                acc_ref[...] += part

            @pl.when(k == nk - 1)
            def _():
                finish(acc_ref[...])

        rd.after(step, n_steps)

    outs = pl.pallas_call(
        body, name=name, grid=grid,
        in_specs=in_specs + rd.in_specs, out_specs=out_specs + rd.out_specs,
        out_shape=out_shape + rd.out_shape,
        scratch_shapes=([pltpu.VMEM((tm, tn), f32)] if nk > 1 else []) + rd.scratch,
        compiler_params=_cp(("arbitrary",) * 3 if rd.exs else ("parallel", "parallel", "arbitrary")),
    )(*args, *rd.args)
    return outs if (emit_acc or rd.exs) else outs[0]


def _rows(T):
    return _pick(T, (256, 128, 64, 32, 16, 8))


def norm_mod_fwd(x, gn, sc, sh, name):
    T, D = x.shape
    tr = _rows(T)

    def body(x_ref, gn_ref, sc_ref, sh_ref, h_ref):
        xv = x_ref[...]
        r = lax.rsqrt(jnp.mean(xv * xv, axis=-1, keepdims=True) + EPS)
        y = (xv * r) * gn_ref[...]
        h_ref[...] = (y * (1.0 + sc_ref[...]) + sh_ref[...]).astype(bf16)

    vec = pl.BlockSpec((1, D), lambda i: (0, 0))
    row = pl.BlockSpec((tr, D), lambda i: (i, 0))
    return pl.pallas_call(
        body, name=name, grid=(T // tr,), in_specs=[row, vec, vec, vec], out_specs=row,
        out_shape=jax.ShapeDtypeStruct((T, D), bf16), compiler_params=_cp(("parallel",)),
    )(x, gn, sc, sh)


def norm_mod_bwd(x, gn, sc, dh, dx_res, name):
    T, D = x.shape
    tr = _rows(T)

    def body(x_ref, gn_ref, sc_ref, dh_ref, dr_ref, dx_ref, dsh_ref, dsc_ref, dgn_ref):
        @pl.when(pl.program_id(0) == 0)
        def _():
            dsh_ref[...] = jnp.zeros_like(dsh_ref)
            dsc_ref[...] = jnp.zeros_like(dsc_ref)
            dgn_ref[...] = jnp.zeros_like(dgn_ref)

        xv = x_ref[...]
        r = lax.rsqrt(jnp.mean(xv * xv, axis=-1, keepdims=True) + EPS)
        xn = xv * r
        gn_v = gn_ref[...]
        dh_v = dh_ref[...]
        dsh_ref[...] += jnp.sum(dh_v, axis=0, keepdims=True)
        dsc_ref[...] += jnp.sum(dh_v * (xn * gn_v), axis=0, keepdims=True)
        dy = dh_v * (1.0 + sc_ref[...])
        dgn_ref[...] += jnp.sum(dy * xn, axis=0, keepdims=True)
        dxn = dy * gn_v
        dx = r * (dxn - xn * jnp.mean(dxn * xn, axis=-1, keepdims=True))
        dx_ref[...] = dr_ref[...] + dx

    vec = pl.BlockSpec((1, D), lambda i: (0, 0))
    row = pl.BlockSpec((tr, D), lambda i: (i, 0))
    vshape = jax.ShapeDtypeStruct((1, D), f32)
    return pl.pallas_call(
        body, name=name, grid=(T // tr,), in_specs=[row, vec, vec, row, row],
        out_specs=[row, vec, vec, vec],
        out_shape=[jax.ShapeDtypeStruct((T, D), f32), vshape, vshape, vshape],
        compiler_params=_cp(("arbitrary",)),
    )(x, gn, sc, dh, dx_res)


def gate_bwd(dx, y, g, name):
    T, D = dx.shape
    tr = _rows(T)

    def body(dx_ref, y_ref, g_ref, dy_ref, dg_ref):
        @pl.when(pl.program_id(0) == 0)
        def _():
            dg_ref[...] = jnp.zeros_like(dg_ref)

        dxv = dx_ref[...]
        dy_ref[...] = (dxv * g_ref[...]).astype(bf16)
        dg_ref[...] += jnp.sum(dxv * y_ref[...].astype(f32), axis=0, keepdims=True)

    vec = pl.BlockSpec((1, D), lambda i: (0, 0))
    row = pl.BlockSpec((tr, D), lambda i: (i, 0))
    return pl.pallas_call(
        body, name=name, grid=(T // tr,), in_specs=[row, row, vec], out_specs=[row, vec],
        out_shape=[jax.ShapeDtypeStruct((T, D), bf16), jax.ShapeDtypeStruct((1, D), f32)],
        compiler_params=_cp(("arbitrary",)),
    )(dx, y, g)


def loss_head(x, fg, tgt, name):
    T, D = x.shape
    tr = _rows(T)

    def body(x_ref, fg_ref, t_ref, loss_ref, dx_ref, dfg_ref):
        @pl.when(pl.program_id(0) == 0)
        def _():
            loss_ref[...] = jnp.zeros_like(loss_ref)
            dfg_ref[...] = jnp.zeros_like(dfg_ref)

        xv = x_ref[...]
        r = lax.rsqrt(jnp.mean(xv * xv, axis=-1, keepdims=True) + EPS)
        xn = xv * r
        fg_v = fg_ref[...]
        err = xn * fg_v - t_ref[...]
        per_tok = jnp.mean(err * err, axis=-1, keepdims=True)
        loss_ref[...] += 0.5 * jnp.sum(per_tok, axis=0, keepdims=True)
        dy = err * (1.0 / D)
        dfg_ref[...] += jnp.sum(dy * xn, axis=0, keepdims=True)
        dxn = dy * fg_v
        dx_ref[...] = r * (dxn - xn * jnp.mean(dxn * xn, axis=-1, keepdims=True))

    vec = pl.BlockSpec((1, D), lambda i: (0, 0))
    row = pl.BlockSpec((tr, D), lambda i: (i, 0))
    one = pl.BlockSpec((1, 1), lambda i: (0, 0))
    return pl.pallas_call(
        body, name=name, grid=(T // tr,), in_specs=[row, vec, row], out_specs=[one, row, vec],
        out_shape=[jax.ShapeDtypeStruct((1, 1), f32), jax.ShapeDtypeStruct((T, D), f32),
                   jax.ShapeDtypeStruct((1, D), f32)],
        compiler_params=_cp(("arbitrary",)),
    )(x, fg, tgt)


def _conv_tiles(T, FP):
    return _pick(T, (512, 256, 128, 64, 32, 16, 8)), _pick(FP, (512, 256, 128))


def _conv_specs(tr, tc, T, FP):
    nj = FP // tc
    r8 = tr // SUBLANES
    last8 = T // SUBLANES - 1
    main = pl.BlockSpec((2, tr, tc), lambda j, i: (0, i, j))
    prev = pl.BlockSpec((2, SUBLANES, tc), lambda j, i: (0, jnp.maximum(i * r8 - 1, 0), j))
    nxt = pl.BlockSpec((2, SUBLANES, tc), lambda j, i: (0, jnp.minimum((i + 1) * r8, last8), j))
    wg = pl.BlockSpec((CONV_W, tc), lambda j, i: (0, j))
    wu = pl.BlockSpec((CONV_W, tc), lambda j, i: (0, j + nj))
    bg = pl.BlockSpec((1, tc), lambda j, i: (0, j))
    bu = pl.BlockSpec((1, tc), lambda j, i: (0, j + nj))
    return main, prev, nxt, wg, wu, bg, bu


def _causal_taps(av, hp_ref, s, has_prev, row):
    h7 = jnp.where(has_prev, hp_ref[s, 7:8, :], 0.0)
    h6 = jnp.where(has_prev, hp_ref[s, 6:7, :], 0.0)
    m1 = jnp.where(row == 0, h7, pltpu.roll(av, 1, 0))
    m2 = jnp.where(row == 0, h6, jnp.where(row == 1, h7, pltpu.roll(av, 2, 0)))
    return m1, m2


def conv_glu_fwd(a3, conv_w, conv_b, name, riders=None):
    _, T, FP = a3.shape
    tr, tc = _conv_tiles(T, FP)
    main, prev, _, wg, wu, bg, bu = _conv_specs(tr, tc, T, FP)
    rd = Riders(riders)
    n_ex = len(rd.exs)
    ni = T // tr

    def body(*refs):
        in_refs, out_refs, scratch_refs = refs[:6 + n_ex], refs[6 + n_ex:7 + 2 * n_ex], refs[7 + 2 * n_ex:]
        rd.split(in_refs, out_refs, scratch_refs)
        a_ref, hp_ref, wg_ref, wu_ref, bg_ref, bu_ref = in_refs[:6]
        act_ref = out_refs[0]
        step = pl.program_id(0) * ni + pl.program_id(1)
        rd.before(step)
        has_prev = pl.program_id(1) > 0
        row = lax.broadcasted_iota(jnp.int32, (tr, tc), 0)

        def conv(s, w_ref, b_ref):
            av = a_ref[s]
            m1, m2 = _causal_taps(av, hp_ref, s, has_prev, row)
            return w_ref[0:1, :] * m2 + w_ref[1:2, :] * m1 + w_ref[2:3, :] * av + b_ref[...]

        gate = conv(0, wg_ref, bg_ref)
        up = conv(1, wu_ref, bu_ref)
        act_ref[...] = ((gate * jax.nn.sigmoid(gate)) * up).astype(bf16)
        rd.after(step, (FP // tc) * ni)

    outs = pl.pallas_call(
        body, name=name, grid=(FP // tc, ni),
        in_specs=[main, prev, wg, wu, bg, bu] + rd.in_specs,
        out_specs=[pl.BlockSpec((tr, tc), lambda j, i: (i, j))] + rd.out_specs,
        out_shape=[jax.ShapeDtypeStruct((T, FP), bf16)] + rd.out_shape,
        scratch_shapes=rd.scratch,
        compiler_params=_cp(("arbitrary", "arbitrary") if rd.exs else ("parallel", "parallel")),
    )(a3, a3, conv_w, conv_w, conv_b, conv_b, *rd.args)
    return outs if rd.exs else outs[0]


def conv_glu_bwd(a3, conv_w, conv_b, dact, name, riders=None):
    _, T, FP = a3.shape
    tr, tc = _conv_tiles(T, FP)
    ni = T // tr
    main, prev, nxt, wg, wu, bg, bu = _conv_specs(tr, tc, T, FP)
    r8 = tr // SUBLANES
    last8 = T // SUBLANES - 1
    d_main = pl.BlockSpec((tr, tc), lambda j, i: (i, j))
    d_next = pl.BlockSpec((SUBLANES, tc), lambda j, i: (jnp.minimum((i + 1) * r8, last8), j))
    rd = Riders(riders)
    n_ex = len(rd.exs)

    def body(*refs):
        in_refs, out_refs, scratch_refs = refs[:9 + n_ex], refs[9 + n_ex:14 + 2 * n_ex], refs[14 + 2 * n_ex:]
        rd.split(in_refs, out_refs, scratch_refs)
        a_ref, hp_ref, hn_ref, d_ref, dn_ref, wg_ref, wu_ref, bg_ref, bu_ref = in_refs[:9]
        da_ref, dwg_ref, dwu_ref, dbg_ref, dbu_ref = out_refs[:5]
        i = pl.program_id(1)
        step = pl.program_id(0) * ni + i
        rd.before(step)
        has_prev = i > 0
        has_next = i < ni - 1
        row = lax.broadcasted_iota(jnp.int32, (tr, tc), 0)
        row8 = lax.broadcasted_iota(jnp.int32, (SUBLANES, tc), 0)

        @pl.when(i == 0)
        def _():
            dwg_ref[...] = jnp.zeros_like(dwg_ref)
            dwu_ref[...] = jnp.zeros_like(dwu_ref)
            dbg_ref[...] = jnp.zeros_like(dbg_ref)
            dbu_ref[...] = jnp.zeros_like(dbu_ref)

        def prep(s, w_ref, b_ref):
            av = a_ref[s]
            m1, m2 = _causal_taps(av, hp_ref, s, has_prev, row)
            w0, w1, w2, bv = w_ref[0:1, :], w_ref[1:2, :], w_ref[2:3, :], b_ref[...]
            pre = w0 * m2 + w1 * m1 + w2 * av + bv
            an = hn_ref[s]
            l1 = a_ref[s, tr - 1:tr, :]
            l2 = a_ref[s, tr - 2:tr - 1, :]
            n1 = jnp.where(row8 == 0, l1, pltpu.roll(an, 1, 0))
            n2 = jnp.where(row8 == 0, l2, jnp.where(row8 == 1, l1, pltpu.roll(an, 2, 0)))
            pre_n = w0 * n2 + w1 * n1 + w2 * an + bv
            return av, m1, m2, pre, pre_n

        def glu_bwd(gate, up, d):
            sg = jax.nn.sigmoid(gate)
            dgate = d * up * (sg * (1.0 + gate * (1.0 - sg)))
            dup = d * (gate * sg)
            return dgate, dup

        def row_of(v8, r):
            return jnp.sum(jnp.where(row8 == r, v8, 0.0), axis=0, keepdims=True)

        def back(dc, dc_n, w_ref):
            n0, n1 = row_of(dc_n, 0), row_of(dc_n, 1)
            p1 = jnp.where(row == tr - 1, n0, pltpu.roll(dc, tr - 1, 0))
            p2 = jnp.where(row == tr - 1, n1, jnp.where(row == tr - 2, n0, pltpu.roll(dc, tr - 2, 0)))
            return w_ref[2:3, :] * dc + w_ref[1:2, :] * p1 + w_ref[0:1, :] * p2

        def tok_sum(v):
            return jnp.sum(v, axis=0, keepdims=True)

        ag, g1, g2, gate, gate_n = prep(0, wg_ref, bg_ref)
        au, u1, u2, up, up_n = prep(1, wu_ref, bu_ref)
        dcg, dcu = glu_bwd(gate, up, d_ref[...])
        dn = jnp.where(has_next, dn_ref[...], 0.0)
        dcg_n, dcu_n = glu_bwd(gate_n, up_n, dn)
        da_ref[0] = back(dcg, dcg_n, wg_ref).astype(bf16)
        da_ref[1] = back(dcu, dcu_n, wu_ref).astype(bf16)
        dwg_ref[0:1, :] += tok_sum(dcg * g2)
        dwg_ref[1:2, :] += tok_sum(dcg * g1)
        dwg_ref[2:3, :] += tok_sum(dcg * ag)
        dwu_ref[0:1, :] += tok_sum(dcu * u2)
        dwu_ref[1:2, :] += tok_sum(dcu * u1)
        dwu_ref[2:3, :] += tok_sum(dcu * au)
        dbg_ref[...] += tok_sum(dcg)
        dbu_ref[...] += tok_sum(dcu)
        rd.after(step, (FP // tc) * ni)

    w_out = pl.BlockSpec((CONV_W, tc), lambda j, i: (0, j))
    b_out = pl.BlockSpec((1, tc), lambda j, i: (0, j))
    return pl.pallas_call(
        body, name=name, grid=(FP // tc, ni),
        in_specs=[main, prev, nxt, d_main, d_next, wg, wu, bg, bu] + rd.in_specs,
        out_specs=[main, w_out, w_out, b_out, b_out] + rd.out_specs,
        out_shape=[jax.ShapeDtypeStruct((2, T, FP), bf16),
                   jax.ShapeDtypeStruct((CONV_W, FP), f32), jax.ShapeDtypeStruct((CONV_W, FP), f32),
                   jax.ShapeDtypeStruct((1, FP), f32), jax.ShapeDtypeStruct((1, FP), f32)] + rd.out_shape,
        scratch_shapes=rd.scratch,
        compiler_params=_cp(("arbitrary", "arbitrary") if rd.exs else ("parallel", "arbitrary")),
    )(a3, a3, a3, dact, dact, conv_w, conv_w, conv_b, conv_b, *rd.args)


_GELU_C = 0.7978845608028654
_GELU_A = 0.044715


def _gelu(x):
    return 0.5 * x * (1.0 + jnp.tanh(_GELU_C * (x + _GELU_A * (x * x * x))))


def _gelu_and_grad(x):
    t = jnp.tanh(_GELU_C * (x + _GELU_A * (x * x * x)))
    g = 0.5 * x * (1.0 + t)
    dg = 0.5 * (1.0 + t) + 0.5 * x * (1.0 - t * t) * (_GELU_C * (1.0 + 3.0 * _GELU_A * (x * x)))
    return g, dg


def _tril_bf16(w):
    r = lax.broadcasted_iota(jnp.int32, w.shape, 0)
    c = lax.broadcasted_iota(jnp.int32, w.shape, 1)
    return jnp.where(r >= c, w, 0.0).astype(bf16)


def gm_gate_fwd(z, vg, ws, bs_t, name):
    T, D2 = z.shape
    D = D2 // 2
    G = D // GM_GROUP
    tr = _pick(T, (256, 128))
    nc = tr // CHUNK

    def body(z_ref, vg_ref, ws_ref, bs_ref, o_ref):
        u = _gelu(z_ref[:, :D])
        v = _gelu(z_ref[:, D:])
        rv = lax.rsqrt(jnp.mean(v * v, axis=-1, keepdims=True) + EPS)
        vn = ((v * rv) * vg_ref[...]).astype(bf16)
        for g in range(G):
            wg = _tril_bf16(ws_ref[g])
            bg = bs_ref[:, g:g + 1]
            cs = slice(g * GM_GROUP, (g + 1) * GM_GROUP)
            for c in range(nc):
                rs = slice(c * CHUNK, (c + 1) * CHUNK)
                sv = jnp.dot(wg, vn[rs, cs], preferred_element_type=f32) + bg
                o_ref[rs, cs] = (u[rs, cs] * sv).astype(bf16)

    return pl.pallas_call(
        body, name=name, grid=(T // tr,),
        in_specs=[pl.BlockSpec((tr, D2), lambda i: (i, 0)),
                  pl.BlockSpec((1, D), lambda i: (0, 0)),
                  pl.BlockSpec((G, CHUNK, CHUNK), lambda i: (0, 0, 0)),
                  pl.BlockSpec((CHUNK, G), lambda i: (0, 0))],
        out_specs=pl.BlockSpec((tr, D), lambda i: (i, 0)),
        out_shape=jax.ShapeDtypeStruct((T, D), bf16),
        compiler_params=_cp(("parallel",)),
    )(z, vg, ws, bs_t)


def gm_gate_bwd(z, vg, ws, bs_t, dgated, name):
    T, D2 = z.shape
    D = D2 // 2
    G = D // GM_GROUP
    tr = _pick(T, (256, 128))
    nc = tr // CHUNK

    def body(z_ref, vg_ref, ws_ref, bs_ref, dg_ref, dz_ref, dws_ref, dbs_ref, dvg_ref,
             du_s, dvn_s):
        @pl.when(pl.program_id(0) == 0)
        def _():
            dws_ref[...] = jnp.zeros_like(dws_ref)
            dbs_ref[...] = jnp.zeros_like(dbs_ref)
            dvg_ref[...] = jnp.zeros_like(dvg_ref)

        u, du_dz = _gelu_and_grad(z_ref[:, :D])
        v, dv_dz = _gelu_and_grad(z_ref[:, D:])
        rv = lax.rsqrt(jnp.mean(v * v, axis=-1, keepdims=True) + EPS)
        vhat = v * rv
        vg_v = vg_ref[...]
        vn = (vhat * vg_v).astype(bf16)
        rr = lax.broadcasted_iota(jnp.int32, (CHUNK, CHUNK), 0)
        cc = lax.broadcasted_iota(jnp.int32, (CHUNK, CHUNK), 1)
        for g in range(G):
            wg = _tril_bf16(ws_ref[g])
            bg = bs_ref[:, g:g + 1]
            cs = slice(g * GM_GROUP, (g + 1) * GM_GROUP)
            dw_acc = jnp.zeros((CHUNK, CHUNK), f32)
            db_acc = jnp.zeros((CHUNK, 1), f32)
            for c in range(nc):
                rs = slice(c * CHUNK, (c + 1) * CHUNK)
                vb = vn[rs, cs]
                sv = jnp.dot(wg, vb, preferred_element_type=f32) + bg
                dgb = dg_ref[rs, cs]
                du_s[rs, cs] = dgb * sv
                dsv = dgb * u[rs, cs]
                dsv_b = dsv.astype(bf16)
                dw_acc += lax.dot_general(dsv_b, vb, (((1,), (1,)), ((), ())),
                                          preferred_element_type=f32)
                db_acc += jnp.sum(dsv, axis=1, keepdims=True)
                dvn_s[rs, cs] = lax.dot_general(wg, dsv_b, (((0,), (0,)), ((), ())),
                                                preferred_element_type=f32)
            dws_ref[g] += jnp.where(rr >= cc, dw_acc, 0.0)
            dbs_ref[:, g:g + 1] += db_acc
        dz_ref[:, :D] = (du_s[...] * du_dz).astype(bf16)
        dvn = dvn_s[...]
        dvg_ref[...] += jnp.sum(dvn * vhat, axis=0, keepdims=True)
        dvh = dvn * vg_v
        dv = rv * (dvh - vhat * jnp.mean(dvh * vhat, axis=-1, keepdims=True))
        dz_ref[:, D:] = (dv * dv_dz).astype(bf16)

    return pl.pallas_call(
        body, name=name, grid=(T // tr,),
        in_specs=[pl.BlockSpec((tr, D2), lambda i: (i, 0)),
                  pl.BlockSpec((1, D), lambda i: (0, 0)),
                  pl.BlockSpec((G, CHUNK, CHUNK), lambda i: (0, 0, 0)),
                  pl.BlockSpec((CHUNK, G), lambda i: (0, 0)),
                  pl.BlockSpec((tr, D), lambda i: (i, 0))],
        out_specs=[pl.BlockSpec((tr, D2), lambda i: (i, 0)),
                   pl.BlockSpec((G, CHUNK, CHUNK), lambda i: (0, 0, 0)),
                   pl.BlockSpec((CHUNK, G), lambda i: (0, 0)),
                   pl.BlockSpec((1, D), lambda i: (0, 0))],
        out_shape=[jax.ShapeDtypeStruct((T, D2), bf16),
                   jax.ShapeDtypeStruct((G, CHUNK, CHUNK), f32),
                   jax.ShapeDtypeStruct((CHUNK, G), f32),
                   jax.ShapeDtypeStruct((1, D), f32)],
        scratch_shapes=[pltpu.VMEM((tr, D), f32), pltpu.VMEM((tr, D), f32)],
        compiler_params=_cp(("arbitrary",)),
    )(z, vg, ws, bs_t, dgated)


def fox_gates_fwd(flog_t, b_col, name):
    H, T = flog_t.shape

    def body(fl_ref, b_ref, o_ref):
        xv = fl_ref[...] + b_ref[...]
        lf = jnp.minimum(xv, 0.0) - jnp.log1p(jnp.exp(-jnp.abs(xv)))
        lane = lax.broadcasted_iota(jnp.int32, (H, T), 1)
        s = 1
        while s < T:
            lf = lf + jnp.where(lane >= s, pltpu.roll(lf, s, 1), 0.0)
            s *= 2
        o_ref[...] = lf * LOG2E

    return pl.pallas_call(
        body, name=name, out_shape=jax.ShapeDtypeStruct((H, T), f32),
        compiler_params=pltpu.CompilerParams(vmem_limit_bytes=VMEM_LIMIT_BYTES),
    )(flog_t, b_col)


def fox_gates_bwd(flog_t, b_col, dF, name):
    H, T = flog_t.shape

    def body(fl_ref, b_ref, d_ref, o_ref, db_ref):
        xv = fl_ref[...] + b_ref[...]
        g = d_ref[...]
        lane = lax.broadcasted_iota(jnp.int32, (H, T), 1)
        s = 1
        while s < T:
            g = g + jnp.where(lane < T - s, pltpu.roll(g, T - s, 1), 0.0)
            s *= 2
        dfl = g * jax.nn.sigmoid(-xv)
        o_ref[...] = dfl
        db_ref[...] = jnp.sum(dfl, axis=1, keepdims=True)

    return pl.pallas_call(
        body, name=name,
        out_shape=[jax.ShapeDtypeStruct((H, T), f32), jax.ShapeDtypeStruct((H, 1), f32)],
        compiler_params=pltpu.CompilerParams(vmem_limit_bytes=VMEM_LIMIT_BYTES),
    )(flog_t, b_col, dF)


_NT = (((1,), (1,)), ((), ()))
_TN = (((0,), (0,)), ((), ()))


def _scores(q, k, fq, fk, diagonal):
    s = lax.dot_general(q, k, _NT, preferred_element_type=f32) * (HEAD_DIM ** -0.5 * LOG2E)
    s = s + fq - fk
    if diagonal:
        rows = lax.broadcasted_iota(jnp.int32, s.shape, 0)
        cols = lax.broadcasted_iota(jnp.int32, s.shape, 1)
        s = jnp.where(cols <= rows, s, NEG)
    return s


def fox_attn_fwd(qkv, f_row, f_col, name, riders=None):
    T, D3 = qkv.shape
    D = D3 // 3
    H = D // HEAD_DIM
    tq = _pick(T, (ATT_TILE, 256, 128))
    nq = T // tq
    rd = Riders(riders)
    pairs = [(i, j) for i in range(nq) for j in range(i + 1)]
    i_tab = np.array([p[0] for p in pairs], np.int32)
    j_tab = np.array([p[1] for p in pairs], np.int32)

    def body(i_ref, j_ref, *refs):
        in_refs, out_refs, scratch_refs = refs[:5 + len(rd.exs)], refs[5 + len(rd.exs):7 + 2 * len(rd.exs)], refs[7 + 2 * len(rd.exs):]
        rd.split(in_refs, out_refs, scratch_refs)
        q_ref, k_ref, v_ref, fq_ref, fk_ref = in_refs[:5]
        o_ref, lse_ref = out_refs[:2]
        m_s, l_s, acc_s = scratch_refs[:3]
        t = pl.program_id(1)
        i, j = i_ref[t], j_ref[t]
        step = pl.program_id(0) * len(pairs) + t
        rd.before(step)

        @pl.when(j == 0)
        def _():
            m_s[...] = jnp.full_like(m_s, NEG)
            l_s[...] = jnp.zeros_like(l_s)
            acc_s[...] = jnp.zeros_like(acc_s)

        def update(masked):
            s = _scores(q_ref[...], k_ref[...], fq_ref[0], fk_ref[0], masked)
            m_prev = m_s[...]
            m_new = jnp.maximum(m_prev, jnp.max(s, axis=1, keepdims=True))
            alpha = jnp.exp2(m_prev - m_new)
            p = jnp.exp2(s - m_new)
            l_s[...] = alpha * l_s[...] + jnp.sum(p, axis=1, keepdims=True)
            acc_s[...] = alpha * acc_s[...] + jnp.dot(p.astype(bf16), v_ref[...],
                                                      preferred_element_type=f32)
            m_s[...] = m_new

        @pl.when(j < i)
        def _():
            update(False)

        @pl.when(j == i)
        def _():
            update(True)
            o_ref[...] = (acc_s[...] / l_s[...]).astype(bf16)
            lse_ref[0] = m_s[...] + jnp.log2(l_s[...])

        rd.after(step, H * len(pairs))

    blk = (tq, HEAD_DIM)
    grid_spec = pltpu.PrefetchScalarGridSpec(
        num_scalar_prefetch=2, grid=(H, len(pairs)),
        in_specs=[pl.BlockSpec(blk, lambda h, t, it, jt: (it[t], h)),
                  pl.BlockSpec(blk, lambda h, t, it, jt: (jt[t], H + h)),
                  pl.BlockSpec(blk, lambda h, t, it, jt: (jt[t], 2 * H + h)),
                  pl.BlockSpec((1, tq, 1), lambda h, t, it, jt: (h, it[t], 0)),
                  pl.BlockSpec((1, 1, tq), lambda h, t, it, jt: (h, 0, jt[t]))] + rd.in_specs,
        out_specs=[pl.BlockSpec(blk, lambda h, t, it, jt: (it[t], h)),
                   pl.BlockSpec((1, tq, 1), lambda h, t, it, jt: (h, it[t], 0))] + rd.out_specs,
        scratch_shapes=[pltpu.VMEM((tq, 1), f32), pltpu.VMEM((tq, 1), f32),
                        pltpu.VMEM((tq, HEAD_DIM), f32)] + rd.scratch)
    return pl.pallas_call(
        body, name=name, grid_spec=grid_spec,
        out_shape=[jax.ShapeDtypeStruct((T, D), bf16), jax.ShapeDtypeStruct((H, T, 1), f32)] + rd.out_shape,
        compiler_params=_cp(("arbitrary", "arbitrary") if rd.exs else ("parallel", "arbitrary")),
    )(i_tab, j_tab, qkv, qkv, qkv, f_col, f_row, *rd.args)


def fox_attn_bwd(qkv, o, do, lse, f_row, f_col, name, riders=None):
    T, D3 = qkv.shape
    D = D3 // 3
    H = D // HEAD_DIM
    t = _pick(T, (ATT_TILE, 256, 128))
    n = T // t
    scale = HEAD_DIM ** -0.5

    pairs = [(j, i) for j in range(n) for i in range(j, n)]
    j_tab = np.array([p[0] for p in pairs], np.int32)
    i_tab = np.array([p[1] for p in pairs], np.int32)

    rd = Riders(riders)
    n_ex = len(rd.exs)

    def body(j_ref, i_ref, *refs):
        in_refs, out_refs, scratch_refs = refs[:8 + n_ex], refs[8 + n_ex:13 + 2 * n_ex], refs[13 + 2 * n_ex:]
        rd.split(in_refs, out_refs, scratch_refs)
        q_ref, k_ref, v_ref, o_ref, do_ref, lse_ref, fq_ref, fk_ref = in_refs[:8]
        dq_ref, dk_ref, dv_ref, cs_ref, rs_ref = out_refs[:5]
        dk_s, dv_s, dq_s, di_s = scratch_refs[:4]
        step = pl.program_id(1)
        rd.before(pl.program_id(0) * len(pairs) + step)
        j, i = j_ref[step], i_ref[step]
        rows = pl.ds(pl.multiple_of(i * t, t), t)

        @pl.when(step == 0)
        def _():
            dq_s[...] = jnp.zeros_like(dq_s)
            rs_ref[...] = jnp.zeros_like(rs_ref)

        @pl.when(j == 0)
        def _():
            di_s[rows, :] = jnp.sum(do_ref[...] * o_ref[...].astype(f32), axis=1, keepdims=True)

        @pl.when(i == j)
        def _():
            dk_s[...] = jnp.zeros_like(dk_s)
            dv_s[...] = jnp.zeros_like(dv_s)
            cs_ref[...] = jnp.zeros_like(cs_ref)

        def accumulate(diagonal):
            q = q_ref[...]
            k = k_ref[...]
            s = _scores(q, k, fq_ref[0], fk_ref[0], diagonal)
            p = jnp.exp2(s - lse_ref[0])
            do_b = do_ref[...].astype(bf16)
            dp = lax.dot_general(do_b, v_ref[...], _NT, preferred_element_type=f32)
            ds = p * (dp - di_s[rows, :])
            ds_b = (ds * scale).astype(bf16)
            cs_ref[0] += jnp.sum(ds, axis=0, keepdims=True)
            rs_ref[0, rows, :] += jnp.sum(ds, axis=1, keepdims=True)
            dv_s[...] += lax.dot_general(p.astype(bf16), do_b, _TN, preferred_element_type=f32)
            dk_s[...] += lax.dot_general(ds_b, q, _TN, preferred_element_type=f32)
            dq_s[rows, :] += jnp.dot(ds_b, k, preferred_element_type=f32)

        @pl.when(i == j)
        def _():
            accumulate(True)

        @pl.when(i > j)
        def _():
            accumulate(False)

        @pl.when(i == n - 1)
        def _():
            dk_ref[...] = dk_s[...].astype(bf16)
            dv_ref[...] = dv_s[...].astype(bf16)

        @pl.when(step == len(pairs) - 1)
        def _():
            dq_ref[...] = dq_s[...].astype(bf16)

        rd.after(pl.program_id(0) * len(pairs) + step, H * len(pairs))

    blk = (t, HEAD_DIM)
    at_q = lambda h, s, jt, it: (it[s], h)
    col_q = pl.BlockSpec((1, t, 1), lambda h, s, jt, it: (h, it[s], 0))
    row_k = pl.BlockSpec((1, 1, t), lambda h, s, jt, it: (h, 0, jt[s]))
    grid_spec = pltpu.PrefetchScalarGridSpec(
        num_scalar_prefetch=2, grid=(H, len(pairs)),
        in_specs=[pl.BlockSpec(blk, at_q),
                  pl.BlockSpec(blk, lambda h, s, jt, it: (jt[s], H + h)),
                  pl.BlockSpec(blk, lambda h, s, jt, it: (jt[s], 2 * H + h)),
                  pl.BlockSpec(blk, at_q), pl.BlockSpec(blk, at_q), col_q, col_q, row_k] + rd.in_specs,
        out_specs=[pl.BlockSpec((T, HEAD_DIM), lambda h, s, jt, it: (0, h)),
                   pl.BlockSpec(blk, lambda h, s, jt, it: (jt[s], h)),
                   pl.BlockSpec(blk, lambda h, s, jt, it: (jt[s], h)),
                   row_k,
                   pl.BlockSpec((1, T, 1), lambda h, s, jt, it: (h, 0, 0))] + rd.out_specs,
        scratch_shapes=[pltpu.VMEM(blk, f32), pltpu.VMEM(blk, f32),
                        pltpu.VMEM((T, HEAD_DIM), f32), pltpu.VMEM((T, 1), f32)] + rd.scratch)
    return pl.pallas_call(
        body, name=name, grid_spec=grid_spec,
        out_shape=[jax.ShapeDtypeStruct((T, D), bf16), jax.ShapeDtypeStruct((T, D), bf16),
                   jax.ShapeDtypeStruct((T, D), bf16), jax.ShapeDtypeStruct((H, 1, T), f32),
                   jax.ShapeDtypeStruct((H, T, 1), f32)] + rd.out_shape,
        compiler_params=_cp(("arbitrary", "arbitrary") if rd.exs else ("parallel", "arbitrary")),
    )(j_tab, i_tab, qkv, qkv, qkv, o, do, lse, f_col, f_row, *rd.args)


def mod_fwd(c16, mod_w, mod_b_loc, name):
    L, D, MW = mod_w.shape
    tn = _pick(MW, (512, 256, 128))

    def body(c_ref, w_ref, b_ref, o_ref):
        cv = c_ref[...]
        ca = (cv * jax.nn.sigmoid(cv)).astype(bf16)
        o_ref[...] = jnp.dot(ca, w_ref[...].astype(bf16), preferred_element_type=f32) + b_ref[...]

    return pl.pallas_call(
        body, name=name, grid=(L, MW // tn),
        in_specs=[pl.BlockSpec((16, D), lambda l, j: (0, 0)),
                  pl.BlockSpec((None, D, tn), lambda l, j: (l, 0, j)),
                  pl.BlockSpec((None, 1, tn), lambda l, j: (l, 0, j))],
        out_specs=pl.BlockSpec((None, 16, tn), lambda l, j: (l, 0, j)),
        out_shape=jax.ShapeDtypeStruct((L, 16, MW), f32),
        compiler_params=_cp(("parallel", "parallel")),
    )(c16, mod_w, mod_b_loc)


def mod_w_bwd(c_t, dmod, name):
    D = c_t.shape[0]
    L, _, MW = dmod.shape
    tn = _pick(MW, (512, 256, 128))

    def body(c_ref, d_ref, o_ref):
        cv = c_ref[...]
        ca = (cv * jax.nn.sigmoid(cv)).astype(bf16)
        o_ref[...] = jnp.dot(ca, d_ref[...].astype(bf16), preferred_element_type=f32)

    return pl.pallas_call(
        body, name=name, grid=(L, MW // tn),
        in_specs=[pl.BlockSpec((D, LANES), lambda l, j: (0, 0)),
                  pl.BlockSpec((None, LANES, tn), lambda l, j: (l, 0, j))],
        out_specs=pl.BlockSpec((None, D, tn), lambda l, j: (l, 0, j)),
        out_shape=jax.ShapeDtypeStruct((L, D, MW), f32),
        compiler_params=_cp(("parallel", "parallel")),
    )(c_t, dmod)


def adamw(w, g, m, v, name, riders=None):
    shape = w.shape
    C = shape[-1] if w.ndim >= 1 else 1
    R = max(w.size // C, 1)
    w2, g2, m2, v2 = (t.reshape(R, C) for t in (w, g, m, v))
    tr = R
    for cand in (2048, 1024, 512, 256, 128, 64, 32, 16, 8):
        if R % cand == 0 and cand * _round_up(C, LANES) <= 256 * 1024:
            tr = cand
            break
    rd = Riders(riders)
    n_ex = len(rd.exs)

    def body(*refs):
        in_refs, out_refs, scratch_refs = refs[:4 + n_ex], refs[4 + n_ex:7 + 2 * n_ex], refs[7 + 2 * n_ex:]
        rd.split(in_refs, out_refs, scratch_refs)
        w_ref, g_ref, m_ref, v_ref = in_refs[:4]
        d_ref, mo_ref, vo_ref = out_refs[:3]
        rd.before(pl.program_id(0))
        gv = g_ref[...]
        mn = ADAM_B1 * m_ref[...] + (1.0 - ADAM_B1) * gv
        vn = ADAM_B2 * v_ref[...] + (1.0 - ADAM_B2) * (gv * gv)
        m_hat = mn / (1.0 - ADAM_B1 ** ADAM_STEP)
        v_hat = vn / (1.0 - ADAM_B2 ** ADAM_STEP)
        d_ref[...] = -ADAM_LR * (m_hat / (jnp.sqrt(v_hat) + ADAM_EPS) + ADAM_WD * w_ref[...])
        mo_ref[...] = mn
        vo_ref[...] = vn
        rd.after(pl.program_id(0), R // tr)

    spec = pl.BlockSpec((tr, C), lambda i: (i, 0))
    sds = jax.ShapeDtypeStruct((R, C), f32)
    outs = pl.pallas_call(
        body, name=name, grid=(R // tr,), in_specs=[spec] * 4 + rd.in_specs,
        out_specs=[spec] * 3 + rd.out_specs, out_shape=[sds, sds, sds] + rd.out_shape,
        scratch_shapes=rd.scratch,
        compiler_params=_cp(("arbitrary",) if rd.exs else ("parallel",)),
    )(w2, g2, m2, v2, *rd.args)
    return tuple(t.reshape(shape) for t in outs[:3]) + tuple(outs[3:])


def reduce_scatter_tail(pair, tag):
    quad = run_exchange(chip_exchange(pair), "rs_chip_exchange_" + tag)
    return sum_slots(quad, "rs_final_sum_" + tag)


def kernel(x, c, mod_w, mod_b, mix_norm_g, ffn_norm_g, attn_w_in, attn_b_f, attn_w_o, gm_w_in, gm_v_g, gm_w_s, gm_b_s, gm_w_o, ffn_w_in, ffn_conv_w, ffn_conv_b, ffn_w_out, final_g, loss_target, m_mod_w, m_mod_b, m_mix_norm_g, m_ffn_norm_g, m_attn_w_in, m_attn_b_f, m_attn_w_o, m_gm_w_in, m_gm_v_g, m_gm_w_s, m_gm_b_s, m_gm_w_o, m_ffn_w_in, m_ffn_conv_w, m_ffn_conv_b, m_ffn_w_out, m_final_g, v_mod_w, v_mod_b, v_mix_norm_g, v_ffn_norm_g, v_attn_w_in, v_attn_b_f, v_attn_w_o, v_gm_w_in, v_gm_v_g, v_gm_w_s, v_gm_b_s, v_gm_w_o, v_ffn_w_in, v_ffn_conv_w, v_ffn_conv_b, v_ffn_w_out, v_final_g):
    xi, yi, ci = lax.axis_index("x"), lax.axis_index("y"), lax.axis_index("c")
    me = 4 * xi + 2 * yi + ci

    _, T, D = x.shape
    L = mod_w.shape[0]
    MW = mod_w.shape[2]
    NA, _, QW = attn_w_in.shape
    NB = gm_w_in.shape[0]
    H = D // HEAD_DIM
    G = D // GM_GROUP
    DR = attn_w_o.shape[1]
    GW = gm_w_in.shape[2]
    FW = ffn_w_in.shape[2]
    FR = ffn_w_out.shape[1]
    FRP = _round_up(FR, LANES // 2)
    FWP = 2 * FRP
    FP = N_CHIPS * FWP
    DFF2 = N_DEV * FW
    assert 2 * FR == FW and N_DEV * QW == 3 * D + H and N_DEV * GW == 2 * D
    c_idx = ci.reshape(1).astype(jnp.int32)

    def pad_ff(t, axis, blocks):
        ax = axis % t.ndim
        t = t.reshape(t.shape[:ax] + (blocks, FR) + t.shape[ax + 1:])
        pad = [(0, 0)] * t.ndim
        pad[ax + 1] = (0, FRP - FR)
        t = jnp.pad(t, pad)
        return t.reshape(t.shape[:ax] + (blocks * FRP,) + t.shape[ax + 2:])

    def unpad_ff(t, axis, blocks):
        ax = axis % t.ndim
        t = t.reshape(t.shape[:ax] + (blocks, FRP) + t.shape[ax + 1:])
        t = lax.slice_in_dim(t, 0, FR, axis=ax + 1)
        return t.reshape(t.shape[:ax] + (blocks * FR,) + t.shape[ax + 2:])

    x0 = x[0]
    tgt = loss_target[0]

    c_all = all_gather(c, "gather_c").reshape(N_DEV, D)
    cw_loc = pad_ff(ffn_conv_w, 2, 2).reshape(L * CONV_W, FWP)
    conv_w_full = all_gather(cw_loc, "gather_conv_w").transpose(1, 0, 2).reshape(L, CONV_W, 2 * FP)
    vg_full = all_gather(gm_v_g, "gather_vg").transpose(1, 0, 2).reshape(NB, 1, D)
    conv_b_full = pad_ff(ffn_conv_b, 1, 2 * N_DEV).reshape(L, 1, 2 * FP)

    c16 = jnp.pad(c_all, ((0, 16 - N_DEV), (0, 0)))
    mod_b_loc = lax.dynamic_slice_in_dim(mod_b, me * MW, MW, axis=1).reshape(L, 1, MW)
    mod_part = mod_fwd(c16, mod_w, mod_b_loc, "mod_fwd")[:, :N_DEV]
    mod_all = all_gather(mod_part, "gather_mod")
    mod_me = lax.dynamic_index_in_dim(mod_all, me, axis=2, keepdims=False)
    mod_me = mod_me.transpose(1, 0, 2).reshape(L, 6, 1, D)

    w_ai_t = jnp.swapaxes(attn_w_in, 1, 2).astype(bf16)
    w_gi_t = jnp.swapaxes(gm_w_in, 1, 2).astype(bf16)
    w_fi_t = pad_ff(jnp.swapaxes(ffn_w_in, 1, 2).astype(bf16), 1, 2)
    w_ao_l = attn_w_o.astype(bf16)
    w_go_l = gm_w_o.astype(bf16)
    w_fo_l = jnp.pad(ffn_w_out.astype(bf16), ((0, 0), (0, FRP - FR), (0, 0)))

    stash = []
    arrived = {}
    xc = x0

    def shard(kind, layer):
        even = layer % 2 == 0
        return {"ffn_in": w_fi_t, "ffn_out": w_fo_l, "mix_in": w_ai_t if even else w_gi_t,
                "mix_out": w_ao_l if even else w_go_l}[kind][layer if kind.startswith("ffn") else layer // 2]

    def need(kind, layer):
        if (kind, layer) not in arrived:
            arrived[(kind, layer)] = all_gather(shard(kind, layer), "gather_" + kind)
        return arrived.pop((kind, layer))

    def hosting(keys, n_own, fn, *args, **kw):
        keys = [k for k in keys if k[1] < L]
        outs = fn(*args, riders=[gather_exchange(shard(*k)) for k in keys], **kw)
        outs = list(outs) if isinstance(outs, (list, tuple)) else [outs]
        arrived.update(zip(keys, outs[n_own:]))
        return outs[:n_own]

    for i in range(L):
        sh1, sc1, g1, sh2, sc2, g2 = (mod_me[i, k] for k in range(6))
        jm = i // 2
        st = {"x_in": xc}
        h = norm_mod_fwd(xc, mix_norm_g[i][None], sc1, sh1, "norm_mod_fwd")
        st["h"] = h
        w_mi = need("mix_in", i)
        w_mo = need("mix_out", i).reshape(D, D)
        if i % 2 == 0:
            w_in_t = w_mi.reshape(N_DEV * QW, D)
            w_qkv_t = w_in_t[:3 * D]
            w_f_t = jnp.pad(w_in_t[3 * D:], ((0, LANES - H), (0, 0)))
            qkv = matmul(h, w_qkv_t, name="fox_qkv", tb=True, out_dtype=bf16)
            flog = matmul(h, w_f_t, name="fox_flog", tb=True)
            flog_t = flog[:, :H].T
            b_col = attn_b_f[jm][:, None]
            F = fox_gates_fwd(flog_t, b_col, "fox_gates_fwd")
            f_row, f_col = F[:, None, :], F[:, :, None]
            o, lse = hosting([("ffn_in", i), ("ffn_in", i + 1), ("mix_in", i + 1), ("mix_out", i + 1)], 2,
                             fox_attn_fwd, qkv, f_row, f_col, "fox_attn_fwd")
            x1, y = matmul(o, w_mo, name="mix_out", resid=xc, gvec=g1, emit_acc=True)
            st.update(qkv=qkv, flog_t=flog_t, b_col=b_col, f_row=f_row, f_col=f_col, o=o, lse=lse,
                      w_qkv_t=w_qkv_t, w_f_t=w_f_t, w_mo=w_mo)
        else:
            w_gi_full = w_mi.reshape(2 * D, D)
            z = matmul(h, w_gi_full, name="gm_in", tb=True)
            bs_t = gm_b_s[jm].T
            gated = gm_gate_fwd(z, vg_full[jm], gm_w_s[jm], bs_t, "gm_gate_fwd")
            x1, y = matmul(gated, w_mo, name="mix_out", resid=xc, gvec=g1, emit_acc=True)
            st.update(z=z, bs_t=bs_t, gated=gated, w_gi_full=w_gi_full, w_mo=w_mo)
        st.update(y=y, x1=x1)
        h2 = norm_mod_fwd(x1, ffn_norm_g[i][None], sc2, sh2, "norm_mod_fwd")
        w_fi_full = need("ffn_in", i).reshape(2 * FP, D)
        fox = i % 2 == 0
        a3, = hosting([("ffn_out", i) if fox else ("mix_in", i + 1)], 1,
                      matmul, h2, w_fi_full, name="ffn_up", tb=True, out_split=True)
        act = conv_glu_fwd(a3, conv_w_full[i], conv_b_full[i], "conv_glu_fwd")
        w_fo_full = need("ffn_out", i).reshape(FP, D)
        xc, f_out = hosting([("ffn_out", i + 1) if fox else ("mix_out", i + 1)], 2,
                            matmul, act, w_fo_full, name="ffn_down", resid=x1, gvec=g2, emit_acc=True)
        st.update(h2=h2, a3=a3, act=act, f=f_out, w_fi_full=w_fi_full, w_fo_full=w_fo_full)
        stash.append(st)

    loss_part, dx, d_final_g = loss_head(xc, final_g[None], tgt, "loss_head")
    loss = lax.psum(loss_part[0, 0], AXES)

    d_mod = [None] * L
    d_mix_g = [None] * L
    d_ffn_g = [None] * L
    d_conv_w = [None] * L
    d_conv_b = [None] * L
    g_wfi = [None] * L
    g_wfo = [None] * L
    g_wai = [None] * NA
    g_wao = [None] * NA
    d_bf = [None] * NA
    g_wgi = [None] * NB
    g_wgo = [None] * NB
    d_ws = [None] * NB
    d_bs = [None] * NB
    d_vg = [None] * NB
    ffn_pairs = None
    mix_pairs = None

    def half_reduce(g8, tag):
        got = run_exchange(pair_exchange(g8), "rs_pair_exchange_" + tag)
        return pair_sum(g8, got, c_idx, "rs_pair_sum_" + tag)

    def finish_mixer(layer, quad_mo, quad_in):
        g_mo = sum_slots(quad_mo, "rs_final_sum_mix_out")
        g_in = sum_slots(quad_in, "rs_final_sum_mix_in").T
        if layer % 2 == 0:
            g_wao[layer // 2], g_wai[layer // 2] = g_mo, g_in
        else:
            g_wgo[layer // 2], g_wgi[layer // 2] = g_mo, g_in

    for i in reversed(range(L)):
        st = stash[i]
        sh1, sc1, g1, sh2, sc2, g2 = (mod_me[i, k] for k in range(6))
        jm = i // 2
        dy, dg2 = gate_bwd(dx, st["f"], g2, "gate_bwd")
        if mix_pairs is None:
            dact = matmul(dy, st["w_fo_full"], name="ffn_down_dx", tb=True)
        else:
            dact, quad_mo = matmul(dy, st["w_fo_full"], name="ffn_down_dx", tb=True,
                                   riders=[chip_exchange(mix_pairs[1])])
        if ffn_pairs is None:
            dw_fo = matmul(st["act"], dy, name="ffn_down_dw", ta=True, out_dtype=bf16)
        else:
            dw_fo, quad = matmul(st["act"], dy, name="ffn_down_dw", ta=True, out_dtype=bf16,
                                 riders=[chip_exchange(ffn_pairs[0])])
            g_wfo[i + 1] = sum_slots(quad, "rs_final_sum_ffn_out")[:FR]
        g8_fo = dw_fo.reshape(N_DEV, FRP, D)
        da3, dwg, dwu, dbg, dbu, got_fo = conv_glu_bwd(st["a3"], conv_w_full[i], conv_b_full[i], dact,
                                                       "conv_glu_bwd", riders=[pair_exchange(g8_fo)])
        pair_fo = pair_sum(g8_fo, got_fo, c_idx, "rs_pair_sum_ffn_out")
        d_conv_w[i] = jnp.concatenate([dwg, dwu], axis=1)
        d_conv_b[i] = jnp.concatenate([dbg, dbu], axis=1)
        if ffn_pairs is None:
            dw_fi_t = matmul(da3, st["h2"], name="ffn_up_dw", ta=True, a_split=True, out_dtype=bf16)
        else:
            dw_fi_t, quad = matmul(da3, st["h2"], name="ffn_up_dw", ta=True, a_split=True, out_dtype=bf16,
                                   riders=[chip_exchange(ffn_pairs[1])])
            g_wfi[i + 1] = unpad_ff(sum_slots(quad, "rs_final_sum_ffn_in"), 0, 2).T
        g8_fi = dw_fi_t.reshape(N_DEV, FWP, D)
        if mix_pairs is None:
            dh2, got_fi = matmul(da3, st["w_fi_full"], name="ffn_up_dx", a_split=True,
                                 riders=[pair_exchange(g8_fi)])
        else:
            dh2, got_fi, quad_in = matmul(da3, st["w_fi_full"], name="ffn_up_dx", a_split=True,
                                          riders=[pair_exchange(g8_fi), chip_exchange(mix_pairs[2])])
            finish_mixer(mix_pairs[0], quad_mo, quad_in)
        ffn_pairs = (pair_fo, pair_sum(g8_fi, got_fi, c_idx, "rs_pair_sum_ffn_in"))
        dx, dsh2, dsc2, d_ffn_g[i] = norm_mod_bwd(st["x1"], ffn_norm_g[i][None], sc2, dh2, dx, "norm_mod_bwd")
        dy, dg1 = gate_bwd(dx, st["y"], g1, "gate_bwd")
        if i % 2 == 0:
            do = matmul(dy, st["w_mo"], name="mix_out_dx", tb=True)
            dw_mo = matmul(st["o"], dy, name="mix_out_dw", ta=True, out_dtype=bf16)
            dq, dk, dv, cs, rs, quad_fo, quad_fi = fox_attn_bwd(
                st["qkv"], st["o"], do, st["lse"], st["f_row"], st["f_col"], "fox_attn_bwd",
                riders=[chip_exchange(ffn_pairs[0]), chip_exchange(ffn_pairs[1])])
            g_wfo[i] = sum_slots(quad_fo, "rs_final_sum_ffn_out")[:FR]
            g_wfi[i] = unpad_ff(sum_slots(quad_fi, "rs_final_sum_ffn_in"), 0, 2).T
            ffn_pairs = None
            dF = rs[:, :, 0] - cs[:, 0, :]
            dflog_t, d_bf[jm] = fox_gates_bwd(st["flog_t"], st["b_col"], dF, "fox_gates_bwd")
            dflog = jnp.pad(dflog_t.T, ((0, 0), (0, LANES - H))).astype(bf16)
            dqkv = jnp.concatenate([dq, dk, dv], axis=1)
            dw_qkv_t = matmul(dqkv, st["h"], name="fox_qkv_dw", ta=True, out_dtype=bf16)
            dw_f_t = matmul(dflog, st["h"], name="fox_flog_dw", ta=True, out_dtype=bf16)
            dw_in_t = jnp.concatenate([dw_qkv_t, dw_f_t[:H]], axis=0)
            dh = matmul(dqkv, st["w_qkv_t"], name="fox_qkv_dx")
            dh = matmul(dflog, st["w_f_t"], name="fox_flog_dx", resid=dh)
        else:
            dgated = matmul(dy, st["w_mo"], name="mix_out_dx", tb=True)
            dw_mo = matmul(st["gated"], dy, name="mix_out_dw", ta=True, out_dtype=bf16)
            dz, d_ws[jm], dbs_t, d_vg[jm] = gm_gate_bwd(st["z"], vg_full[jm], gm_w_s[jm], st["bs_t"], dgated, "gm_gate_bwd")
            d_bs[jm] = dbs_t.T
            dw_in_t = matmul(dz, st["h"], name="gm_in_dw", ta=True, out_dtype=bf16)
            dh = matmul(dz, st["w_gi_full"], name="gm_in_dx")
        mix_pairs = (i, half_reduce(dw_mo.reshape(N_DEV, DR, D), "mix_out"),
                     half_reduce(dw_in_t.reshape(N_DEV, -1, D), "mix_in"))
        dx, dsh1, dsc1, d_mix_g[i] = norm_mod_bwd(st["x_in"], mix_norm_g[i][None], sc1, dh, dx, "norm_mod_bwd")
        d_mod[i] = jnp.concatenate([dsh1, dsc1, dg1, dsh2, dsc2, dg2], axis=0)

    grad_x = dx[None]
    if ffn_pairs is not None:
        g_wfo[0] = reduce_scatter_tail(ffn_pairs[0], "ffn_out")[:FR]
        g_wfi[0] = unpad_ff(reduce_scatter_tail(ffn_pairs[1], "ffn_in"), 0, 2).T

    def gathered_sum(rows, tag, mult=SUBLANES):
        n = rows.shape[0]
        rows = jnp.pad(rows, ((0, _round_up(n, mult) - n), (0, 0)))
        every = all_gather(rows, "gather_small_grads_" + tag)
        return every, sum_slots(every, "sum_small_grads_" + tag)

    rows_d = jnp.concatenate([jnp.concatenate(d_mod, axis=0), jnp.concatenate(d_mix_g, axis=0),
                              jnp.concatenate(d_ffn_g, axis=0), jnp.concatenate(d_vg, axis=0), d_final_g], axis=0)
    every_d, sum_d = gathered_sum(rows_d, "d")
    r0 = L * 6
    grad_mod_b = sum_d[:r0].reshape(L, 6 * D)
    grad_mix_g, grad_ffn_g = sum_d[r0:r0 + L], sum_d[r0 + L:r0 + 2 * L]
    grad_vg_full = sum_d[r0 + 2 * L:r0 + 2 * L + NB]
    grad_final_g = sum_d[r0 + 2 * L + NB]
    grad_vg = lax.dynamic_slice_in_dim(grad_vg_full, me * DR, DR, axis=1)

    rows_f = jnp.concatenate([jnp.concatenate(d_conv_w, axis=0), jnp.concatenate(d_conv_b, axis=0)], axis=0)
    _, sum_f = gathered_sum(rows_f, "f")
    sum_f = unpad_ff(sum_f, 1, 2 * N_DEV)
    grad_conv_w = lax.dynamic_slice_in_dim(sum_f[:L * CONV_W].reshape(L, CONV_W, DFF2), me * FW, FW, axis=2)
    grad_conv_b = sum_f[L * CONV_W:L * CONV_W + L]

    rows_c = jnp.concatenate([jnp.stack(d_ws).reshape(NB * G * CHUNK, CHUNK), jnp.stack(d_bs).reshape(NB * G, CHUNK),
                              jnp.pad(jnp.stack(d_bf).reshape(NA, H), ((0, 0), (0, LANES - H)))], axis=0)
    _, sum_c = gathered_sum(rows_c, "c", mult=SLOT_ROWS)
    n_ws = NB * G * CHUNK
    grad_ws = sum_c[:n_ws].reshape(NB, G, CHUNK, CHUNK)
    grad_bs = sum_c[n_ws:n_ws + NB * G].reshape(NB, G, CHUNK)
    grad_bf = sum_c[n_ws + NB * G:n_ws + NB * G + NA, :H]

    dmod_all = every_d[:, :r0].reshape(N_DEV, L, 6 * D)
    dmod_loc = lax.dynamic_slice_in_dim(dmod_all, me * MW, MW, axis=2).transpose(1, 0, 2)
    dmod_loc = jnp.pad(dmod_loc, ((0, 0), (0, LANES - N_DEV), (0, 0)))
    c_t = jnp.pad(c_all.T, ((0, 0), (0, LANES - N_DEV)))
    grad_mod_w = mod_w_bwd(c_t, dmod_loc, "mod_w_bwd")

    first = adamw(mod_w, grad_mod_w, m_mod_w, v_mod_w, "adamw",
                  riders=[chip_exchange(mix_pairs[1]), chip_exchange(mix_pairs[2])])
    finish_mixer(mix_pairs[0], first[3], first[4])
    grad_attn_w_in, grad_attn_w_o = jnp.stack(g_wai), jnp.stack(g_wao)
    grad_gm_w_in, grad_gm_w_o = jnp.stack(g_wgi), jnp.stack(g_wgo)
    grad_ffn_w_in, grad_ffn_w_out = jnp.stack(g_wfi), jnp.stack(g_wfo)

    weights = [mod_w, mod_b, mix_norm_g, ffn_norm_g, attn_w_in, attn_b_f, attn_w_o, gm_w_in, gm_v_g, gm_w_s,
               gm_b_s, gm_w_o, ffn_w_in, ffn_conv_w, ffn_conv_b, ffn_w_out, final_g]
    grads = [grad_mod_w, grad_mod_b, grad_mix_g, grad_ffn_g, grad_attn_w_in, grad_bf, grad_attn_w_o,
             grad_gm_w_in, grad_vg, grad_ws, grad_bs, grad_gm_w_o, grad_ffn_w_in, grad_conv_w, grad_conv_b,
             grad_ffn_w_out, grad_final_g]
    ms = [m_mod_w, m_mod_b, m_mix_norm_g, m_ffn_norm_g, m_attn_w_in, m_attn_b_f, m_attn_w_o, m_gm_w_in, m_gm_v_g,
          m_gm_w_s, m_gm_b_s, m_gm_w_o, m_ffn_w_in, m_ffn_conv_w, m_ffn_conv_b, m_ffn_w_out, m_final_g]
    vs = [v_mod_w, v_mod_b, v_mix_norm_g, v_ffn_norm_g, v_attn_w_in, v_attn_b_f, v_attn_w_o, v_gm_w_in, v_gm_v_g,
          v_gm_w_s, v_gm_b_s, v_gm_w_o, v_ffn_w_in, v_ffn_conv_w, v_ffn_conv_b, v_ffn_w_out, v_final_g]
    deltas, new_ms, new_vs = [], [], []
    for k, (w, g, m_, v_) in enumerate(zip(weights, grads, ms, vs)):
        d_, mn_, vn_ = first[:3] if k == 0 else adamw(w, g, m_, v_, "adamw")
        deltas.append(d_)
        new_ms.append(mn_)
        new_vs.append(vn_)

    return (loss, grad_x, *grads, *deltas, *new_ms, *new_vs)
```

```python
import numpy as np
import jax
import jax.numpy as jnp
from jax import lax
from jax.experimental import pallas as pl
from jax.experimental.pallas import tpu as pltpu

f32 = jnp.float32
bf16 = jnp.bfloat16

AXES = ("x", "y", "c")
N_DEV = 8
N_CHIPS = 4
LANES = 128
SUBLANES = 8
HEAD_DIM = 128
CHUNK = 128
GM_GROUP = 128
CONV_W = 3
EPS = 1e-6
NEG = -1e30
VMEM_LIMIT_BYTES = 56 * 1024 * 1024
MATMUL_VMEM_BYTES = 40 * 1024 * 1024
SLOT_ROWS = 512
LOG2E = 1.4426950408889634
ATT_Q_TILE = 1024

ADAM_LR = 0.001
ADAM_B1 = 0.9
ADAM_B2 = 0.999
ADAM_EPS = 1e-08
ADAM_WD = 0.01
ADAM_STEP = 10

MESH = pl.DeviceIdType.MESH
ANY = pl.BlockSpec(memory_space=pl.ANY)


def _cp(sem):
    return pltpu.CompilerParams(dimension_semantics=sem, vmem_limit_bytes=VMEM_LIMIT_BYTES)


def _pick(n, prefs):
    for p in prefs:
        if n % p == 0:
            return p
    return n


def _round_up(n, m):
    return (n + m - 1) // m * m


class Exchange:
    def __init__(self, src, out_shape, scratch, phases):
        self.src = src
        self.out_shape = out_shape
        self.scratch = scratch
        self.phases = phases


def gather_exchange(xl):
    def phases(x_ref, out_ref, send_sems, recv_sems, local_sem):
        x, y, c = lax.axis_index("x"), lax.axis_index("y"), lax.axis_index("c")
        me, sibling = (x, y, c), (x, y, 1 - c)
        chips = [(1 - x, y), (x, 1 - y), (1 - x, 1 - y)]

        def slot(px, py, pc):
            return out_ref.at[4 * px + 2 * py + pc]

        def copy(k, block, to, src=None):
            return pltpu.make_async_remote_copy(
                src_ref=slot(*block) if src is None else src, dst_ref=slot(*block),
                send_sem=send_sems.at[k], recv_sem=recv_sems.at[k],
                device_id=to, device_id_type=MESH)

        mine = pltpu.make_async_copy(x_ref, slot(*me), local_sem)
        first = [copy(0, me, sibling, src=x_ref)]
        first += [copy(1 + j, me, (*chip, c), src=x_ref) for j, chip in enumerate(chips)]
        passed = [copy(4 + j, (*chip, c), sibling) for j, chip in enumerate(chips)]

        def start():
            mine.start()
            for cp in first:
                cp.start()

        def hand_on():
            for j, chip in enumerate(chips):
                copy(1 + j, (*chip, c), me).wait_recv()
                passed[j].start()

        def finish():
            copy(0, sibling, me).wait_recv()
            for j, chip in enumerate(chips):
                copy(4 + j, (*chip, 1 - c), me).wait_recv()
            for cp in first + passed:
                cp.wait_send()
            mine.wait()

        return start, hand_on, finish

    return Exchange(xl, jax.ShapeDtypeStruct((N_DEV,) + xl.shape, xl.dtype),
                    [pltpu.SemaphoreType.DMA((7,)), pltpu.SemaphoreType.DMA((7,)),
                     pltpu.SemaphoreType.DMA], phases)


def pair_exchange(g8):
    _, R, W = g8.shape

    def phases(g_ref, out_ref, send_sems, recv_sems):
        x, y, c = lax.axis_index("x"), lax.axis_index("y"), lax.axis_index("c")
        copies = [pltpu.make_async_remote_copy(
            src_ref=g_ref.at[2 * q + (1 - c)], dst_ref=out_ref.at[q],
            send_sem=send_sems.at[q], recv_sem=recv_sems.at[q],
            device_id=(x, y, 1 - c), device_id_type=MESH) for q in range(N_CHIPS)]

        def start():
            for cp in copies:
                cp.start()

        def finish():
            for cp in copies:
                cp.wait()

        return start, None, finish

    return Exchange(g8, jax.ShapeDtypeStruct((N_CHIPS, R, W), g8.dtype),
                    [pltpu.SemaphoreType.DMA((N_CHIPS,)), pltpu.SemaphoreType.DMA((N_CHIPS,))], phases)


def chip_exchange(p4):
    def phases(p_ref, out_ref, send_sems, recv_sems, local_sem):
        x, y, c = lax.axis_index("x"), lax.axis_index("y"), lax.axis_index("c")
        my_q = 2 * x + y
        chips = [(1 - x, y), (x, 1 - y), (1 - x, 1 - y)]
        mine = pltpu.make_async_copy(p_ref.at[my_q], out_ref.at[my_q], local_sem)
        copies = [pltpu.make_async_remote_copy(
            src_ref=p_ref.at[2 * px + py], dst_ref=out_ref.at[my_q],
            send_sem=send_sems.at[k], recv_sem=recv_sems.at[k],
            device_id=(px, py, c), device_id_type=MESH) for k, (px, py) in enumerate(chips)]

        def start():
            mine.start()
            for cp in copies:
                cp.start()

        def finish():
            for k, (px, py) in enumerate(chips):
                pltpu.make_async_remote_copy(
                    src_ref=p_ref.at[my_q], dst_ref=out_ref.at[2 * px + py],
                    send_sem=send_sems.at[k], recv_sem=recv_sems.at[k],
                    device_id=(px, py, c), device_id_type=MESH).wait_recv()
            for cp in copies:
                cp.wait_send()
            mine.wait()

        return start, None, finish

    return Exchange(p4, jax.ShapeDtypeStruct(p4.shape, p4.dtype),
                    [pltpu.SemaphoreType.DMA((3,)), pltpu.SemaphoreType.DMA((3,)),
                     pltpu.SemaphoreType.DMA], phases)


def run_exchange(ex, name):
    def body(src_ref, out_ref, *sems):
        start, hand_on, finish = ex.phases(src_ref, out_ref, *sems)
        start()
        if hand_on is not None:
            hand_on()
        finish()

    return pl.pallas_call(body, name=name, out_shape=ex.out_shape, in_specs=[ANY], out_specs=ANY,
                          scratch_shapes=ex.scratch)(ex.src)


def all_gather(xl, name):
    return run_exchange(gather_exchange(xl), name)


class Riders:
    def __init__(self, exchanges):
        self.exs = list(exchanges or [])
        self.in_specs = [ANY] * len(self.exs)
        self.args = [ex.src for ex in self.exs]
        self.out_specs = [ANY] * len(self.exs)
        self.out_shape = [ex.out_shape for ex in self.exs]
        self.scratch = [s for ex in self.exs for s in ex.scratch]

    def split(self, in_refs, out_refs, scratch_refs):
        n = len(self.exs)
        self.refs = []
        pos = len(scratch_refs) - len(self.scratch)
        for k, ex in enumerate(self.exs):
            sems = scratch_refs[pos:pos + len(ex.scratch)]
            pos += len(ex.scratch)
            self.refs.append((in_refs[len(in_refs) - n + k], out_refs[len(out_refs) - n + k], sems))

    def _parts(self):
        return [ex.phases(src, out, *sems) for ex, (src, out, sems) in zip(self.exs, self.refs)]

    def before(self, step):
        if not self.exs:
            return
        parts = self._parts()

        @pl.when(step == 0)
        def _():
            for start, _, _ in parts:
                start()

    def after(self, step, n_steps):
        if not self.exs:
            return
        parts = self._parts()
        mid = (3 * n_steps) // 4

        if any(h is not None for _, h, _ in parts):
            @pl.when(step == mid)
            def _():
                for _, hand_on, _ in parts:
                    if hand_on is not None:
                        hand_on()

        @pl.when(step == n_steps - 1)
        def _():
            for _, _, finish in parts:
                finish()


def pair_sum(g8, got4, c_idx, name):
    _, R, W = g8.shape
    tr = _pick(R, (512, 256, 128, 64, 32, 16))
    g5 = g8.reshape(N_CHIPS, 2, R, W)

    def body(c_ref, a_ref, b_ref, o_ref):
        o_ref[...] = (a_ref[...].astype(f32) + b_ref[...].astype(f32)).astype(o_ref.dtype)

    grid_spec = pltpu.PrefetchScalarGridSpec(
        num_scalar_prefetch=1, grid=(N_CHIPS, R // tr),
        in_specs=[pl.BlockSpec((None, None, tr, W), lambda q, r, cr: (q, cr[0], r, 0)),
                  pl.BlockSpec((None, tr, W), lambda q, r, cr: (q, r, 0))],
        out_specs=pl.BlockSpec((None, tr, W), lambda q, r, cr: (q, r, 0)))
    return pl.pallas_call(
        body, name=name, grid_spec=grid_spec,
        out_shape=jax.ShapeDtypeStruct((N_CHIPS, R, W), g8.dtype),
        compiler_params=_cp(("parallel", "parallel")),
    )(c_idx, g5, got4)


def sum_slots(xs, name, out_dtype=f32):
    S, R, W = xs.shape
    tr = _pick(R, (512, 256, 128, 64, 32, 16, 8))

    def body(x_ref, o_ref):
        acc = x_ref[0].astype(f32)
        for s in range(1, S):
            acc = acc + x_ref[s].astype(f32)
        o_ref[...] = acc.astype(o_ref.dtype)

    return pl.pallas_call(
        body, name=name, grid=(R // tr,),
        in_specs=[pl.BlockSpec((S, tr, W), lambda r: (0, r, 0))],
        out_specs=pl.BlockSpec((tr, W), lambda r: (r, 0)),
        out_shape=jax.ShapeDtypeStruct((R, W), out_dtype),
        compiler_params=_cp(("parallel",)),
    )(xs)


def matmul(a, b, *, name, ta=False, tb=False, a_split=False, b_split=False, out_split=False,
           out_dtype=f32, resid=None, gvec=None, emit_acc=False, riders=None):
    rd = Riders(riders)
    if a_split:
        rows, cols = a.shape[1], 2 * a.shape[2]
        M, K = (cols, rows) if ta else (rows, cols)
    else:
        M, K = (a.shape[1], a.shape[0]) if ta else a.shape
    if b_split:
        assert not tb
        N = 2 * b.shape[2]
        assert b.shape[1] == K
    else:
        N = b.shape[0] if tb else b.shape[1]
        assert (b.shape[1] if tb else b.shape[0]) == K, (a.shape, b.shape, name)

    m_split = a_split and ta
    k_split = a_split and not ta
    n_split = b_split or out_split
    tm = _pick(M // 2 if m_split else M, (1024, 512, 256, 128, 64, 32, 16, 8))
    k_len, n_len = (K // 2 if k_split else K), (N // 2 if n_split else N)
    out_bytes = jnp.dtype(out_dtype).itemsize + (4 if resid is not None else 0) + (2 if emit_acc else 0)

    def fits(tk_, tn_):
        operands = 2 * (tm * tk_ * a.dtype.itemsize + tk_ * tn_ * b.dtype.itemsize)
        acc = tm * tn_ * 4 if tk_ < K else 0
        return operands + acc + 2 * tm * tn_ * out_bytes <= MATMUL_VMEM_BYTES

    tk_options = [d for d in range(k_len, 0, -LANES) if k_len % d == 0 and d % LANES == 0]
    tn_options = [t for t in (1024, 512, 256, 128) if n_len % t == 0]
    tk, tn = next(((tk_, tn_) for tn_min in (512, 128) for tk_ in tk_options for tn_ in tn_options
                   if tn_ >= tn_min and fits(tk_, tn_)), (tk_options[-1], tn_options[-1]))
    nk = K // tk
    n_half = (N // 2) // tn if n_split else 0
    k_half = (K // 2) // tk if k_split else 0
    m_half = (M // 2) // tm if m_split else 0

    if m_split:
        a_spec = pl.BlockSpec((None, tk, tm), lambda i, j, k: (i // m_half, k, i % m_half))
    elif k_split:
        a_spec = pl.BlockSpec((None, tm, tk), lambda i, j, k: (k // k_half, i, k % k_half))
    elif ta:
        a_spec = pl.BlockSpec((tk, tm), lambda i, j, k: (k, i))
    else:
        a_spec = pl.BlockSpec((tm, tk), lambda i, j, k: (i, k))
    if b_split:
        b_spec = pl.BlockSpec((None, tk, tn), lambda i, j, k: (j // n_half, k, j % n_half))
    elif tb:
        b_spec = pl.BlockSpec((tn, tk), lambda i, j, k: (j, k))
    else:
        b_spec = pl.BlockSpec((tk, tn), lambda i, j, k: (k, j))
    if out_split:
        o_spec = pl.BlockSpec((None, tm, tn), lambda i, j, k: (j // n_half, i, j % n_half))
        o_shape = (2, M, N // 2)
    else:
        o_spec = pl.BlockSpec((tm, tn), lambda i, j, k: (i, j))
        o_shape = (M, N)

    in_specs = [a_spec, b_spec]
    args = [a, b]
    if resid is not None:
        in_specs.append(pl.BlockSpec((tm, tn), lambda i, j, k: (i, j)))
        args.append(resid)
    if gvec is not None:
        in_specs.append(pl.BlockSpec((1, tn), lambda i, j, k: (0, j)))
        args.append(gvec)
    out_specs = [o_spec]
    out_shape = [jax.ShapeDtypeStruct(o_shape, out_dtype)]
    if emit_acc:
        out_specs.append(pl.BlockSpec((tm, tn), lambda i, j, k: (i, j)))
        out_shape.append(jax.ShapeDtypeStruct((M, N), bf16))
    dims = (((0 if ta else 1,), (1 if tb else 0,)), ((), ()))
    has_r, has_g = resid is not None, gvec is not None
    n_in, n_out = len(in_specs) + len(rd.exs), len(out_specs) + len(rd.exs)
    grid = (M // tm, N // tn, nk)
    n_steps = grid[0] * grid[1] * grid[2]

    def body(*refs):
        in_refs, out_refs, scratch_refs = refs[:n_in], refs[n_in:n_in + n_out], refs[n_in + n_out:]
        rd.split(in_refs, out_refs, scratch_refs)
        a_ref, b_ref = in_refs[0], in_refs[1]
        pos = 2
        r_ref = g_ref = None
        if has_r:
            r_ref = in_refs[pos]
            pos += 1
        if has_g:
            g_ref = in_refs[pos]
        o_ref = out_refs[0]
        y_ref = out_refs[1] if emit_acc else None
        step = (pl.program_id(0) * grid[1] + pl.program_id(1)) * nk + pl.program_id(2)
        rd.before(step)

        def finish(acc):
            if emit_acc:
                y_ref[...] = acc.astype(bf16)
            if has_g:
                acc = acc * g_ref[...]
            if has_r:
                acc = r_ref[...] + acc
            o_ref[...] = acc.astype(o_ref.dtype)

        part = lax.dot_general(a_ref[...].astype(bf16), b_ref[...].astype(bf16), dims,
                               preferred_element_type=f32)
        if nk == 1:
            finish(part)
        else:
            acc_ref = scratch_refs[0]
            k = pl.program_id(2)

            @pl.when(k == 0)
            def _():
                acc_ref[...] = part

            @pl.when(k > 0)
            def _():
                acc_ref[...] += part

            @pl.when(k == nk - 1)
            def _():
                finish(acc_ref[...])

        rd.after(step, n_steps)

    outs = pl.pallas_call(
        body, name=name, grid=grid,
        in_specs=in_specs + rd.in_specs, out_specs=out_specs + rd.out_specs,
        out_shape=out_shape + rd.out_shape,
        scratch_shapes=([pltpu.VMEM((tm, tn), f32)] if nk > 1 else []) + rd.scratch,
        compiler_params=_cp(("arbitrary",) * 3 if rd.exs else ("parallel", "parallel", "arbitrary")),
    )(*args, *rd.args)
    return outs if (emit_acc or rd.exs) else outs[0]


def _rows(T):
    return _pick(T, (256, 128, 64, 32, 16, 8))


def norm_mod_fwd(x, gn, sc, sh, name):
    T, D = x.shape
    tr = _rows(T)

    def body(x_ref, gn_ref, sc_ref, sh_ref, h_ref):
        xv = x_ref[...]
        r = lax.rsqrt(jnp.mean(xv * xv, axis=-1, keepdims=True) + EPS)
        y = (xv * r) * gn_ref[...]
        h_ref[...] = (y * (1.0 + sc_ref[...]) + sh_ref[...]).astype(bf16)

    vec = pl.BlockSpec((1, D), lambda i: (0, 0))
    row = pl.BlockSpec((tr, D), lambda i: (i, 0))
    return pl.pallas_call(
        body, name=name, grid=(T // tr,), in_specs=[row, vec, vec, vec], out_specs=row,
        out_shape=jax.ShapeDtypeStruct((T, D), bf16), compiler_params=_cp(("parallel",)),
    )(x, gn, sc, sh)


def norm_mod_bwd(x, gn, sc, dh, dx_res, name):
    T, D = x.shape
    tr = _rows(T)

    def body(x_ref, gn_ref, sc_ref, dh_ref, dr_ref, dx_ref, dsh_ref, dsc_ref, dgn_ref):
        @pl.when(pl.program_id(0) == 0)
        def _():
            dsh_ref[...] = jnp.zeros_like(dsh_ref)
            dsc_ref[...] = jnp.zeros_like(dsc_ref)
            dgn_ref[...] = jnp.zeros_like(dgn_ref)

        xv = x_ref[...]
        r = lax.rsqrt(jnp.mean(xv * xv, axis=-1, keepdims=True) + EPS)
        xn = xv * r
        gn_v = gn_ref[...]
        dh_v = dh_ref[...]
        dsh_ref[...] += jnp.sum(dh_v, axis=0, keepdims=True)
        dsc_ref[...] += jnp.sum(dh_v * (xn * gn_v), axis=0, keepdims=True)
        dy = dh_v * (1.0 + sc_ref[...])
        dgn_ref[...] += jnp.sum(dy * xn, axis=0, keepdims=True)
        dxn = dy * gn_v
        dx = r * (dxn - xn * jnp.mean(dxn * xn, axis=-1, keepdims=True))
        dx_ref[...] = dr_ref[...] + dx

    vec = pl.BlockSpec((1, D), lambda i: (0, 0))
    row = pl.BlockSpec((tr, D), lambda i: (i, 0))
    vshape = jax.ShapeDtypeStruct((1, D), f32)
    return pl.pallas_call(
        body, name=name, grid=(T // tr,), in_specs=[row, vec, vec, row, row],
        out_specs=[row, vec, vec, vec],
        out_shape=[jax.ShapeDtypeStruct((T, D), f32), vshape, vshape, vshape],
        compiler_params=_cp(("arbitrary",)),
    )(x, gn, sc, dh, dx_res)


def gate_bwd(dx, y, g, name):
    T, D = dx.shape
    tr = _rows(T)

    def body(dx_ref, y_ref, g_ref, dy_ref, dg_ref):
        @pl.when(pl.program_id(0) == 0)
        def _():
            dg_ref[...] = jnp.zeros_like(dg_ref)

        dxv = dx_ref[...]
        dy_ref[...] = (dxv * g_ref[...]).astype(bf16)
        dg_ref[...] += jnp.sum(dxv * y_ref[...].astype(f32), axis=0, keepdims=True)

    vec = pl.BlockSpec((1, D), lambda i: (0, 0))
    row = pl.BlockSpec((tr, D), lambda i: (i, 0))
    return pl.pallas_call(
        body, name=name, grid=(T // tr,), in_specs=[row, row, vec], out_specs=[row, vec],
        out_shape=[jax.ShapeDtypeStruct((T, D), bf16), jax.ShapeDtypeStruct((1, D), f32)],
        compiler_params=_cp(("arbitrary",)),
    )(dx, y, g)


def loss_head(x, fg, tgt, name):
    T, D = x.shape
    tr = _rows(T)

    def body(x_ref, fg_ref, t_ref, loss_ref, dx_ref, dfg_ref):
        @pl.when(pl.program_id(0) == 0)
        def _():
            loss_ref[...] = jnp.zeros_like(loss_ref)
            dfg_ref[...] = jnp.zeros_like(dfg_ref)

        xv = x_ref[...]
        r = lax.rsqrt(jnp.mean(xv * xv, axis=-1, keepdims=True) + EPS)
        xn = xv * r
        fg_v = fg_ref[...]
        err = xn * fg_v - t_ref[...]
        per_tok = jnp.mean(err * err, axis=-1, keepdims=True)
        loss_ref[...] += 0.5 * jnp.sum(per_tok, axis=0, keepdims=True)
        dy = err * (1.0 / D)
        dfg_ref[...] += jnp.sum(dy * xn, axis=0, keepdims=True)
        dxn = dy * fg_v
        dx_ref[...] = r * (dxn - xn * jnp.mean(dxn * xn, axis=-1, keepdims=True))

    vec = pl.BlockSpec((1, D), lambda i: (0, 0))
    row = pl.BlockSpec((tr, D), lambda i: (i, 0))
    one = pl.BlockSpec((1, 1), lambda i: (0, 0))
    return pl.pallas_call(
        body, name=name, grid=(T // tr,), in_specs=[row, vec, row], out_specs=[one, row, vec],
        out_shape=[jax.ShapeDtypeStruct((1, 1), f32), jax.ShapeDtypeStruct((T, D), f32),
                   jax.ShapeDtypeStruct((1, D), f32)],
        compiler_params=_cp(("arbitrary",)),
    )(x, fg, tgt)


def _conv_tiles(T, FP):
    return _pick(T, (512, 256, 128, 64, 32, 16, 8)), _pick(FP, (512, 256, 128))


def _conv_specs(tr, tc, T, FP):
    nj = FP // tc
    r8 = tr // SUBLANES
    last8 = T // SUBLANES - 1
    main = pl.BlockSpec((2, tr, tc), lambda j, i: (0, i, j))
    prev = pl.BlockSpec((2, SUBLANES, tc), lambda j, i: (0, jnp.maximum(i * r8 - 1, 0), j))
    nxt = pl.BlockSpec((2, SUBLANES, tc), lambda j, i: (0, jnp.minimum((i + 1) * r8, last8), j))
    wg = pl.BlockSpec((CONV_W, tc), lambda j, i: (0, j))
    wu = pl.BlockSpec((CONV_W, tc), lambda j, i: (0, j + nj))
    bg = pl.BlockSpec((1, tc), lambda j, i: (0, j))
    bu = pl.BlockSpec((1, tc), lambda j, i: (0, j + nj))
    return main, prev, nxt, wg, wu, bg, bu


def _causal_taps(av, hp_ref, s, has_prev, row):
    h7 = jnp.where(has_prev, hp_ref[s, 7:8, :], 0.0)
    h6 = jnp.where(has_prev, hp_ref[s, 6:7, :], 0.0)
    m1 = jnp.where(row == 0, h7, pltpu.roll(av, 1, 0))
    m2 = jnp.where(row == 0, h6, jnp.where(row == 1, h7, pltpu.roll(av, 2, 0)))
    return m1, m2


def conv_glu_fwd(a3, conv_w, conv_b, name, riders=None):
    _, T, FP = a3.shape
    tr, tc = _conv_tiles(T, FP)
    main, prev, _, wg, wu, bg, bu = _conv_specs(tr, tc, T, FP)
    rd = Riders(riders)
    n_ex = len(rd.exs)
    ni = T // tr

    def body(*refs):
        in_refs, out_refs, scratch_refs = refs[:6 + n_ex], refs[6 + n_ex:7 + 2 * n_ex], refs[7 + 2 * n_ex:]
        rd.split(in_refs, out_refs, scratch_refs)
        a_ref, hp_ref, wg_ref, wu_ref, bg_ref, bu_ref = in_refs[:6]
        act_ref = out_refs[0]
        step = pl.program_id(0) * ni + pl.program_id(1)
        rd.before(step)
        has_prev = pl.program_id(1) > 0
        row = lax.broadcasted_iota(jnp.int32, (tr, tc), 0)

        def conv(s, w_ref, b_ref):
            av = a_ref[s]
            m1, m2 = _causal_taps(av, hp_ref, s, has_prev, row)
            return w_ref[0:1, :] * m2 + w_ref[1:2, :] * m1 + w_ref[2:3, :] * av + b_ref[...]

        gate = conv(0, wg_ref, bg_ref)
        up = conv(1, wu_ref, bu_ref)
        act_ref[...] = ((gate * jax.nn.sigmoid(gate)) * up).astype(bf16)
        rd.after(step, (FP // tc) * ni)

    outs = pl.pallas_call(
        body, name=name, grid=(FP // tc, ni),
        in_specs=[main, prev, wg, wu, bg, bu] + rd.in_specs,
        out_specs=[pl.BlockSpec((tr, tc), lambda j, i: (i, j))] + rd.out_specs,
        out_shape=[jax.ShapeDtypeStruct((T, FP), bf16)] + rd.out_shape,
        scratch_shapes=rd.scratch,
        compiler_params=_cp(("arbitrary", "arbitrary") if rd.exs else ("parallel", "parallel")),
    )(a3, a3, conv_w, conv_w, conv_b, conv_b, *rd.args)
    return outs if rd.exs else outs[0]


def conv_glu_bwd(a3, conv_w, conv_b, dact, name, riders=None):
    _, T, FP = a3.shape
    tr, tc = _conv_tiles(T, FP)
    ni = T // tr
    main, prev, nxt, wg, wu, bg, bu = _conv_specs(tr, tc, T, FP)
    r8 = tr // SUBLANES
    last8 = T // SUBLANES - 1
    d_main = pl.BlockSpec((tr, tc), lambda j, i: (i, j))
    d_next = pl.BlockSpec((SUBLANES, tc), lambda j, i: (jnp.minimum((i + 1) * r8, last8), j))
    rd = Riders(riders)
    n_ex = len(rd.exs)

    def body(*refs):
        in_refs, out_refs, scratch_refs = refs[:9 + n_ex], refs[9 + n_ex:14 + 2 * n_ex], refs[14 + 2 * n_ex:]
        rd.split(in_refs, out_refs, scratch_refs)
        a_ref, hp_ref, hn_ref, d_ref, dn_ref, wg_ref, wu_ref, bg_ref, bu_ref = in_refs[:9]
        da_ref, dwg_ref, dwu_ref, dbg_ref, dbu_ref = out_refs[:5]
        i = pl.program_id(1)
        step = pl.program_id(0) * ni + i
        rd.before(step)
        has_prev = i > 0
        has_next = i < ni - 1
        row = lax.broadcasted_iota(jnp.int32, (tr, tc), 0)
        row8 = lax.broadcasted_iota(jnp.int32, (SUBLANES, tc), 0)

        @pl.when(i == 0)
        def _():
            dwg_ref[...] = jnp.zeros_like(dwg_ref)
            dwu_ref[...] = jnp.zeros_like(dwu_ref)
            dbg_ref[...] = jnp.zeros_like(dbg_ref)
            dbu_ref[...] = jnp.zeros_like(dbu_ref)

        def prep(s, w_ref, b_ref):
            av = a_ref[s]
            m1, m2 = _causal_taps(av, hp_ref, s, has_prev, row)
            w0, w1, w2, bv = w_ref[0:1, :], w_ref[1:2, :], w_ref[2:3, :], b_ref[...]
            pre = w0 * m2 + w1 * m1 + w2 * av + bv
            an = hn_ref[s]
            l1 = a_ref[s, tr - 1:tr, :]
            l2 = a_ref[s, tr - 2:tr - 1, :]
            n1 = jnp.where(row8 == 0, l1, pltpu.roll(an, 1, 0))
            n2 = jnp.where(row8 == 0, l2, jnp.where(row8 == 1, l1, pltpu.roll(an, 2, 0)))
            pre_n = w0 * n2 + w1 * n1 + w2 * an + bv
            return av, m1, m2, pre, pre_n

        def glu_bwd(gate, up, d):
            sg = jax.nn.sigmoid(gate)
            dgate = d * up * (sg * (1.0 + gate * (1.0 - sg)))
            dup = d * (gate * sg)
            return dgate, dup

        def row_of(v8, r):
            return jnp.sum(jnp.where(row8 == r, v8, 0.0), axis=0, keepdims=True)

        def back(dc, dc_n, w_ref):
            n0, n1 = row_of(dc_n, 0), row_of(dc_n, 1)
            p1 = jnp.where(row == tr - 1, n0, pltpu.roll(dc, tr - 1, 0))
            p2 = jnp.where(row == tr - 1, n1, jnp.where(row == tr - 2, n0, pltpu.roll(dc, tr - 2, 0)))
            return w_ref[2:3, :] * dc + w_ref[1:2, :] * p1 + w_ref[0:1, :] * p2

        def tok_sum(v):
            return jnp.sum(v, axis=0, keepdims=True)

        ag, g1, g2, gate, gate_n = prep(0, wg_ref, bg_ref)
        au, u1, u2, up, up_n = prep(1, wu_ref, bu_ref)
        dcg, dcu = glu_bwd(gate, up, d_ref[...])
        dn = jnp.where(has_next, dn_ref[...], 0.0)
        dcg_n, dcu_n = glu_bwd(gate_n, up_n, dn)
        da_ref[0] = back(dcg, dcg_n, wg_ref).astype(bf16)
        da_ref[1] = back(dcu, dcu_n, wu_ref).astype(bf16)
        dwg_ref[0:1, :] += tok_sum(dcg * g2)
        dwg_ref[1:2, :] += tok_sum(dcg * g1)
        dwg_ref[2:3, :] += tok_sum(dcg * ag)
        dwu_ref[0:1, :] += tok_sum(dcu * u2)
        dwu_ref[1:2, :] += tok_sum(dcu * u1)
        dwu_ref[2:3, :] += tok_sum(dcu * au)
        dbg_ref[...] += tok_sum(dcg)
        dbu_ref[...] += tok_sum(dcu)
        rd.after(step, (FP // tc) * ni)

    w_out = pl.BlockSpec((CONV_W, tc), lambda j, i: (0, j))
    b_out = pl.BlockSpec((1, tc), lambda j, i: (0, j))
    return pl.pallas_call(
        body, name=name, grid=(FP // tc, ni),
        in_specs=[main, prev, nxt, d_main, d_next, wg, wu, bg, bu] + rd.in_specs,
        out_specs=[main, w_out, w_out, b_out, b_out] + rd.out_specs,
        out_shape=[jax.ShapeDtypeStruct((2, T, FP), bf16),
                   jax.ShapeDtypeStruct((CONV_W, FP), f32), jax.ShapeDtypeStruct((CONV_W, FP), f32),
                   jax.ShapeDtypeStruct((1, FP), f32), jax.ShapeDtypeStruct((1, FP), f32)] + rd.out_shape,
        scratch_shapes=rd.scratch,
        compiler_params=_cp(("arbitrary", "arbitrary") if rd.exs else ("parallel", "arbitrary")),
    )(a3, a3, a3, dact, dact, conv_w, conv_w, conv_b, conv_b, *rd.args)


_GELU_C = 0.7978845608028654
_GELU_A = 0.044715


def _gelu(x):
    return 0.5 * x * (1.0 + jnp.tanh(_GELU_C * (x + _GELU_A * (x * x * x))))


def _gelu_and_grad(x):
    t = jnp.tanh(_GELU_C * (x + _GELU_A * (x * x * x)))
    g = 0.5 * x * (1.0 + t)
    dg = 0.5 * (1.0 + t) + 0.5 * x * (1.0 - t * t) * (_GELU_C * (1.0 + 3.0 * _GELU_A * (x * x)))
    return g, dg


def _tril_bf16(w):
    r = lax.broadcasted_iota(jnp.int32, w.shape, 0)
    c = lax.broadcasted_iota(jnp.int32, w.shape, 1)
    return jnp.where(r >= c, w, 0.0).astype(bf16)


def gm_gate_fwd(z, vg, ws, bs_t, name):
    T, D2 = z.shape
    D = D2 // 2
    G = D // GM_GROUP
    tr = _pick(T, (256, 128))
    nc = tr // CHUNK

    def body(z_ref, vg_ref, ws_ref, bs_ref, o_ref):
        u = _gelu(z_ref[:, :D])
        v = _gelu(z_ref[:, D:])
        rv = lax.rsqrt(jnp.mean(v * v, axis=-1, keepdims=True) + EPS)
        vn = ((v * rv) * vg_ref[...]).astype(bf16)
        for g in range(G):
            wg = _tril_bf16(ws_ref[g])
            bg = bs_ref[:, g:g + 1]
            cs = slice(g * GM_GROUP, (g + 1) * GM_GROUP)
            for c in range(nc):
                rs = slice(c * CHUNK, (c + 1) * CHUNK)
                sv = jnp.dot(wg, vn[rs, cs], preferred_element_type=f32) + bg
                o_ref[rs, cs] = (u[rs, cs] * sv).astype(bf16)

    return pl.pallas_call(
        body, name=name, grid=(T // tr,),
        in_specs=[pl.BlockSpec((tr, D2), lambda i: (i, 0)),
                  pl.BlockSpec((1, D), lambda i: (0, 0)),
                  pl.BlockSpec((G, CHUNK, CHUNK), lambda i: (0, 0, 0)),
                  pl.BlockSpec((CHUNK, G), lambda i: (0, 0))],
        out_specs=pl.BlockSpec((tr, D), lambda i: (i, 0)),
        out_shape=jax.ShapeDtypeStruct((T, D), bf16),
        compiler_params=_cp(("parallel",)),
    )(z, vg, ws, bs_t)


def gm_gate_bwd(z, vg, ws, bs_t, dgated, name):
    T, D2 = z.shape
    D = D2 // 2
    G = D // GM_GROUP
    tr = _pick(T, (256, 128))
    nc = tr // CHUNK

    def body(z_ref, vg_ref, ws_ref, bs_ref, dg_ref, dz_ref, dws_ref, dbs_ref, dvg_ref,
             du_s, dvn_s):
        @pl.when(pl.program_id(0) == 0)
        def _():
            dws_ref[...] = jnp.zeros_like(dws_ref)
            dbs_ref[...] = jnp.zeros_like(dbs_ref)
            dvg_ref[...] = jnp.zeros_like(dvg_ref)

        u, du_dz = _gelu_and_grad(z_ref[:, :D])
        v, dv_dz = _gelu_and_grad(z_ref[:, D:])
        rv = lax.rsqrt(jnp.mean(v * v, axis=-1, keepdims=True) + EPS)
        vhat = v * rv
        vg_v = vg_ref[...]
        vn = (vhat * vg_v).astype(bf16)
        rr = lax.broadcasted_iota(jnp.int32, (CHUNK, CHUNK), 0)
        cc = lax.broadcasted_iota(jnp.int32, (CHUNK, CHUNK), 1)
        for g in range(G):
            wg = _tril_bf16(ws_ref[g])
            bg = bs_ref[:, g:g + 1]
            cs = slice(g * GM_GROUP, (g + 1) * GM_GROUP)
            dw_acc = jnp.zeros((CHUNK, CHUNK), f32)
            db_acc = jnp.zeros((CHUNK, 1), f32)
            for c in range(nc):
                rs = slice(c * CHUNK, (c + 1) * CHUNK)
                vb = vn[rs, cs]
                sv = jnp.dot(wg, vb, preferred_element_type=f32) + bg
                dgb = dg_ref[rs, cs]
                du_s[rs, cs] = dgb * sv
                dsv = dgb * u[rs, cs]
                dsv_b = dsv.astype(bf16)
                dw_acc += lax.dot_general(dsv_b, vb, (((1,), (1,)), ((), ())),
                                          preferred_element_type=f32)
                db_acc += jnp.sum(dsv, axis=1, keepdims=True)
                dvn_s[rs, cs] = lax.dot_general(wg, dsv_b, (((0,), (0,)), ((), ())),
                                                preferred_element_type=f32)
            dws_ref[g] += jnp.where(rr >= cc, dw_acc, 0.0)
            dbs_ref[:, g:g + 1] += db_acc
        dz_ref[:, :D] = (du_s[...] * du_dz).astype(bf16)
        dvn = dvn_s[...]
        dvg_ref[...] += jnp.sum(dvn * vhat, axis=0, keepdims=True)
        dvh = dvn * vg_v
        dv = rv * (dvh - vhat * jnp.mean(dvh * vhat, axis=-1, keepdims=True))
        dz_ref[:, D:] = (dv * dv_dz).astype(bf16)

    return pl.pallas_call(
        body, name=name, grid=(T // tr,),
        in_specs=[pl.BlockSpec((tr, D2), lambda i: (i, 0)),
                  pl.BlockSpec((1, D), lambda i: (0, 0)),
                  pl.BlockSpec((G, CHUNK, CHUNK), lambda i: (0, 0, 0)),
                  pl.BlockSpec((CHUNK, G), lambda i: (0, 0)),
                  pl.BlockSpec((tr, D), lambda i: (i, 0))],
        out_specs=[pl.BlockSpec((tr, D2), lambda i: (i, 0)),
                   pl.BlockSpec((G, CHUNK, CHUNK), lambda i: (0, 0, 0)),
                   pl.BlockSpec((CHUNK, G), lambda i: (0, 0)),
                   pl.BlockSpec((1, D), lambda i: (0, 0))],
        out_shape=[jax.ShapeDtypeStruct((T, D2), bf16),
                   jax.ShapeDtypeStruct((G, CHUNK, CHUNK), f32),
                   jax.ShapeDtypeStruct((CHUNK, G), f32),
                   jax.ShapeDtypeStruct((1, D), f32)],
        scratch_shapes=[pltpu.VMEM((tr, D), f32), pltpu.VMEM((tr, D), f32)],
        compiler_params=_cp(("arbitrary",)),
    )(z, vg, ws, bs_t, dgated)


def fox_gates_fwd(flog_t, b_col, name):
    H, T = flog_t.shape

    def body(fl_ref, b_ref, o_ref):
        xv = fl_ref[...] + b_ref[...]
        lf = jnp.minimum(xv, 0.0) - jnp.log1p(jnp.exp(-jnp.abs(xv)))
        lane = lax.broadcasted_iota(jnp.int32, (H, T), 1)
        s = 1
        while s < T:
            lf = lf + jnp.where(lane >= s, pltpu.roll(lf, s, 1), 0.0)
            s *= 2
        o_ref[...] = lf * LOG2E

    return pl.pallas_call(
        body, name=name, out_shape=jax.ShapeDtypeStruct((H, T), f32),
        compiler_params=pltpu.CompilerParams(vmem_limit_bytes=VMEM_LIMIT_BYTES),
    )(flog_t, b_col)


def fox_gates_bwd(flog_t, b_col, dF, name):
    H, T = flog_t.shape

    def body(fl_ref, b_ref, d_ref, o_ref, db_ref):
        xv = fl_ref[...] + b_ref[...]
        g = d_ref[...]
        lane = lax.broadcasted_iota(jnp.int32, (H, T), 1)
        s = 1
        while s < T:
            g = g + jnp.where(lane < T - s, pltpu.roll(g, T - s, 1), 0.0)
            s *= 2
        dfl = g * jax.nn.sigmoid(-xv)
        o_ref[...] = dfl
        db_ref[...] = jnp.sum(dfl, axis=1, keepdims=True)

    return pl.pallas_call(
        body, name=name,
        out_shape=[jax.ShapeDtypeStruct((H, T), f32), jax.ShapeDtypeStruct((H, 1), f32)],
        compiler_params=pltpu.CompilerParams(vmem_limit_bytes=VMEM_LIMIT_BYTES),
    )(flog_t, b_col, dF)


_NT = (((1,), (1,)), ((), ()))
_TN = (((0,), (0,)), ((), ()))


def _scores(q, k, fq, fk, col0=None):
    s = lax.dot_general(q, k, _NT, preferred_element_type=f32) * (HEAD_DIM ** -0.5 * LOG2E)
    s = s + fq - fk
    if col0 is not None:
        rows = lax.broadcasted_iota(jnp.int32, s.shape, 0)
        cols = col0 + lax.broadcasted_iota(jnp.int32, s.shape, 1)
        s = jnp.where(cols <= rows, s, NEG)
    return s


def fox_attn_fwd(qkv, f_row, f_col, name, riders=None):
    T, D3 = qkv.shape
    D = D3 // 3
    H = D // HEAD_DIM
    tq = _pick(T, (ATT_Q_TILE, 512, 256))
    tk = tq // 2
    nq = T // tq
    rd = Riders(riders)
    pairs = [(i, j) for i in range(nq) for j in range(2 * i + 2)]
    i_tab = np.array([p[0] for p in pairs], np.int32)
    j_tab = np.array([p[1] for p in pairs], np.int32)

    def body(i_ref, j_ref, *refs):
        in_refs, out_refs, scratch_refs = refs[:5 + len(rd.exs)], refs[5 + len(rd.exs):7 + 2 * len(rd.exs)], refs[7 + 2 * len(rd.exs):]
        rd.split(in_refs, out_refs, scratch_refs)
        q_ref, k_ref, v_ref, fq_ref, fk_ref = in_refs[:5]
        o_ref, lse_ref = out_refs[:2]
        m_s, l_s, acc_s = scratch_refs[:3]
        t = pl.program_id(1)
        i, j = i_ref[t], j_ref[t]
        step = pl.program_id(0) * len(pairs) + t
        rd.before(step)

        @pl.when(j == 0)
        def _():
            m_s[...] = jnp.full_like(m_s, NEG)
            l_s[...] = jnp.zeros_like(l_s)
            acc_s[...] = jnp.zeros_like(acc_s)

        def update(col0):
            s = _scores(q_ref[...], k_ref[...], fq_ref[0], fk_ref[0], col0)
            m_prev = m_s[...]
            m_new = jnp.maximum(m_prev, jnp.max(s, axis=1, keepdims=True))
            alpha = jnp.exp2(m_prev - m_new)
            p = jnp.exp2(s - m_new)
            l_s[...] = alpha * l_s[...] + jnp.sum(p, axis=1, keepdims=True)
            acc_s[...] = alpha * acc_s[...] + jnp.dot(p.astype(bf16), v_ref[...],
                                                      preferred_element_type=f32)
            m_s[...] = m_new

        @pl.when(j < 2 * i)
        def _():
            update(None)

        @pl.when(j >= 2 * i)
        def _():
            update((j - 2 * i) * tk)

        @pl.when(j == 2 * i + 1)
        def _():
            o_ref[...] = (acc_s[...] / l_s[...]).astype(bf16)
            lse_ref[0] = m_s[...] + jnp.log2(l_s[...])

        rd.after(step, H * len(pairs))

    blk = (tq, HEAD_DIM)
    kblk = (tk, HEAD_DIM)
    grid_spec = pltpu.PrefetchScalarGridSpec(
        num_scalar_prefetch=2, grid=(H, len(pairs)),
        in_specs=[pl.BlockSpec(blk, lambda h, t, it, jt: (it[t], h)),
                  pl.BlockSpec(kblk, lambda h, t, it, jt: (jt[t], H + h)),
                  pl.BlockSpec(kblk, lambda h, t, it, jt: (jt[t], 2 * H + h)),
                  pl.BlockSpec((1, tq, 1), lambda h, t, it, jt: (h, it[t], 0)),
                  pl.BlockSpec((1, 1, tk), lambda h, t, it, jt: (h, 0, jt[t]))] + rd.in_specs,
        out_specs=[pl.BlockSpec(blk, lambda h, t, it, jt: (it[t], h)),
                   pl.BlockSpec((1, tq, 1), lambda h, t, it, jt: (h, it[t], 0))] + rd.out_specs,
        scratch_shapes=[pltpu.VMEM((tq, 1), f32), pltpu.VMEM((tq, 1), f32),
                        pltpu.VMEM((tq, HEAD_DIM), f32)] + rd.scratch)
    return pl.pallas_call(
        body, name=name, grid_spec=grid_spec,
        out_shape=[jax.ShapeDtypeStruct((T, D), bf16), jax.ShapeDtypeStruct((H, T, 1), f32)] + rd.out_shape,
        compiler_params=_cp(("arbitrary", "arbitrary") if rd.exs else ("parallel", "arbitrary")),
    )(i_tab, j_tab, qkv, qkv, qkv, f_col, f_row, *rd.args)


def fox_attn_bwd(qkv, o, do, lse, f_row, f_col, name, riders=None):
    T, D3 = qkv.shape
    D = D3 // 3
    H = D // HEAD_DIM
    tq = _pick(T, (ATT_Q_TILE, 512, 256))
    tk = tq // 2
    nq, nk = T // tq, T // tk
    scale = HEAD_DIM ** -0.5

    pairs = [(j, i) for j in range(nk) for i in range(j // 2, nq)]
    j_tab = np.array([p[0] for p in pairs], np.int32)
    i_tab = np.array([p[1] for p in pairs], np.int32)

    rd = Riders(riders)
    n_ex = len(rd.exs)

    def body(j_ref, i_ref, *refs):
        in_refs, out_refs, scratch_refs = refs[:8 + n_ex], refs[8 + n_ex:13 + 2 * n_ex], refs[13 + 2 * n_ex:]
        rd.split(in_refs, out_refs, scratch_refs)
        q_ref, k_ref, v_ref, o_ref, do_ref, lse_ref, fq_ref, fk_ref = in_refs[:8]
        dq_ref, dk_ref, dv_ref, cs_ref, rs_ref = out_refs[:5]
        dk_s, dv_s, dq_s, di_s = scratch_refs[:4]
        step = pl.program_id(1)
        rd.before(pl.program_id(0) * len(pairs) + step)
        j, i = j_ref[step], i_ref[step]
        rows = pl.ds(pl.multiple_of(i * tq, tq), tq)
        first_i = j // 2

        @pl.when(step == 0)
        def _():
            dq_s[...] = jnp.zeros_like(dq_s)
            rs_ref[...] = jnp.zeros_like(rs_ref)

        @pl.when(j == 0)
        def _():
            di_s[rows, :] = jnp.sum(do_ref[...] * o_ref[...].astype(f32), axis=1, keepdims=True)

        @pl.when(i == first_i)
        def _():
            dk_s[...] = jnp.zeros_like(dk_s)
            dv_s[...] = jnp.zeros_like(dv_s)
            cs_ref[...] = jnp.zeros_like(cs_ref)

        def accumulate(col0):
            q = q_ref[...]
            k = k_ref[...]
            s = _scores(q, k, fq_ref[0], fk_ref[0], col0)
            p = jnp.exp2(s - lse_ref[0])
            do_b = do_ref[...].astype(bf16)
            dp = lax.dot_general(do_b, v_ref[...], _NT, preferred_element_type=f32)
            ds = p * (dp - di_s[rows, :])
            ds_b = (ds * scale).astype(bf16)
            cs_ref[0] += jnp.sum(ds, axis=0, keepdims=True)
            rs_ref[0, rows, :] += jnp.sum(ds, axis=1, keepdims=True)
            dv_s[...] += lax.dot_general(p.astype(bf16), do_b, _TN, preferred_element_type=f32)
            dk_s[...] += lax.dot_general(ds_b, q, _TN, preferred_element_type=f32)
            dq_s[rows, :] += jnp.dot(ds_b, k, preferred_element_type=f32)

        @pl.when(i == first_i)
        def _():
            accumulate((j - 2 * i) * tk)

        @pl.when(i > first_i)
        def _():
            accumulate(None)

        @pl.when(i == nq - 1)
        def _():
            dk_ref[...] = dk_s[...].astype(bf16)
            dv_ref[...] = dv_s[...].astype(bf16)

        @pl.when(step == len(pairs) - 1)
        def _():
            dq_ref[...] = dq_s[...].astype(bf16)

        rd.after(pl.program_id(0) * len(pairs) + step, H * len(pairs))

    blk = (tq, HEAD_DIM)
    kblk = (tk, HEAD_DIM)
    at_q = lambda h, s, jt, it: (it[s], h)
    col_q = pl.BlockSpec((1, tq, 1), lambda h, s, jt, it: (h, it[s], 0))
    row_k = pl.BlockSpec((1, 1, tk), lambda h, s, jt, it: (h, 0, jt[s]))
    grid_spec = pltpu.PrefetchScalarGridSpec(
        num_scalar_prefetch=2, grid=(H, len(pairs)),
        in_specs=[pl.BlockSpec(blk, at_q),
                  pl.BlockSpec(kblk, lambda h, s, jt, it: (jt[s], H + h)),
                  pl.BlockSpec(kblk, lambda h, s, jt, it: (jt[s], 2 * H + h)),
                  pl.BlockSpec(blk, at_q), pl.BlockSpec(blk, at_q), col_q, col_q, row_k] + rd.in_specs,
        out_specs=[pl.BlockSpec((T, HEAD_DIM), lambda h, s, jt, it: (0, h)),
                   pl.BlockSpec(kblk, lambda h, s, jt, it: (jt[s], h)),
                   pl.BlockSpec(kblk, lambda h, s, jt, it: (jt[s], h)),
                   row_k,
                   pl.BlockSpec((1, T, 1), lambda h, s, jt, it: (h, 0, 0))] + rd.out_specs,
        scratch_shapes=[pltpu.VMEM(kblk, f32), pltpu.VMEM(kblk, f32),
                        pltpu.VMEM((T, HEAD_DIM), f32), pltpu.VMEM((T, 1), f32)] + rd.scratch)
    return pl.pallas_call(
        body, name=name, grid_spec=grid_spec,
        out_shape=[jax.ShapeDtypeStruct((T, D), bf16), jax.ShapeDtypeStruct((T, D), bf16),
                   jax.ShapeDtypeStruct((T, D), bf16), jax.ShapeDtypeStruct((H, 1, T), f32),
                   jax.ShapeDtypeStruct((H, T, 1), f32)] + rd.out_shape,
        compiler_params=_cp(("arbitrary", "arbitrary") if rd.exs else ("parallel", "arbitrary")),
    )(j_tab, i_tab, qkv, qkv, qkv, o, do, lse, f_col, f_row, *rd.args)


def mod_fwd(c16, mod_w, mod_b_loc, name):
    L, D, MW = mod_w.shape
    tn = _pick(MW, (512, 256, 128))

    def body(c_ref, w_ref, b_ref, o_ref):
        cv = c_ref[...]
        ca = (cv * jax.nn.sigmoid(cv)).astype(bf16)
        o_ref[...] = jnp.dot(ca, w_ref[...].astype(bf16), preferred_element_type=f32) + b_ref[...]

    return pl.pallas_call(
        body, name=name, grid=(L, MW // tn),
        in_specs=[pl.BlockSpec((16, D), lambda l, j: (0, 0)),
                  pl.BlockSpec((None, D, tn), lambda l, j: (l, 0, j)),
                  pl.BlockSpec((None, 1, tn), lambda l, j: (l, 0, j))],
        out_specs=pl.BlockSpec((None, 16, tn), lambda l, j: (l, 0, j)),
        out_shape=jax.ShapeDtypeStruct((L, 16, MW), f32),
        compiler_params=_cp(("parallel", "parallel")),
    )(c16, mod_w, mod_b_loc)


def mod_w_bwd(c_t, dmod, name):
    D = c_t.shape[0]
    L, _, MW = dmod.shape
    tn = _pick(MW, (512, 256, 128))

    def body(c_ref, d_ref, o_ref):
        cv = c_ref[...]
        ca = (cv * jax.nn.sigmoid(cv)).astype(bf16)
        o_ref[...] = jnp.dot(ca, d_ref[...].astype(bf16), preferred_element_type=f32)

    return pl.pallas_call(
        body, name=name, grid=(L, MW // tn),
        in_specs=[pl.BlockSpec((D, LANES), lambda l, j: (0, 0)),
                  pl.BlockSpec((None, LANES, tn), lambda l, j: (l, 0, j))],
        out_specs=pl.BlockSpec((None, D, tn), lambda l, j: (l, 0, j)),
        out_shape=jax.ShapeDtypeStruct((L, D, MW), f32),
        compiler_params=_cp(("parallel", "parallel")),
    )(c_t, dmod)


def adamw(w, g, m, v, name, riders=None):
    shape = w.shape
    C = shape[-1] if w.ndim >= 1 else 1
    R = max(w.size // C, 1)
    w2, g2, m2, v2 = (t.reshape(R, C) for t in (w, g, m, v))
    tr = R
    for cand in (2048, 1024, 512, 256, 128, 64, 32, 16, 8):
        if R % cand == 0 and cand * _round_up(C, LANES) <= 256 * 1024:
            tr = cand
            break
    rd = Riders(riders)
    n_ex = len(rd.exs)

    def body(*refs):
        in_refs, out_refs, scratch_refs = refs[:4 + n_ex], refs[4 + n_ex:7 + 2 * n_ex], refs[7 + 2 * n_ex:]
        rd.split(in_refs, out_refs, scratch_refs)
        w_ref, g_ref, m_ref, v_ref = in_refs[:4]
        d_ref, mo_ref, vo_ref = out_refs[:3]
        rd.before(pl.program_id(0))
        gv = g_ref[...]
        mn = ADAM_B1 * m_ref[...] + (1.0 - ADAM_B1) * gv
        vn = ADAM_B2 * v_ref[...] + (1.0 - ADAM_B2) * (gv * gv)
        m_hat = mn / (1.0 - ADAM_B1 ** ADAM_STEP)
        v_hat = vn / (1.0 - ADAM_B2 ** ADAM_STEP)
        d_ref[...] = -ADAM_LR * (m_hat / (jnp.sqrt(v_hat) + ADAM_EPS) + ADAM_WD * w_ref[...])
        mo_ref[...] = mn
        vo_ref[...] = vn
        rd.after(pl.program_id(0), R // tr)

    spec = pl.BlockSpec((tr, C), lambda i: (i, 0))
    sds = jax.ShapeDtypeStruct((R, C), f32)
    outs = pl.pallas_call(
        body, name=name, grid=(R // tr,), in_specs=[spec] * 4 + rd.in_specs,
        out_specs=[spec] * 3 + rd.out_specs, out_shape=[sds, sds, sds] + rd.out_shape,
        scratch_shapes=rd.scratch,
        compiler_params=_cp(("arbitrary",) if rd.exs else ("parallel",)),
    )(w2, g2, m2, v2, *rd.args)
    return tuple(t.reshape(shape) for t in outs[:3]) + tuple(outs[3:])


def reduce_scatter_tail(pair, tag):
    quad = run_exchange(chip_exchange(pair), "rs_chip_exchange_" + tag)
    return sum_slots(quad, "rs_final_sum_" + tag)


def kernel(x, c, mod_w, mod_b, mix_norm_g, ffn_norm_g, attn_w_in, attn_b_f, attn_w_o, gm_w_in, gm_v_g, gm_w_s, gm_b_s, gm_w_o, ffn_w_in, ffn_conv_w, ffn_conv_b, ffn_w_out, final_g, loss_target, m_mod_w, m_mod_b, m_mix_norm_g, m_ffn_norm_g, m_attn_w_in, m_attn_b_f, m_attn_w_o, m_gm_w_in, m_gm_v_g, m_gm_w_s, m_gm_b_s, m_gm_w_o, m_ffn_w_in, m_ffn_conv_w, m_ffn_conv_b, m_ffn_w_out, m_final_g, v_mod_w, v_mod_b, v_mix_norm_g, v_ffn_norm_g, v_attn_w_in, v_attn_b_f, v_attn_w_o, v_gm_w_in, v_gm_v_g, v_gm_w_s, v_gm_b_s, v_gm_w_o, v_ffn_w_in, v_ffn_conv_w, v_ffn_conv_b, v_ffn_w_out, v_final_g):
    xi, yi, ci = lax.axis_index("x"), lax.axis_index("y"), lax.axis_index("c")
    me = 4 * xi + 2 * yi + ci

    _, T, D = x.shape
    L = mod_w.shape[0]
    MW = mod_w.shape[2]
    NA, _, QW = attn_w_in.shape
    NB = gm_w_in.shape[0]
    H = D // HEAD_DIM
    G = D // GM_GROUP
    DR = attn_w_o.shape[1]
    GW = gm_w_in.shape[2]
    FW = ffn_w_in.shape[2]
    FR = ffn_w_out.shape[1]
    FRP = _round_up(FR, LANES // 2)
    FWP = 2 * FRP
    FP = N_CHIPS * FWP
    DFF2 = N_DEV * FW
    assert 2 * FR == FW and N_DEV * QW == 3 * D + H and N_DEV * GW == 2 * D
    c_idx = ci.reshape(1).astype(jnp.int32)

    def pad_ff(t, axis, blocks):
        ax = axis % t.ndim
        t = t.reshape(t.shape[:ax] + (blocks, FR) + t.shape[ax + 1:])
        pad = [(0, 0)] * t.ndim
        pad[ax + 1] = (0, FRP - FR)
        t = jnp.pad(t, pad)
        return t.reshape(t.shape[:ax] + (blocks * FRP,) + t.shape[ax + 2:])

    def unpad_ff(t, axis, blocks):
        ax = axis % t.ndim
        t = t.reshape(t.shape[:ax] + (blocks, FRP) + t.shape[ax + 1:])
        t = lax.slice_in_dim(t, 0, FR, axis=ax + 1)
        return t.reshape(t.shape[:ax] + (blocks * FR,) + t.shape[ax + 2:])

    x0 = x[0]
    tgt = loss_target[0]

    c_all = all_gather(c, "gather_c").reshape(N_DEV, D)
    cw_loc = pad_ff(ffn_conv_w, 2, 2).reshape(L * CONV_W, FWP)
    conv_w_full = all_gather(cw_loc, "gather_conv_w").transpose(1, 0, 2).reshape(L, CONV_W, 2 * FP)
    vg_full = all_gather(gm_v_g, "gather_vg").transpose(1, 0, 2).reshape(NB, 1, D)
    conv_b_full = pad_ff(ffn_conv_b, 1, 2 * N_DEV).reshape(L, 1, 2 * FP)

    c16 = jnp.pad(c_all, ((0, 16 - N_DEV), (0, 0)))
    mod_b_loc = lax.dynamic_slice_in_dim(mod_b, me * MW, MW, axis=1).reshape(L, 1, MW)
    mod_part = mod_fwd(c16, mod_w, mod_b_loc, "mod_fwd")[:, :N_DEV]
    mod_all = all_gather(mod_part, "gather_mod")
    mod_me = lax.dynamic_index_in_dim(mod_all, me, axis=2, keepdims=False)
    mod_me = mod_me.transpose(1, 0, 2).reshape(L, 6, 1, D)

    w_ai_t = jnp.swapaxes(attn_w_in, 1, 2).astype(bf16)
    w_gi_t = jnp.swapaxes(gm_w_in, 1, 2).astype(bf16)
    w_fi_t = pad_ff(jnp.swapaxes(ffn_w_in, 1, 2).astype(bf16), 1, 2)
    w_ao_l = attn_w_o.astype(bf16)
    w_go_l = gm_w_o.astype(bf16)
    w_fo_l = jnp.pad(ffn_w_out.astype(bf16), ((0, 0), (0, FRP - FR), (0, 0)))

    stash = []
    arrived = {}
    xc = x0

    def shard(kind, layer):
        even = layer % 2 == 0
        return {"ffn_in": w_fi_t, "ffn_out": w_fo_l, "mix_in": w_ai_t if even else w_gi_t,
                "mix_out": w_ao_l if even else w_go_l}[kind][layer if kind.startswith("ffn") else layer // 2]

    def need(kind, layer):
        if (kind, layer) not in arrived:
            arrived[(kind, layer)] = all_gather(shard(kind, layer), "gather_" + kind)
        return arrived.pop((kind, layer))

    def hosting(keys, n_own, fn, *args, **kw):
        keys = [k for k in keys if k[1] < L]
        outs = fn(*args, riders=[gather_exchange(shard(*k)) for k in keys], **kw)
        outs = list(outs) if isinstance(outs, (list, tuple)) else [outs]
        arrived.update(zip(keys, outs[n_own:]))
        return outs[:n_own]

    for i in range(L):
        sh1, sc1, g1, sh2, sc2, g2 = (mod_me[i, k] for k in range(6))
        jm = i // 2
        st = {"x_in": xc}
        h = norm_mod_fwd(xc, mix_norm_g[i][None], sc1, sh1, "norm_mod_fwd")
        st["h"] = h
        w_mi = need("mix_in", i)
        w_mo = need("mix_out", i).reshape(D, D)
        if i % 2 == 0:
            w_in_t = w_mi.reshape(N_DEV * QW, D)
            w_qkv_t = w_in_t[:3 * D]
            w_f_t = jnp.pad(w_in_t[3 * D:], ((0, LANES - H), (0, 0)))
            qkv = matmul(h, w_qkv_t, name="fox_qkv", tb=True, out_dtype=bf16)
            flog = matmul(h, w_f_t, name="fox_flog", tb=True)
            flog_t = flog[:, :H].T
            b_col = attn_b_f[jm][:, None]
            F = fox_gates_fwd(flog_t, b_col, "fox_gates_fwd")
            f_row, f_col = F[:, None, :], F[:, :, None]
            o, lse = hosting([("ffn_in", i), ("ffn_in", i + 1), ("mix_in", i + 1), ("mix_out", i + 1)], 2,
                             fox_attn_fwd, qkv, f_row, f_col, "fox_attn_fwd")
            x1, y = matmul(o, w_mo, name="mix_out", resid=xc, gvec=g1, emit_acc=True)
            st.update(qkv=qkv, flog_t=flog_t, b_col=b_col, f_row=f_row, f_col=f_col, o=o, lse=lse,
                      w_qkv_t=w_qkv_t, w_f_t=w_f_t, w_mo=w_mo)
        else:
            w_gi_full = w_mi.reshape(2 * D, D)
            z = matmul(h, w_gi_full, name="gm_in", tb=True)
            bs_t = gm_b_s[jm].T
            gated = gm_gate_fwd(z, vg_full[jm], gm_w_s[jm], bs_t, "gm_gate_fwd")
            x1, y = matmul(gated, w_mo, name="mix_out", resid=xc, gvec=g1, emit_acc=True)
            st.update(z=z, bs_t=bs_t, gated=gated, w_gi_full=w_gi_full, w_mo=w_mo)
        st.update(y=y, x1=x1)
        h2 = norm_mod_fwd(x1, ffn_norm_g[i][None], sc2, sh2, "norm_mod_fwd")
        w_fi_full = need("ffn_in", i).reshape(2 * FP, D)
        fox = i % 2 == 0
        a3, = hosting([("ffn_out", i) if fox else ("mix_in", i + 1)], 1,
                      matmul, h2, w_fi_full, name="ffn_up", tb=True, out_split=True)
        act = conv_glu_fwd(a3, conv_w_full[i], conv_b_full[i], "conv_glu_fwd")
        w_fo_full = need("ffn_out", i).reshape(FP, D)
        xc, f_out = hosting([("ffn_out", i + 1) if fox else ("mix_out", i + 1)], 2,
                            matmul, act, w_fo_full, name="ffn_down", resid=x1, gvec=g2, emit_acc=True)
        st.update(h2=h2, a3=a3, act=act, f=f_out, w_fi_full=w_fi_full, w_fo_full=w_fo_full)
        stash.append(st)

    loss_part, dx, d_final_g = loss_head(xc, final_g[None], tgt, "loss_head")
    loss = lax.psum(loss_part[0, 0], AXES)

    d_mod = [None] * L
    d_mix_g = [None] * L
    d_ffn_g = [None] * L
    d_conv_w = [None] * L
    d_conv_b = [None] * L
    g_wfi = [None] * L
    g_wfo = [None] * L
    g_wai = [None] * NA
    g_wao = [None] * NA
    d_bf = [None] * NA
    g_wgi = [None] * NB
    g_wgo = [None] * NB
    d_ws = [None] * NB
    d_bs = [None] * NB
    d_vg = [None] * NB
    ffn_pairs = None
    mix_pairs = None

    def finish_mixer(layer, quad_mo, quad_in):
        g_mo = sum_slots(quad_mo, "rs_final_sum_mix_out")
        g_in = sum_slots(quad_in, "rs_final_sum_mix_in").T
        if layer % 2 == 0:
            g_wao[layer // 2], g_wai[layer // 2] = g_mo, g_in
        else:
            g_wgo[layer // 2], g_wgi[layer // 2] = g_mo, g_in

    for i in reversed(range(L)):
        st = stash[i]
        sh1, sc1, g1, sh2, sc2, g2 = (mod_me[i, k] for k in range(6))
        jm = i // 2
        dy, dg2 = gate_bwd(dx, st["f"], g2, "gate_bwd")
        if mix_pairs is None:
            dact = matmul(dy, st["w_fo_full"], name="ffn_down_dx", tb=True)
        else:
            dact, quad_mo = matmul(dy, st["w_fo_full"], name="ffn_down_dx", tb=True,
                                   riders=[chip_exchange(mix_pairs[1])])
        if ffn_pairs is None:
            dw_fo = matmul(st["act"], dy, name="ffn_down_dw", ta=True, out_dtype=bf16)
        else:
            dw_fo, quad = matmul(st["act"], dy, name="ffn_down_dw", ta=True, out_dtype=bf16,
                                 riders=[chip_exchange(ffn_pairs[0])])
            g_wfo[i + 1] = sum_slots(quad, "rs_final_sum_ffn_out")[:FR]
        g8_fo = dw_fo.reshape(N_DEV, FRP, D)
        da3, dwg, dwu, dbg, dbu, got_fo = conv_glu_bwd(st["a3"], conv_w_full[i], conv_b_full[i], dact,
                                                       "conv_glu_bwd", riders=[pair_exchange(g8_fo)])
        pair_fo = pair_sum(g8_fo, got_fo, c_idx, "rs_pair_sum_ffn_out")
        d_conv_w[i] = jnp.concatenate([dwg, dwu], axis=1)
        d_conv_b[i] = jnp.concatenate([dbg, dbu], axis=1)
        if ffn_pairs is None:
            dw_fi_t = matmul(da3, st["h2"], name="ffn_up_dw", ta=True, a_split=True, out_dtype=bf16)
        else:
            dw_fi_t, quad = matmul(da3, st["h2"], name="ffn_up_dw", ta=True, a_split=True, out_dtype=bf16,
                                   riders=[chip_exchange(ffn_pairs[1])])
            g_wfi[i + 1] = unpad_ff(sum_slots(quad, "rs_final_sum_ffn_in"), 0, 2).T
        g8_fi = dw_fi_t.reshape(N_DEV, FWP, D)
        if mix_pairs is None:
            dh2, got_fi = matmul(da3, st["w_fi_full"], name="ffn_up_dx", a_split=True,
                                 riders=[pair_exchange(g8_fi)])
        else:
            dh2, got_fi, quad_in = matmul(da3, st["w_fi_full"], name="ffn_up_dx", a_split=True,
                                          riders=[pair_exchange(g8_fi), chip_exchange(mix_pairs[2])])
            finish_mixer(mix_pairs[0], quad_mo, quad_in)
        ffn_pairs = (pair_fo, pair_sum(g8_fi, got_fi, c_idx, "rs_pair_sum_ffn_in"))
        dx, dsh2, dsc2, d_ffn_g[i] = norm_mod_bwd(st["x1"], ffn_norm_g[i][None], sc2, dh2, dx, "norm_mod_bwd")
        dy, dg1 = gate_bwd(dx, st["y"], g1, "gate_bwd")
        if i % 2 == 0:
            do = matmul(dy, st["w_mo"], name="mix_out_dx", tb=True)
            g8_mo = matmul(st["o"], dy, name="mix_out_dw", ta=True, out_dtype=bf16).reshape(N_DEV, DR, D)
            dq, dk, dv, cs, rs, quad_fo, quad_fi, got_mo = fox_attn_bwd(
                st["qkv"], st["o"], do, st["lse"], st["f_row"], st["f_col"], "fox_attn_bwd",
                riders=[chip_exchange(ffn_pairs[0]), chip_exchange(ffn_pairs[1]), pair_exchange(g8_mo)])
            g_wfo[i] = sum_slots(quad_fo, "rs_final_sum_ffn_out")[:FR]
            g_wfi[i] = unpad_ff(sum_slots(quad_fi, "rs_final_sum_ffn_in"), 0, 2).T
            ffn_pairs = None
            dF = rs[:, :, 0] - cs[:, 0, :]
            dflog_t, d_bf[jm] = fox_gates_bwd(st["flog_t"], st["b_col"], dF, "fox_gates_bwd")
            dflog = jnp.pad(dflog_t.T, ((0, 0), (0, LANES - H))).astype(bf16)
            dqkv = jnp.concatenate([dq, dk, dv], axis=1)
            dw_qkv_t = matmul(dqkv, st["h"], name="fox_qkv_dw", ta=True, out_dtype=bf16)
            dw_f_t = matmul(dflog, st["h"], name="fox_flog_dw", ta=True, out_dtype=bf16)
            g8_in = jnp.concatenate([dw_qkv_t, dw_f_t[:H]], axis=0).reshape(N_DEV, QW, D)
            dh, got_in = matmul(dqkv, st["w_qkv_t"], name="fox_qkv_dx", riders=[pair_exchange(g8_in)])
            dh = matmul(dflog, st["w_f_t"], name="fox_flog_dx", resid=dh)
        else:
            dgated = matmul(dy, st["w_mo"], name="mix_out_dx", tb=True)
            g8_mo = matmul(st["gated"], dy, name="mix_out_dw", ta=True, out_dtype=bf16).reshape(N_DEV, DR, D)
            dz, d_ws[jm], dbs_t, d_vg[jm] = gm_gate_bwd(st["z"], vg_full[jm], gm_w_s[jm], st["bs_t"], dgated, "gm_gate_bwd")
            d_bs[jm] = dbs_t.T
            dw_in_t, got_mo = matmul(dz, st["h"], name="gm_in_dw", ta=True, out_dtype=bf16,
                                     riders=[pair_exchange(g8_mo)])
            g8_in = dw_in_t.reshape(N_DEV, GW, D)
            dh, got_in = matmul(dz, st["w_gi_full"], name="gm_in_dx", riders=[pair_exchange(g8_in)])
        mix_pairs = (i, pair_sum(g8_mo, got_mo, c_idx, "rs_pair_sum_mix_out"),
                     pair_sum(g8_in, got_in, c_idx, "rs_pair_sum_mix_in"))
        dx, dsh1, dsc1, d_mix_g[i] = norm_mod_bwd(st["x_in"], mix_norm_g[i][None], sc1, dh, dx, "norm_mod_bwd")
        d_mod[i] = jnp.concatenate([dsh1, dsc1, dg1, dsh2, dsc2, dg2], axis=0)

    grad_x = dx[None]
    if ffn_pairs is not None:
        g_wfo[0] = reduce_scatter_tail(ffn_pairs[0], "ffn_out")[:FR]
        g_wfi[0] = unpad_ff(reduce_scatter_tail(ffn_pairs[1], "ffn_in"), 0, 2).T

    def gathered_sum(rows, tag, mult=SUBLANES):
        n = rows.shape[0]
        rows = jnp.pad(rows, ((0, _round_up(n, mult) - n), (0, 0)))
        every = all_gather(rows, "gather_small_grads_" + tag)
        return every, sum_slots(every, "sum_small_grads_" + tag)

    rows_d = jnp.concatenate([jnp.concatenate(d_mod, axis=0), jnp.concatenate(d_mix_g, axis=0),
                              jnp.concatenate(d_ffn_g, axis=0), jnp.concatenate(d_vg, axis=0), d_final_g], axis=0)
    every_d, sum_d = gathered_sum(rows_d, "d")
    r0 = L * 6
    grad_mod_b = sum_d[:r0].reshape(L, 6 * D)
    grad_mix_g, grad_ffn_g = sum_d[r0:r0 + L], sum_d[r0 + L:r0 + 2 * L]
    grad_vg_full = sum_d[r0 + 2 * L:r0 + 2 * L + NB]
    grad_final_g = sum_d[r0 + 2 * L + NB]
    grad_vg = lax.dynamic_slice_in_dim(grad_vg_full, me * DR, DR, axis=1)

    rows_f = jnp.concatenate([jnp.concatenate(d_conv_w, axis=0), jnp.concatenate(d_conv_b, axis=0)], axis=0)
    _, sum_f = gathered_sum(rows_f, "f")
    sum_f = unpad_ff(sum_f, 1, 2 * N_DEV)
    grad_conv_w = lax.dynamic_slice_in_dim(sum_f[:L * CONV_W].reshape(L, CONV_W, DFF2), me * FW, FW, axis=2)
    grad_conv_b = sum_f[L * CONV_W:L * CONV_W + L]

    rows_c = jnp.concatenate([jnp.stack(d_ws).reshape(NB * G * CHUNK, CHUNK), jnp.stack(d_bs).reshape(NB * G, CHUNK),
                              jnp.pad(jnp.stack(d_bf).reshape(NA, H), ((0, 0), (0, LANES - H)))], axis=0)
    _, sum_c = gathered_sum(rows_c, "c", mult=SLOT_ROWS)
    n_ws = NB * G * CHUNK
    grad_ws = sum_c[:n_ws].reshape(NB, G, CHUNK, CHUNK)
    grad_bs = sum_c[n_ws:n_ws + NB * G].reshape(NB, G, CHUNK)
    grad_bf = sum_c[n_ws + NB * G:n_ws + NB * G + NA, :H]

    dmod_all = every_d[:, :r0].reshape(N_DEV, L, 6 * D)
    dmod_loc = lax.dynamic_slice_in_dim(dmod_all, me * MW, MW, axis=2).transpose(1, 0, 2)
    dmod_loc = jnp.pad(dmod_loc, ((0, 0), (0, LANES - N_DEV), (0, 0)))
    c_t = jnp.pad(c_all.T, ((0, 0), (0, LANES - N_DEV)))
    grad_mod_w = mod_w_bwd(c_t, dmod_loc, "mod_w_bwd")

    first = adamw(mod_w, grad_mod_w, m_mod_w, v_mod_w, "adamw",
                  riders=[chip_exchange(mix_pairs[1]), chip_exchange(mix_pairs[2])])
    finish_mixer(mix_pairs[0], first[3], first[4])
    grad_attn_w_in, grad_attn_w_o = jnp.stack(g_wai), jnp.stack(g_wao)
    grad_gm_w_in, grad_gm_w_o = jnp.stack(g_wgi), jnp.stack(g_wgo)
    grad_ffn_w_in, grad_ffn_w_out = jnp.stack(g_wfi), jnp.stack(g_wfo)

    weights = [mod_w, mod_b, mix_norm_g, ffn_norm_g, attn_w_in, attn_b_f, attn_w_o, gm_w_in, gm_v_g, gm_w_s,
               gm_b_s, gm_w_o, ffn_w_in, ffn_conv_w, ffn_conv_b, ffn_w_out, final_g]
    grads = [grad_mod_w, grad_mod_b, grad_mix_g, grad_ffn_g, grad_attn_w_in, grad_bf, grad_attn_w_o,
             grad_gm_w_in, grad_vg, grad_ws, grad_bs, grad_gm_w_o, grad_ffn_w_in, grad_conv_w, grad_conv_b,
             grad_ffn_w_out, grad_final_g]
    ms = [m_mod_w, m_mod_b, m_mix_norm_g, m_ffn_norm_g, m_attn_w_in, m_attn_b_f, m_attn_w_o, m_gm_w_in, m_gm_v_g,
          m_gm_w_s, m_gm_b_s, m_gm_w_o, m_ffn_w_in, m_ffn_conv_w, m_ffn_conv_b, m_ffn_w_out, m_final_g]
    vs = [v_mod_w, v_mod_b, v_mix_norm_g, v_ffn_norm_g, v_attn_w_in, v_attn_b_f, v_attn_w_o, v_gm_w_in, v_gm_v_g,
          v_gm_w_s, v_gm_b_s, v_gm_w_o, v_ffn_w_in, v_ffn_conv_w, v_ffn_conv_b, v_ffn_w_out, v_final_g]
    deltas, new_ms, new_vs = [], [], []
    for k, (w, g, m_, v_) in enumerate(zip(weights, grads, ms, vs)):
        d_, mn_, vn_ = first[:3] if k == 0 else adamw(w, g, m_, v_, "adamw")
        deltas.append(d_)
        new_ms.append(mn_)
        new_vs.append(vn_)

    return (loss, grad_x, *grads, *deltas, *new_ms, *new_vs)
```

```python
import numpy as np
import jax
import jax.numpy as jnp
from jax import lax
from jax.experimental import pallas as pl
from jax.experimental.pallas import tpu as pltpu

f32 = jnp.float32
bf16 = jnp.bfloat16

AXES = ("x", "y", "c")
N_DEV = 8
N_CHIPS = 4
LANES = 128
SUBLANES = 8
HEAD_DIM = 128
CHUNK = 128
GM_GROUP = 128
CONV_W = 3
EPS = 1e-6
NEG = -1e30
VMEM_LIMIT_BYTES = 56 * 1024 * 1024
MATMUL_VMEM_BYTES = 40 * 1024 * 1024
SLOT_ROWS = 512
LOG2E = 1.4426950408889634
ATT_Q_TILE = 1024
ATT_KEYS_PER_Q = 1

ADAM_LR = 0.001
ADAM_B1 = 0.9
ADAM_B2 = 0.999
ADAM_EPS = 1e-08
ADAM_WD = 0.01
ADAM_STEP = 10

MESH = pl.DeviceIdType.MESH
ANY = pl.BlockSpec(memory_space=pl.ANY)


def _cp(sem):
    return pltpu.CompilerParams(dimension_semantics=sem, vmem_limit_bytes=VMEM_LIMIT_BYTES)


def _pick(n, prefs):
    for p in prefs:
        if n % p == 0:
            return p
    return n


def _round_up(n, m):
    return (n + m - 1) // m * m


class Exchange:
    def __init__(self, src, out_shape, scratch, phases):
        self.src = src
        self.out_shape = out_shape
        self.scratch = scratch
        self.phases = phases


def gather_exchange(xl):
    def phases(x_ref, out_ref, send_sems, recv_sems, local_sem):
        x, y, c = lax.axis_index("x"), lax.axis_index("y"), lax.axis_index("c")
        me, sibling = (x, y, c), (x, y, 1 - c)
        chips = [(1 - x, y), (x, 1 - y), (1 - x, 1 - y)]

        def slot(px, py, pc):
            return out_ref.at[4 * px + 2 * py + pc]

        def copy(k, block, to, src=None):
            return pltpu.make_async_remote_copy(
                src_ref=slot(*block) if src is None else src, dst_ref=slot(*block),
                send_sem=send_sems.at[k], recv_sem=recv_sems.at[k],
                device_id=to, device_id_type=MESH)

        mine = pltpu.make_async_copy(x_ref, slot(*me), local_sem)
        first = [copy(0, me, sibling, src=x_ref)]
        first += [copy(1 + j, me, (*chip, c), src=x_ref) for j, chip in enumerate(chips)]
        passed = [copy(4 + j, (*chip, c), sibling) for j, chip in enumerate(chips)]

        def start():
            mine.start()
            for cp in first:
                cp.start()

        def hand_on():
            for j, chip in enumerate(chips):
                copy(1 + j, (*chip, c), me).wait_recv()
                passed[j].start()

        def finish():
            copy(0, sibling, me).wait_recv()
            for j, chip in enumerate(chips):
                copy(4 + j, (*chip, 1 - c), me).wait_recv()
            for cp in first + passed:
                cp.wait_send()
            mine.wait()

        return start, hand_on, finish

    return Exchange(xl, jax.ShapeDtypeStruct((N_DEV,) + xl.shape, xl.dtype),
                    [pltpu.SemaphoreType.DMA((7,)), pltpu.SemaphoreType.DMA((7,)),
                     pltpu.SemaphoreType.DMA], phases)


def pair_exchange(g8):
    _, R, W = g8.shape

    def phases(g_ref, out_ref, send_sems, recv_sems):
        x, y, c = lax.axis_index("x"), lax.axis_index("y"), lax.axis_index("c")
        copies = [pltpu.make_async_remote_copy(
            src_ref=g_ref.at[2 * q + (1 - c)], dst_ref=out_ref.at[q],
            send_sem=send_sems.at[q], recv_sem=recv_sems.at[q],
            device_id=(x, y, 1 - c), device_id_type=MESH) for q in range(N_CHIPS)]

        def start():
            for cp in copies:
                cp.start()

        def finish():
            for cp in copies:
                cp.wait()

        return start, None, finish

    return Exchange(g8, jax.ShapeDtypeStruct((N_CHIPS, R, W), g8.dtype),
                    [pltpu.SemaphoreType.DMA((N_CHIPS,)), pltpu.SemaphoreType.DMA((N_CHIPS,))], phases)


def chip_exchange(p4):
    def phases(p_ref, out_ref, send_sems, recv_sems, local_sem):
        x, y, c = lax.axis_index("x"), lax.axis_index("y"), lax.axis_index("c")
        my_q = 2 * x + y
        chips = [(1 - x, y), (x, 1 - y), (1 - x, 1 - y)]
        mine = pltpu.make_async_copy(p_ref.at[my_q], out_ref.at[my_q], local_sem)
        copies = [pltpu.make_async_remote_copy(
            src_ref=p_ref.at[2 * px + py], dst_ref=out_ref.at[my_q],
            send_sem=send_sems.at[k], recv_sem=recv_sems.at[k],
            device_id=(px, py, c), device_id_type=MESH) for k, (px, py) in enumerate(chips)]

        def start():
            mine.start()
            for cp in copies:
                cp.start()

        def finish():
            for k, (px, py) in enumerate(chips):
                pltpu.make_async_remote_copy(
                    src_ref=p_ref.at[my_q], dst_ref=out_ref.at[2 * px + py],
                    send_sem=send_sems.at[k], recv_sem=recv_sems.at[k],
                    device_id=(px, py, c), device_id_type=MESH).wait_recv()
            for cp in copies:
                cp.wait_send()
            mine.wait()

        return start, None, finish

    return Exchange(p4, jax.ShapeDtypeStruct(p4.shape, p4.dtype),
                    [pltpu.SemaphoreType.DMA((3,)), pltpu.SemaphoreType.DMA((3,)),
                     pltpu.SemaphoreType.DMA], phases)


def run_exchange(ex, name):
    def body(src_ref, out_ref, *sems):
        start, hand_on, finish = ex.phases(src_ref, out_ref, *sems)
        start()
        if hand_on is not None:
            hand_on()
        finish()

    return pl.pallas_call(body, name=name, out_shape=ex.out_shape, in_specs=[ANY], out_specs=ANY,
                          scratch_shapes=ex.scratch)(ex.src)


def all_gather(xl, name):
    return run_exchange(gather_exchange(xl), name)


class Riders:
    def __init__(self, exchanges):
        self.exs = list(exchanges or [])
        self.in_specs = [ANY] * len(self.exs)
        self.args = [ex.src for ex in self.exs]
        self.out_specs = [ANY] * len(self.exs)
        self.out_shape = [ex.out_shape for ex in self.exs]
        self.scratch = [s for ex in self.exs for s in ex.scratch]

    def split(self, in_refs, out_refs, scratch_refs):
        n = len(self.exs)
        self.refs = []
        pos = len(scratch_refs) - len(self.scratch)
        for k, ex in enumerate(self.exs):
            sems = scratch_refs[pos:pos + len(ex.scratch)]
            pos += len(ex.scratch)
            self.refs.append((in_refs[len(in_refs) - n + k], out_refs[len(out_refs) - n + k], sems))

    def _parts(self):
        return [ex.phases(src, out, *sems) for ex, (src, out, sems) in zip(self.exs, self.refs)]

    def before(self, step):
        if not self.exs:
            return
        parts = self._parts()

        @pl.when(step == 0)
        def _():
            for start, _, _ in parts:
                start()

    def after(self, step, n_steps):
        if not self.exs:
            return
        parts = self._parts()
        mid = (3 * n_steps) // 4

        if any(h is not None for _, h, _ in parts):
            @pl.when(step == mid)
            def _():
                for _, hand_on, _ in parts:
                    if hand_on is not None:
                        hand_on()

        @pl.when(step == n_steps - 1)
        def _():
            for _, _, finish in parts:
                finish()


def pair_sum(g8, got4, c_idx, name):
    _, R, W = g8.shape
    tr = _pick(R, (512, 256, 128, 64, 32, 16))
    g5 = g8.reshape(N_CHIPS, 2, R, W)

    def body(c_ref, a_ref, b_ref, o_ref):
        o_ref[...] = (a_ref[...].astype(f32) + b_ref[...].astype(f32)).astype(o_ref.dtype)

    grid_spec = pltpu.PrefetchScalarGridSpec(
        num_scalar_prefetch=1, grid=(N_CHIPS, R // tr),
        in_specs=[pl.BlockSpec((None, None, tr, W), lambda q, r, cr: (q, cr[0], r, 0)),
                  pl.BlockSpec((None, tr, W), lambda q, r, cr: (q, r, 0))],
        out_specs=pl.BlockSpec((None, tr, W), lambda q, r, cr: (q, r, 0)))
    return pl.pallas_call(
        body, name=name, grid_spec=grid_spec,
        out_shape=jax.ShapeDtypeStruct((N_CHIPS, R, W), g8.dtype),
        compiler_params=_cp(("parallel", "parallel")),
    )(c_idx, g5, got4)


def sum_slots(xs, name, out_dtype=f32):
    S, R, W = xs.shape
    tr = _pick(R, (512, 256, 128, 64, 32, 16, 8))

    def body(x_ref, o_ref):
        acc = x_ref[0].astype(f32)
        for s in range(1, S):
            acc = acc + x_ref[s].astype(f32)
        o_ref[...] = acc.astype(o_ref.dtype)

    return pl.pallas_call(
        body, name=name, grid=(R // tr,),
        in_specs=[pl.BlockSpec((S, tr, W), lambda r: (0, r, 0))],
        out_specs=pl.BlockSpec((tr, W), lambda r: (r, 0)),
        out_shape=jax.ShapeDtypeStruct((R, W), out_dtype),
        compiler_params=_cp(("parallel",)),
    )(xs)


def matmul(a, b, *, name, ta=False, tb=False, a_split=False, b_split=False, out_split=False,
           out_dtype=f32, resid=None, gvec=None, emit_acc=False, riders=None):
    rd = Riders(riders)
    if a_split:
        rows, cols = a.shape[1], 2 * a.shape[2]
        M, K = (cols, rows) if ta else (rows, cols)
    else:
        M, K = (a.shape[1], a.shape[0]) if ta else a.shape
    if b_split:
        assert not tb
        N = 2 * b.shape[2]
        assert b.shape[1] == K
    else:
        N = b.shape[0] if tb else b.shape[1]
        assert (b.shape[1] if tb else b.shape[0]) == K, (a.shape, b.shape, name)

    m_split = a_split and ta
    k_split = a_split and not ta
    n_split = b_split or out_split
    tm = _pick(M // 2 if m_split else M, (1024, 512, 256, 128, 64, 32, 16, 8))
    k_len, n_len = (K // 2 if k_split else K), (N // 2 if n_split else N)
    out_bytes = jnp.dtype(out_dtype).itemsize + (4 if resid is not None else 0) + (2 if emit_acc else 0)

    def fits(tk_, tn_):
        operands = 2 * (tm * tk_ * a.dtype.itemsize + tk_ * tn_ * b.dtype.itemsize)
        acc = tm * tn_ * 4 if tk_ < K else 0
        return operands + acc + 2 * tm * tn_ * out_bytes <= MATMUL_VMEM_BYTES

    tk_options = [d for d in range(k_len, 0, -LANES) if k_len % d == 0 and d % LANES == 0]
    tn_options = [t for t in (1024, 512, 256, 128) if n_len % t == 0]
    tk, tn = next(((tk_, tn_) for tn_min in (512, 128) for tk_ in tk_options for tn_ in tn_options
                   if tn_ >= tn_min and fits(tk_, tn_)), (tk_options[-1], tn_options[-1]))
    nk = K // tk
    n_half = (N // 2) // tn if n_split else 0
    k_half = (K // 2) // tk if k_split else 0
    m_half = (M // 2) // tm if m_split else 0

    if m_split:
        a_spec = pl.BlockSpec((None, tk, tm), lambda i, j, k: (i // m_half, k, i % m_half))
    elif k_split:
        a_spec = pl.BlockSpec((None, tm, tk), lambda i, j, k: (k // k_half, i, k % k_half))
    elif ta:
        a_spec = pl.BlockSpec((tk, tm), lambda i, j, k: (k, i))
    else:
        a_spec = pl.BlockSpec((tm, tk), lambda i, j, k: (i, k))
    if b_split:
        b_spec = pl.BlockSpec((None, tk, tn), lambda i, j, k: (j // n_half, k, j % n_half))
    elif tb:
        b_spec = pl.BlockSpec((tn, tk), lambda i, j, k: (j, k))
    else:
        b_spec = pl.BlockSpec((tk, tn), lambda i, j, k: (k, j))
    if out_split:
        o_spec = pl.BlockSpec((None, tm, tn), lambda i, j, k: (j // n_half, i, j % n_half))
        o_shape = (2, M, N // 2)
    else:
        o_spec = pl.BlockSpec((tm, tn), lambda i, j, k: (i, j))
        o_shape = (M, N)

    in_specs = [a_spec, b_spec]
    args = [a, b]
    if resid is not None:
        in_specs.append(pl.BlockSpec((tm, tn), lambda i, j, k: (i, j)))
        args.append(resid)
    if gvec is not None:
        in_specs.append(pl.BlockSpec((1, tn), lambda i, j, k: (0, j)))
        args.append(gvec)
    out_specs = [o_spec]
    out_shape = [jax.ShapeDtypeStruct(o_shape, out_dtype)]
    if emit_acc:
        out_specs.append(pl.BlockSpec((tm, tn), lambda i, j, k: (i, j)))
        out_shape.append(jax.ShapeDtypeStruct((M, N), bf16))
    dims = (((0 if ta else 1,), (1 if tb else 0,)), ((), ()))
    has_r, has_g = resid is not None, gvec is not None
    n_in, n_out = len(in_specs) + len(rd.exs), len(out_specs) + len(rd.exs)
    grid = (M // tm, N // tn, nk)
    n_steps = grid[0] * grid[1] * grid[2]

    def body(*refs):
        in_refs, out_refs, scratch_refs = refs[:n_in], refs[n_in:n_in + n_out], refs[n_in + n_out:]
        rd.split(in_refs, out_refs, scratch_refs)
        a_ref, b_ref = in_refs[0], in_refs[1]
        pos = 2
        r_ref = g_ref = None
        if has_r:
            r_ref = in_refs[pos]
            pos += 1
        if has_g:
            g_ref = in_refs[pos]
        o_ref = out_refs[0]
        y_ref = out_refs[1] if emit_acc else None
        step = (pl.program_id(0) * grid[1] + pl.program_id(1)) * nk + pl.program_id(2)
        rd.before(step)

        def finish(acc):
            if emit_acc:
                y_ref[...] = acc.astype(bf16)
            if has_g:
                acc = acc * g_ref[...]
            if has_r:
                acc = r_ref[...] + acc
            o_ref[...] = acc.astype(o_ref.dtype)

        part = lax.dot_general(a_ref[...].astype(bf16), b_ref[...].astype(bf16), dims,
                               preferred_element_type=f32)
        if nk == 1:
            finish(part)
        else:
            acc_ref = scratch_refs[0]
            k = pl.program_id(2)

            @pl.when(k == 0)
            def _():
                acc_ref[...] = part

            @pl.when(k > 0)
            def _():
                acc_ref[...] += part

            @pl.when(k == nk - 1)
            def _():
                finish(acc_ref[...])

        rd.after(step, n_steps)

    outs = pl.pallas_call(
        body, name=name, grid=grid,
        in_specs=in_specs + rd.in_specs, out_specs=out_specs + rd.out_specs,
        out_shape=out_shape + rd.out_shape,
        scratch_shapes=([pltpu.VMEM((tm, tn), f32)] if nk > 1 else []) + rd.scratch,
        compiler_params=_cp(("arbitrary",) * 3 if rd.exs else ("parallel", "parallel", "arbitrary")),
    )(*args, *rd.args)
    return outs if (emit_acc or rd.exs) else outs[0]


def _rows(T):
    return _pick(T, (256, 128, 64, 32, 16, 8))


def norm_mod_fwd(x, gn, sc, sh, name):
    T, D = x.shape
    tr = _rows(T)

    def body(x_ref, gn_ref, sc_ref, sh_ref, h_ref):
        xv = x_ref[...]
        r = lax.rsqrt(jnp.mean(xv * xv, axis=-1, keepdims=True) + EPS)
        y = (xv * r) * gn_ref[...]
        h_ref[...] = (y * (1.0 + sc_ref[...]) + sh_ref[...]).astype(bf16)

    vec = pl.BlockSpec((1, D), lambda i: (0, 0))
    row = pl.BlockSpec((tr, D), lambda i: (i, 0))
    return pl.pallas_call(
        body, name=name, grid=(T // tr,), in_specs=[row, vec, vec, vec], out_specs=row,
        out_shape=jax.ShapeDtypeStruct((T, D), bf16), compiler_params=_cp(("parallel",)),
    )(x, gn, sc, sh)


def norm_mod_bwd(x, gn, sc, dh, dx_res, name):
    T, D = x.shape
    tr = _rows(T)

    def body(x_ref, gn_ref, sc_ref, dh_ref, dr_ref, dx_ref, dsh_ref, dsc_ref, dgn_ref):
        @pl.when(pl.program_id(0) == 0)
        def _():
            dsh_ref[...] = jnp.zeros_like(dsh_ref)
            dsc_ref[...] = jnp.zeros_like(dsc_ref)
            dgn_ref[...] = jnp.zeros_like(dgn_ref)

        xv = x_ref[...]
        r = lax.rsqrt(jnp.mean(xv * xv, axis=-1, keepdims=True) + EPS)
        xn = xv * r
        gn_v = gn_ref[...]
        dh_v = dh_ref[...]
        dsh_ref[...] += jnp.sum(dh_v, axis=0, keepdims=True)
        dsc_ref[...] += jnp.sum(dh_v * (xn * gn_v), axis=0, keepdims=True)
        dy = dh_v * (1.0 + sc_ref[...])
        dgn_ref[...] += jnp.sum(dy * xn, axis=0, keepdims=True)
        dxn = dy * gn_v
        dx = r * (dxn - xn * jnp.mean(dxn * xn, axis=-1, keepdims=True))
        dx_ref[...] = dr_ref[...] + dx

    vec = pl.BlockSpec((1, D), lambda i: (0, 0))
    row = pl.BlockSpec((tr, D), lambda i: (i, 0))
    vshape = jax.ShapeDtypeStruct((1, D), f32)
    return pl.pallas_call(
        body, name=name, grid=(T // tr,), in_specs=[row, vec, vec, row, row],
        out_specs=[row, vec, vec, vec],
        out_shape=[jax.ShapeDtypeStruct((T, D), f32), vshape, vshape, vshape],
        compiler_params=_cp(("arbitrary",)),
    )(x, gn, sc, dh, dx_res)


def gate_bwd(dx, y, g, name):
    T, D = dx.shape
    tr = _rows(T)

    def body(dx_ref, y_ref, g_ref, dy_ref, dg_ref):
        @pl.when(pl.program_id(0) == 0)
        def _():
            dg_ref[...] = jnp.zeros_like(dg_ref)

        dxv = dx_ref[...]
        dy_ref[...] = (dxv * g_ref[...]).astype(bf16)
        dg_ref[...] += jnp.sum(dxv * y_ref[...].astype(f32), axis=0, keepdims=True)

    vec = pl.BlockSpec((1, D), lambda i: (0, 0))
    row = pl.BlockSpec((tr, D), lambda i: (i, 0))
    return pl.pallas_call(
        body, name=name, grid=(T // tr,), in_specs=[row, row, vec], out_specs=[row, vec],
        out_shape=[jax.ShapeDtypeStruct((T, D), bf16), jax.ShapeDtypeStruct((1, D), f32)],
        compiler_params=_cp(("arbitrary",)),
    )(dx, y, g)


def loss_head(x, fg, tgt, name):
    T, D = x.shape
    tr = _rows(T)

    def body(x_ref, fg_ref, t_ref, loss_ref, dx_ref, dfg_ref):
        @pl.when(pl.program_id(0) == 0)
        def _():
            loss_ref[...] = jnp.zeros_like(loss_ref)
            dfg_ref[...] = jnp.zeros_like(dfg_ref)

        xv = x_ref[...]
        r = lax.rsqrt(jnp.mean(xv * xv, axis=-1, keepdims=True) + EPS)
        xn = xv * r
        fg_v = fg_ref[...]
        err = xn * fg_v - t_ref[...]
        per_tok = jnp.mean(err * err, axis=-1, keepdims=True)
        loss_ref[...] += 0.5 * jnp.sum(per_tok, axis=0, keepdims=True)
        dy = err * (1.0 / D)
        dfg_ref[...] += jnp.sum(dy * xn, axis=0, keepdims=True)
        dxn = dy * fg_v
        dx_ref[...] = r * (dxn - xn * jnp.mean(dxn * xn, axis=-1, keepdims=True))

    vec = pl.BlockSpec((1, D), lambda i: (0, 0))
    row = pl.BlockSpec((tr, D), lambda i: (i, 0))
    one = pl.BlockSpec((1, 1), lambda i: (0, 0))
    return pl.pallas_call(
        body, name=name, grid=(T // tr,), in_specs=[row, vec, row], out_specs=[one, row, vec],
        out_shape=[jax.ShapeDtypeStruct((1, 1), f32), jax.ShapeDtypeStruct((T, D), f32),
                   jax.ShapeDtypeStruct((1, D), f32)],
        compiler_params=_cp(("arbitrary",)),
    )(x, fg, tgt)


def _conv_tiles(T, FP):
    return _pick(T, (512, 256, 128, 64, 32, 16, 8)), _pick(FP, (512, 256, 128))


def _conv_specs(tr, tc, T, FP):
    nj = FP // tc
    r8 = tr // SUBLANES
    last8 = T // SUBLANES - 1
    main = pl.BlockSpec((2, tr, tc), lambda j, i: (0, i, j))
    prev = pl.BlockSpec((2, SUBLANES, tc), lambda j, i: (0, jnp.maximum(i * r8 - 1, 0), j))
    nxt = pl.BlockSpec((2, SUBLANES, tc), lambda j, i: (0, jnp.minimum((i + 1) * r8, last8), j))
    wg = pl.BlockSpec((CONV_W, tc), lambda j, i: (0, j))
    wu = pl.BlockSpec((CONV_W, tc), lambda j, i: (0, j + nj))
    bg = pl.BlockSpec((1, tc), lambda j, i: (0, j))
    bu = pl.BlockSpec((1, tc), lambda j, i: (0, j + nj))
    return main, prev, nxt, wg, wu, bg, bu


def _causal_taps(av, hp_ref, s, has_prev, row):
    h7 = jnp.where(has_prev, hp_ref[s, 7:8, :], 0.0)
    h6 = jnp.where(has_prev, hp_ref[s, 6:7, :], 0.0)
    m1 = jnp.where(row == 0, h7, pltpu.roll(av, 1, 0))
    m2 = jnp.where(row == 0, h6, jnp.where(row == 1, h7, pltpu.roll(av, 2, 0)))
    return m1, m2


def conv_glu_fwd(a3, conv_w, conv_b, name, riders=None):
    _, T, FP = a3.shape
    tr, tc = _conv_tiles(T, FP)
    main, prev, _, wg, wu, bg, bu = _conv_specs(tr, tc, T, FP)
    rd = Riders(riders)
    n_ex = len(rd.exs)
    ni = T // tr

    def body(*refs):
        in_refs, out_refs, scratch_refs = refs[:6 + n_ex], refs[6 + n_ex:7 + 2 * n_ex], refs[7 + 2 * n_ex:]
        rd.split(in_refs, out_refs, scratch_refs)
        a_ref, hp_ref, wg_ref, wu_ref, bg_ref, bu_ref = in_refs[:6]
        act_ref = out_refs[0]
        step = pl.program_id(0) * ni + pl.program_id(1)
        rd.before(step)
        has_prev = pl.program_id(1) > 0
        row = lax.broadcasted_iota(jnp.int32, (tr, tc), 0)

        def conv(s, w_ref, b_ref):
            av = a_ref[s]
            m1, m2 = _causal_taps(av, hp_ref, s, has_prev, row)
            return w_ref[0:1, :] * m2 + w_ref[1:2, :] * m1 + w_ref[2:3, :] * av + b_ref[...]

        gate = conv(0, wg_ref, bg_ref)
        up = conv(1, wu_ref, bu_ref)
        act_ref[...] = ((gate * jax.nn.sigmoid(gate)) * up).astype(bf16)
        rd.after(step, (FP // tc) * ni)

    outs = pl.pallas_call(
        body, name=name, grid=(FP // tc, ni),
        in_specs=[main, prev, wg, wu, bg, bu] + rd.in_specs,
        out_specs=[pl.BlockSpec((tr, tc), lambda j, i: (i, j))] + rd.out_specs,
        out_shape=[jax.ShapeDtypeStruct((T, FP), bf16)] + rd.out_shape,
        scratch_shapes=rd.scratch,
        compiler_params=_cp(("arbitrary", "arbitrary") if rd.exs else ("parallel", "parallel")),
    )(a3, a3, conv_w, conv_w, conv_b, conv_b, *rd.args)
    return outs if rd.exs else outs[0]


def conv_glu_bwd(a3, conv_w, conv_b, dact, name, riders=None):
    _, T, FP = a3.shape
    tr, tc = _conv_tiles(T, FP)
    ni = T // tr
    main, prev, nxt, wg, wu, bg, bu = _conv_specs(tr, tc, T, FP)
    r8 = tr // SUBLANES
    last8 = T // SUBLANES - 1
    d_main = pl.BlockSpec((tr, tc), lambda j, i: (i, j))
    d_next = pl.BlockSpec((SUBLANES, tc), lambda j, i: (jnp.minimum((i + 1) * r8, last8), j))
    rd = Riders(riders)
    n_ex = len(rd.exs)

    def body(*refs):
        in_refs, out_refs, scratch_refs = refs[:9 + n_ex], refs[9 + n_ex:14 + 2 * n_ex], refs[14 + 2 * n_ex:]
        rd.split(in_refs, out_refs, scratch_refs)
        a_ref, hp_ref, hn_ref, d_ref, dn_ref, wg_ref, wu_ref, bg_ref, bu_ref = in_refs[:9]
        da_ref, dwg_ref, dwu_ref, dbg_ref, dbu_ref = out_refs[:5]
        i = pl.program_id(1)
        step = pl.program_id(0) * ni + i
        rd.before(step)
        has_prev = i > 0
        has_next = i < ni - 1
        row = lax.broadcasted_iota(jnp.int32, (tr, tc), 0)
        row8 = lax.broadcasted_iota(jnp.int32, (SUBLANES, tc), 0)

        @pl.when(i == 0)
        def _():
            dwg_ref[...] = jnp.zeros_like(dwg_ref)
            dwu_ref[...] = jnp.zeros_like(dwu_ref)
            dbg_ref[...] = jnp.zeros_like(dbg_ref)
            dbu_ref[...] = jnp.zeros_like(dbu_ref)

        def prep(s, w_ref, b_ref):
            av = a_ref[s]
            m1, m2 = _causal_taps(av, hp_ref, s, has_prev, row)
            w0, w1, w2, bv = w_ref[0:1, :], w_ref[1:2, :], w_ref[2:3, :], b_ref[...]
            pre = w0 * m2 + w1 * m1 + w2 * av + bv
            an = hn_ref[s]
            l1 = a_ref[s, tr - 1:tr, :]
            l2 = a_ref[s, tr - 2:tr - 1, :]
            n1 = jnp.where(row8 == 0, l1, pltpu.roll(an, 1, 0))
            n2 = jnp.where(row8 == 0, l2, jnp.where(row8 == 1, l1, pltpu.roll(an, 2, 0)))
            pre_n = w0 * n2 + w1 * n1 + w2 * an + bv
            return av, m1, m2, pre, pre_n

        def glu_bwd(gate, up, d):
            sg = jax.nn.sigmoid(gate)
            dgate = d * up * (sg * (1.0 + gate * (1.0 - sg)))
            dup = d * (gate * sg)
            return dgate, dup

        def row_of(v8, r):
            return jnp.sum(jnp.where(row8 == r, v8, 0.0), axis=0, keepdims=True)

        def back(dc, dc_n, w_ref):
            n0, n1 = row_of(dc_n, 0), row_of(dc_n, 1)
            p1 = jnp.where(row == tr - 1, n0, pltpu.roll(dc, tr - 1, 0))
            p2 = jnp.where(row == tr - 1, n1, jnp.where(row == tr - 2, n0, pltpu.roll(dc, tr - 2, 0)))
            return w_ref[2:3, :] * dc + w_ref[1:2, :] * p1 + w_ref[0:1, :] * p2

        def tok_sum(v):
            return jnp.sum(v, axis=0, keepdims=True)

        ag, g1, g2, gate, gate_n = prep(0, wg_ref, bg_ref)
        au, u1, u2, up, up_n = prep(1, wu_ref, bu_ref)
        dcg, dcu = glu_bwd(gate, up, d_ref[...])
        dn = jnp.where(has_next, dn_ref[...], 0.0)
        dcg_n, dcu_n = glu_bwd(gate_n, up_n, dn)
        da_ref[0] = back(dcg, dcg_n, wg_ref).astype(bf16)
        da_ref[1] = back(dcu, dcu_n, wu_ref).astype(bf16)
        dwg_ref[0:1, :] += tok_sum(dcg * g2)
        dwg_ref[1:2, :] += tok_sum(dcg * g1)
        dwg_ref[2:3, :] += tok_sum(dcg * ag)
        dwu_ref[0:1, :] += tok_sum(dcu * u2)
        dwu_ref[1:2, :] += tok_sum(dcu * u1)
        dwu_ref[2:3, :] += tok_sum(dcu * au)
        dbg_ref[...] += tok_sum(dcg)
        dbu_ref[...] += tok_sum(dcu)
        rd.after(step, (FP // tc) * ni)

    w_out = pl.BlockSpec((CONV_W, tc), lambda j, i: (0, j))
    b_out = pl.BlockSpec((1, tc), lambda j, i: (0, j))
    return pl.pallas_call(
        body, name=name, grid=(FP // tc, ni),
        in_specs=[main, prev, nxt, d_main, d_next, wg, wu, bg, bu] + rd.in_specs,
        out_specs=[main, w_out, w_out, b_out, b_out] + rd.out_specs,
        out_shape=[jax.ShapeDtypeStruct((2, T, FP), bf16),
                   jax.ShapeDtypeStruct((CONV_W, FP), f32), jax.ShapeDtypeStruct((CONV_W, FP), f32),
                   jax.ShapeDtypeStruct((1, FP), f32), jax.ShapeDtypeStruct((1, FP), f32)] + rd.out_shape,
        scratch_shapes=rd.scratch,
        compiler_params=_cp(("arbitrary", "arbitrary") if rd.exs else ("parallel", "arbitrary")),
    )(a3, a3, a3, dact, dact, conv_w, conv_w, conv_b, conv_b, *rd.args)


_GELU_C = 0.7978845608028654
_GELU_A = 0.044715


def _gelu(x):
    return 0.5 * x * (1.0 + jnp.tanh(_GELU_C * (x + _GELU_A * (x * x * x))))


def _gelu_and_grad(x):
    t = jnp.tanh(_GELU_C * (x + _GELU_A * (x * x * x)))
    g = 0.5 * x * (1.0 + t)
    dg = 0.5 * (1.0 + t) + 0.5 * x * (1.0 - t * t) * (_GELU_C * (1.0 + 3.0 * _GELU_A * (x * x)))
    return g, dg


def _tril_bf16(w):
    r = lax.broadcasted_iota(jnp.int32, w.shape, 0)
    c = lax.broadcasted_iota(jnp.int32, w.shape, 1)
    return jnp.where(r >= c, w, 0.0).astype(bf16)


def gm_gate_fwd(z, vg, ws, bs_t, name):
    T, D2 = z.shape
    D = D2 // 2
    G = D // GM_GROUP
    tr = _pick(T, (256, 128))
    nc = tr // CHUNK

    def body(z_ref, vg_ref, ws_ref, bs_ref, o_ref):
        u = _gelu(z_ref[:, :D])
        v = _gelu(z_ref[:, D:])
        rv = lax.rsqrt(jnp.mean(v * v, axis=-1, keepdims=True) + EPS)
        vn = ((v * rv) * vg_ref[...]).astype(bf16)
        for g in range(G):
            wg = _tril_bf16(ws_ref[g])
            bg = bs_ref[:, g:g + 1]
            cs = slice(g * GM_GROUP, (g + 1) * GM_GROUP)
            for c in range(nc):
                rs = slice(c * CHUNK, (c + 1) * CHUNK)
                sv = jnp.dot(wg, vn[rs, cs], preferred_element_type=f32) + bg
                o_ref[rs, cs] = (u[rs, cs] * sv).astype(bf16)

    return pl.pallas_call(
        body, name=name, grid=(T // tr,),
        in_specs=[pl.BlockSpec((tr, D2), lambda i: (i, 0)),
                  pl.BlockSpec((1, D), lambda i: (0, 0)),
                  pl.BlockSpec((G, CHUNK, CHUNK), lambda i: (0, 0, 0)),
                  pl.BlockSpec((CHUNK, G), lambda i: (0, 0))],
        out_specs=pl.BlockSpec((tr, D), lambda i: (i, 0)),
        out_shape=jax.ShapeDtypeStruct((T, D), bf16),
        compiler_params=_cp(("parallel",)),
    )(z, vg, ws, bs_t)


def gm_gate_bwd(z, vg, ws, bs_t, dgated, name):
    T, D2 = z.shape
    D = D2 // 2
    G = D // GM_GROUP
    tr = _pick(T, (256, 128))
    nc = tr // CHUNK

    def body(z_ref, vg_ref, ws_ref, bs_ref, dg_ref, dz_ref, dws_ref, dbs_ref, dvg_ref,
             du_s, dvn_s):
        @pl.when(pl.program_id(0) == 0)
        def _():
            dws_ref[...] = jnp.zeros_like(dws_ref)
            dbs_ref[...] = jnp.zeros_like(dbs_ref)
            dvg_ref[...] = jnp.zeros_like(dvg_ref)

        u, du_dz = _gelu_and_grad(z_ref[:, :D])
        v, dv_dz = _gelu_and_grad(z_ref[:, D:])
        rv = lax.rsqrt(jnp.mean(v * v, axis=-1, keepdims=True) + EPS)
        vhat = v * rv
        vg_v = vg_ref[...]
        vn = (vhat * vg_v).astype(bf16)
        rr = lax.broadcasted_iota(jnp.int32, (CHUNK, CHUNK), 0)
        cc = lax.broadcasted_iota(jnp.int32, (CHUNK, CHUNK), 1)
        for g in range(G):
            wg = _tril_bf16(ws_ref[g])
            bg = bs_ref[:, g:g + 1]
            cs = slice(g * GM_GROUP, (g + 1) * GM_GROUP)
            dw_acc = jnp.zeros((CHUNK, CHUNK), f32)
            db_acc = jnp.zeros((CHUNK, 1), f32)
            for c in range(nc):
                rs = slice(c * CHUNK, (c + 1) * CHUNK)
                vb = vn[rs, cs]
                sv = jnp.dot(wg, vb, preferred_element_type=f32) + bg
                dgb = dg_ref[rs, cs]
                du_s[rs, cs] = dgb * sv
                dsv = dgb * u[rs, cs]
                dsv_b = dsv.astype(bf16)
                dw_acc += lax.dot_general(dsv_b, vb, (((1,), (1,)), ((), ())),
                                          preferred_element_type=f32)
                db_acc += jnp.sum(dsv, axis=1, keepdims=True)
                dvn_s[rs, cs] = lax.dot_general(wg, dsv_b, (((0,), (0,)), ((), ())),
                                                preferred_element_type=f32)
            dws_ref[g] += jnp.where(rr >= cc, dw_acc, 0.0)
            dbs_ref[:, g:g + 1] += db_acc
        dz_ref[:, :D] = (du_s[...] * du_dz).astype(bf16)
        dvn = dvn_s[...]
        dvg_ref[...] += jnp.sum(dvn * vhat, axis=0, keepdims=True)
        dvh = dvn * vg_v
        dv = rv * (dvh - vhat * jnp.mean(dvh * vhat, axis=-1, keepdims=True))
        dz_ref[:, D:] = (dv * dv_dz).astype(bf16)

    return pl.pallas_call(
        body, name=name, grid=(T // tr,),
        in_specs=[pl.BlockSpec((tr, D2), lambda i: (i, 0)),
                  pl.BlockSpec((1, D), lambda i: (0, 0)),
                  pl.BlockSpec((G, CHUNK, CHUNK), lambda i: (0, 0, 0)),
                  pl.BlockSpec((CHUNK, G), lambda i: (0, 0)),
                  pl.BlockSpec((tr, D), lambda i: (i, 0))],
        out_specs=[pl.BlockSpec((tr, D2), lambda i: (i, 0)),
                   pl.BlockSpec((G, CHUNK, CHUNK), lambda i: (0, 0, 0)),
                   pl.BlockSpec((CHUNK, G), lambda i: (0, 0)),
                   pl.BlockSpec((1, D), lambda i: (0, 0))],
        out_shape=[jax.ShapeDtypeStruct((T, D2), bf16),
                   jax.ShapeDtypeStruct((G, CHUNK, CHUNK), f32),
                   jax.ShapeDtypeStruct((CHUNK, G), f32),
                   jax.ShapeDtypeStruct((1, D), f32)],
        scratch_shapes=[pltpu.VMEM((tr, D), f32), pltpu.VMEM((tr, D), f32)],
        compiler_params=_cp(("arbitrary",)),
    )(z, vg, ws, bs_t, dgated)


def fox_gates_fwd(flog_t, b_col, name):
    H, T = flog_t.shape

    def body(fl_ref, b_ref, o_ref):
        xv = fl_ref[...] + b_ref[...]
        lf = jnp.minimum(xv, 0.0) - jnp.log1p(jnp.exp(-jnp.abs(xv)))
        lane = lax.broadcasted_iota(jnp.int32, (H, T), 1)
        s = 1
        while s < T:
            lf = lf + jnp.where(lane >= s, pltpu.roll(lf, s, 1), 0.0)
            s *= 2
        o_ref[...] = lf * LOG2E

    return pl.pallas_call(
        body, name=name, out_shape=jax.ShapeDtypeStruct((H, T), f32),
        compiler_params=pltpu.CompilerParams(vmem_limit_bytes=VMEM_LIMIT_BYTES),
    )(flog_t, b_col)


def fox_gates_bwd(flog_t, b_col, dF, name):
    H, T = flog_t.shape

    def body(fl_ref, b_ref, d_ref, o_ref, db_ref):
        xv = fl_ref[...] + b_ref[...]
        g = d_ref[...]
        lane = lax.broadcasted_iota(jnp.int32, (H, T), 1)
        s = 1
        while s < T:
            g = g + jnp.where(lane < T - s, pltpu.roll(g, T - s, 1), 0.0)
            s *= 2
        dfl = g * jax.nn.sigmoid(-xv)
        o_ref[...] = dfl
        db_ref[...] = jnp.sum(dfl, axis=1, keepdims=True)

    return pl.pallas_call(
        body, name=name,
        out_shape=[jax.ShapeDtypeStruct((H, T), f32), jax.ShapeDtypeStruct((H, 1), f32)],
        compiler_params=pltpu.CompilerParams(vmem_limit_bytes=VMEM_LIMIT_BYTES),
    )(flog_t, b_col, dF)


_NT = (((1,), (1,)), ((), ()))
_TN = (((0,), (0,)), ((), ()))


def _scores(q, k, fq, fk, col0=None):
    s = lax.dot_general(q, k, _NT, preferred_element_type=f32) * (HEAD_DIM ** -0.5 * LOG2E)
    s = s + fq - fk
    if col0 is not None:
        rows = lax.broadcasted_iota(jnp.int32, s.shape, 0)
        cols = col0 + lax.broadcasted_iota(jnp.int32, s.shape, 1)
        s = jnp.where(cols <= rows, s, NEG)
    return s


def fox_attn_fwd(qkv, f_row, f_col, name, riders=None):
    T, D3 = qkv.shape
    D = D3 // 3
    H = D // HEAD_DIM
    tq = _pick(T, (ATT_Q_TILE, 512, 256))
    kq = ATT_KEYS_PER_Q
    tk = tq // kq
    nq = T // tq
    rd = Riders(riders)
    pairs = [(i, j) for i in range(nq) for j in range(kq * i + kq)]
    i_tab = np.array([p[0] for p in pairs], np.int32)
    j_tab = np.array([p[1] for p in pairs], np.int32)

    def body(i_ref, j_ref, *refs):
        in_refs, out_refs, scratch_refs = refs[:5 + len(rd.exs)], refs[5 + len(rd.exs):7 + 2 * len(rd.exs)], refs[7 + 2 * len(rd.exs):]
        rd.split(in_refs, out_refs, scratch_refs)
        q_ref, k_ref, v_ref, fq_ref, fk_ref = in_refs[:5]
        o_ref, lse_ref = out_refs[:2]
        m_s, l_s, acc_s = scratch_refs[:3]
        t = pl.program_id(1)
        i, j = i_ref[t], j_ref[t]
        step = pl.program_id(0) * len(pairs) + t
        rd.before(step)

        @pl.when(j == 0)
        def _():
            m_s[...] = jnp.full_like(m_s, NEG)
            l_s[...] = jnp.zeros_like(l_s)
            acc_s[...] = jnp.zeros_like(acc_s)

        def update(col0):
            s = _scores(q_ref[...], k_ref[...], fq_ref[0], fk_ref[0], col0)
            m_prev = m_s[...]
            m_new = jnp.maximum(m_prev, jnp.max(s, axis=1, keepdims=True))
            alpha = jnp.exp2(m_prev - m_new)
            p = jnp.exp2(s - m_new)
            l_s[...] = alpha * l_s[...] + jnp.sum(p, axis=1, keepdims=True)
            acc_s[...] = alpha * acc_s[...] + jnp.dot(p.astype(bf16), v_ref[...],
                                                      preferred_element_type=f32)
            m_s[...] = m_new

        @pl.when(j < kq * i)
        def _():
            update(None)

        @pl.when(j >= kq * i)
        def _():
            update((j - kq * i) * tk)

        @pl.when(j == kq * i + kq - 1)
        def _():
            o_ref[...] = (acc_s[...] / l_s[...]).astype(bf16)
            lse_ref[0] = m_s[...] + jnp.log2(l_s[...])

        rd.after(step, H * len(pairs))

    blk = (tq, HEAD_DIM)
    kblk = (tk, HEAD_DIM)
    grid_spec = pltpu.PrefetchScalarGridSpec(
        num_scalar_prefetch=2, grid=(H, len(pairs)),
        in_specs=[pl.BlockSpec(blk, lambda h, t, it, jt: (it[t], h)),
                  pl.BlockSpec(kblk, lambda h, t, it, jt: (jt[t], H + h)),
                  pl.BlockSpec(kblk, lambda h, t, it, jt: (jt[t], 2 * H + h)),
                  pl.BlockSpec((1, tq, 1), lambda h, t, it, jt: (h, it[t], 0)),
                  pl.BlockSpec((1, 1, tk), lambda h, t, it, jt: (h, 0, jt[t]))] + rd.in_specs,
        out_specs=[pl.BlockSpec(blk, lambda h, t, it, jt: (it[t], h)),
                   pl.BlockSpec((1, tq, 1), lambda h, t, it, jt: (h, it[t], 0))] + rd.out_specs,
        scratch_shapes=[pltpu.VMEM((tq, 1), f32), pltpu.VMEM((tq, 1), f32),
                        pltpu.VMEM((tq, HEAD_DIM), f32)] + rd.scratch)
    return pl.pallas_call(
        body, name=name, grid_spec=grid_spec,
        out_shape=[jax.ShapeDtypeStruct((T, D), bf16), jax.ShapeDtypeStruct((H, T, 1), f32)] + rd.out_shape,
        compiler_params=_cp(("arbitrary", "arbitrary") if rd.exs else ("parallel", "arbitrary")),
    )(i_tab, j_tab, qkv, qkv, qkv, f_col, f_row, *rd.args)


def fox_attn_bwd(qkv, o, do, lse, f_row, f_col, name, riders=None):
    T, D3 = qkv.shape
    D = D3 // 3
    H = D // HEAD_DIM
    tq = _pick(T, (ATT_Q_TILE, 512, 256))
    kq = ATT_KEYS_PER_Q
    tk = tq // kq
    nq, nk = T // tq, T // tk
    scale = HEAD_DIM ** -0.5

    pairs = [(j, i) for j in range(nk) for i in range(j // kq, nq)]
    j_tab = np.array([p[0] for p in pairs], np.int32)
    i_tab = np.array([p[1] for p in pairs], np.int32)

    rd = Riders(riders)
    n_ex = len(rd.exs)

    def body(j_ref, i_ref, *refs):
        in_refs, out_refs, scratch_refs = refs[:8 + n_ex], refs[8 + n_ex:13 + 2 * n_ex], refs[13 + 2 * n_ex:]
        rd.split(in_refs, out_refs, scratch_refs)
        q_ref, k_ref, v_ref, o_ref, do_ref, lse_ref, fq_ref, fk_ref = in_refs[:8]
        dq_ref, dk_ref, dv_ref, cs_ref, rs_ref = out_refs[:5]
        dk_s, dv_s, dq_s, di_s = scratch_refs[:4]
        step = pl.program_id(1)
        rd.before(pl.program_id(0) * len(pairs) + step)
        j, i = j_ref[step], i_ref[step]
        rows = pl.ds(pl.multiple_of(i * tq, tq), tq)
        first_i = j // kq

        @pl.when(step == 0)
        def _():
            dq_s[...] = jnp.zeros_like(dq_s)
            rs_ref[...] = jnp.zeros_like(rs_ref)

        @pl.when(j == 0)
        def _():
            di_s[rows, :] = jnp.sum(do_ref[...] * o_ref[...].astype(f32), axis=1, keepdims=True)

        @pl.when(i == first_i)
        def _():
            dk_s[...] = jnp.zeros_like(dk_s)
            dv_s[...] = jnp.zeros_like(dv_s)
            cs_ref[...] = jnp.zeros_like(cs_ref)

        def accumulate(col0):
            q = q_ref[...]
            k = k_ref[...]
            s = _scores(q, k, fq_ref[0], fk_ref[0], col0)
            p = jnp.exp2(s - lse_ref[0])
            do_b = do_ref[...].astype(bf16)
            dp = lax.dot_general(do_b, v_ref[...], _NT, preferred_element_type=f32)
            ds = p * (dp - di_s[rows, :])
            ds_b = (ds * scale).astype(bf16)
            cs_ref[0] += jnp.sum(ds, axis=0, keepdims=True)
            rs_ref[0, rows, :] += jnp.sum(ds, axis=1, keepdims=True)
            dv_s[...] += lax.dot_general(p.astype(bf16), do_b, _TN, preferred_element_type=f32)
            dk_s[...] += lax.dot_general(ds_b, q, _TN, preferred_element_type=f32)
            dq_s[rows, :] += jnp.dot(ds_b, k, preferred_element_type=f32)

        @pl.when(i == first_i)
        def _():
            accumulate((j - kq * i) * tk)

        @pl.when(i > first_i)
        def _():
            accumulate(None)

        @pl.when(i == nq - 1)
        def _():
            dk_ref[...] = dk_s[...].astype(bf16)
            dv_ref[...] = dv_s[...].astype(bf16)

        @pl.when(step == len(pairs) - 1)
        def _():
            dq_ref[...] = dq_s[...].astype(bf16)

        rd.after(pl.program_id(0) * len(pairs) + step, H * len(pairs))

    blk = (tq, HEAD_DIM)
    kblk = (tk, HEAD_DIM)
    at_q = lambda h, s, jt, it: (it[s], h)
    col_q = pl.BlockSpec((1, tq, 1), lambda h, s, jt, it: (h, it[s], 0))
    row_k = pl.BlockSpec((1, 1, tk), lambda h, s, jt, it: (h, 0, jt[s]))
    grid_spec = pltpu.PrefetchScalarGridSpec(
        num_scalar_prefetch=2, grid=(H, len(pairs)),
        in_specs=[pl.BlockSpec(blk, at_q),
                  pl.BlockSpec(kblk, lambda h, s, jt, it: (jt[s], H + h)),
                  pl.BlockSpec(kblk, lambda h, s, jt, it: (jt[s], 2 * H + h)),
                  pl.BlockSpec(blk, at_q), pl.BlockSpec(blk, at_q), col_q, col_q, row_k] + rd.in_specs,
        out_specs=[pl.BlockSpec((T, HEAD_DIM), lambda h, s, jt, it: (0, h)),
                   pl.BlockSpec(kblk, lambda h, s, jt, it: (jt[s], h)),
                   pl.BlockSpec(kblk, lambda h, s, jt, it: (jt[s], h)),
                   row_k,
                   pl.BlockSpec((1, T, 1), lambda h, s, jt, it: (h, 0, 0))] + rd.out_specs,
        scratch_shapes=[pltpu.VMEM(kblk, f32), pltpu.VMEM(kblk, f32),
                        pltpu.VMEM((T, HEAD_DIM), f32), pltpu.VMEM((T, 1), f32)] + rd.scratch)
    return pl.pallas_call(
        body, name=name, grid_spec=grid_spec,
        out_shape=[jax.ShapeDtypeStruct((T, D), bf16), jax.ShapeDtypeStruct((T, D), bf16),
                   jax.ShapeDtypeStruct((T, D), bf16), jax.ShapeDtypeStruct((H, 1, T), f32),
                   jax.ShapeDtypeStruct((H, T, 1), f32)] + rd.out_shape,
        compiler_params=_cp(("arbitrary", "arbitrary") if rd.exs else ("parallel", "arbitrary")),
    )(j_tab, i_tab, qkv, qkv, qkv, o, do, lse, f_col, f_row, *rd.args)


def mod_fwd(c16, mod_w, mod_b_loc, name):
    L, D, MW = mod_w.shape
    tn = _pick(MW, (512, 256, 128))

    def body(c_ref, w_ref, b_ref, o_ref):
        cv = c_ref[...]
        ca = (cv * jax.nn.sigmoid(cv)).astype(bf16)
        o_ref[...] = jnp.dot(ca, w_ref[...].astype(bf16), preferred_element_type=f32) + b_ref[...]

    return pl.pallas_call(
        body, name=name, grid=(L, MW // tn),
        in_specs=[pl.BlockSpec((16, D), lambda l, j: (0, 0)),
                  pl.BlockSpec((None, D, tn), lambda l, j: (l, 0, j)),
                  pl.BlockSpec((None, 1, tn), lambda l, j: (l, 0, j))],
        out_specs=pl.BlockSpec((None, 16, tn), lambda l, j: (l, 0, j)),
        out_shape=jax.ShapeDtypeStruct((L, 16, MW), f32),
        compiler_params=_cp(("parallel", "parallel")),
    )(c16, mod_w, mod_b_loc)


def mod_w_bwd(c_t, dmod, name):
    D = c_t.shape[0]
    L, _, MW = dmod.shape
    tn = _pick(MW, (512, 256, 128))

    def body(c_ref, d_ref, o_ref):
        cv = c_ref[...]
        ca = (cv * jax.nn.sigmoid(cv)).astype(bf16)
        o_ref[...] = jnp.dot(ca, d_ref[...].astype(bf16), preferred_element_type=f32)

    return pl.pallas_call(
        body, name=name, grid=(L, MW // tn),
        in_specs=[pl.BlockSpec((D, LANES), lambda l, j: (0, 0)),
                  pl.BlockSpec((None, LANES, tn), lambda l, j: (l, 0, j))],
        out_specs=pl.BlockSpec((None, D, tn), lambda l, j: (l, 0, j)),
        out_shape=jax.ShapeDtypeStruct((L, D, MW), f32),
        compiler_params=_cp(("parallel", "parallel")),
    )(c_t, dmod)


def adamw(w, g, m, v, name, riders=None):
    shape = w.shape
    C = shape[-1] if w.ndim >= 1 else 1
    R = max(w.size // C, 1)
    w2, g2, m2, v2 = (t.reshape(R, C) for t in (w, g, m, v))
    tr = R
    for cand in (2048, 1024, 512, 256, 128, 64, 32, 16, 8):
        if R % cand == 0 and cand * _round_up(C, LANES) <= 256 * 1024:
            tr = cand
            break
    rd = Riders(riders)
    n_ex = len(rd.exs)

    def body(*refs):
        in_refs, out_refs, scratch_refs = refs[:4 + n_ex], refs[4 + n_ex:7 + 2 * n_ex], refs[7 + 2 * n_ex:]
        rd.split(in_refs, out_refs, scratch_refs)
        w_ref, g_ref, m_ref, v_ref = in_refs[:4]
        d_ref, mo_ref, vo_ref = out_refs[:3]
        rd.before(pl.program_id(0))
        gv = g_ref[...]
        mn = ADAM_B1 * m_ref[...] + (1.0 - ADAM_B1) * gv
        vn = ADAM_B2 * v_ref[...] + (1.0 - ADAM_B2) * (gv * gv)
        m_hat = mn / (1.0 - ADAM_B1 ** ADAM_STEP)
        v_hat = vn / (1.0 - ADAM_B2 ** ADAM_STEP)
        d_ref[...] = -ADAM_LR * (m_hat / (jnp.sqrt(v_hat) + ADAM_EPS) + ADAM_WD * w_ref[...])
        mo_ref[...] = mn
        vo_ref[...] = vn
        rd.after(pl.program_id(0), R // tr)

    spec = pl.BlockSpec((tr, C), lambda i: (i, 0))
    sds = jax.ShapeDtypeStruct((R, C), f32)
    outs = pl.pallas_call(
        body, name=name, grid=(R // tr,), in_specs=[spec] * 4 + rd.in_specs,
        out_specs=[spec] * 3 + rd.out_specs, out_shape=[sds, sds, sds] + rd.out_shape,
        scratch_shapes=rd.scratch,
        compiler_params=_cp(("arbitrary",) if rd.exs else ("parallel",)),
    )(w2, g2, m2, v2, *rd.args)
    return tuple(t.reshape(shape) for t in outs[:3]) + tuple(outs[3:])


def reduce_scatter_tail(pair, tag):
    quad = run_exchange(chip_exchange(pair), "rs_chip_exchange_" + tag)
    return sum_slots(quad, "rs_final_sum_" + tag)


def kernel(x, c, mod_w, mod_b, mix_norm_g, ffn_norm_g, attn_w_in, attn_b_f, attn_w_o, gm_w_in, gm_v_g, gm_w_s, gm_b_s, gm_w_o, ffn_w_in, ffn_conv_w, ffn_conv_b, ffn_w_out, final_g, loss_target, m_mod_w, m_mod_b, m_mix_norm_g, m_ffn_norm_g, m_attn_w_in, m_attn_b_f, m_attn_w_o, m_gm_w_in, m_gm_v_g, m_gm_w_s, m_gm_b_s, m_gm_w_o, m_ffn_w_in, m_ffn_conv_w, m_ffn_conv_b, m_ffn_w_out, m_final_g, v_mod_w, v_mod_b, v_mix_norm_g, v_ffn_norm_g, v_attn_w_in, v_attn_b_f, v_attn_w_o, v_gm_w_in, v_gm_v_g, v_gm_w_s, v_gm_b_s, v_gm_w_o, v_ffn_w_in, v_ffn_conv_w, v_ffn_conv_b, v_ffn_w_out, v_final_g):
    xi, yi, ci = lax.axis_index("x"), lax.axis_index("y"), lax.axis_index("c")
    me = 4 * xi + 2 * yi + ci

    _, T, D = x.shape
    L = mod_w.shape[0]
    MW = mod_w.shape[2]
    NA, _, QW = attn_w_in.shape
    NB = gm_w_in.shape[0]
    H = D // HEAD_DIM
    G = D // GM_GROUP
    DR = attn_w_o.shape[1]
    GW = gm_w_in.shape[2]
    FW = ffn_w_in.shape[2]
    FR = ffn_w_out.shape[1]
    FRP = _round_up(FR, LANES // 2)
    FWP = 2 * FRP
    FP = N_CHIPS * FWP
    DFF2 = N_DEV * FW
    assert 2 * FR == FW and N_DEV * QW == 3 * D + H and N_DEV * GW == 2 * D
    c_idx = ci.reshape(1).astype(jnp.int32)

    def pad_ff(t, axis, blocks):
        ax = axis % t.ndim
        t = t.reshape(t.shape[:ax] + (blocks, FR) + t.shape[ax + 1:])
        pad = [(0, 0)] * t.ndim
        pad[ax + 1] = (0, FRP - FR)
        t = jnp.pad(t, pad)
        return t.reshape(t.shape[:ax] + (blocks * FRP,) + t.shape[ax + 2:])

    def unpad_ff(t, axis, blocks):
        ax = axis % t.ndim
        t = t.reshape(t.shape[:ax] + (blocks, FRP) + t.shape[ax + 1:])
        t = lax.slice_in_dim(t, 0, FR, axis=ax + 1)
        return t.reshape(t.shape[:ax] + (blocks * FR,) + t.shape[ax + 2:])

    x0 = x[0]
    tgt = loss_target[0]

    c_all = all_gather(c, "gather_c").reshape(N_DEV, D)
    cw_loc = pad_ff(ffn_conv_w, 2, 2).reshape(L * CONV_W, FWP)
    conv_w_full = all_gather(cw_loc, "gather_conv_w").transpose(1, 0, 2).reshape(L, CONV_W, 2 * FP)
    vg_full = all_gather(gm_v_g, "gather_vg").transpose(1, 0, 2).reshape(NB, 1, D)
    conv_b_full = pad_ff(ffn_conv_b, 1, 2 * N_DEV).reshape(L, 1, 2 * FP)

    c16 = jnp.pad(c_all, ((0, 16 - N_DEV), (0, 0)))
    mod_b_loc = lax.dynamic_slice_in_dim(mod_b, me * MW, MW, axis=1).reshape(L, 1, MW)
    mod_part = mod_fwd(c16, mod_w, mod_b_loc, "mod_fwd")[:, :N_DEV]
    mod_all = all_gather(mod_part, "gather_mod")
    mod_me = lax.dynamic_index_in_dim(mod_all, me, axis=2, keepdims=False)
    mod_me = mod_me.transpose(1, 0, 2).reshape(L, 6, 1, D)

    w_ai_t = jnp.swapaxes(attn_w_in, 1, 2).astype(bf16)
    w_gi_t = jnp.swapaxes(gm_w_in, 1, 2).astype(bf16)
    w_fi_t = pad_ff(jnp.swapaxes(ffn_w_in, 1, 2).astype(bf16), 1, 2)
    w_ao_l = attn_w_o.astype(bf16)
    w_go_l = gm_w_o.astype(bf16)
    w_fo_l = jnp.pad(ffn_w_out.astype(bf16), ((0, 0), (0, FRP - FR), (0, 0)))

    stash = []
    arrived = {}
    xc = x0

    def shard(kind, layer):
        even = layer % 2 == 0
        return {"ffn_in": w_fi_t, "ffn_out": w_fo_l, "mix_in": w_ai_t if even else w_gi_t,
                "mix_out": w_ao_l if even else w_go_l}[kind][layer if kind.startswith("ffn") else layer // 2]

    def need(kind, layer):
        if (kind, layer) not in arrived:
            arrived[(kind, layer)] = all_gather(shard(kind, layer), "gather_" + kind)
        return arrived.pop((kind, layer))

    def hosting(keys, n_own, fn, *args, **kw):
        keys = [k for k in keys if k[1] < L]
        outs = fn(*args, riders=[gather_exchange(shard(*k)) for k in keys], **kw)
        outs = list(outs) if isinstance(outs, (list, tuple)) else [outs]
        arrived.update(zip(keys, outs[n_own:]))
        return outs[:n_own]

    for i in range(L):
        sh1, sc1, g1, sh2, sc2, g2 = (mod_me[i, k] for k in range(6))
        jm = i // 2
        st = {"x_in": xc}
        h = norm_mod_fwd(xc, mix_norm_g[i][None], sc1, sh1, "norm_mod_fwd")
        st["h"] = h
        w_mi = need("mix_in", i)
        w_mo = need("mix_out", i).reshape(D, D)
        if i % 2 == 0:
            w_in_t = w_mi.reshape(N_DEV * QW, D)
            w_qkv_t = w_in_t[:3 * D]
            w_f_t = jnp.pad(w_in_t[3 * D:], ((0, LANES - H), (0, 0)))
            qkv = matmul(h, w_qkv_t, name="fox_qkv", tb=True, out_dtype=bf16)
            flog = matmul(h, w_f_t, name="fox_flog", tb=True)
            flog_t = flog[:, :H].T
            b_col = attn_b_f[jm][:, None]
            F = fox_gates_fwd(flog_t, b_col, "fox_gates_fwd")
            f_row, f_col = F[:, None, :], F[:, :, None]
            o, lse = hosting([("ffn_in", i), ("ffn_in", i + 1), ("mix_in", i + 1), ("mix_out", i + 1)], 2,
                             fox_attn_fwd, qkv, f_row, f_col, "fox_attn_fwd")
            x1, y = matmul(o, w_mo, name="mix_out", resid=xc, gvec=g1, emit_acc=True)
            st.update(qkv=qkv, flog_t=flog_t, b_col=b_col, f_row=f_row, f_col=f_col, o=o, lse=lse,
                      w_qkv_t=w_qkv_t, w_f_t=w_f_t, w_mo=w_mo)
        else:
            w_gi_full = w_mi.reshape(2 * D, D)
            z = matmul(h, w_gi_full, name="gm_in", tb=True)
            bs_t = gm_b_s[jm].T
            gated = gm_gate_fwd(z, vg_full[jm], gm_w_s[jm], bs_t, "gm_gate_fwd")
            x1, y = matmul(gated, w_mo, name="mix_out", resid=xc, gvec=g1, emit_acc=True)
            st.update(z=z, bs_t=bs_t, gated=gated, w_gi_full=w_gi_full, w_mo=w_mo)
        st.update(y=y, x1=x1)
        h2 = norm_mod_fwd(x1, ffn_norm_g[i][None], sc2, sh2, "norm_mod_fwd")
        w_fi_full = need("ffn_in", i).reshape(2 * FP, D)
        fox = i % 2 == 0
        a3, = hosting([("ffn_out", i) if fox else ("mix_in", i + 1)], 1,
                      matmul, h2, w_fi_full, name="ffn_up", tb=True, out_split=True)
        act = conv_glu_fwd(a3, conv_w_full[i], conv_b_full[i], "conv_glu_fwd")
        w_fo_full = need("ffn_out", i).reshape(FP, D)
        xc, f_out = hosting([("ffn_out", i + 1) if fox else ("mix_out", i + 1)], 2,
                            matmul, act, w_fo_full, name="ffn_down", resid=x1, gvec=g2, emit_acc=True)
        st.update(h2=h2, a3=a3, act=act, f=f_out, w_fi_full=w_fi_full, w_fo_full=w_fo_full)
        stash.append(st)

    loss_part, dx, d_final_g = loss_head(xc, final_g[None], tgt, "loss_head")
    loss = lax.psum(loss_part[0, 0], AXES)

    d_mod = [None] * L
    d_mix_g = [None] * L
    d_ffn_g = [None] * L
    d_conv_w = [None] * L
    d_conv_b = [None] * L
    g_wfi = [None] * L
    g_wfo = [None] * L
    g_wai = [None] * NA
    g_wao = [None] * NA
    d_bf = [None] * NA
    g_wgi = [None] * NB
    g_wgo = [None] * NB
    d_ws = [None] * NB
    d_bs = [None] * NB
    d_vg = [None] * NB
    ffn_pairs = None
    mix_pairs = None

    def finish_mixer(layer, quad_mo, quad_in):
        g_mo = sum_slots(quad_mo, "rs_final_sum_mix_out")
        g_in = sum_slots(quad_in, "rs_final_sum_mix_in").T
        if layer % 2 == 0:
            g_wao[layer // 2], g_wai[layer // 2] = g_mo, g_in
        else:
            g_wgo[layer // 2], g_wgi[layer // 2] = g_mo, g_in

    for i in reversed(range(L)):
        st = stash[i]
        sh1, sc1, g1, sh2, sc2, g2 = (mod_me[i, k] for k in range(6))
        jm = i // 2
        dy, dg2 = gate_bwd(dx, st["f"], g2, "gate_bwd")
        if mix_pairs is None:
            dact = matmul(dy, st["w_fo_full"], name="ffn_down_dx", tb=True)
        else:
            dact, quad_mo = matmul(dy, st["w_fo_full"], name="ffn_down_dx", tb=True,
                                   riders=[chip_exchange(mix_pairs[1])])
        if ffn_pairs is None:
            dw_fo = matmul(st["act"], dy, name="ffn_down_dw", ta=True, out_dtype=bf16)
        else:
            dw_fo, quad = matmul(st["act"], dy, name="ffn_down_dw", ta=True, out_dtype=bf16,
                                 riders=[chip_exchange(ffn_pairs[0])])
            g_wfo[i + 1] = sum_slots(quad, "rs_final_sum_ffn_out")[:FR]
        g8_fo = dw_fo.reshape(N_DEV, FRP, D)
        da3, dwg, dwu, dbg, dbu, got_fo = conv_glu_bwd(st["a3"], conv_w_full[i], conv_b_full[i], dact,
                                                       "conv_glu_bwd", riders=[pair_exchange(g8_fo)])
        pair_fo = pair_sum(g8_fo, got_fo, c_idx, "rs_pair_sum_ffn_out")
        d_conv_w[i] = jnp.concatenate([dwg, dwu], axis=1)
        d_conv_b[i] = jnp.concatenate([dbg, dbu], axis=1)
        if ffn_pairs is None:
            dw_fi_t = matmul(da3, st["h2"], name="ffn_up_dw", ta=True, a_split=True, out_dtype=bf16)
        else:
            dw_fi_t, quad = matmul(da3, st["h2"], name="ffn_up_dw", ta=True, a_split=True, out_dtype=bf16,
                                   riders=[chip_exchange(ffn_pairs[1])])
            g_wfi[i + 1] = unpad_ff(sum_slots(quad, "rs_final_sum_ffn_in"), 0, 2).T
        g8_fi = dw_fi_t.reshape(N_DEV, FWP, D)
        if mix_pairs is None:
            dh2, got_fi = matmul(da3, st["w_fi_full"], name="ffn_up_dx", a_split=True,
                                 riders=[pair_exchange(g8_fi)])
        else:
            dh2, got_fi, quad_in = matmul(da3, st["w_fi_full"], name="ffn_up_dx", a_split=True,
                                          riders=[pair_exchange(g8_fi), chip_exchange(mix_pairs[2])])
            finish_mixer(mix_pairs[0], quad_mo, quad_in)
        ffn_pairs = (pair_fo, pair_sum(g8_fi, got_fi, c_idx, "rs_pair_sum_ffn_in"))
        dx, dsh2, dsc2, d_ffn_g[i] = norm_mod_bwd(st["x1"], ffn_norm_g[i][None], sc2, dh2, dx, "norm_mod_bwd")
        dy, dg1 = gate_bwd(dx, st["y"], g1, "gate_bwd")
        if i % 2 == 0:
            do = matmul(dy, st["w_mo"], name="mix_out_dx", tb=True)
            g8_mo = matmul(st["o"], dy, name="mix_out_dw", ta=True, out_dtype=bf16).reshape(N_DEV, DR, D)
            dq, dk, dv, cs, rs, quad_fo, quad_fi, got_mo = fox_attn_bwd(
                st["qkv"], st["o"], do, st["lse"], st["f_row"], st["f_col"], "fox_attn_bwd",
                riders=[chip_exchange(ffn_pairs[0]), chip_exchange(ffn_pairs[1]), pair_exchange(g8_mo)])
            g_wfo[i] = sum_slots(quad_fo, "rs_final_sum_ffn_out")[:FR]
            g_wfi[i] = unpad_ff(sum_slots(quad_fi, "rs_final_sum_ffn_in"), 0, 2).T
            ffn_pairs = None
            dF = rs[:, :, 0] - cs[:, 0, :]
            dflog_t, d_bf[jm] = fox_gates_bwd(st["flog_t"], st["b_col"], dF, "fox_gates_bwd")
            dflog = jnp.pad(dflog_t.T, ((0, 0), (0, LANES - H))).astype(bf16)
            dqkv = jnp.concatenate([dq, dk, dv], axis=1)
            dw_qkv_t = matmul(dqkv, st["h"], name="fox_qkv_dw", ta=True, out_dtype=bf16)
            dw_f_t = matmul(dflog, st["h"], name="fox_flog_dw", ta=True, out_dtype=bf16)
            g8_in = jnp.concatenate([dw_qkv_t, dw_f_t[:H]], axis=0).reshape(N_DEV, QW, D)
            dh, got_in = matmul(dqkv, st["w_qkv_t"], name="fox_qkv_dx", riders=[pair_exchange(g8_in)])
            dh = matmul(dflog, st["w_f_t"], name="fox_flog_dx", resid=dh)
        else:
            dgated = matmul(dy, st["w_mo"], name="mix_out_dx", tb=True)
            g8_mo = matmul(st["gated"], dy, name="mix_out_dw", ta=True, out_dtype=bf16).reshape(N_DEV, DR, D)
            dz, d_ws[jm], dbs_t, d_vg[jm] = gm_gate_bwd(st["z"], vg_full[jm], gm_w_s[jm], st["bs_t"], dgated, "gm_gate_bwd")
            d_bs[jm] = dbs_t.T
            dw_in_t, got_mo = matmul(dz, st["h"], name="gm_in_dw", ta=True, out_dtype=bf16,
                                     riders=[pair_exchange(g8_mo)])
            g8_in = dw_in_t.reshape(N_DEV, GW, D)
            dh, got_in = matmul(dz, st["w_gi_full"], name="gm_in_dx", riders=[pair_exchange(g8_in)])
        mix_pairs = (i, pair_sum(g8_mo, got_mo, c_idx, "rs_pair_sum_mix_out"),
                     pair_sum(g8_in, got_in, c_idx, "rs_pair_sum_mix_in"))
        dx, dsh1, dsc1, d_mix_g[i] = norm_mod_bwd(st["x_in"], mix_norm_g[i][None], sc1, dh, dx, "norm_mod_bwd")
        d_mod[i] = jnp.concatenate([dsh1, dsc1, dg1, dsh2, dsc2, dg2], axis=0)

    grad_x = dx[None]
    if ffn_pairs is not None:
        g_wfo[0] = reduce_scatter_tail(ffn_pairs[0], "ffn_out")[:FR]
        g_wfi[0] = unpad_ff(reduce_scatter_tail(ffn_pairs[1], "ffn_in"), 0, 2).T

    def gathered_sum(rows, tag, mult=SUBLANES):
        n = rows.shape[0]
        rows = jnp.pad(rows, ((0, _round_up(n, mult) - n), (0, 0)))
        every = all_gather(rows, "gather_small_grads_" + tag)
        return every, sum_slots(every, "sum_small_grads_" + tag)

    rows_d = jnp.concatenate([jnp.concatenate(d_mod, axis=0), jnp.concatenate(d_mix_g, axis=0),
                              jnp.concatenate(d_ffn_g, axis=0), jnp.concatenate(d_vg, axis=0), d_final_g], axis=0)
    every_d, sum_d = gathered_sum(rows_d, "d")
    r0 = L * 6
    grad_mod_b = sum_d[:r0].reshape(L, 6 * D)
    grad_mix_g, grad_ffn_g = sum_d[r0:r0 + L], sum_d[r0 + L:r0 + 2 * L]
    grad_vg_full = sum_d[r0 + 2 * L:r0 + 2 * L + NB]
    grad_final_g = sum_d[r0 + 2 * L + NB]
    grad_vg = lax.dynamic_slice_in_dim(grad_vg_full, me * DR, DR, axis=1)

    rows_f = jnp.concatenate([jnp.concatenate(d_conv_w, axis=0), jnp.concatenate(d_conv_b, axis=0)], axis=0)
    _, sum_f = gathered_sum(rows_f, "f")
    sum_f = unpad_ff(sum_f, 1, 2 * N_DEV)
    grad_conv_w = lax.dynamic_slice_in_dim(sum_f[:L * CONV_W].reshape(L, CONV_W, DFF2), me * FW, FW, axis=2)
    grad_conv_b = sum_f[L * CONV_W:L * CONV_W + L]

    rows_c = jnp.concatenate([jnp.stack(d_ws).reshape(NB * G * CHUNK, CHUNK), jnp.stack(d_bs).reshape(NB * G, CHUNK),
                              jnp.pad(jnp.stack(d_bf).reshape(NA, H), ((0, 0), (0, LANES - H)))], axis=0)
    _, sum_c = gathered_sum(rows_c, "c", mult=SLOT_ROWS)
    n_ws = NB * G * CHUNK
    grad_ws = sum_c[:n_ws].reshape(NB, G, CHUNK, CHUNK)
    grad_bs = sum_c[n_ws:n_ws + NB * G].reshape(NB, G, CHUNK)
    grad_bf = sum_c[n_ws + NB * G:n_ws + NB * G + NA, :H]

    dmod_all = every_d[:, :r0].reshape(N_DEV, L, 6 * D)
    dmod_loc = lax.dynamic_slice_in_dim(dmod_all, me * MW, MW, axis=2).transpose(1, 0, 2)
    dmod_loc = jnp.pad(dmod_loc, ((0, 0), (0, LANES - N_DEV), (0, 0)))
    c_t = jnp.pad(c_all.T, ((0, 0), (0, LANES - N_DEV)))
    grad_mod_w = mod_w_bwd(c_t, dmod_loc, "mod_w_bwd")

    first = adamw(mod_w, grad_mod_w, m_mod_w, v_mod_w, "adamw",
                  riders=[chip_exchange(mix_pairs[1]), chip_exchange(mix_pairs[2])])
    finish_mixer(mix_pairs[0], first[3], first[4])
    grad_attn_w_in, grad_attn_w_o = jnp.stack(g_wai), jnp.stack(g_wao)
    grad_gm_w_in, grad_gm_w_o = jnp.stack(g_wgi), jnp.stack(g_wgo)
    grad_ffn_w_in, grad_ffn_w_out = jnp.stack(g_wfi), jnp.stack(g_wfo)

    weights = [mod_w, mod_b, mix_norm_g, ffn_norm_g, attn_w_in, attn_b_f, attn_w_o, gm_w_in, gm_v_g, gm_w_s,
               gm_b_s, gm_w_o, ffn_w_in, ffn_conv_w, ffn_conv_b, ffn_w_out, final_g]
    grads = [grad_mod_w, grad_mod_b, grad_mix_g, grad_ffn_g, grad_attn_w_in, grad_bf, grad_attn_w_o,
             grad_gm_w_in, grad_vg, grad_ws, grad_bs, grad_gm_w_o, grad_ffn_w_in, grad_conv_w, grad_conv_b,
             grad_ffn_w_out, grad_final_g]
    ms = [m_mod_w, m_mod_b, m_mix_norm_g, m_ffn_norm_g, m_attn_w_in, m_attn_b_f, m_attn_w_o, m_gm_w_in, m_gm_v_g,
          m_gm_w_s, m_gm_b_s, m_gm_w_o, m_ffn_w_in, m_ffn_conv_w, m_ffn_conv_b, m_ffn_w_out, m_final_g]
    vs = [v_mod_w, v_mod_b, v_mix_norm_g, v_ffn_norm_g, v_attn_w_in, v_attn_b_f, v_attn_w_o, v_gm_w_in, v_gm_v_g,
          v_gm_w_s, v_gm_b_s, v_gm_w_o, v_ffn_w_in, v_ffn_conv_w, v_ffn_conv_b, v_ffn_w_out, v_final_g]
    deltas, new_ms, new_vs = [], [], []
    for k, (w, g, m_, v_) in enumerate(zip(weights, grads, ms, vs)):
        d_, mn_, vn_ = first[:3] if k == 0 else adamw(w, g, m_, v_, "adamw")
        deltas.append(d_)
        new_ms.append(mn_)
        new_vs.append(vn_)

    return (loss, grad_x, *grads, *deltas, *new_ms, *new_vs)
```

```python
import numpy as np
import jax
import jax.numpy as jnp
from jax import lax
from jax.experimental import pallas as pl
from jax.experimental.pallas import tpu as pltpu

f32 = jnp.float32
bf16 = jnp.bfloat16

AXES = ("x", "y", "c")
N_DEV = 8
N_CHIPS = 4
LANES = 128
SUBLANES = 8
HEAD_DIM = 128
CHUNK = 128
GM_GROUP = 128
CONV_W = 3
EPS = 1e-6
NEG = -1e30
VMEM_LIMIT_BYTES = 56 * 1024 * 1024
MATMUL_VMEM_BYTES = 40 * 1024 * 1024
SLOT_ROWS = 512
LOG2E = 1.4426950408889634
ATT_Q_TILE = 1024
ATT_KEYS_PER_Q = 1

ADAM_LR = 0.001
ADAM_B1 = 0.9
ADAM_B2 = 0.999
ADAM_EPS = 1e-08
ADAM_WD = 0.01
ADAM_STEP = 10

MESH = pl.DeviceIdType.MESH
ANY = pl.BlockSpec(memory_space=pl.ANY)


def _cp(sem):
    return pltpu.CompilerParams(dimension_semantics=sem, vmem_limit_bytes=VMEM_LIMIT_BYTES)


def _pick(n, prefs):
    for p in prefs:
        if n % p == 0:
            return p
    return n


def _round_up(n, m):
    return (n + m - 1) // m * m


class Exchange:
    def __init__(self, src, out_shape, scratch, phases):
        self.src = src
        self.out_shape = out_shape
        self.scratch = scratch
        self.phases = phases


def gather_exchange(xl):
    def phases(x_ref, out_ref, send_sems, recv_sems, local_sem):
        x, y, c = lax.axis_index("x"), lax.axis_index("y"), lax.axis_index("c")
        me, sibling = (x, y, c), (x, y, 1 - c)
        chips = [(1 - x, y), (x, 1 - y), (1 - x, 1 - y)]

        def slot(px, py, pc):
            return out_ref.at[4 * px + 2 * py + pc]

        def copy(k, block, to, src=None):
            return pltpu.make_async_remote_copy(
                src_ref=slot(*block) if src is None else src, dst_ref=slot(*block),
                send_sem=send_sems.at[k], recv_sem=recv_sems.at[k],
                device_id=to, device_id_type=MESH)

        mine = pltpu.make_async_copy(x_ref, slot(*me), local_sem)
        first = [copy(0, me, sibling, src=x_ref)]
        first += [copy(1 + j, me, (*chip, c), src=x_ref) for j, chip in enumerate(chips)]
        passed = [copy(4 + j, (*chip, c), sibling) for j, chip in enumerate(chips)]

        def start():
            mine.start()
            for cp in first:
                cp.start()

        def hand_on():
            for j, chip in enumerate(chips):
                copy(1 + j, (*chip, c), me).wait_recv()
                passed[j].start()

        def finish():
            copy(0, sibling, me).wait_recv()
            for j, chip in enumerate(chips):
                copy(4 + j, (*chip, 1 - c), me).wait_recv()
            for cp in first + passed:
                cp.wait_send()
            mine.wait()

        return start, hand_on, finish

    return Exchange(xl, jax.ShapeDtypeStruct((N_DEV,) + xl.shape, xl.dtype),
                    [pltpu.SemaphoreType.DMA((7,)), pltpu.SemaphoreType.DMA((7,)),
                     pltpu.SemaphoreType.DMA], phases)


def pair_exchange(g8):
    _, R, W = g8.shape

    def phases(g_ref, out_ref, send_sems, recv_sems):
        x, y, c = lax.axis_index("x"), lax.axis_index("y"), lax.axis_index("c")
        copies = [pltpu.make_async_remote_copy(
            src_ref=g_ref.at[2 * q + (1 - c)], dst_ref=out_ref.at[q],
            send_sem=send_sems.at[q], recv_sem=recv_sems.at[q],
            device_id=(x, y, 1 - c), device_id_type=MESH) for q in range(N_CHIPS)]

        def start():
            for cp in copies:
                cp.start()

        def finish():
            for cp in copies:
                cp.wait()

        return start, None, finish

    return Exchange(g8, jax.ShapeDtypeStruct((N_CHIPS, R, W), g8.dtype),
                    [pltpu.SemaphoreType.DMA((N_CHIPS,)), pltpu.SemaphoreType.DMA((N_CHIPS,))], phases)


def chip_exchange(p4):
    def phases(p_ref, out_ref, send_sems, recv_sems, local_sem):
        x, y, c = lax.axis_index("x"), lax.axis_index("y"), lax.axis_index("c")
        my_q = 2 * x + y
        chips = [(1 - x, y), (x, 1 - y), (1 - x, 1 - y)]
        mine = pltpu.make_async_copy(p_ref.at[my_q], out_ref.at[my_q], local_sem)
        copies = [pltpu.make_async_remote_copy(
            src_ref=p_ref.at[2 * px + py], dst_ref=out_ref.at[my_q],
            send_sem=send_sems.at[k], recv_sem=recv_sems.at[k],
            device_id=(px, py, c), device_id_type=MESH) for k, (px, py) in enumerate(chips)]

        def start():
            mine.start()
            for cp in copies:
                cp.start()

        def finish():
            for k, (px, py) in enumerate(chips):
                pltpu.make_async_remote_copy(
                    src_ref=p_ref.at[my_q], dst_ref=out_ref.at[2 * px + py],
                    send_sem=send_sems.at[k], recv_sem=recv_sems.at[k],
                    device_id=(px, py, c), device_id_type=MESH).wait_recv()
            for cp in copies:
                cp.wait_send()
            mine.wait()

        return start, None, finish

    return Exchange(p4, jax.ShapeDtypeStruct(p4.shape, p4.dtype),
                    [pltpu.SemaphoreType.DMA((3,)), pltpu.SemaphoreType.DMA((3,)),
                     pltpu.SemaphoreType.DMA], phases)


def run_exchange(ex, name):
    def body(src_ref, out_ref, *sems):
        start, hand_on, finish = ex.phases(src_ref, out_ref, *sems)
        start()
        if hand_on is not None:
            hand_on()
        finish()

    return pl.pallas_call(body, name=name, out_shape=ex.out_shape, in_specs=[ANY], out_specs=ANY,
                          scratch_shapes=ex.scratch)(ex.src)


def all_gather(xl, name):
    return run_exchange(gather_exchange(xl), name)


class Riders:
    def __init__(self, exchanges):
        self.exs = list(exchanges or [])
        self.in_specs = [ANY] * len(self.exs)
        self.args = [ex.src for ex in self.exs]
        self.out_specs = [ANY] * len(self.exs)
        self.out_shape = [ex.out_shape for ex in self.exs]
        self.scratch = [s for ex in self.exs for s in ex.scratch]

    def split(self, in_refs, out_refs, scratch_refs):
        n = len(self.exs)
        self.refs = []
        pos = len(scratch_refs) - len(self.scratch)
        for k, ex in enumerate(self.exs):
            sems = scratch_refs[pos:pos + len(ex.scratch)]
            pos += len(ex.scratch)
            self.refs.append((in_refs[len(in_refs) - n + k], out_refs[len(out_refs) - n + k], sems))

    def _parts(self):
        return [ex.phases(src, out, *sems) for ex, (src, out, sems) in zip(self.exs, self.refs)]

    def before(self, step):
        if not self.exs:
            return
        parts = self._parts()

        @pl.when(step == 0)
        def _():
            for start, _, _ in parts:
                start()

    def after(self, step, n_steps):
        if not self.exs:
            return
        parts = self._parts()
        mid = (3 * n_steps) // 4

        if any(h is not None for _, h, _ in parts):
            @pl.when(step == mid)
            def _():
                for _, hand_on, _ in parts:
                    if hand_on is not None:
                        hand_on()

        @pl.when(step == n_steps - 1)
        def _():
            for _, _, finish in parts:
                finish()


def pair_sum(g8, got4, c_idx, name):
    _, R, W = g8.shape
    tr = _pick(R, (512, 256, 128, 64, 32, 16))
    g5 = g8.reshape(N_CHIPS, 2, R, W)

    def body(c_ref, a_ref, b_ref, o_ref):
        o_ref[...] = (a_ref[...].astype(f32) + b_ref[...].astype(f32)).astype(o_ref.dtype)

    grid_spec = pltpu.PrefetchScalarGridSpec(
        num_scalar_prefetch=1, grid=(N_CHIPS, R // tr),
        in_specs=[pl.BlockSpec((None, None, tr, W), lambda q, r, cr: (q, cr[0], r, 0)),
                  pl.BlockSpec((None, tr, W), lambda q, r, cr: (q, r, 0))],
        out_specs=pl.BlockSpec((None, tr, W), lambda q, r, cr: (q, r, 0)))
    return pl.pallas_call(
        body, name=name, grid_spec=grid_spec,
        out_shape=jax.ShapeDtypeStruct((N_CHIPS, R, W), g8.dtype),
        compiler_params=_cp(("parallel", "parallel")),
    )(c_idx, g5, got4)


def sum_slots(xs, name, out_dtype=f32):
    S, R, W = xs.shape
    tr = _pick(R, (512, 256, 128, 64, 32, 16, 8))

    def body(x_ref, o_ref):
        acc = x_ref[0].astype(f32)
        for s in range(1, S):
            acc = acc + x_ref[s].astype(f32)
        o_ref[...] = acc.astype(o_ref.dtype)

    return pl.pallas_call(
        body, name=name, grid=(R // tr,),
        in_specs=[pl.BlockSpec((S, tr, W), lambda r: (0, r, 0))],
        out_specs=pl.BlockSpec((tr, W), lambda r: (r, 0)),
        out_shape=jax.ShapeDtypeStruct((R, W), out_dtype),
        compiler_params=_cp(("parallel",)),
    )(xs)


def matmul(a, b, *, name, ta=False, tb=False, a_split=False, b_split=False, out_split=False,
           out_dtype=f32, resid=None, gvec=None, emit_acc=False, riders=None):
    rd = Riders(riders)
    if a_split:
        rows, cols = a.shape[1], 2 * a.shape[2]
        M, K = (cols, rows) if ta else (rows, cols)
    else:
        M, K = (a.shape[1], a.shape[0]) if ta else a.shape
    if b_split:
        assert not tb
        N = 2 * b.shape[2]
        assert b.shape[1] == K
    else:
        N = b.shape[0] if tb else b.shape[1]
        assert (b.shape[1] if tb else b.shape[0]) == K, (a.shape, b.shape, name)

    m_split = a_split and ta
    k_split = a_split and not ta
    n_split = b_split or out_split
    tm = _pick(M // 2 if m_split else M, (1024, 512, 256, 128, 64, 32, 16, 8))
    k_len, n_len = (K // 2 if k_split else K), (N // 2 if n_split else N)
    out_bytes = jnp.dtype(out_dtype).itemsize + (4 if resid is not None else 0) + (2 if emit_acc else 0)

    def fits(tk_, tn_):
        operands = 2 * (tm * tk_ * a.dtype.itemsize + tk_ * tn_ * b.dtype.itemsize)
        acc = tm * tn_ * 4 if tk_ < K else 0
        return operands + acc + 2 * tm * tn_ * out_bytes <= MATMUL_VMEM_BYTES

    tk_options = [d for d in range(k_len, 0, -LANES) if k_len % d == 0 and d % LANES == 0]
    tn_options = [t for t in (1024, 512, 256, 128) if n_len % t == 0]
    tk, tn = next(((tk_, tn_) for tn_min in (512, 128) for tk_ in tk_options for tn_ in tn_options
                   if tn_ >= tn_min and fits(tk_, tn_)), (tk_options[-1], tn_options[-1]))
    nk = K // tk
    n_half = (N // 2) // tn if n_split else 0
    k_half = (K // 2) // tk if k_split else 0
    m_half = (M // 2) // tm if m_split else 0

    if m_split:
        a_spec = pl.BlockSpec((None, tk, tm), lambda i, j, k: (i // m_half, k, i % m_half))
    elif k_split:
        a_spec = pl.BlockSpec((None, tm, tk), lambda i, j, k: (k // k_half, i, k % k_half))
    elif ta:
        a_spec = pl.BlockSpec((tk, tm), lambda i, j, k: (k, i))
    else:
        a_spec = pl.BlockSpec((tm, tk), lambda i, j, k: (i, k))
    if b_split:
        b_spec = pl.BlockSpec((None, tk, tn), lambda i, j, k: (j // n_half, k, j % n_half))
    elif tb:
        b_spec = pl.BlockSpec((tn, tk), lambda i, j, k: (j, k))
    else:
        b_spec = pl.BlockSpec((tk, tn), lambda i, j, k: (k, j))
    if out_split:
        o_spec = pl.BlockSpec((None, tm, tn), lambda i, j, k: (j // n_half, i, j % n_half))
        o_shape = (2, M, N // 2)
    else:
        o_spec = pl.BlockSpec((tm, tn), lambda i, j, k: (i, j))
        o_shape = (M, N)

    in_specs = [a_spec, b_spec]
    args = [a, b]
    if resid is not None:
        in_specs.append(pl.BlockSpec((tm, tn), lambda i, j, k: (i, j)))
        args.append(resid)
    if gvec is not None:
        in_specs.append(pl.BlockSpec((1, tn), lambda i, j, k: (0, j)))
        args.append(gvec)
    out_specs = [o_spec]
    out_shape = [jax.ShapeDtypeStruct(o_shape, out_dtype)]
    if emit_acc:
        out_specs.append(pl.BlockSpec((tm, tn), lambda i, j, k: (i, j)))
        out_shape.append(jax.ShapeDtypeStruct((M, N), bf16))
    dims = (((0 if ta else 1,), (1 if tb else 0,)), ((), ()))
    has_r, has_g = resid is not None, gvec is not None
    n_in, n_out = len(in_specs) + len(rd.exs), len(out_specs) + len(rd.exs)
    grid = (M // tm, N // tn, nk)
    n_steps = grid[0] * grid[1] * grid[2]

    def body(*refs):
        in_refs, out_refs, scratch_refs = refs[:n_in], refs[n_in:n_in + n_out], refs[n_in + n_out:]
        rd.split(in_refs, out_refs, scratch_refs)
        a_ref, b_ref = in_refs[0], in_refs[1]
        pos = 2
        r_ref = g_ref = None
        if has_r:
            r_ref = in_refs[pos]
            pos += 1
        if has_g:
            g_ref = in_refs[pos]
        o_ref = out_refs[0]
        y_ref = out_refs[1] if emit_acc else None
        step = (pl.program_id(0) * grid[1] + pl.program_id(1)) * nk + pl.program_id(2)
        rd.before(step)

        def finish(acc):
            if emit_acc:
                y_ref[...] = acc.astype(bf16)
            if has_g:
                acc = acc * g_ref[...]
            if has_r:
                acc = r_ref[...] + acc
            o_ref[...] = acc.astype(o_ref.dtype)

        part = lax.dot_general(a_ref[...].astype(bf16), b_ref[...].astype(bf16), dims,
                               preferred_element_type=f32)
        if nk == 1:
            finish(part)
        else:
            acc_ref = scratch_refs[0]
            k = pl.program_id(2)

            @pl.when(k == 0)
            def _():
                acc_ref[...] = part

            @pl.when(k > 0)
            def _():
                acc_ref[...] += part

            @pl.when(k == nk - 1)
            def _():
                finish(acc_ref[...])

        rd.after(step, n_steps)

    outs = pl.pallas_call(
        body, name=name, grid=grid,
        in_specs=in_specs + rd.in_specs, out_specs=out_specs + rd.out_specs,
        out_shape=out_shape + rd.out_shape,
        scratch_shapes=([pltpu.VMEM((tm, tn), f32)] if nk > 1 else []) + rd.scratch,
        compiler_params=_cp(("arbitrary",) * 3 if rd.exs else ("parallel", "parallel", "arbitrary")),
    )(*args, *rd.args)
    return outs if (emit_acc or rd.exs) else outs[0]


def _rows(T):
    return _pick(T, (256, 128, 64, 32, 16, 8))


def norm_mod_fwd(x, gn, sc, sh, name):
    T, D = x.shape
    tr = _rows(T)

    def body(x_ref, gn_ref, sc_ref, sh_ref, h_ref):
        xv = x_ref[...]
        r = lax.rsqrt(jnp.mean(xv * xv, axis=-1, keepdims=True) + EPS)
        y = (xv * r) * gn_ref[...]
        h_ref[...] = (y * (1.0 + sc_ref[...]) + sh_ref[...]).astype(bf16)

    vec = pl.BlockSpec((1, D), lambda i: (0, 0))
    row = pl.BlockSpec((tr, D), lambda i: (i, 0))
    return pl.pallas_call(
        body, name=name, grid=(T // tr,), in_specs=[row, vec, vec, vec], out_specs=row,
        out_shape=jax.ShapeDtypeStruct((T, D), bf16), compiler_params=_cp(("parallel",)),
    )(x, gn, sc, sh)


def norm_mod_bwd(x, gn, sc, dh, dx_res, name):
    T, D = x.shape
    tr = _rows(T)

    def body(x_ref, gn_ref, sc_ref, dh_ref, dr_ref, dx_ref, dsh_ref, dsc_ref, dgn_ref):
        @pl.when(pl.program_id(0) == 0)
        def _():
            dsh_ref[...] = jnp.zeros_like(dsh_ref)
            dsc_ref[...] = jnp.zeros_like(dsc_ref)
            dgn_ref[...] = jnp.zeros_like(dgn_ref)

        xv = x_ref[...]
        r = lax.rsqrt(jnp.mean(xv * xv, axis=-1, keepdims=True) + EPS)
        xn = xv * r
        gn_v = gn_ref[...]
        dh_v = dh_ref[...]
        dsh_ref[...] += jnp.sum(dh_v, axis=0, keepdims=True)
        dsc_ref[...] += jnp.sum(dh_v * (xn * gn_v), axis=0, keepdims=True)
        dy = dh_v * (1.0 + sc_ref[...])
        dgn_ref[...] += jnp.sum(dy * xn, axis=0, keepdims=True)
        dxn = dy * gn_v
        dx = r * (dxn - xn * jnp.mean(dxn * xn, axis=-1, keepdims=True))
        dx_ref[...] = dr_ref[...] + dx

    vec = pl.BlockSpec((1, D), lambda i: (0, 0))
    row = pl.BlockSpec((tr, D), lambda i: (i, 0))
    vshape = jax.ShapeDtypeStruct((1, D), f32)
    return pl.pallas_call(
        body, name=name, grid=(T // tr,), in_specs=[row, vec, vec, row, row],
        out_specs=[row, vec, vec, vec],
        out_shape=[jax.ShapeDtypeStruct((T, D), f32), vshape, vshape, vshape],
        compiler_params=_cp(("arbitrary",)),
    )(x, gn, sc, dh, dx_res)


def gate_bwd(dx, y, g, name):
    T, D = dx.shape
    tr = _rows(T)

    def body(dx_ref, y_ref, g_ref, dy_ref, dg_ref):
        @pl.when(pl.program_id(0) == 0)
        def _():
            dg_ref[...] = jnp.zeros_like(dg_ref)

        dxv = dx_ref[...]
        dy_ref[...] = (dxv * g_ref[...]).astype(bf16)
        dg_ref[...] += jnp.sum(dxv * y_ref[...].astype(f32), axis=0, keepdims=True)

    vec = pl.BlockSpec((1, D), lambda i: (0, 0))
    row = pl.BlockSpec((tr, D), lambda i: (i, 0))
    return pl.pallas_call(
        body, name=name, grid=(T // tr,), in_specs=[row, row, vec], out_specs=[row, vec],
        out_shape=[jax.ShapeDtypeStruct((T, D), bf16), jax.ShapeDtypeStruct((1, D), f32)],
        compiler_params=_cp(("arbitrary",)),
    )(dx, y, g)


def loss_head(x, fg, tgt, name):
    T, D = x.shape
    tr = _rows(T)

    def body(x_ref, fg_ref, t_ref, loss_ref, dx_ref, dfg_ref):
        @pl.when(pl.program_id(0) == 0)
        def _():
            loss_ref[...] = jnp.zeros_like(loss_ref)
            dfg_ref[...] = jnp.zeros_like(dfg_ref)

        xv = x_ref[...]
        r = lax.rsqrt(jnp.mean(xv * xv, axis=-1, keepdims=True) + EPS)
        xn = xv * r
        fg_v = fg_ref[...]
        err = xn * fg_v - t_ref[...]
        per_tok = jnp.mean(err * err, axis=-1, keepdims=True)
        loss_ref[...] += 0.5 * jnp.sum(per_tok, axis=0, keepdims=True)
        dy = err * (1.0 / D)
        dfg_ref[...] += jnp.sum(dy * xn, axis=0, keepdims=True)
        dxn = dy * fg_v
        dx_ref[...] = r * (dxn - xn * jnp.mean(dxn * xn, axis=-1, keepdims=True))

    vec = pl.BlockSpec((1, D), lambda i: (0, 0))
    row = pl.BlockSpec((tr, D), lambda i: (i, 0))
    one = pl.BlockSpec((1, 1), lambda i: (0, 0))
    return pl.pallas_call(
        body, name=name, grid=(T // tr,), in_specs=[row, vec, row], out_specs=[one, row, vec],
        out_shape=[jax.ShapeDtypeStruct((1, 1), f32), jax.ShapeDtypeStruct((T, D), f32),
                   jax.ShapeDtypeStruct((1, D), f32)],
        compiler_params=_cp(("arbitrary",)),
    )(x, fg, tgt)


def _conv_tiles(T, FP):
    return _pick(T, (512, 256, 128, 64, 32, 16, 8)), _pick(FP, (512, 256, 128))


def _conv_specs(tr, tc, T, FP):
    nj = FP // tc
    r8 = tr // SUBLANES
    last8 = T // SUBLANES - 1
    main = pl.BlockSpec((2, tr, tc), lambda j, i: (0, i, j))
    prev = pl.BlockSpec((2, SUBLANES, tc), lambda j, i: (0, jnp.maximum(i * r8 - 1, 0), j))
    nxt = pl.BlockSpec((2, SUBLANES, tc), lambda j, i: (0, jnp.minimum((i + 1) * r8, last8), j))
    wg = pl.BlockSpec((CONV_W, tc), lambda j, i: (0, j))
    wu = pl.BlockSpec((CONV_W, tc), lambda j, i: (0, j + nj))
    bg = pl.BlockSpec((1, tc), lambda j, i: (0, j))
    bu = pl.BlockSpec((1, tc), lambda j, i: (0, j + nj))
    return main, prev, nxt, wg, wu, bg, bu


def _causal_taps(av, hp_ref, s, has_prev, row):
    h7 = jnp.where(has_prev, hp_ref[s, 7:8, :], 0.0)
    h6 = jnp.where(has_prev, hp_ref[s, 6:7, :], 0.0)
    m1 = jnp.where(row == 0, h7, pltpu.roll(av, 1, 0))
    m2 = jnp.where(row == 0, h6, jnp.where(row == 1, h7, pltpu.roll(av, 2, 0)))
    return m1, m2


def conv_glu_fwd(a3, conv_w, conv_b, name, riders=None):
    _, T, FP = a3.shape
    tr, tc = _conv_tiles(T, FP)
    main, prev, _, wg, wu, bg, bu = _conv_specs(tr, tc, T, FP)
    rd = Riders(riders)
    n_ex = len(rd.exs)
    ni = T // tr

    def body(*refs):
        in_refs, out_refs, scratch_refs = refs[:6 + n_ex], refs[6 + n_ex:7 + 2 * n_ex], refs[7 + 2 * n_ex:]
        rd.split(in_refs, out_refs, scratch_refs)
        a_ref, hp_ref, wg_ref, wu_ref, bg_ref, bu_ref = in_refs[:6]
        act_ref = out_refs[0]
        step = pl.program_id(0) * ni + pl.program_id(1)
        rd.before(step)
        has_prev = pl.program_id(1) > 0
        row = lax.broadcasted_iota(jnp.int32, (tr, tc), 0)

        def conv(s, w_ref, b_ref):
            av = a_ref[s]
            m1, m2 = _causal_taps(av, hp_ref, s, has_prev, row)
            return w_ref[0:1, :] * m2 + w_ref[1:2, :] * m1 + w_ref[2:3, :] * av + b_ref[...]

        gate = conv(0, wg_ref, bg_ref)
        up = conv(1, wu_ref, bu_ref)
        act_ref[...] = ((gate * jax.nn.sigmoid(gate)) * up).astype(bf16)
        rd.after(step, (FP // tc) * ni)

    outs = pl.pallas_call(
        body, name=name, grid=(FP // tc, ni),
        in_specs=[main, prev, wg, wu, bg, bu] + rd.in_specs,
        out_specs=[pl.BlockSpec((tr, tc), lambda j, i: (i, j))] + rd.out_specs,
        out_shape=[jax.ShapeDtypeStruct((T, FP), bf16)] + rd.out_shape,
        scratch_shapes=rd.scratch,
        compiler_params=_cp(("arbitrary", "arbitrary") if rd.exs else ("parallel", "parallel")),
    )(a3, a3, conv_w, conv_w, conv_b, conv_b, *rd.args)
    return outs if rd.exs else outs[0]


def conv_glu_bwd(a3, conv_w, conv_b, dact, name, riders=None):
    _, T, FP = a3.shape
    tr, tc = _conv_tiles(T, FP)
    ni = T // tr
    main, prev, nxt, wg, wu, bg, bu = _conv_specs(tr, tc, T, FP)
    r8 = tr // SUBLANES
    last8 = T // SUBLANES - 1
    d_main = pl.BlockSpec((tr, tc), lambda j, i: (i, j))
    d_next = pl.BlockSpec((SUBLANES, tc), lambda j, i: (jnp.minimum((i + 1) * r8, last8), j))
    rd = Riders(riders)
    n_ex = len(rd.exs)

    def body(*refs):
        in_refs, out_refs, scratch_refs = refs[:9 + n_ex], refs[9 + n_ex:14 + 2 * n_ex], refs[14 + 2 * n_ex:]
        rd.split(in_refs, out_refs, scratch_refs)
        a_ref, hp_ref, hn_ref, d_ref, dn_ref, wg_ref, wu_ref, bg_ref, bu_ref = in_refs[:9]
        da_ref, dwg_ref, dwu_ref, dbg_ref, dbu_ref = out_refs[:5]
        i = pl.program_id(1)
        step = pl.program_id(0) * ni + i
        rd.before(step)
        has_prev = i > 0
        has_next = i < ni - 1
        row = lax.broadcasted_iota(jnp.int32, (tr, tc), 0)
        row8 = lax.broadcasted_iota(jnp.int32, (SUBLANES, tc), 0)

        @pl.when(i == 0)
        def _():
            dwg_ref[...] = jnp.zeros_like(dwg_ref)
            dwu_ref[...] = jnp.zeros_like(dwu_ref)
            dbg_ref[...] = jnp.zeros_like(dbg_ref)
            dbu_ref[...] = jnp.zeros_like(dbu_ref)

        def prep(s, w_ref, b_ref):
            av = a_ref[s]
            m1, m2 = _causal_taps(av, hp_ref, s, has_prev, row)
            w0, w1, w2, bv = w_ref[0:1, :], w_ref[1:2, :], w_ref[2:3, :], b_ref[...]
            pre = w0 * m2 + w1 * m1 + w2 * av + bv
            an = hn_ref[s]
            l1 = a_ref[s, tr - 1:tr, :]
            l2 = a_ref[s, tr - 2:tr - 1, :]
            n1 = jnp.where(row8 == 0, l1, pltpu.roll(an, 1, 0))
            n2 = jnp.where(row8 == 0, l2, jnp.where(row8 == 1, l1, pltpu.roll(an, 2, 0)))
            pre_n = w0 * n2 + w1 * n1 + w2 * an + bv
            return av, m1, m2, pre, pre_n

        def glu_bwd(gate, up, d):
            sg = jax.nn.sigmoid(gate)
            dgate = d * up * (sg * (1.0 + gate * (1.0 - sg)))
            dup = d * (gate * sg)
            return dgate, dup

        def row_of(v8, r):
            return jnp.sum(jnp.where(row8 == r, v8, 0.0), axis=0, keepdims=True)

        def back(dc, dc_n, w_ref):
            n0, n1 = row_of(dc_n, 0), row_of(dc_n, 1)
            p1 = jnp.where(row == tr - 1, n0, pltpu.roll(dc, tr - 1, 0))
            p2 = jnp.where(row == tr - 1, n1, jnp.where(row == tr - 2, n0, pltpu.roll(dc, tr - 2, 0)))
            return w_ref[2:3, :] * dc + w_ref[1:2, :] * p1 + w_ref[0:1, :] * p2

        def tok_sum(v):
            return jnp.sum(v, axis=0, keepdims=True)

        ag, g1, g2, gate, gate_n = prep(0, wg_ref, bg_ref)
        au, u1, u2, up, up_n = prep(1, wu_ref, bu_ref)
        dcg, dcu = glu_bwd(gate, up, d_ref[...])
        dn = jnp.where(has_next, dn_ref[...], 0.0)
        dcg_n, dcu_n = glu_bwd(gate_n, up_n, dn)
        da_ref[0] = back(dcg, dcg_n, wg_ref).astype(bf16)
        da_ref[1] = back(dcu, dcu_n, wu_ref).astype(bf16)
        dwg_ref[0:1, :] += tok_sum(dcg * g2)
        dwg_ref[1:2, :] += tok_sum(dcg * g1)
        dwg_ref[2:3, :] += tok_sum(dcg * ag)
        dwu_ref[0:1, :] += tok_sum(dcu * u2)
        dwu_ref[1:2, :] += tok_sum(dcu * u1)
        dwu_ref[2:3, :] += tok_sum(dcu * au)
        dbg_ref[...] += tok_sum(dcg)
        dbu_ref[...] += tok_sum(dcu)
        rd.after(step, (FP // tc) * ni)

    w_out = pl.BlockSpec((CONV_W, tc), lambda j, i: (0, j))
    b_out = pl.BlockSpec((1, tc), lambda j, i: (0, j))
    return pl.pallas_call(
        body, name=name, grid=(FP // tc, ni),
        in_specs=[main, prev, nxt, d_main, d_next, wg, wu, bg, bu] + rd.in_specs,
        out_specs=[main, w_out, w_out, b_out, b_out] + rd.out_specs,
        out_shape=[jax.ShapeDtypeStruct((2, T, FP), bf16),
                   jax.ShapeDtypeStruct((CONV_W, FP), f32), jax.ShapeDtypeStruct((CONV_W, FP), f32),
                   jax.ShapeDtypeStruct((1, FP), f32), jax.ShapeDtypeStruct((1, FP), f32)] + rd.out_shape,
        scratch_shapes=rd.scratch,
        compiler_params=_cp(("arbitrary", "arbitrary") if rd.exs else ("parallel", "arbitrary")),
    )(a3, a3, a3, dact, dact, conv_w, conv_w, conv_b, conv_b, *rd.args)


_GELU_C = 0.7978845608028654
_GELU_A = 0.044715


def _gelu(x):
    return 0.5 * x * (1.0 + jnp.tanh(_GELU_C * (x + _GELU_A * (x * x * x))))


def _gelu_and_grad(x):
    t = jnp.tanh(_GELU_C * (x + _GELU_A * (x * x * x)))
    g = 0.5 * x * (1.0 + t)
    dg = 0.5 * (1.0 + t) + 0.5 * x * (1.0 - t * t) * (_GELU_C * (1.0 + 3.0 * _GELU_A * (x * x)))
    return g, dg


def _tril_bf16(w):
    r = lax.broadcasted_iota(jnp.int32, w.shape, 0)
    c = lax.broadcasted_iota(jnp.int32, w.shape, 1)
    return jnp.where(r >= c, w, 0.0).astype(bf16)


def gm_gate_fwd(z, vg, ws, bs_t, name):
    T, D2 = z.shape
    D = D2 // 2
    G = D // GM_GROUP
    tr = _pick(T, (256, 128))
    nc = tr // CHUNK

    def body(z_ref, vg_ref, ws_ref, bs_ref, o_ref):
        u = _gelu(z_ref[:, :D])
        v = _gelu(z_ref[:, D:])
        rv = lax.rsqrt(jnp.mean(v * v, axis=-1, keepdims=True) + EPS)
        vn = ((v * rv) * vg_ref[...]).astype(bf16)
        for g in range(G):
            wg = _tril_bf16(ws_ref[g])
            bg = bs_ref[:, g:g + 1]
            cs = slice(g * GM_GROUP, (g + 1) * GM_GROUP)
            for c in range(nc):
                rs = slice(c * CHUNK, (c + 1) * CHUNK)
                sv = jnp.dot(wg, vn[rs, cs], preferred_element_type=f32) + bg
                o_ref[rs, cs] = (u[rs, cs] * sv).astype(bf16)

    return pl.pallas_call(
        body, name=name, grid=(T // tr,),
        in_specs=[pl.BlockSpec((tr, D2), lambda i: (i, 0)),
                  pl.BlockSpec((1, D), lambda i: (0, 0)),
                  pl.BlockSpec((G, CHUNK, CHUNK), lambda i: (0, 0, 0)),
                  pl.BlockSpec((CHUNK, G), lambda i: (0, 0))],
        out_specs=pl.BlockSpec((tr, D), lambda i: (i, 0)),
        out_shape=jax.ShapeDtypeStruct((T, D), bf16),
        compiler_params=_cp(("parallel",)),
    )(z, vg, ws, bs_t)


def gm_gate_bwd(z, vg, ws, bs_t, dgated, name):
    T, D2 = z.shape
    D = D2 // 2
    G = D // GM_GROUP
    tr = _pick(T, (256, 128))
    nc = tr // CHUNK

    def body(z_ref, vg_ref, ws_ref, bs_ref, dg_ref, dz_ref, dws_ref, dbs_ref, dvg_ref,
             du_s, dvn_s):
        @pl.when(pl.program_id(0) == 0)
        def _():
            dws_ref[...] = jnp.zeros_like(dws_ref)
            dbs_ref[...] = jnp.zeros_like(dbs_ref)
            dvg_ref[...] = jnp.zeros_like(dvg_ref)

        u, du_dz = _gelu_and_grad(z_ref[:, :D])
        v, dv_dz = _gelu_and_grad(z_ref[:, D:])
        rv = lax.rsqrt(jnp.mean(v * v, axis=-1, keepdims=True) + EPS)
        vhat = v * rv
        vg_v = vg_ref[...]
        vn = (vhat * vg_v).astype(bf16)
        rr = lax.broadcasted_iota(jnp.int32, (CHUNK, CHUNK), 0)
        cc = lax.broadcasted_iota(jnp.int32, (CHUNK, CHUNK), 1)
        for g in range(G):
            wg = _tril_bf16(ws_ref[g])
            bg = bs_ref[:, g:g + 1]
            cs = slice(g * GM_GROUP, (g + 1) * GM_GROUP)
            dw_acc = jnp.zeros((CHUNK, CHUNK), f32)
            db_acc = jnp.zeros((CHUNK, 1), f32)
            for c in range(nc):
                rs = slice(c * CHUNK, (c + 1) * CHUNK)
                vb = vn[rs, cs]
                sv = jnp.dot(wg, vb, preferred_element_type=f32) + bg
                dgb = dg_ref[rs, cs]
                du_s[rs, cs] = dgb * sv
                dsv = dgb * u[rs, cs]
                dsv_b = dsv.astype(bf16)
                dw_acc += lax.dot_general(dsv_b, vb, (((1,), (1,)), ((), ())),
                                          preferred_element_type=f32)
                db_acc += jnp.sum(dsv, axis=1, keepdims=True)
                dvn_s[rs, cs] = lax.dot_general(wg, dsv_b, (((0,), (0,)), ((), ())),
                                                preferred_element_type=f32)
            dws_ref[g] += jnp.where(rr >= cc, dw_acc, 0.0)
            dbs_ref[:, g:g + 1] += db_acc
        dz_ref[:, :D] = (du_s[...] * du_dz).astype(bf16)
        dvn = dvn_s[...]
        dvg_ref[...] += jnp.sum(dvn * vhat, axis=0, keepdims=True)
        dvh = dvn * vg_v
        dv = rv * (dvh - vhat * jnp.mean(dvh * vhat, axis=-1, keepdims=True))
        dz_ref[:, D:] = (dv * dv_dz).astype(bf16)

    return pl.pallas_call(
        body, name=name, grid=(T // tr,),
        in_specs=[pl.BlockSpec((tr, D2), lambda i: (i, 0)),
                  pl.BlockSpec((1, D), lambda i: (0, 0)),
                  pl.BlockSpec((G, CHUNK, CHUNK), lambda i: (0, 0, 0)),
                  pl.BlockSpec((CHUNK, G), lambda i: (0, 0)),
                  pl.BlockSpec((tr, D), lambda i: (i, 0))],
        out_specs=[pl.BlockSpec((tr, D2), lambda i: (i, 0)),
                   pl.BlockSpec((G, CHUNK, CHUNK), lambda i: (0, 0, 0)),
                   pl.BlockSpec((CHUNK, G), lambda i: (0, 0)),
                   pl.BlockSpec((1, D), lambda i: (0, 0))],
        out_shape=[jax.ShapeDtypeStruct((T, D2), bf16),
                   jax.ShapeDtypeStruct((G, CHUNK, CHUNK), f32),
                   jax.ShapeDtypeStruct((CHUNK, G), f32),
                   jax.ShapeDtypeStruct((1, D), f32)],
        scratch_shapes=[pltpu.VMEM((tr, D), f32), pltpu.VMEM((tr, D), f32)],
        compiler_params=_cp(("arbitrary",)),
    )(z, vg, ws, bs_t, dgated)


def fox_gates_fwd(flog_t, b_col, name):
    H, T = flog_t.shape

    def body(fl_ref, b_ref, o_ref):
        xv = fl_ref[...] + b_ref[...]
        lf = jnp.minimum(xv, 0.0) - jnp.log1p(jnp.exp(-jnp.abs(xv)))
        lane = lax.broadcasted_iota(jnp.int32, (H, T), 1)
        s = 1
        while s < T:
            lf = lf + jnp.where(lane >= s, pltpu.roll(lf, s, 1), 0.0)
            s *= 2
        o_ref[...] = lf * LOG2E

    return pl.pallas_call(
        body, name=name, out_shape=jax.ShapeDtypeStruct((H, T), f32),
        compiler_params=pltpu.CompilerParams(vmem_limit_bytes=VMEM_LIMIT_BYTES),
    )(flog_t, b_col)


def fox_gates_bwd(flog_t, b_col, dF, name):
    H, T = flog_t.shape

    def body(fl_ref, b_ref, d_ref, o_ref, db_ref):
        xv = fl_ref[...] + b_ref[...]
        g = d_ref[...]
        lane = lax.broadcasted_iota(jnp.int32, (H, T), 1)
        s = 1
        while s < T:
            g = g + jnp.where(lane < T - s, pltpu.roll(g, T - s, 1), 0.0)
            s *= 2
        dfl = g * jax.nn.sigmoid(-xv)
        o_ref[...] = dfl
        db_ref[...] = jnp.sum(dfl, axis=1, keepdims=True)

    return pl.pallas_call(
        body, name=name,
        out_shape=[jax.ShapeDtypeStruct((H, T), f32), jax.ShapeDtypeStruct((H, 1), f32)],
        compiler_params=pltpu.CompilerParams(vmem_limit_bytes=VMEM_LIMIT_BYTES),
    )(flog_t, b_col, dF)


_NT = (((1,), (1,)), ((), ()))
_TN = (((0,), (0,)), ((), ()))


def _scores(q, k, fq, fk, col0=None):
    s = lax.dot_general(q, k, _NT, preferred_element_type=f32) * (HEAD_DIM ** -0.5 * LOG2E)
    s = s + fq - fk
    if col0 is not None:
        rows = lax.broadcasted_iota(jnp.int32, s.shape, 0)
        cols = col0 + lax.broadcasted_iota(jnp.int32, s.shape, 1)
        s = jnp.where(cols <= rows, s, NEG)
    return s


def fox_attn_fwd(qkv, f_row, f_col, name, riders=None):
    T, D3 = qkv.shape
    D = D3 // 3
    H = D // HEAD_DIM
    tq = _pick(T, (ATT_Q_TILE, 512, 256))
    kq = ATT_KEYS_PER_Q
    tk = tq // kq
    nq = T // tq
    rd = Riders(riders)
    pairs = [(i, j) for i in range(nq) for j in range(kq * i + kq)]
    i_tab = np.array([p[0] for p in pairs], np.int32)
    j_tab = np.array([p[1] for p in pairs], np.int32)

    def body(i_ref, j_ref, *refs):
        in_refs, out_refs, scratch_refs = refs[:5 + len(rd.exs)], refs[5 + len(rd.exs):7 + 2 * len(rd.exs)], refs[7 + 2 * len(rd.exs):]
        rd.split(in_refs, out_refs, scratch_refs)
        q_ref, k_ref, v_ref, fq_ref, fk_ref = in_refs[:5]
        o_ref, lse_ref = out_refs[:2]
        m_s, l_s, acc_s = scratch_refs[:3]
        t = pl.program_id(1)
        i, j = i_ref[t], j_ref[t]
        step = pl.program_id(0) * len(pairs) + t
        rd.before(step)

        @pl.when(j == 0)
        def _():
            m_s[...] = jnp.full_like(m_s, NEG)
            l_s[...] = jnp.zeros_like(l_s)
            acc_s[...] = jnp.zeros_like(acc_s)

        def update(col0):
            s = _scores(q_ref[...], k_ref[...], fq_ref[0], fk_ref[0], col0)
            m_prev = m_s[...]
            m_new = jnp.maximum(m_prev, jnp.max(s, axis=1, keepdims=True))
            alpha = jnp.exp2(m_prev - m_new)
            p = jnp.exp2(s - m_new)
            l_s[...] = alpha * l_s[...] + jnp.sum(p, axis=1, keepdims=True)
            acc_s[...] = alpha * acc_s[...] + jnp.dot(p.astype(bf16), v_ref[...],
                                                      preferred_element_type=f32)
            m_s[...] = m_new

        @pl.when(j < kq * i)
        def _():
            update(None)

        @pl.when(j >= kq * i)
        def _():
            update((j - kq * i) * tk)

        @pl.when(j == kq * i + kq - 1)
        def _():
            o_ref[...] = (acc_s[...] / l_s[...]).astype(bf16)
            lse_ref[0] = m_s[...] + jnp.log2(l_s[...])

        rd.after(step, H * len(pairs))

    blk = (tq, HEAD_DIM)
    kblk = (tk, HEAD_DIM)
    grid_spec = pltpu.PrefetchScalarGridSpec(
        num_scalar_prefetch=2, grid=(H, len(pairs)),
        in_specs=[pl.BlockSpec(blk, lambda h, t, it, jt: (it[t], h)),
                  pl.BlockSpec(kblk, lambda h, t, it, jt: (jt[t], H + h)),
                  pl.BlockSpec(kblk, lambda h, t, it, jt: (jt[t], 2 * H + h)),
                  pl.BlockSpec((1, tq, 1), lambda h, t, it, jt: (h, it[t], 0)),
                  pl.BlockSpec((1, 1, tk), lambda h, t, it, jt: (h, 0, jt[t]))] + rd.in_specs,
        out_specs=[pl.BlockSpec(blk, lambda h, t, it, jt: (it[t], h)),
                   pl.BlockSpec((1, tq, 1), lambda h, t, it, jt: (h, it[t], 0))] + rd.out_specs,
        scratch_shapes=[pltpu.VMEM((tq, 1), f32), pltpu.VMEM((tq, 1), f32),
                        pltpu.VMEM((tq, HEAD_DIM), f32)] + rd.scratch)
    return pl.pallas_call(
        body, name=name, grid_spec=grid_spec,
        out_shape=[jax.ShapeDtypeStruct((T, D), bf16), jax.ShapeDtypeStruct((H, T, 1), f32)] + rd.out_shape,
        compiler_params=_cp(("arbitrary", "arbitrary") if rd.exs else ("parallel", "arbitrary")),
    )(i_tab, j_tab, qkv, qkv, qkv, f_col, f_row, *rd.args)


def fox_attn_bwd(qkv, o, do, lse, f_row, f_col, name, riders=None):
    T, D3 = qkv.shape
    D = D3 // 3
    H = D // HEAD_DIM
    tq = _pick(T, (ATT_Q_TILE, 512, 256))
    kq = ATT_KEYS_PER_Q
    tk = tq // kq
    nq, nk = T // tq, T // tk
    scale = HEAD_DIM ** -0.5

    pairs = [(j, i) for j in range(nk) for i in range(j // kq, nq)]
    j_tab = np.array([p[0] for p in pairs], np.int32)
    i_tab = np.array([p[1] for p in pairs], np.int32)

    rd = Riders(riders)
    n_ex = len(rd.exs)

    def body(j_ref, i_ref, *refs):
        in_refs, out_refs, scratch_refs = refs[:8 + n_ex], refs[8 + n_ex:13 + 2 * n_ex], refs[13 + 2 * n_ex:]
        rd.split(in_refs, out_refs, scratch_refs)
        q_ref, k_ref, v_ref, o_ref, do_ref, lse_ref, fq_ref, fk_ref = in_refs[:8]
        dq_ref, dk_ref, dv_ref, cs_ref, rs_ref = out_refs[:5]
        dk_s, dv_s, dq_s, di_s = scratch_refs[:4]
        step = pl.program_id(1)
        rd.before(pl.program_id(0) * len(pairs) + step)
        j, i = j_ref[step], i_ref[step]
        rows = pl.ds(pl.multiple_of(i * tq, tq), tq)
        first_i = j // kq

        @pl.when(step == 0)
        def _():
            dq_s[...] = jnp.zeros_like(dq_s)
            rs_ref[...] = jnp.zeros_like(rs_ref)

        @pl.when(j == 0)
        def _():
            di_s[rows, :] = jnp.sum(do_ref[...] * o_ref[...].astype(f32), axis=1, keepdims=True)

        @pl.when(i == first_i)
        def _():
            dk_s[...] = jnp.zeros_like(dk_s)
            dv_s[...] = jnp.zeros_like(dv_s)
            cs_ref[...] = jnp.zeros_like(cs_ref)

        def accumulate(col0):
            q = q_ref[...]
            k = k_ref[...]
            s = _scores(q, k, fq_ref[0], fk_ref[0], col0)
            p = jnp.exp2(s - lse_ref[0])
            do_b = do_ref[...].astype(bf16)
            dp = lax.dot_general(do_b, v_ref[...], _NT, preferred_element_type=f32)
            ds = p * (dp - di_s[rows, :])
            ds_b = (ds * scale).astype(bf16)
            cs_ref[0] += jnp.sum(ds, axis=0, keepdims=True)
            rs_ref[0, rows, :] += jnp.sum(ds, axis=1, keepdims=True)
            dv_s[...] += lax.dot_general(p.astype(bf16), do_b, _TN, preferred_element_type=f32)
            dk_s[...] += lax.dot_general(ds_b, q, _TN, preferred_element_type=f32)
            dq_s[rows, :] += jnp.dot(ds_b, k, preferred_element_type=f32)

        @pl.when(i == first_i)
        def _():
            accumulate((j - kq * i) * tk)

        @pl.when(i > first_i)
        def _():
            accumulate(None)

        @pl.when(i == nq - 1)
        def _():
            dk_ref[...] = dk_s[...].astype(bf16)
            dv_ref[...] = dv_s[...].astype(bf16)

        @pl.when(step == len(pairs) - 1)
        def _():
            dq_ref[...] = dq_s[...].astype(bf16)

        rd.after(pl.program_id(0) * len(pairs) + step, H * len(pairs))

    blk = (tq, HEAD_DIM)
    kblk = (tk, HEAD_DIM)
    at_q = lambda h, s, jt, it: (it[s], h)
    col_q = pl.BlockSpec((1, tq, 1), lambda h, s, jt, it: (h, it[s], 0))
    row_k = pl.BlockSpec((1, 1, tk), lambda h, s, jt, it: (h, 0, jt[s]))
    grid_spec = pltpu.PrefetchScalarGridSpec(
        num_scalar_prefetch=2, grid=(H, len(pairs)),
        in_specs=[pl.BlockSpec(blk, at_q),
                  pl.BlockSpec(kblk, lambda h, s, jt, it: (jt[s], H + h)),
                  pl.BlockSpec(kblk, lambda h, s, jt, it: (jt[s], 2 * H + h)),
                  pl.BlockSpec(blk, at_q), pl.BlockSpec(blk, at_q), col_q, col_q, row_k] + rd.in_specs,
        out_specs=[pl.BlockSpec((T, HEAD_DIM), lambda h, s, jt, it: (0, h)),
                   pl.BlockSpec(kblk, lambda h, s, jt, it: (jt[s], h)),
                   pl.BlockSpec(kblk, lambda h, s, jt, it: (jt[s], h)),
                   row_k,
                   pl.BlockSpec((1, T, 1), lambda h, s, jt, it: (h, 0, 0))] + rd.out_specs,
        scratch_shapes=[pltpu.VMEM(kblk, f32), pltpu.VMEM(kblk, f32),
                        pltpu.VMEM((T, HEAD_DIM), f32), pltpu.VMEM((T, 1), f32)] + rd.scratch)
    return pl.pallas_call(
        body, name=name, grid_spec=grid_spec,
        out_shape=[jax.ShapeDtypeStruct((T, D), bf16), jax.ShapeDtypeStruct((T, D), bf16),
                   jax.ShapeDtypeStruct((T, D), bf16), jax.ShapeDtypeStruct((H, 1, T), f32),
                   jax.ShapeDtypeStruct((H, T, 1), f32)] + rd.out_shape,
        compiler_params=_cp(("arbitrary", "arbitrary") if rd.exs else ("parallel", "arbitrary")),
    )(j_tab, i_tab, qkv, qkv, qkv, o, do, lse, f_col, f_row, *rd.args)


def mod_fwd(c16, mod_w, mod_b_loc, name):
    L, D, MW = mod_w.shape
    tn = _pick(MW, (512, 256, 128))

    def body(c_ref, w_ref, b_ref, o_ref):
        cv = c_ref[...]
        ca = (cv * jax.nn.sigmoid(cv)).astype(bf16)
        o_ref[...] = jnp.dot(ca, w_ref[...].astype(bf16), preferred_element_type=f32) + b_ref[...]

    return pl.pallas_call(
        body, name=name, grid=(L, MW // tn),
        in_specs=[pl.BlockSpec((16, D), lambda l, j: (0, 0)),
                  pl.BlockSpec((None, D, tn), lambda l, j: (l, 0, j)),
                  pl.BlockSpec((None, 1, tn), lambda l, j: (l, 0, j))],
        out_specs=pl.BlockSpec((None, 16, tn), lambda l, j: (l, 0, j)),
        out_shape=jax.ShapeDtypeStruct((L, 16, MW), f32),
        compiler_params=_cp(("parallel", "parallel")),
    )(c16, mod_w, mod_b_loc)


def mod_w_bwd(c_t, dmod, name):
    D = c_t.shape[0]
    L, _, MW = dmod.shape
    tn = _pick(MW, (512, 256, 128))

    def body(c_ref, d_ref, o_ref):
        cv = c_ref[...]
        ca = (cv * jax.nn.sigmoid(cv)).astype(bf16)
        o_ref[...] = jnp.dot(ca, d_ref[...].astype(bf16), preferred_element_type=f32)

    return pl.pallas_call(
        body, name=name, grid=(L, MW // tn),
        in_specs=[pl.BlockSpec((D, LANES), lambda l, j: (0, 0)),
                  pl.BlockSpec((None, LANES, tn), lambda l, j: (l, 0, j))],
        out_specs=pl.BlockSpec((None, D, tn), lambda l, j: (l, 0, j)),
        out_shape=jax.ShapeDtypeStruct((L, D, MW), f32),
        compiler_params=_cp(("parallel", "parallel")),
    )(c_t, dmod)


def adamw(w, g, m, v, name, riders=None):
    shape = w.shape
    C = shape[-1] if w.ndim >= 1 else 1
    R = max(w.size // C, 1)
    w2, g2, m2, v2 = (t.reshape(R, C) for t in (w, g, m, v))
    tr = R
    for cand in (2048, 1024, 512, 256, 128, 64, 32, 16, 8):
        if R % cand == 0 and cand * _round_up(C, LANES) <= 256 * 1024:
            tr = cand
            break
    rd = Riders(riders)
    n_ex = len(rd.exs)

    def body(*refs):
        in_refs, out_refs, scratch_refs = refs[:4 + n_ex], refs[4 + n_ex:7 + 2 * n_ex], refs[7 + 2 * n_ex:]
        rd.split(in_refs, out_refs, scratch_refs)
        w_ref, g_ref, m_ref, v_ref = in_refs[:4]
        d_ref, mo_ref, vo_ref = out_refs[:3]
        rd.before(pl.program_id(0))
        gv = g_ref[...]
        mn = ADAM_B1 * m_ref[...] + (1.0 - ADAM_B1) * gv
        vn = ADAM_B2 * v_ref[...] + (1.0 - ADAM_B2) * (gv * gv)
        m_hat = mn / (1.0 - ADAM_B1 ** ADAM_STEP)
        v_hat = vn / (1.0 - ADAM_B2 ** ADAM_STEP)
        d_ref[...] = -ADAM_LR * (m_hat / (jnp.sqrt(v_hat) + ADAM_EPS) + ADAM_WD * w_ref[...])
        mo_ref[...] = mn
        vo_ref[...] = vn
        rd.after(pl.program_id(0), R // tr)

    spec = pl.BlockSpec((tr, C), lambda i: (i, 0))
    sds = jax.ShapeDtypeStruct((R, C), f32)
    outs = pl.pallas_call(
        body, name=name, grid=(R // tr,), in_specs=[spec] * 4 + rd.in_specs,
        out_specs=[spec] * 3 + rd.out_specs, out_shape=[sds, sds, sds] + rd.out_shape,
        scratch_shapes=rd.scratch,
        compiler_params=_cp(("arbitrary",) if rd.exs else ("parallel",)),
    )(w2, g2, m2, v2, *rd.args)
    return tuple(t.reshape(shape) for t in outs[:3]) + tuple(outs[3:])


def reduce_scatter_tail(pair, tag):
    quad = run_exchange(chip_exchange(pair), "rs_chip_exchange_" + tag)
    return sum_slots(quad, "rs_final_sum_" + tag)


def kernel(x, c, mod_w, mod_b, mix_norm_g, ffn_norm_g, attn_w_in, attn_b_f, attn_w_o, gm_w_in, gm_v_g, gm_w_s, gm_b_s, gm_w_o, ffn_w_in, ffn_conv_w, ffn_conv_b, ffn_w_out, final_g, loss_target, m_mod_w, m_mod_b, m_mix_norm_g, m_ffn_norm_g, m_attn_w_in, m_attn_b_f, m_attn_w_o, m_gm_w_in, m_gm_v_g, m_gm_w_s, m_gm_b_s, m_gm_w_o, m_ffn_w_in, m_ffn_conv_w, m_ffn_conv_b, m_ffn_w_out, m_final_g, v_mod_w, v_mod_b, v_mix_norm_g, v_ffn_norm_g, v_attn_w_in, v_attn_b_f, v_attn_w_o, v_gm_w_in, v_gm_v_g, v_gm_w_s, v_gm_b_s, v_gm_w_o, v_ffn_w_in, v_ffn_conv_w, v_ffn_conv_b, v_ffn_w_out, v_final_g):
    xi, yi, ci = lax.axis_index("x"), lax.axis_index("y"), lax.axis_index("c")
    me = 4 * xi + 2 * yi + ci

    _, T, D = x.shape
    L = mod_w.shape[0]
    MW = mod_w.shape[2]
    NA, _, QW = attn_w_in.shape
    NB = gm_w_in.shape[0]
    H = D // HEAD_DIM
    G = D // GM_GROUP
    DR = attn_w_o.shape[1]
    GW = gm_w_in.shape[2]
    FW = ffn_w_in.shape[2]
    FR = ffn_w_out.shape[1]
    FRP = _round_up(FR, LANES // 2)
    FWP = 2 * FRP
    FP = N_CHIPS * FWP
    DFF2 = N_DEV * FW
    assert 2 * FR == FW and N_DEV * QW == 3 * D + H and N_DEV * GW == 2 * D
    c_idx = ci.reshape(1).astype(jnp.int32)

    def pad_ff(t, axis, blocks):
        ax = axis % t.ndim
        t = t.reshape(t.shape[:ax] + (blocks, FR) + t.shape[ax + 1:])
        pad = [(0, 0)] * t.ndim
        pad[ax + 1] = (0, FRP - FR)
        t = jnp.pad(t, pad)
        return t.reshape(t.shape[:ax] + (blocks * FRP,) + t.shape[ax + 2:])

    def unpad_ff(t, axis, blocks):
        ax = axis % t.ndim
        t = t.reshape(t.shape[:ax] + (blocks, FRP) + t.shape[ax + 1:])
        t = lax.slice_in_dim(t, 0, FR, axis=ax + 1)
        return t.reshape(t.shape[:ax] + (blocks * FR,) + t.shape[ax + 2:])

    x0 = x[0]
    tgt = loss_target[0]

    c_all = all_gather(c, "gather_c").reshape(N_DEV, D)
    cw_loc = pad_ff(ffn_conv_w, 2, 2).reshape(L * CONV_W, FWP)
    conv_w_full = all_gather(cw_loc, "gather_conv_w").transpose(1, 0, 2).reshape(L, CONV_W, 2 * FP)
    vg_full = all_gather(gm_v_g, "gather_vg").transpose(1, 0, 2).reshape(NB, 1, D)
    conv_b_full = pad_ff(ffn_conv_b, 1, 2 * N_DEV).reshape(L, 1, 2 * FP)

    c16 = jnp.pad(c_all, ((0, 16 - N_DEV), (0, 0)))
    mod_b_loc = lax.dynamic_slice_in_dim(mod_b, me * MW, MW, axis=1).reshape(L, 1, MW)
    mod_part = mod_fwd(c16, mod_w, mod_b_loc, "mod_fwd")[:, :N_DEV]
    mod_all = all_gather(mod_part, "gather_mod")
    mod_me = lax.dynamic_index_in_dim(mod_all, me, axis=2, keepdims=False)
    mod_me = mod_me.transpose(1, 0, 2).reshape(L, 6, 1, D)

    w_ai_t = jnp.swapaxes(attn_w_in, 1, 2).astype(bf16)
    w_gi_t = jnp.swapaxes(gm_w_in, 1, 2).astype(bf16)
    w_fi_t = pad_ff(jnp.swapaxes(ffn_w_in, 1, 2).astype(bf16), 1, 2)
    w_ao_l = attn_w_o.astype(bf16)
    w_go_l = gm_w_o.astype(bf16)
    w_fo_l = jnp.pad(ffn_w_out.astype(bf16), ((0, 0), (0, FRP - FR), (0, 0)))

    stash = []
    arrived = {}
    xc = x0

    def shard(kind, layer):
        even = layer % 2 == 0
        return {"ffn_in": w_fi_t, "ffn_out": w_fo_l, "mix_in": w_ai_t if even else w_gi_t,
                "mix_out": w_ao_l if even else w_go_l}[kind][layer if kind.startswith("ffn") else layer // 2]

    def need(kind, layer):
        if (kind, layer) not in arrived:
            arrived[(kind, layer)] = all_gather(shard(kind, layer), "gather_" + kind)
        return arrived.pop((kind, layer))

    def hosting(keys, n_own, fn, *args, **kw):
        keys = [k for k in keys if k[1] < L]
        outs = fn(*args, riders=[gather_exchange(shard(*k)) for k in keys], **kw)
        outs = list(outs) if isinstance(outs, (list, tuple)) else [outs]
        arrived.update(zip(keys, outs[n_own:]))
        return outs[:n_own]

    for i in range(L):
        sh1, sc1, g1, sh2, sc2, g2 = (mod_me[i, k] for k in range(6))
        jm = i // 2
        st = {"x_in": xc}
        h = norm_mod_fwd(xc, mix_norm_g[i][None], sc1, sh1, "norm_mod_fwd")
        st["h"] = h
        w_mi = need("mix_in", i)
        w_mo = need("mix_out", i).reshape(D, D)
        if i % 2 == 0:
            w_in_t = w_mi.reshape(N_DEV * QW, D)
            w_qkv_t = w_in_t[:3 * D]
            w_f_t = jnp.pad(w_in_t[3 * D:], ((0, LANES - H), (0, 0)))
            qkv = matmul(h, w_qkv_t, name="fox_qkv", tb=True, out_dtype=bf16)
            flog = matmul(h, w_f_t, name="fox_flog", tb=True)
            flog_t = flog[:, :H].T
            b_col = attn_b_f[jm][:, None]
            F = fox_gates_fwd(flog_t, b_col, "fox_gates_fwd")
            f_row, f_col = F[:, None, :], F[:, :, None]
            o, lse = hosting([("ffn_in", i), ("ffn_in", i + 1)], 2,
                             fox_attn_fwd, qkv, f_row, f_col, "fox_attn_fwd")
            x1, y = matmul(o, w_mo, name="mix_out", resid=xc, gvec=g1, emit_acc=True)
            st.update(qkv=qkv, flog_t=flog_t, b_col=b_col, f_row=f_row, f_col=f_col, o=o, lse=lse,
                      w_qkv_t=w_qkv_t, w_f_t=w_f_t, w_mo=w_mo)
        else:
            w_gi_full = w_mi.reshape(2 * D, D)
            z = matmul(h, w_gi_full, name="gm_in", tb=True)
            bs_t = gm_b_s[jm].T
            gated = gm_gate_fwd(z, vg_full[jm], gm_w_s[jm], bs_t, "gm_gate_fwd")
            x1, y = matmul(gated, w_mo, name="mix_out", resid=xc, gvec=g1, emit_acc=True)
            st.update(z=z, bs_t=bs_t, gated=gated, w_gi_full=w_gi_full, w_mo=w_mo)
        st.update(y=y, x1=x1)
        h2 = norm_mod_fwd(x1, ffn_norm_g[i][None], sc2, sh2, "norm_mod_fwd")
        w_fi_full = need("ffn_in", i).reshape(2 * FP, D)
        fox = i % 2 == 0
        a3, = hosting([("ffn_out", i), ("mix_out", i + 1)] if fox else [("mix_in", i + 1)], 1,
                      matmul, h2, w_fi_full, name="ffn_up", tb=True, out_split=True)
        act, = hosting([("mix_in", i + 1)] if fox else [], 1,
                       conv_glu_fwd, a3, conv_w_full[i], conv_b_full[i], "conv_glu_fwd")
        w_fo_full = need("ffn_out", i).reshape(FP, D)
        xc, f_out = hosting([("ffn_out", i + 1) if fox else ("mix_out", i + 1)], 2,
                            matmul, act, w_fo_full, name="ffn_down", resid=x1, gvec=g2, emit_acc=True)
        st.update(h2=h2, a3=a3, act=act, f=f_out, w_fi_full=w_fi_full, w_fo_full=w_fo_full)
        stash.append(st)

    loss_part, dx, d_final_g = loss_head(xc, final_g[None], tgt, "loss_head")
    loss = lax.psum(loss_part[0, 0], AXES)

    d_mod = [None] * L
    d_mix_g = [None] * L
    d_ffn_g = [None] * L
    d_conv_w = [None] * L
    d_conv_b = [None] * L
    g_wfi = [None] * L
    g_wfo = [None] * L
    g_wai = [None] * NA
    g_wao = [None] * NA
    d_bf = [None] * NA
    g_wgi = [None] * NB
    g_wgo = [None] * NB
    d_ws = [None] * NB
    d_bs = [None] * NB
    d_vg = [None] * NB
    ffn_pairs = None
    mix_pairs = None

    def finish_mixer(layer, quad_mo, quad_in):
        g_mo = sum_slots(quad_mo, "rs_final_sum_mix_out")
        g_in = sum_slots(quad_in, "rs_final_sum_mix_in").T
        if layer % 2 == 0:
            g_wao[layer // 2], g_wai[layer // 2] = g_mo, g_in
        else:
            g_wgo[layer // 2], g_wgi[layer // 2] = g_mo, g_in

    for i in reversed(range(L)):
        st = stash[i]
        sh1, sc1, g1, sh2, sc2, g2 = (mod_me[i, k] for k in range(6))
        jm = i // 2
        dy, dg2 = gate_bwd(dx, st["f"], g2, "gate_bwd")
        if mix_pairs is None:
            dact = matmul(dy, st["w_fo_full"], name="ffn_down_dx", tb=True)
        else:
            dact, quad_mo = matmul(dy, st["w_fo_full"], name="ffn_down_dx", tb=True,
                                   riders=[chip_exchange(mix_pairs[1])])
        if ffn_pairs is None:
            dw_fo = matmul(st["act"], dy, name="ffn_down_dw", ta=True, out_dtype=bf16)
        else:
            dw_fo, quad = matmul(st["act"], dy, name="ffn_down_dw", ta=True, out_dtype=bf16,
                                 riders=[chip_exchange(ffn_pairs[0])])
            g_wfo[i + 1] = sum_slots(quad, "rs_final_sum_ffn_out")[:FR]
        g8_fo = dw_fo.reshape(N_DEV, FRP, D)
        da3, dwg, dwu, dbg, dbu, got_fo = conv_glu_bwd(st["a3"], conv_w_full[i], conv_b_full[i], dact,
                                                       "conv_glu_bwd", riders=[pair_exchange(g8_fo)])
        pair_fo = pair_sum(g8_fo, got_fo, c_idx, "rs_pair_sum_ffn_out")
        d_conv_w[i] = jnp.concatenate([dwg, dwu], axis=1)
        d_conv_b[i] = jnp.concatenate([dbg, dbu], axis=1)
        if ffn_pairs is None:
            dw_fi_t = matmul(da3, st["h2"], name="ffn_up_dw", ta=True, a_split=True, out_dtype=bf16)
        else:
            dw_fi_t, quad = matmul(da3, st["h2"], name="ffn_up_dw", ta=True, a_split=True, out_dtype=bf16,
                                   riders=[chip_exchange(ffn_pairs[1])])
            g_wfi[i + 1] = unpad_ff(sum_slots(quad, "rs_final_sum_ffn_in"), 0, 2).T
        g8_fi = dw_fi_t.reshape(N_DEV, FWP, D)
        if mix_pairs is None:
            dh2, got_fi = matmul(da3, st["w_fi_full"], name="ffn_up_dx", a_split=True,
                                 riders=[pair_exchange(g8_fi)])
        else:
            dh2, got_fi, quad_in = matmul(da3, st["w_fi_full"], name="ffn_up_dx", a_split=True,
                                          riders=[pair_exchange(g8_fi), chip_exchange(mix_pairs[2])])
            finish_mixer(mix_pairs[0], quad_mo, quad_in)
        ffn_pairs = (pair_fo, pair_sum(g8_fi, got_fi, c_idx, "rs_pair_sum_ffn_in"))
        dx, dsh2, dsc2, d_ffn_g[i] = norm_mod_bwd(st["x1"], ffn_norm_g[i][None], sc2, dh2, dx, "norm_mod_bwd")
        dy, dg1 = gate_bwd(dx, st["y"], g1, "gate_bwd")
        if i % 2 == 0:
            do = matmul(dy, st["w_mo"], name="mix_out_dx", tb=True)
            g8_mo = matmul(st["o"], dy, name="mix_out_dw", ta=True, out_dtype=bf16).reshape(N_DEV, DR, D)
            dq, dk, dv, cs, rs, quad_fo, quad_fi, got_mo = fox_attn_bwd(
                st["qkv"], st["o"], do, st["lse"], st["f_row"], st["f_col"], "fox_attn_bwd",
                riders=[chip_exchange(ffn_pairs[0]), chip_exchange(ffn_pairs[1]), pair_exchange(g8_mo)])
            g_wfo[i] = sum_slots(quad_fo, "rs_final_sum_ffn_out")[:FR]
            g_wfi[i] = unpad_ff(sum_slots(quad_fi, "rs_final_sum_ffn_in"), 0, 2).T
            ffn_pairs = None
            dF = rs[:, :, 0] - cs[:, 0, :]
            dflog_t, d_bf[jm] = fox_gates_bwd(st["flog_t"], st["b_col"], dF, "fox_gates_bwd")
            dflog = jnp.pad(dflog_t.T, ((0, 0), (0, LANES - H))).astype(bf16)
            dqkv = jnp.concatenate([dq, dk, dv], axis=1)
            dw_qkv_t = matmul(dqkv, st["h"], name="fox_qkv_dw", ta=True, out_dtype=bf16)
            dw_f_t = matmul(dflog, st["h"], name="fox_flog_dw", ta=True, out_dtype=bf16)
            g8_in = jnp.concatenate([dw_qkv_t, dw_f_t[:H]], axis=0).reshape(N_DEV, QW, D)
            dh, got_in = matmul(dqkv, st["w_qkv_t"], name="fox_qkv_dx", riders=[pair_exchange(g8_in)])
            dh = matmul(dflog, st["w_f_t"], name="fox_flog_dx", resid=dh)
        else:
            dgated = matmul(dy, st["w_mo"], name="mix_out_dx", tb=True)
            g8_mo = matmul(st["gated"], dy, name="mix_out_dw", ta=True, out_dtype=bf16).reshape(N_DEV, DR, D)
            dz, d_ws[jm], dbs_t, d_vg[jm] = gm_gate_bwd(st["z"], vg_full[jm], gm_w_s[jm], st["bs_t"], dgated, "gm_gate_bwd")
            d_bs[jm] = dbs_t.T
            dw_in_t, got_mo = matmul(dz, st["h"], name="gm_in_dw", ta=True, out_dtype=bf16,
                                     riders=[pair_exchange(g8_mo)])
            g8_in = dw_in_t.reshape(N_DEV, GW, D)
            dh, got_in = matmul(dz, st["w_gi_full"], name="gm_in_dx", riders=[pair_exchange(g8_in)])
        mix_pairs = (i, pair_sum(g8_mo, got_mo, c_idx, "rs_pair_sum_mix_out"),
                     pair_sum(g8_in, got_in, c_idx, "rs_pair_sum_mix_in"))
        dx, dsh1, dsc1, d_mix_g[i] = norm_mod_bwd(st["x_in"], mix_norm_g[i][None], sc1, dh, dx, "norm_mod_bwd")
        d_mod[i] = jnp.concatenate([dsh1, dsc1, dg1, dsh2, dsc2, dg2], axis=0)

    grad_x = dx[None]
    if ffn_pairs is not None:
        g_wfo[0] = reduce_scatter_tail(ffn_pairs[0], "ffn_out")[:FR]
        g_wfi[0] = unpad_ff(reduce_scatter_tail(ffn_pairs[1], "ffn_in"), 0, 2).T

    def gathered_sum(rows, tag, mult=SUBLANES):
        n = rows.shape[0]
        rows = jnp.pad(rows, ((0, _round_up(n, mult) - n), (0, 0)))
        every = all_gather(rows, "gather_small_grads_" + tag)
        return every, sum_slots(every, "sum_small_grads_" + tag)

    rows_d = jnp.concatenate([jnp.concatenate(d_mod, axis=0), jnp.concatenate(d_mix_g, axis=0),
                              jnp.concatenate(d_ffn_g, axis=0), jnp.concatenate(d_vg, axis=0), d_final_g], axis=0)
    every_d, sum_d = gathered_sum(rows_d, "d")
    r0 = L * 6
    grad_mod_b = sum_d[:r0].reshape(L, 6 * D)
    grad_mix_g, grad_ffn_g = sum_d[r0:r0 + L], sum_d[r0 + L:r0 + 2 * L]
    grad_vg_full = sum_d[r0 + 2 * L:r0 + 2 * L + NB]
    grad_final_g = sum_d[r0 + 2 * L + NB]
    grad_vg = lax.dynamic_slice_in_dim(grad_vg_full, me * DR, DR, axis=1)

    rows_f = jnp.concatenate([jnp.concatenate(d_conv_w, axis=0), jnp.concatenate(d_conv_b, axis=0)], axis=0)
    _, sum_f = gathered_sum(rows_f, "f")
    sum_f = unpad_ff(sum_f, 1, 2 * N_DEV)
    grad_conv_w = lax.dynamic_slice_in_dim(sum_f[:L * CONV_W].reshape(L, CONV_W, DFF2), me * FW, FW, axis=2)
    grad_conv_b = sum_f[L * CONV_W:L * CONV_W + L]

    rows_c = jnp.concatenate([jnp.stack(d_ws).reshape(NB * G * CHUNK, CHUNK), jnp.stack(d_bs).reshape(NB * G, CHUNK),
                              jnp.pad(jnp.stack(d_bf).reshape(NA, H), ((0, 0), (0, LANES - H)))], axis=0)
    _, sum_c = gathered_sum(rows_c, "c", mult=SLOT_ROWS)
    n_ws = NB * G * CHUNK
    grad_ws = sum_c[:n_ws].reshape(NB, G, CHUNK, CHUNK)
    grad_bs = sum_c[n_ws:n_ws + NB * G].reshape(NB, G, CHUNK)
    grad_bf = sum_c[n_ws + NB * G:n_ws + NB * G + NA, :H]

    dmod_all = every_d[:, :r0].reshape(N_DEV, L, 6 * D)
    dmod_loc = lax.dynamic_slice_in_dim(dmod_all, me * MW, MW, axis=2).transpose(1, 0, 2)
    dmod_loc = jnp.pad(dmod_loc, ((0, 0), (0, LANES - N_DEV), (0, 0)))
    c_t = jnp.pad(c_all.T, ((0, 0), (0, LANES - N_DEV)))
    grad_mod_w = mod_w_bwd(c_t, dmod_loc, "mod_w_bwd")

    first = adamw(mod_w, grad_mod_w, m_mod_w, v_mod_w, "adamw",
                  riders=[chip_exchange(mix_pairs[1]), chip_exchange(mix_pairs[2])])
    finish_mixer(mix_pairs[0], first[3], first[4])
    grad_attn_w_in, grad_attn_w_o = jnp.stack(g_wai), jnp.stack(g_wao)
    grad_gm_w_in, grad_gm_w_o = jnp.stack(g_wgi), jnp.stack(g_wgo)
    grad_ffn_w_in, grad_ffn_w_out = jnp.stack(g_wfi), jnp.stack(g_wfo)

    weights = [mod_w, mod_b, mix_norm_g, ffn_norm_g, attn_w_in, attn_b_f, attn_w_o, gm_w_in, gm_v_g, gm_w_s,
               gm_b_s, gm_w_o, ffn_w_in, ffn_conv_w, ffn_conv_b, ffn_w_out, final_g]
    grads = [grad_mod_w, grad_mod_b, grad_mix_g, grad_ffn_g, grad_attn_w_in, grad_bf, grad_attn_w_o,
             grad_gm_w_in, grad_vg, grad_ws, grad_bs, grad_gm_w_o, grad_ffn_w_in, grad_conv_w, grad_conv_b,
             grad_ffn_w_out, grad_final_g]
    ms = [m_mod_w, m_mod_b, m_mix_norm_g, m_ffn_norm_g, m_attn_w_in, m_attn_b_f, m_attn_w_o, m_gm_w_in, m_gm_v_g,
          m_gm_w_s, m_gm_b_s, m_gm_w_o, m_ffn_w_in, m_ffn_conv_w, m_ffn_conv_b, m_ffn_w_out, m_final_g]
    vs = [v_mod_w, v_mod_b, v_mix_norm_g, v_ffn_norm_g, v_attn_w_in, v_attn_b_f, v_attn_w_o, v_gm_w_in, v_gm_v_g,
          v_gm_w_s, v_gm_b_s, v_gm_w_o, v_ffn_w_in, v_ffn_conv_w, v_ffn_conv_b, v_ffn_w_out, v_final_g]
    deltas, new_ms, new_vs = [], [], []
    for k, (w, g, m_, v_) in enumerate(zip(weights, grads, ms, vs)):
        d_, mn_, vn_ = first[:3] if k == 0 else adamw(w, g, m_, v_, "adamw")
        deltas.append(d_)
        new_ms.append(mn_)
        new_vs.append(vn_)

    return (loss, grad_x, *grads, *deltas, *new_ms, *new_vs)
```

```python
import numpy as np
import jax
import jax.numpy as jnp
from jax import lax
from jax.experimental import pallas as pl
from jax.experimental.pallas import tpu as pltpu

f32 = jnp.float32
bf16 = jnp.bfloat16

AXES = ("x", "y", "c")
N_DEV = 8
N_CHIPS = 4
LANES = 128
SUBLANES = 8
HEAD_DIM = 128
CHUNK = 128
GM_GROUP = 128
CONV_W = 3
EPS = 1e-6
NEG = -1e30
VMEM_LIMIT_BYTES = 56 * 1024 * 1024
MATMUL_VMEM_BYTES = 40 * 1024 * 1024
SLOT_ROWS = 512
LOG2E = 1.4426950408889634
ATT_Q_TILE = 1024
ATT_KEYS_PER_Q = 1

ADAM_LR = 0.001
ADAM_B1 = 0.9
ADAM_B2 = 0.999
ADAM_EPS = 1e-08
ADAM_WD = 0.01
ADAM_STEP = 10

MESH = pl.DeviceIdType.MESH
ANY = pl.BlockSpec(memory_space=pl.ANY)


def _cp(sem):
    return pltpu.CompilerParams(dimension_semantics=sem, vmem_limit_bytes=VMEM_LIMIT_BYTES)


def _pick(n, prefs):
    for p in prefs:
        if n % p == 0:
            return p
    return n


def _round_up(n, m):
    return (n + m - 1) // m * m


class Exchange:
    def __init__(self, src, out_shape, scratch, phases):
        self.src = src
        self.out_shape = out_shape
        self.scratch = scratch
        self.phases = phases


def gather_exchange(xl):
    def phases(x_ref, out_ref, send_sems, recv_sems, local_sem):
        x, y, c = lax.axis_index("x"), lax.axis_index("y"), lax.axis_index("c")
        me, sibling = (x, y, c), (x, y, 1 - c)
        chips = [(1 - x, y), (x, 1 - y), (1 - x, 1 - y)]

        def slot(px, py, pc):
            return out_ref.at[4 * px + 2 * py + pc]

        def copy(k, block, to, src=None):
            return pltpu.make_async_remote_copy(
                src_ref=slot(*block) if src is None else src, dst_ref=slot(*block),
                send_sem=send_sems.at[k], recv_sem=recv_sems.at[k],
                device_id=to, device_id_type=MESH)

        mine = pltpu.make_async_copy(x_ref, slot(*me), local_sem)
        first = [copy(0, me, sibling, src=x_ref)]
        first += [copy(1 + j, me, (*chip, c), src=x_ref) for j, chip in enumerate(chips)]
        passed = [copy(4 + j, (*chip, c), sibling) for j, chip in enumerate(chips)]

        def start():
            mine.start()
            for cp in first:
                cp.start()

        def hand_on():
            for j, chip in enumerate(chips):
                copy(1 + j, (*chip, c), me).wait_recv()
                passed[j].start()

        def finish():
            copy(0, sibling, me).wait_recv()
            for j, chip in enumerate(chips):
                copy(4 + j, (*chip, 1 - c), me).wait_recv()
            for cp in first + passed:
                cp.wait_send()
            mine.wait()

        return start, hand_on, finish

    return Exchange(xl, jax.ShapeDtypeStruct((N_DEV,) + xl.shape, xl.dtype),
                    [pltpu.SemaphoreType.DMA((7,)), pltpu.SemaphoreType.DMA((7,)),
                     pltpu.SemaphoreType.DMA], phases)


def pair_exchange(g8):
    _, R, W = g8.shape

    def phases(g_ref, out_ref, send_sems, recv_sems):
        x, y, c = lax.axis_index("x"), lax.axis_index("y"), lax.axis_index("c")
        copies = [pltpu.make_async_remote_copy(
            src_ref=g_ref.at[2 * q + (1 - c)], dst_ref=out_ref.at[q],
            send_sem=send_sems.at[q], recv_sem=recv_sems.at[q],
            device_id=(x, y, 1 - c), device_id_type=MESH) for q in range(N_CHIPS)]

        def start():
            for cp in copies:
                cp.start()

        def finish():
            for cp in copies:
                cp.wait()

        return start, None, finish

    return Exchange(g8, jax.ShapeDtypeStruct((N_CHIPS, R, W), g8.dtype),
                    [pltpu.SemaphoreType.DMA((N_CHIPS,)), pltpu.SemaphoreType.DMA((N_CHIPS,))], phases)


def chip_exchange(p4):
    def phases(p_ref, out_ref, send_sems, recv_sems, local_sem):
        x, y, c = lax.axis_index("x"), lax.axis_index("y"), lax.axis_index("c")
        my_q = 2 * x + y
        chips = [(1 - x, y), (x, 1 - y), (1 - x, 1 - y)]
        mine = pltpu.make_async_copy(p_ref.at[my_q], out_ref.at[my_q], local_sem)
        copies = [pltpu.make_async_remote_copy(
            src_ref=p_ref.at[2 * px + py], dst_ref=out_ref.at[my_q],
            send_sem=send_sems.at[k], recv_sem=recv_sems.at[k],
            device_id=(px, py, c), device_id_type=MESH) for k, (px, py) in enumerate(chips)]

        def start():
            mine.start()
            for cp in copies:
                cp.start()

        def finish():
            for k, (px, py) in enumerate(chips):
                pltpu.make_async_remote_copy(
                    src_ref=p_ref.at[my_q], dst_ref=out_ref.at[2 * px + py],
                    send_sem=send_sems.at[k], recv_sem=recv_sems.at[k],
                    device_id=(px, py, c), device_id_type=MESH).wait_recv()
            for cp in copies:
                cp.wait_send()
            mine.wait()

        return start, None, finish

    return Exchange(p4, jax.ShapeDtypeStruct(p4.shape, p4.dtype),
                    [pltpu.SemaphoreType.DMA((3,)), pltpu.SemaphoreType.DMA((3,)),
                     pltpu.SemaphoreType.DMA], phases)


def run_exchange(ex, name):
    def body(src_ref, out_ref, *sems):
        start, hand_on, finish = ex.phases(src_ref, out_ref, *sems)
        start()
        if hand_on is not None:
            hand_on()
        finish()

    return pl.pallas_call(body, name=name, out_shape=ex.out_shape, in_specs=[ANY], out_specs=ANY,
                          scratch_shapes=ex.scratch)(ex.src)


def all_gather(xl, name):
    return run_exchange(gather_exchange(xl), name)


class Riders:
    def __init__(self, exchanges):
        self.exs = list(exchanges or [])
        self.in_specs = [ANY] * len(self.exs)
        self.args = [ex.src for ex in self.exs]
        self.out_specs = [ANY] * len(self.exs)
        self.out_shape = [ex.out_shape for ex in self.exs]
        self.scratch = [s for ex in self.exs for s in ex.scratch]

    def split(self, in_refs, out_refs, scratch_refs):
        n = len(self.exs)
        self.refs = []
        pos = len(scratch_refs) - len(self.scratch)
        for k, ex in enumerate(self.exs):
            sems = scratch_refs[pos:pos + len(ex.scratch)]
            pos += len(ex.scratch)
            self.refs.append((in_refs[len(in_refs) - n + k], out_refs[len(out_refs) - n + k], sems))

    def _parts(self):
        return [ex.phases(src, out, *sems) for ex, (src, out, sems) in zip(self.exs, self.refs)]

    def before(self, step):
        if not self.exs:
            return
        parts = self._parts()

        @pl.when(step == 0)
        def _():
            for start, _, _ in parts:
                start()

    def after(self, step, n_steps):
        if not self.exs:
            return
        parts = self._parts()
        mid = (3 * n_steps) // 4

        if any(h is not None for _, h, _ in parts):
            @pl.when(step == mid)
            def _():
                for _, hand_on, _ in parts:
                    if hand_on is not None:
                        hand_on()

        @pl.when(step == n_steps - 1)
        def _():
            for _, _, finish in parts:
                finish()


def pair_sum(g8, got4, c_idx, name):
    _, R, W = g8.shape
    whole_block_fits = 6 * R * W * g8.dtype.itemsize <= MATMUL_VMEM_BYTES
    tr = R if whole_block_fits else _pick(R, (512, 256, 128, 64, 32, 16))
    g5 = g8.reshape(N_CHIPS, 2, R, W)

    def body(c_ref, a_ref, b_ref, o_ref):
        o_ref[...] = (a_ref[...].astype(f32) + b_ref[...].astype(f32)).astype(o_ref.dtype)

    grid_spec = pltpu.PrefetchScalarGridSpec(
        num_scalar_prefetch=1, grid=(N_CHIPS, R // tr),
        in_specs=[pl.BlockSpec((None, None, tr, W), lambda q, r, cr: (q, cr[0], r, 0)),
                  pl.BlockSpec((None, tr, W), lambda q, r, cr: (q, r, 0))],
        out_specs=pl.BlockSpec((None, tr, W), lambda q, r, cr: (q, r, 0)))
    return pl.pallas_call(
        body, name=name, grid_spec=grid_spec,
        out_shape=jax.ShapeDtypeStruct((N_CHIPS, R, W), g8.dtype),
        compiler_params=_cp(("parallel", "parallel")),
    )(c_idx, g5, got4)


def sum_slots(xs, name, out_dtype=f32):
    S, R, W = xs.shape
    tr = _pick(R, (512, 256, 128, 64, 32, 16, 8))

    def body(x_ref, o_ref):
        acc = x_ref[0].astype(f32)
        for s in range(1, S):
            acc = acc + x_ref[s].astype(f32)
        o_ref[...] = acc.astype(o_ref.dtype)

    return pl.pallas_call(
        body, name=name, grid=(R // tr,),
        in_specs=[pl.BlockSpec((S, tr, W), lambda r: (0, r, 0))],
        out_specs=pl.BlockSpec((tr, W), lambda r: (r, 0)),
        out_shape=jax.ShapeDtypeStruct((R, W), out_dtype),
        compiler_params=_cp(("parallel",)),
    )(xs)


def matmul(a, b, *, name, ta=False, tb=False, a_split=False, b_split=False, out_split=False,
           out_dtype=f32, resid=None, gvec=None, emit_acc=False, riders=None):
    rd = Riders(riders)
    if a_split:
        rows, cols = a.shape[1], 2 * a.shape[2]
        M, K = (cols, rows) if ta else (rows, cols)
    else:
        M, K = (a.shape[1], a.shape[0]) if ta else a.shape
    if b_split:
        assert not tb
        N = 2 * b.shape[2]
        assert b.shape[1] == K
    else:
        N = b.shape[0] if tb else b.shape[1]
        assert (b.shape[1] if tb else b.shape[0]) == K, (a.shape, b.shape, name)

    m_split = a_split and ta
    k_split = a_split and not ta
    n_split = b_split or out_split
    tm = _pick(M // 2 if m_split else M, (1024, 512, 256, 128, 64, 32, 16, 8))
    k_len, n_len = (K // 2 if k_split else K), (N // 2 if n_split else N)
    out_bytes = jnp.dtype(out_dtype).itemsize + (4 if resid is not None else 0) + (2 if emit_acc else 0)

    def fits(tk_, tn_):
        operands = 2 * (tm * tk_ * a.dtype.itemsize + tk_ * tn_ * b.dtype.itemsize)
        acc = tm * tn_ * 4 if tk_ < K else 0
        return operands + acc + 2 * tm * tn_ * out_bytes <= MATMUL_VMEM_BYTES

    tk_options = [d for d in range(k_len, 0, -LANES) if k_len % d == 0 and d % LANES == 0]
    tn_options = [t for t in (1024, 512, 256, 128) if n_len % t == 0]
    tk, tn = next(((tk_, tn_) for tn_min in (512, 128) for tk_ in tk_options for tn_ in tn_options
                   if tn_ >= tn_min and fits(tk_, tn_)), (tk_options[-1], tn_options[-1]))
    nk = K // tk
    n_half = (N // 2) // tn if n_split else 0
    k_half = (K // 2) // tk if k_split else 0
    m_half = (M // 2) // tm if m_split else 0

    if m_split:
        a_spec = pl.BlockSpec((None, tk, tm), lambda i, j, k: (i // m_half, k, i % m_half))
    elif k_split:
        a_spec = pl.BlockSpec((None, tm, tk), lambda i, j, k: (k // k_half, i, k % k_half))
    elif ta:
        a_spec = pl.BlockSpec((tk, tm), lambda i, j, k: (k, i))
    else:
        a_spec = pl.BlockSpec((tm, tk), lambda i, j, k: (i, k))
    if b_split:
        b_spec = pl.BlockSpec((None, tk, tn), lambda i, j, k: (j // n_half, k, j % n_half))
    elif tb:
        b_spec = pl.BlockSpec((tn, tk), lambda i, j, k: (j, k))
    else:
        b_spec = pl.BlockSpec((tk, tn), lambda i, j, k: (k, j))
    if out_split:
        o_spec = pl.BlockSpec((None, tm, tn), lambda i, j, k: (j // n_half, i, j % n_half))
        o_shape = (2, M, N // 2)
    else:
        o_spec = pl.BlockSpec((tm, tn), lambda i, j, k: (i, j))
        o_shape = (M, N)

    in_specs = [a_spec, b_spec]
    args = [a, b]
    if resid is not None:
        in_specs.append(pl.BlockSpec((tm, tn), lambda i, j, k: (i, j)))
        args.append(resid)
    if gvec is not None:
        in_specs.append(pl.BlockSpec((1, tn), lambda i, j, k: (0, j)))
        args.append(gvec)
    out_specs = [o_spec]
    out_shape = [jax.ShapeDtypeStruct(o_shape, out_dtype)]
    if emit_acc:
        out_specs.append(pl.BlockSpec((tm, tn), lambda i, j, k: (i, j)))
        out_shape.append(jax.ShapeDtypeStruct((M, N), bf16))
    dims = (((0 if ta else 1,), (1 if tb else 0,)), ((), ()))
    has_r, has_g = resid is not None, gvec is not None
    n_in, n_out = len(in_specs) + len(rd.exs), len(out_specs) + len(rd.exs)
    grid = (M // tm, N // tn, nk)
    n_steps = grid[0] * grid[1] * grid[2]

    def body(*refs):
        in_refs, out_refs, scratch_refs = refs[:n_in], refs[n_in:n_in + n_out], refs[n_in + n_out:]
        rd.split(in_refs, out_refs, scratch_refs)
        a_ref, b_ref = in_refs[0], in_refs[1]
        pos = 2
        r_ref = g_ref = None
        if has_r:
            r_ref = in_refs[pos]
            pos += 1
        if has_g:
            g_ref = in_refs[pos]
        o_ref = out_refs[0]
        y_ref = out_refs[1] if emit_acc else None
        step = (pl.program_id(0) * grid[1] + pl.program_id(1)) * nk + pl.program_id(2)
        rd.before(step)

        def finish(acc):
            if emit_acc:
                y_ref[...] = acc.astype(bf16)
            if has_g:
                acc = acc * g_ref[...]
            if has_r:
                acc = r_ref[...] + acc
            o_ref[...] = acc.astype(o_ref.dtype)

        part = lax.dot_general(a_ref[...].astype(bf16), b_ref[...].astype(bf16), dims,
                               preferred_element_type=f32)
        if nk == 1:
            finish(part)
        else:
            acc_ref = scratch_refs[0]
            k = pl.program_id(2)

            @pl.when(k == 0)
            def _():
                acc_ref[...] = part

            @pl.when(k > 0)
            def _():
                acc_ref[...] += part

            @pl.when(k == nk - 1)
            def _():
                finish(acc_ref[...])

        rd.after(step, n_steps)

    outs = pl.pallas_call(
        body, name=name, grid=grid,
        in_specs=in_specs + rd.in_specs, out_specs=out_specs + rd.out_specs,
        out_shape=out_shape + rd.out_shape,
        scratch_shapes=([pltpu.VMEM((tm, tn), f32)] if nk > 1 else []) + rd.scratch,
        compiler_params=_cp(("arbitrary",) * 3 if rd.exs else ("parallel", "parallel", "arbitrary")),
    )(*args, *rd.args)
    return outs if (emit_acc or rd.exs) else outs[0]


def _rows(T):
    return _pick(T, (256, 128, 64, 32, 16, 8))


def norm_mod_fwd(x, gn, sc, sh, name):
    T, D = x.shape
    tr = _rows(T)

    def body(x_ref, gn_ref, sc_ref, sh_ref, h_ref):
        xv = x_ref[...]
        r = lax.rsqrt(jnp.mean(xv * xv, axis=-1, keepdims=True) + EPS)
        y = (xv * r) * gn_ref[...]
        h_ref[...] = (y * (1.0 + sc_ref[...]) + sh_ref[...]).astype(bf16)

    vec = pl.BlockSpec((1, D), lambda i: (0, 0))
    row = pl.BlockSpec((tr, D), lambda i: (i, 0))
    return pl.pallas_call(
        body, name=name, grid=(T // tr,), in_specs=[row, vec, vec, vec], out_specs=row,
        out_shape=jax.ShapeDtypeStruct((T, D), bf16), compiler_params=_cp(("parallel",)),
    )(x, gn, sc, sh)


def norm_mod_bwd(x, gn, sc, dh, dx_res, name):
    T, D = x.shape
    tr = _rows(T)

    def body(x_ref, gn_ref, sc_ref, dh_ref, dr_ref, dx_ref, dsh_ref, dsc_ref, dgn_ref):
        @pl.when(pl.program_id(0) == 0)
        def _():
            dsh_ref[...] = jnp.zeros_like(dsh_ref)
            dsc_ref[...] = jnp.zeros_like(dsc_ref)
            dgn_ref[...] = jnp.zeros_like(dgn_ref)

        xv = x_ref[...]
        r = lax.rsqrt(jnp.mean(xv * xv, axis=-1, keepdims=True) + EPS)
        xn = xv * r
        gn_v = gn_ref[...]
        dh_v = dh_ref[...]
        dsh_ref[...] += jnp.sum(dh_v, axis=0, keepdims=True)
        dsc_ref[...] += jnp.sum(dh_v * (xn * gn_v), axis=0, keepdims=True)
        dy = dh_v * (1.0 + sc_ref[...])
        dgn_ref[...] += jnp.sum(dy * xn, axis=0, keepdims=True)
        dxn = dy * gn_v
        dx = r * (dxn - xn * jnp.mean(dxn * xn, axis=-1, keepdims=True))
        dx_ref[...] = dr_ref[...] + dx

    vec = pl.BlockSpec((1, D), lambda i: (0, 0))
    row = pl.BlockSpec((tr, D), lambda i: (i, 0))
    vshape = jax.ShapeDtypeStruct((1, D), f32)
    return pl.pallas_call(
        body, name=name, grid=(T // tr,), in_specs=[row, vec, vec, row, row],
        out_specs=[row, vec, vec, vec],
        out_shape=[jax.ShapeDtypeStruct((T, D), f32), vshape, vshape, vshape],
        compiler_params=_cp(("arbitrary",)),
    )(x, gn, sc, dh, dx_res)


def gate_bwd(dx, y, g, name):
    T, D = dx.shape
    tr = _rows(T)

    def body(dx_ref, y_ref, g_ref, dy_ref, dg_ref):
        @pl.when(pl.program_id(0) == 0)
        def _():
            dg_ref[...] = jnp.zeros_like(dg_ref)

        dxv = dx_ref[...]
        dy_ref[...] = (dxv * g_ref[...]).astype(bf16)
        dg_ref[...] += jnp.sum(dxv * y_ref[...].astype(f32), axis=0, keepdims=True)

    vec = pl.BlockSpec((1, D), lambda i: (0, 0))
    row = pl.BlockSpec((tr, D), lambda i: (i, 0))
    return pl.pallas_call(
        body, name=name, grid=(T // tr,), in_specs=[row, row, vec], out_specs=[row, vec],
        out_shape=[jax.ShapeDtypeStruct((T, D), bf16), jax.ShapeDtypeStruct((1, D), f32)],
        compiler_params=_cp(("arbitrary",)),
    )(dx, y, g)


def loss_head(x, fg, tgt, name):
    T, D = x.shape
    tr = _rows(T)

    def body(x_ref, fg_ref, t_ref, loss_ref, dx_ref, dfg_ref):
        @pl.when(pl.program_id(0) == 0)
        def _():
            loss_ref[...] = jnp.zeros_like(loss_ref)
            dfg_ref[...] = jnp.zeros_like(dfg_ref)

        xv = x_ref[...]
        r = lax.rsqrt(jnp.mean(xv * xv, axis=-1, keepdims=True) + EPS)
        xn = xv * r
        fg_v = fg_ref[...]
        err = xn * fg_v - t_ref[...]
        per_tok = jnp.mean(err * err, axis=-1, keepdims=True)
        loss_ref[...] += 0.5 * jnp.sum(per_tok, axis=0, keepdims=True)
        dy = err * (1.0 / D)
        dfg_ref[...] += jnp.sum(dy * xn, axis=0, keepdims=True)
        dxn = dy * fg_v
        dx_ref[...] = r * (dxn - xn * jnp.mean(dxn * xn, axis=-1, keepdims=True))

    vec = pl.BlockSpec((1, D), lambda i: (0, 0))
    row = pl.BlockSpec((tr, D), lambda i: (i, 0))
    one = pl.BlockSpec((1, 1), lambda i: (0, 0))
    return pl.pallas_call(
        body, name=name, grid=(T // tr,), in_specs=[row, vec, row], out_specs=[one, row, vec],
        out_shape=[jax.ShapeDtypeStruct((1, 1), f32), jax.ShapeDtypeStruct((T, D), f32),
                   jax.ShapeDtypeStruct((1, D), f32)],
        compiler_params=_cp(("arbitrary",)),
    )(x, fg, tgt)


def _conv_tiles(T, FP):
    return _pick(T, (512, 256, 128, 64, 32, 16, 8)), _pick(FP, (512, 256, 128))


def _conv_specs(tr, tc, T, FP):
    nj = FP // tc
    r8 = tr // SUBLANES
    last8 = T // SUBLANES - 1
    main = pl.BlockSpec((2, tr, tc), lambda j, i: (0, i, j))
    prev = pl.BlockSpec((2, SUBLANES, tc), lambda j, i: (0, jnp.maximum(i * r8 - 1, 0), j))
    nxt = pl.BlockSpec((2, SUBLANES, tc), lambda j, i: (0, jnp.minimum((i + 1) * r8, last8), j))
    wg = pl.BlockSpec((CONV_W, tc), lambda j, i: (0, j))
    wu = pl.BlockSpec((CONV_W, tc), lambda j, i: (0, j + nj))
    bg = pl.BlockSpec((1, tc), lambda j, i: (0, j))
    bu = pl.BlockSpec((1, tc), lambda j, i: (0, j + nj))
    return main, prev, nxt, wg, wu, bg, bu


def _causal_taps(av, hp_ref, s, has_prev, row):
    h7 = jnp.where(has_prev, hp_ref[s, 7:8, :], 0.0)
    h6 = jnp.where(has_prev, hp_ref[s, 6:7, :], 0.0)
    m1 = jnp.where(row == 0, h7, pltpu.roll(av, 1, 0))
    m2 = jnp.where(row == 0, h6, jnp.where(row == 1, h7, pltpu.roll(av, 2, 0)))
    return m1, m2


def conv_glu_fwd(a3, conv_w, conv_b, name, riders=None):
    _, T, FP = a3.shape
    tr, tc = _conv_tiles(T, FP)
    main, prev, _, wg, wu, bg, bu = _conv_specs(tr, tc, T, FP)
    rd = Riders(riders)
    n_ex = len(rd.exs)
    ni = T // tr

    def body(*refs):
        in_refs, out_refs, scratch_refs = refs[:6 + n_ex], refs[6 + n_ex:7 + 2 * n_ex], refs[7 + 2 * n_ex:]
        rd.split(in_refs, out_refs, scratch_refs)
        a_ref, hp_ref, wg_ref, wu_ref, bg_ref, bu_ref = in_refs[:6]
        act_ref = out_refs[0]
        step = pl.program_id(0) * ni + pl.program_id(1)
        rd.before(step)
        has_prev = pl.program_id(1) > 0
        row = lax.broadcasted_iota(jnp.int32, (tr, tc), 0)

        def conv(s, w_ref, b_ref):
            av = a_ref[s]
            m1, m2 = _causal_taps(av, hp_ref, s, has_prev, row)
            return w_ref[0:1, :] * m2 + w_ref[1:2, :] * m1 + w_ref[2:3, :] * av + b_ref[...]

        gate = conv(0, wg_ref, bg_ref)
        up = conv(1, wu_ref, bu_ref)
        act_ref[...] = ((gate * jax.nn.sigmoid(gate)) * up).astype(bf16)
        rd.after(step, (FP // tc) * ni)

    outs = pl.pallas_call(
        body, name=name, grid=(FP // tc, ni),
        in_specs=[main, prev, wg, wu, bg, bu] + rd.in_specs,
        out_specs=[pl.BlockSpec((tr, tc), lambda j, i: (i, j))] + rd.out_specs,
        out_shape=[jax.ShapeDtypeStruct((T, FP), bf16)] + rd.out_shape,
        scratch_shapes=rd.scratch,
        compiler_params=_cp(("arbitrary", "arbitrary") if rd.exs else ("parallel", "parallel")),
    )(a3, a3, conv_w, conv_w, conv_b, conv_b, *rd.args)
    return outs if rd.exs else outs[0]


def conv_glu_bwd(a3, conv_w, conv_b, dact, name, riders=None):
    _, T, FP = a3.shape
    tr, tc = _conv_tiles(T, FP)
    ni = T // tr
    main, prev, nxt, wg, wu, bg, bu = _conv_specs(tr, tc, T, FP)
    r8 = tr // SUBLANES
    last8 = T // SUBLANES - 1
    d_main = pl.BlockSpec((tr, tc), lambda j, i: (i, j))
    d_next = pl.BlockSpec((SUBLANES, tc), lambda j, i: (jnp.minimum((i + 1) * r8, last8), j))
    rd = Riders(riders)
    n_ex = len(rd.exs)

    def body(*refs):
        in_refs, out_refs, scratch_refs = refs[:9 + n_ex], refs[9 + n_ex:14 + 2 * n_ex], refs[14 + 2 * n_ex:]
        rd.split(in_refs, out_refs, scratch_refs)
        a_ref, hp_ref, hn_ref, d_ref, dn_ref, wg_ref, wu_ref, bg_ref, bu_ref = in_refs[:9]
        da_ref, dwg_ref, dwu_ref, dbg_ref, dbu_ref = out_refs[:5]
        i = pl.program_id(1)
        step = pl.program_id(0) * ni + i
        rd.before(step)
        has_prev = i > 0
        has_next = i < ni - 1
        row = lax.broadcasted_iota(jnp.int32, (tr, tc), 0)
        row8 = lax.broadcasted_iota(jnp.int32, (SUBLANES, tc), 0)

        @pl.when(i == 0)
        def _():
            dwg_ref[...] = jnp.zeros_like(dwg_ref)
            dwu_ref[...] = jnp.zeros_like(dwu_ref)
            dbg_ref[...] = jnp.zeros_like(dbg_ref)
            dbu_ref[...] = jnp.zeros_like(dbu_ref)

        def prep(s, w_ref, b_ref):
            av = a_ref[s]
            m1, m2 = _causal_taps(av, hp_ref, s, has_prev, row)
            w0, w1, w2, bv = w_ref[0:1, :], w_ref[1:2, :], w_ref[2:3, :], b_ref[...]
            pre = w0 * m2 + w1 * m1 + w2 * av + bv
            an = hn_ref[s]
            l1 = a_ref[s, tr - 1:tr, :]
            l2 = a_ref[s, tr - 2:tr - 1, :]
            n1 = jnp.where(row8 == 0, l1, pltpu.roll(an, 1, 0))
            n2 = jnp.where(row8 == 0, l2, jnp.where(row8 == 1, l1, pltpu.roll(an, 2, 0)))
            pre_n = w0 * n2 + w1 * n1 + w2 * an + bv
            return av, m1, m2, pre, pre_n

        def glu_bwd(gate, up, d):
            sg = jax.nn.sigmoid(gate)
            dgate = d * up * (sg * (1.0 + gate * (1.0 - sg)))
            dup = d * (gate * sg)
            return dgate, dup

        def row_of(v8, r):
            return jnp.sum(jnp.where(row8 == r, v8, 0.0), axis=0, keepdims=True)

        def back(dc, dc_n, w_ref):
            n0, n1 = row_of(dc_n, 0), row_of(dc_n, 1)
            p1 = jnp.where(row == tr - 1, n0, pltpu.roll(dc, tr - 1, 0))
            p2 = jnp.where(row == tr - 1, n1, jnp.where(row == tr - 2, n0, pltpu.roll(dc, tr - 2, 0)))
            return w_ref[2:3, :] * dc + w_ref[1:2, :] * p1 + w_ref[0:1, :] * p2

        def tok_sum(v):
            return jnp.sum(v, axis=0, keepdims=True)

        ag, g1, g2, gate, gate_n = prep(0, wg_ref, bg_ref)
        au, u1, u2, up, up_n = prep(1, wu_ref, bu_ref)
        dcg, dcu = glu_bwd(gate, up, d_ref[...])
        dn = jnp.where(has_next, dn_ref[...], 0.0)
        dcg_n, dcu_n = glu_bwd(gate_n, up_n, dn)
        da_ref[0] = back(dcg, dcg_n, wg_ref).astype(bf16)
        da_ref[1] = back(dcu, dcu_n, wu_ref).astype(bf16)
        dwg_ref[0:1, :] += tok_sum(dcg * g2)
        dwg_ref[1:2, :] += tok_sum(dcg * g1)
        dwg_ref[2:3, :] += tok_sum(dcg * ag)
        dwu_ref[0:1, :] += tok_sum(dcu * u2)
        dwu_ref[1:2, :] += tok_sum(dcu * u1)
        dwu_ref[2:3, :] += tok_sum(dcu * au)
        dbg_ref[...] += tok_sum(dcg)
        dbu_ref[...] += tok_sum(dcu)
        rd.after(step, (FP // tc) * ni)

    w_out = pl.BlockSpec((CONV_W, tc), lambda j, i: (0, j))
    b_out = pl.BlockSpec((1, tc), lambda j, i: (0, j))
    return pl.pallas_call(
        body, name=name, grid=(FP // tc, ni),
        in_specs=[main, prev, nxt, d_main, d_next, wg, wu, bg, bu] + rd.in_specs,
        out_specs=[main, w_out, w_out, b_out, b_out] + rd.out_specs,
        out_shape=[jax.ShapeDtypeStruct((2, T, FP), bf16),
                   jax.ShapeDtypeStruct((CONV_W, FP), f32), jax.ShapeDtypeStruct((CONV_W, FP), f32),
                   jax.ShapeDtypeStruct((1, FP), f32), jax.ShapeDtypeStruct((1, FP), f32)] + rd.out_shape,
        scratch_shapes=rd.scratch,
        compiler_params=_cp(("arbitrary", "arbitrary") if rd.exs else ("parallel", "arbitrary")),
    )(a3, a3, a3, dact, dact, conv_w, conv_w, conv_b, conv_b, *rd.args)


_GELU_C = 0.7978845608028654
_GELU_A = 0.044715


def _gelu(x):
    return 0.5 * x * (1.0 + jnp.tanh(_GELU_C * (x + _GELU_A * (x * x * x))))


def _gelu_and_grad(x):
    t = jnp.tanh(_GELU_C * (x + _GELU_A * (x * x * x)))
    g = 0.5 * x * (1.0 + t)
    dg = 0.5 * (1.0 + t) + 0.5 * x * (1.0 - t * t) * (_GELU_C * (1.0 + 3.0 * _GELU_A * (x * x)))
    return g, dg


def _tril_bf16(w):
    r = lax.broadcasted_iota(jnp.int32, w.shape, 0)
    c = lax.broadcasted_iota(jnp.int32, w.shape, 1)
    return jnp.where(r >= c, w, 0.0).astype(bf16)


def gm_gate_fwd(z, vg, ws, bs_t, name):
    T, D2 = z.shape
    D = D2 // 2
    G = D // GM_GROUP
    tr = _pick(T, (256, 128))
    nc = tr // CHUNK

    def body(z_ref, vg_ref, ws_ref, bs_ref, o_ref):
        u = _gelu(z_ref[:, :D])
        v = _gelu(z_ref[:, D:])
        rv = lax.rsqrt(jnp.mean(v * v, axis=-1, keepdims=True) + EPS)
        vn = ((v * rv) * vg_ref[...]).astype(bf16)
        for g in range(G):
            wg = _tril_bf16(ws_ref[g])
            bg = bs_ref[:, g:g + 1]
            cs = slice(g * GM_GROUP, (g + 1) * GM_GROUP)
            for c in range(nc):
                rs = slice(c * CHUNK, (c + 1) * CHUNK)
                sv = jnp.dot(wg, vn[rs, cs], preferred_element_type=f32) + bg
                o_ref[rs, cs] = (u[rs, cs] * sv).astype(bf16)

    return pl.pallas_call(
        body, name=name, grid=(T // tr,),
        in_specs=[pl.BlockSpec((tr, D2), lambda i: (i, 0)),
                  pl.BlockSpec((1, D), lambda i: (0, 0)),
                  pl.BlockSpec((G, CHUNK, CHUNK), lambda i: (0, 0, 0)),
                  pl.BlockSpec((CHUNK, G), lambda i: (0, 0))],
        out_specs=pl.BlockSpec((tr, D), lambda i: (i, 0)),
        out_shape=jax.ShapeDtypeStruct((T, D), bf16),
        compiler_params=_cp(("parallel",)),
    )(z, vg, ws, bs_t)


def gm_gate_bwd(z, vg, ws, bs_t, dgated, name):
    T, D2 = z.shape
    D = D2 // 2
    G = D // GM_GROUP
    tr = _pick(T, (256, 128))
    nc = tr // CHUNK

    def body(z_ref, vg_ref, ws_ref, bs_ref, dg_ref, dz_ref, dws_ref, dbs_ref, dvg_ref,
             du_s, dvn_s):
        @pl.when(pl.program_id(0) == 0)
        def _():
            dws_ref[...] = jnp.zeros_like(dws_ref)
            dbs_ref[...] = jnp.zeros_like(dbs_ref)
            dvg_ref[...] = jnp.zeros_like(dvg_ref)

        u, du_dz = _gelu_and_grad(z_ref[:, :D])
        v, dv_dz = _gelu_and_grad(z_ref[:, D:])
        rv = lax.rsqrt(jnp.mean(v * v, axis=-1, keepdims=True) + EPS)
        vhat = v * rv
        vg_v = vg_ref[...]
        vn = (vhat * vg_v).astype(bf16)
        rr = lax.broadcasted_iota(jnp.int32, (CHUNK, CHUNK), 0)
        cc = lax.broadcasted_iota(jnp.int32, (CHUNK, CHUNK), 1)
        for g in range(G):
            wg = _tril_bf16(ws_ref[g])
            bg = bs_ref[:, g:g + 1]
            cs = slice(g * GM_GROUP, (g + 1) * GM_GROUP)
            dw_acc = jnp.zeros((CHUNK, CHUNK), f32)
            db_acc = jnp.zeros((CHUNK, 1), f32)
            for c in range(nc):
                rs = slice(c * CHUNK, (c + 1) * CHUNK)
                vb = vn[rs, cs]
                sv = jnp.dot(wg, vb, preferred_element_type=f32) + bg
                dgb = dg_ref[rs, cs]
                du_s[rs, cs] = dgb * sv
                dsv = dgb * u[rs, cs]
                dsv_b = dsv.astype(bf16)
                dw_acc += lax.dot_general(dsv_b, vb, (((1,), (1,)), ((), ())),
                                          preferred_element_type=f32)
                db_acc += jnp.sum(dsv, axis=1, keepdims=True)
                dvn_s[rs, cs] = lax.dot_general(wg, dsv_b, (((0,), (0,)), ((), ())),
                                                preferred_element_type=f32)
            dws_ref[g] += jnp.where(rr >= cc, dw_acc, 0.0)
            dbs_ref[:, g:g + 1] += db_acc
        dz_ref[:, :D] = (du_s[...] * du_dz).astype(bf16)
        dvn = dvn_s[...]
        dvg_ref[...] += jnp.sum(dvn * vhat, axis=0, keepdims=True)
        dvh = dvn * vg_v
        dv = rv * (dvh - vhat * jnp.mean(dvh * vhat, axis=-1, keepdims=True))
        dz_ref[:, D:] = (dv * dv_dz).astype(bf16)

    return pl.pallas_call(
        body, name=name, grid=(T // tr,),
        in_specs=[pl.BlockSpec((tr, D2), lambda i: (i, 0)),
                  pl.BlockSpec((1, D), lambda i: (0, 0)),
                  pl.BlockSpec((G, CHUNK, CHUNK), lambda i: (0, 0, 0)),
                  pl.BlockSpec((CHUNK, G), lambda i: (0, 0)),
                  pl.BlockSpec((tr, D), lambda i: (i, 0))],
        out_specs=[pl.BlockSpec((tr, D2), lambda i: (i, 0)),
                   pl.BlockSpec((G, CHUNK, CHUNK), lambda i: (0, 0, 0)),
                   pl.BlockSpec((CHUNK, G), lambda i: (0, 0)),
                   pl.BlockSpec((1, D), lambda i: (0, 0))],
        out_shape=[jax.ShapeDtypeStruct((T, D2), bf16),
                   jax.ShapeDtypeStruct((G, CHUNK, CHUNK), f32),
                   jax.ShapeDtypeStruct((CHUNK, G), f32),
                   jax.ShapeDtypeStruct((1, D), f32)],
        scratch_shapes=[pltpu.VMEM((tr, D), f32), pltpu.VMEM((tr, D), f32)],
        compiler_params=_cp(("arbitrary",)),
    )(z, vg, ws, bs_t, dgated)


def fox_gates_fwd(flog_t, b_col, name):
    H, T = flog_t.shape

    def body(fl_ref, b_ref, o_ref):
        xv = fl_ref[...] + b_ref[...]
        lf = jnp.minimum(xv, 0.0) - jnp.log1p(jnp.exp(-jnp.abs(xv)))
        lane = lax.broadcasted_iota(jnp.int32, (H, T), 1)
        s = 1
        while s < T:
            lf = lf + jnp.where(lane >= s, pltpu.roll(lf, s, 1), 0.0)
            s *= 2
        o_ref[...] = lf * LOG2E

    return pl.pallas_call(
        body, name=name, out_shape=jax.ShapeDtypeStruct((H, T), f32),
        compiler_params=pltpu.CompilerParams(vmem_limit_bytes=VMEM_LIMIT_BYTES),
    )(flog_t, b_col)


def fox_gates_bwd(flog_t, b_col, dF, name):
    H, T = flog_t.shape

    def body(fl_ref, b_ref, d_ref, o_ref, db_ref):
        xv = fl_ref[...] + b_ref[...]
        g = d_ref[...]
        lane = lax.broadcasted_iota(jnp.int32, (H, T), 1)
        s = 1
        while s < T:
            g = g + jnp.where(lane < T - s, pltpu.roll(g, T - s, 1), 0.0)
            s *= 2
        dfl = g * jax.nn.sigmoid(-xv)
        o_ref[...] = dfl
        db_ref[...] = jnp.sum(dfl, axis=1, keepdims=True)

    return pl.pallas_call(
        body, name=name,
        out_shape=[jax.ShapeDtypeStruct((H, T), f32), jax.ShapeDtypeStruct((H, 1), f32)],
        compiler_params=pltpu.CompilerParams(vmem_limit_bytes=VMEM_LIMIT_BYTES),
    )(flog_t, b_col, dF)


_NT = (((1,), (1,)), ((), ()))
_TN = (((0,), (0,)), ((), ()))


def _scores(q, k, fq, fk, col0=None):
    s = lax.dot_general(q, k, _NT, preferred_element_type=f32) * (HEAD_DIM ** -0.5 * LOG2E)
    s = s + fq - fk
    if col0 is not None:
        rows = lax.broadcasted_iota(jnp.int32, s.shape, 0)
        cols = col0 + lax.broadcasted_iota(jnp.int32, s.shape, 1)
        s = jnp.where(cols <= rows, s, NEG)
    return s


def fox_attn_fwd(qkv, f_row, f_col, name, riders=None):
    T, D3 = qkv.shape
    D = D3 // 3
    H = D // HEAD_DIM
    tq = _pick(T, (ATT_Q_TILE, 512, 256))
    kq = ATT_KEYS_PER_Q
    tk = tq // kq
    nq = T // tq
    rd = Riders(riders)
    pairs = [(i, j) for i in range(nq) for j in range(kq * i + kq)]
    i_tab = np.array([p[0] for p in pairs], np.int32)
    j_tab = np.array([p[1] for p in pairs], np.int32)

    def body(i_ref, j_ref, *refs):
        in_refs, out_refs, scratch_refs = refs[:5 + len(rd.exs)], refs[5 + len(rd.exs):7 + 2 * len(rd.exs)], refs[7 + 2 * len(rd.exs):]
        rd.split(in_refs, out_refs, scratch_refs)
        q_ref, k_ref, v_ref, fq_ref, fk_ref = in_refs[:5]
        o_ref, lse_ref = out_refs[:2]
        m_s, l_s, acc_s = scratch_refs[:3]
        t = pl.program_id(1)
        i, j = i_ref[t], j_ref[t]
        step = pl.program_id(0) * len(pairs) + t
        rd.before(step)

        @pl.when(j == 0)
        def _():
            m_s[...] = jnp.full_like(m_s, NEG)
            l_s[...] = jnp.zeros_like(l_s)
            acc_s[...] = jnp.zeros_like(acc_s)

        def update(col0):
            s = _scores(q_ref[...], k_ref[...], fq_ref[0], fk_ref[0], col0)
            m_prev = m_s[...]
            m_new = jnp.maximum(m_prev, jnp.max(s, axis=1, keepdims=True))
            alpha = jnp.exp2(m_prev - m_new)
            p = jnp.exp2(s - m_new)
            l_s[...] = alpha * l_s[...] + jnp.sum(p, axis=1, keepdims=True)
            acc_s[...] = alpha * acc_s[...] + jnp.dot(p.astype(bf16), v_ref[...],
                                                      preferred_element_type=f32)
            m_s[...] = m_new

        @pl.when(j < kq * i)
        def _():
            update(None)

        @pl.when(j >= kq * i)
        def _():
            update((j - kq * i) * tk)

        @pl.when(j == kq * i + kq - 1)
        def _():
            o_ref[...] = (acc_s[...] / l_s[...]).astype(bf16)
            lse_ref[0] = m_s[...] + jnp.log2(l_s[...])

        rd.after(step, H * len(pairs))

    blk = (tq, HEAD_DIM)
    kblk = (tk, HEAD_DIM)
    grid_spec = pltpu.PrefetchScalarGridSpec(
        num_scalar_prefetch=2, grid=(H, len(pairs)),
        in_specs=[pl.BlockSpec(blk, lambda h, t, it, jt: (it[t], h)),
                  pl.BlockSpec(kblk, lambda h, t, it, jt: (jt[t], H + h)),
                  pl.BlockSpec(kblk, lambda h, t, it, jt: (jt[t], 2 * H + h)),
                  pl.BlockSpec((1, tq, 1), lambda h, t, it, jt: (h, it[t], 0)),
                  pl.BlockSpec((1, 1, tk), lambda h, t, it, jt: (h, 0, jt[t]))] + rd.in_specs,
        out_specs=[pl.BlockSpec(blk, lambda h, t, it, jt: (it[t], h)),
                   pl.BlockSpec((1, tq, 1), lambda h, t, it, jt: (h, it[t], 0))] + rd.out_specs,
        scratch_shapes=[pltpu.VMEM((tq, 1), f32), pltpu.VMEM((tq, 1), f32),
                        pltpu.VMEM((tq, HEAD_DIM), f32)] + rd.scratch)
    return pl.pallas_call(
        body, name=name, grid_spec=grid_spec,
        out_shape=[jax.ShapeDtypeStruct((T, D), bf16), jax.ShapeDtypeStruct((H, T, 1), f32)] + rd.out_shape,
        compiler_params=_cp(("arbitrary", "arbitrary") if rd.exs else ("parallel", "arbitrary")),
    )(i_tab, j_tab, qkv, qkv, qkv, f_col, f_row, *rd.args)


def fox_attn_bwd(qkv, o, do, lse, f_row, f_col, name, riders=None):
    T, D3 = qkv.shape
    D = D3 // 3
    H = D // HEAD_DIM
    tq = _pick(T, (ATT_Q_TILE, 512, 256))
    kq = ATT_KEYS_PER_Q
    tk = tq // kq
    nq, nk = T // tq, T // tk
    scale = HEAD_DIM ** -0.5

    pairs = [(j, i) for j in range(nk) for i in range(j // kq, nq)]
    j_tab = np.array([p[0] for p in pairs], np.int32)
    i_tab = np.array([p[1] for p in pairs], np.int32)

    rd = Riders(riders)
    n_ex = len(rd.exs)

    def body(j_ref, i_ref, *refs):
        in_refs, out_refs, scratch_refs = refs[:8 + n_ex], refs[8 + n_ex:13 + 2 * n_ex], refs[13 + 2 * n_ex:]
        rd.split(in_refs, out_refs, scratch_refs)
        q_ref, k_ref, v_ref, o_ref, do_ref, lse_ref, fq_ref, fk_ref = in_refs[:8]
        dq_ref, dk_ref, dv_ref, cs_ref, rs_ref = out_refs[:5]
        dk_s, dv_s, dq_s, di_s = scratch_refs[:4]
        step = pl.program_id(1)
        rd.before(pl.program_id(0) * len(pairs) + step)
        j, i = j_ref[step], i_ref[step]
        rows = pl.ds(pl.multiple_of(i * tq, tq), tq)
        first_i = j // kq

        @pl.when(step == 0)
        def _():
            dq_s[...] = jnp.zeros_like(dq_s)
            rs_ref[...] = jnp.zeros_like(rs_ref)

        @pl.when(j == 0)
        def _():
            di_s[rows, :] = jnp.sum(do_ref[...] * o_ref[...].astype(f32), axis=1, keepdims=True)

        @pl.when(i == first_i)
        def _():
            dk_s[...] = jnp.zeros_like(dk_s)
            dv_s[...] = jnp.zeros_like(dv_s)
            cs_ref[...] = jnp.zeros_like(cs_ref)

        def accumulate(col0):
            q = q_ref[...]
            k = k_ref[...]
            s = _scores(q, k, fq_ref[0], fk_ref[0], col0)
            p = jnp.exp2(s - lse_ref[0])
            do_b = do_ref[...].astype(bf16)
            dp = lax.dot_general(do_b, v_ref[...], _NT, preferred_element_type=f32)
            ds = p * (dp - di_s[rows, :])
            ds_b = (ds * scale).astype(bf16)
            cs_ref[0] += jnp.sum(ds, axis=0, keepdims=True)
            rs_ref[0, rows, :] += jnp.sum(ds, axis=1, keepdims=True)
            dv_s[...] += lax.dot_general(p.astype(bf16), do_b, _TN, preferred_element_type=f32)
            dk_s[...] += lax.dot_general(ds_b, q, _TN, preferred_element_type=f32)
            dq_s[rows, :] += jnp.dot(ds_b, k, preferred_element_type=f32)

        @pl.when(i == first_i)
        def _():
            accumulate((j - kq * i) * tk)

        @pl.when(i > first_i)
        def _():
            accumulate(None)

        @pl.when(i == nq - 1)
        def _():
            dk_ref[...] = dk_s[...].astype(bf16)
            dv_ref[...] = dv_s[...].astype(bf16)

        @pl.when(step == len(pairs) - 1)
        def _():
            dq_ref[...] = dq_s[...].astype(bf16)

        rd.after(pl.program_id(0) * len(pairs) + step, H * len(pairs))

    blk = (tq, HEAD_DIM)
    kblk = (tk, HEAD_DIM)
    at_q = lambda h, s, jt, it: (it[s], h)
    col_q = pl.BlockSpec((1, tq, 1), lambda h, s, jt, it: (h, it[s], 0))
    row_k = pl.BlockSpec((1, 1, tk), lambda h, s, jt, it: (h, 0, jt[s]))
    grid_spec = pltpu.PrefetchScalarGridSpec(
        num_scalar_prefetch=2, grid=(H, len(pairs)),
        in_specs=[pl.BlockSpec(blk, at_q),
                  pl.BlockSpec(kblk, lambda h, s, jt, it: (jt[s], H + h)),
                  pl.BlockSpec(kblk, lambda h, s, jt, it: (jt[s], 2 * H + h)),
                  pl.BlockSpec(blk, at_q), pl.BlockSpec(blk, at_q), col_q, col_q, row_k] + rd.in_specs,
        out_specs=[pl.BlockSpec((T, HEAD_DIM), lambda h, s, jt, it: (0, h)),
                   pl.BlockSpec(kblk, lambda h, s, jt, it: (jt[s], h)),
                   pl.BlockSpec(kblk, lambda h, s, jt, it: (jt[s], h)),
                   row_k,
                   pl.BlockSpec((1, T, 1), lambda h, s, jt, it: (h, 0, 0))] + rd.out_specs,
        scratch_shapes=[pltpu.VMEM(kblk, f32), pltpu.VMEM(kblk, f32),
                        pltpu.VMEM((T, HEAD_DIM), f32), pltpu.VMEM((T, 1), f32)] + rd.scratch)
    return pl.pallas_call(
        body, name=name, grid_spec=grid_spec,
        out_shape=[jax.ShapeDtypeStruct((T, D), bf16), jax.ShapeDtypeStruct((T, D), bf16),
                   jax.ShapeDtypeStruct((T, D), bf16), jax.ShapeDtypeStruct((H, 1, T), f32),
                   jax.ShapeDtypeStruct((H, T, 1), f32)] + rd.out_shape,
        compiler_params=_cp(("arbitrary", "arbitrary") if rd.exs else ("parallel", "arbitrary")),
    )(j_tab, i_tab, qkv, qkv, qkv, o, do, lse, f_col, f_row, *rd.args)


def mod_fwd(c16, mod_w, mod_b_loc, name):
    L, D, MW = mod_w.shape
    tn = _pick(MW, (512, 256, 128))

    def body(c_ref, w_ref, b_ref, o_ref):
        cv = c_ref[...]
        ca = (cv * jax.nn.sigmoid(cv)).astype(bf16)
        o_ref[...] = jnp.dot(ca, w_ref[...].astype(bf16), preferred_element_type=f32) + b_ref[...]

    return pl.pallas_call(
        body, name=name, grid=(L, MW // tn),
        in_specs=[pl.BlockSpec((16, D), lambda l, j: (0, 0)),
                  pl.BlockSpec((None, D, tn), lambda l, j: (l, 0, j)),
                  pl.BlockSpec((None, 1, tn), lambda l, j: (l, 0, j))],
        out_specs=pl.BlockSpec((None, 16, tn), lambda l, j: (l, 0, j)),
        out_shape=jax.ShapeDtypeStruct((L, 16, MW), f32),
        compiler_params=_cp(("parallel", "parallel")),
    )(c16, mod_w, mod_b_loc)


def mod_w_bwd(c_t, dmod, name):
    D = c_t.shape[0]
    L, _, MW = dmod.shape
    tn = _pick(MW, (512, 256, 128))

    def body(c_ref, d_ref, o_ref):
        cv = c_ref[...]
        ca = (cv * jax.nn.sigmoid(cv)).astype(bf16)
        o_ref[...] = jnp.dot(ca, d_ref[...].astype(bf16), preferred_element_type=f32)

    return pl.pallas_call(
        body, name=name, grid=(L, MW // tn),
        in_specs=[pl.BlockSpec((D, LANES), lambda l, j: (0, 0)),
                  pl.BlockSpec((None, LANES, tn), lambda l, j: (l, 0, j))],
        out_specs=pl.BlockSpec((None, D, tn), lambda l, j: (l, 0, j)),
        out_shape=jax.ShapeDtypeStruct((L, D, MW), f32),
        compiler_params=_cp(("parallel", "parallel")),
    )(c_t, dmod)


def adamw(w, g, m, v, name, riders=None):
    shape = w.shape
    C = shape[-1] if w.ndim >= 1 else 1
    R = max(w.size // C, 1)
    w2, g2, m2, v2 = (t.reshape(R, C) for t in (w, g, m, v))
    tr = R
    for cand in (2048, 1024, 512, 256, 128, 64, 32, 16, 8):
        if R % cand == 0 and cand * _round_up(C, LANES) <= 256 * 1024:
            tr = cand
            break
    rd = Riders(riders)
    n_ex = len(rd.exs)

    def body(*refs):
        in_refs, out_refs, scratch_refs = refs[:4 + n_ex], refs[4 + n_ex:7 + 2 * n_ex], refs[7 + 2 * n_ex:]
        rd.split(in_refs, out_refs, scratch_refs)
        w_ref, g_ref, m_ref, v_ref = in_refs[:4]
        d_ref, mo_ref, vo_ref = out_refs[:3]
        rd.before(pl.program_id(0))
        gv = g_ref[...]
        mn = ADAM_B1 * m_ref[...] + (1.0 - ADAM_B1) * gv
        vn = ADAM_B2 * v_ref[...] + (1.0 - ADAM_B2) * (gv * gv)
        m_hat = mn / (1.0 - ADAM_B1 ** ADAM_STEP)
        v_hat = vn / (1.0 - ADAM_B2 ** ADAM_STEP)
        d_ref[...] = -ADAM_LR * (m_hat / (jnp.sqrt(v_hat) + ADAM_EPS) + ADAM_WD * w_ref[...])
        mo_ref[...] = mn
        vo_ref[...] = vn
        rd.after(pl.program_id(0), R // tr)

    spec = pl.BlockSpec((tr, C), lambda i: (i, 0))
    sds = jax.ShapeDtypeStruct((R, C), f32)
    outs = pl.pallas_call(
        body, name=name, grid=(R // tr,), in_specs=[spec] * 4 + rd.in_specs,
        out_specs=[spec] * 3 + rd.out_specs, out_shape=[sds, sds, sds] + rd.out_shape,
        scratch_shapes=rd.scratch,
        compiler_params=_cp(("arbitrary",) if rd.exs else ("parallel",)),
    )(w2, g2, m2, v2, *rd.args)
    return tuple(t.reshape(shape) for t in outs[:3]) + tuple(outs[3:])


def reduce_scatter_tail(pair, tag):
    quad = run_exchange(chip_exchange(pair), "rs_chip_exchange_" + tag)
    return sum_slots(quad, "rs_final_sum_" + tag)


def kernel(x, c, mod_w, mod_b, mix_norm_g, ffn_norm_g, attn_w_in, attn_b_f, attn_w_o, gm_w_in, gm_v_g, gm_w_s, gm_b_s, gm_w_o, ffn_w_in, ffn_conv_w, ffn_conv_b, ffn_w_out, final_g, loss_target, m_mod_w, m_mod_b, m_mix_norm_g, m_ffn_norm_g, m_attn_w_in, m_attn_b_f, m_attn_w_o, m_gm_w_in, m_gm_v_g, m_gm_w_s, m_gm_b_s, m_gm_w_o, m_ffn_w_in, m_ffn_conv_w, m_ffn_conv_b, m_ffn_w_out, m_final_g, v_mod_w, v_mod_b, v_mix_norm_g, v_ffn_norm_g, v_attn_w_in, v_attn_b_f, v_attn_w_o, v_gm_w_in, v_gm_v_g, v_gm_w_s, v_gm_b_s, v_gm_w_o, v_ffn_w_in, v_ffn_conv_w, v_ffn_conv_b, v_ffn_w_out, v_final_g):
    xi, yi, ci = lax.axis_index("x"), lax.axis_index("y"), lax.axis_index("c")
    me = 4 * xi + 2 * yi + ci

    _, T, D = x.shape
    L = mod_w.shape[0]
    MW = mod_w.shape[2]
    NA, _, QW = attn_w_in.shape
    NB = gm_w_in.shape[0]
    H = D // HEAD_DIM
    G = D // GM_GROUP
    DR = attn_w_o.shape[1]
    GW = gm_w_in.shape[2]
    FW = ffn_w_in.shape[2]
    FR = ffn_w_out.shape[1]
    FRP = _round_up(FR, LANES // 2)
    FWP = 2 * FRP
    FP = N_CHIPS * FWP
    DFF2 = N_DEV * FW
    assert 2 * FR == FW and N_DEV * QW == 3 * D + H and N_DEV * GW == 2 * D
    c_idx = ci.reshape(1).astype(jnp.int32)

    def pad_ff(t, axis, blocks):
        ax = axis % t.ndim
        t = t.reshape(t.shape[:ax] + (blocks, FR) + t.shape[ax + 1:])
        pad = [(0, 0)] * t.ndim
        pad[ax + 1] = (0, FRP - FR)
        t = jnp.pad(t, pad)
        return t.reshape(t.shape[:ax] + (blocks * FRP,) + t.shape[ax + 2:])

    def unpad_ff(t, axis, blocks):
        ax = axis % t.ndim
        t = t.reshape(t.shape[:ax] + (blocks, FRP) + t.shape[ax + 1:])
        t = lax.slice_in_dim(t, 0, FR, axis=ax + 1)
        return t.reshape(t.shape[:ax] + (blocks * FR,) + t.shape[ax + 2:])

    x0 = x[0]
    tgt = loss_target[0]

    c_all = all_gather(c, "gather_c").reshape(N_DEV, D)
    cw_loc = pad_ff(ffn_conv_w, 2, 2).reshape(L * CONV_W, FWP)
    conv_w_full = all_gather(cw_loc, "gather_conv_w").transpose(1, 0, 2).reshape(L, CONV_W, 2 * FP)
    vg_full = all_gather(gm_v_g, "gather_vg").transpose(1, 0, 2).reshape(NB, 1, D)
    conv_b_full = pad_ff(ffn_conv_b, 1, 2 * N_DEV).reshape(L, 1, 2 * FP)

    c16 = jnp.pad(c_all, ((0, 16 - N_DEV), (0, 0)))
    mod_b_loc = lax.dynamic_slice_in_dim(mod_b, me * MW, MW, axis=1).reshape(L, 1, MW)
    mod_part = mod_fwd(c16, mod_w, mod_b_loc, "mod_fwd")[:, :N_DEV]
    mod_all = all_gather(mod_part, "gather_mod")
    mod_me = lax.dynamic_index_in_dim(mod_all, me, axis=2, keepdims=False)
    mod_me = mod_me.transpose(1, 0, 2).reshape(L, 6, 1, D)

    w_ai_t = jnp.swapaxes(attn_w_in, 1, 2).astype(bf16)
    w_gi_t = jnp.swapaxes(gm_w_in, 1, 2).astype(bf16)
    w_fi_t = pad_ff(jnp.swapaxes(ffn_w_in, 1, 2).astype(bf16), 1, 2)
    w_ao_l = attn_w_o.astype(bf16)
    w_go_l = gm_w_o.astype(bf16)
    w_fo_l = jnp.pad(ffn_w_out.astype(bf16), ((0, 0), (0, FRP - FR), (0, 0)))

    stash = []
    arrived = {}
    xc = x0

    def shard(kind, layer):
        even = layer % 2 == 0
        return {"ffn_in": w_fi_t, "ffn_out": w_fo_l, "mix_in": w_ai_t if even else w_gi_t,
                "mix_out": w_ao_l if even else w_go_l}[kind][layer if kind.startswith("ffn") else layer // 2]

    def need(kind, layer):
        if (kind, layer) not in arrived:
            arrived[(kind, layer)] = all_gather(shard(kind, layer), "gather_" + kind)
        return arrived.pop((kind, layer))

    def hosting(keys, n_own, fn, *args, **kw):
        keys = [k for k in keys if k[1] < L]
        outs = fn(*args, riders=[gather_exchange(shard(*k)) for k in keys], **kw)
        outs = list(outs) if isinstance(outs, (list, tuple)) else [outs]
        arrived.update(zip(keys, outs[n_own:]))
        return outs[:n_own]

    for i in range(L):
        sh1, sc1, g1, sh2, sc2, g2 = (mod_me[i, k] for k in range(6))
        jm = i // 2
        st = {"x_in": xc}
        h = norm_mod_fwd(xc, mix_norm_g[i][None], sc1, sh1, "norm_mod_fwd")
        st["h"] = h
        w_mi = need("mix_in", i)
        w_mo = need("mix_out", i).reshape(D, D)
        if i % 2 == 0:
            w_in_t = w_mi.reshape(N_DEV * QW, D)
            w_qkv_t = w_in_t[:3 * D]
            w_f_t = jnp.pad(w_in_t[3 * D:], ((0, LANES - H), (0, 0)))
            qkv = matmul(h, w_qkv_t, name="fox_qkv", tb=True, out_dtype=bf16)
            flog = matmul(h, w_f_t, name="fox_flog", tb=True)
            flog_t = flog[:, :H].T
            b_col = attn_b_f[jm][:, None]
            F = fox_gates_fwd(flog_t, b_col, "fox_gates_fwd")
            f_row, f_col = F[:, None, :], F[:, :, None]
            o, lse = hosting([("ffn_in", i), ("ffn_in", i + 1)], 2,
                             fox_attn_fwd, qkv, f_row, f_col, "fox_attn_fwd")
            x1, y = matmul(o, w_mo, name="mix_out", resid=xc, gvec=g1, emit_acc=True)
            st.update(qkv=qkv, flog_t=flog_t, b_col=b_col, f_row=f_row, f_col=f_col, o=o, lse=lse,
                      w_qkv_t=w_qkv_t, w_f_t=w_f_t, w_mo=w_mo)
        else:
            w_gi_full = w_mi.reshape(2 * D, D)
            z = matmul(h, w_gi_full, name="gm_in", tb=True)
            bs_t = gm_b_s[jm].T
            gated = gm_gate_fwd(z, vg_full[jm], gm_w_s[jm], bs_t, "gm_gate_fwd")
            x1, y = matmul(gated, w_mo, name="mix_out", resid=xc, gvec=g1, emit_acc=True)
            st.update(z=z, bs_t=bs_t, gated=gated, w_gi_full=w_gi_full, w_mo=w_mo)
        st.update(y=y, x1=x1)
        h2 = norm_mod_fwd(x1, ffn_norm_g[i][None], sc2, sh2, "norm_mod_fwd")
        w_fi_full = need("ffn_in", i).reshape(2 * FP, D)
        fox = i % 2 == 0
        a3, = hosting([("ffn_out", i), ("mix_out", i + 1)] if fox else [("mix_in", i + 1)], 1,
                      matmul, h2, w_fi_full, name="ffn_up", tb=True, out_split=True)
        act, = hosting([("mix_in", i + 1)] if fox else [], 1,
                       conv_glu_fwd, a3, conv_w_full[i], conv_b_full[i], "conv_glu_fwd")
        w_fo_full = need("ffn_out", i).reshape(FP, D)
        xc, f_out = hosting([("ffn_out", i + 1) if fox else ("mix_out", i + 1)], 2,
                            matmul, act, w_fo_full, name="ffn_down", resid=x1, gvec=g2, emit_acc=True)
        st.update(h2=h2, a3=a3, act=act, f=f_out, w_fi_full=w_fi_full, w_fo_full=w_fo_full)
        stash.append(st)

    loss_part, dx, d_final_g = loss_head(xc, final_g[None], tgt, "loss_head")
    loss = lax.psum(loss_part[0, 0], AXES)

    d_mod = [None] * L
    d_mix_g = [None] * L
    d_ffn_g = [None] * L
    d_conv_w = [None] * L
    d_conv_b = [None] * L
    g_wfi = [None] * L
    g_wfo = [None] * L
    g_wai = [None] * NA
    g_wao = [None] * NA
    d_bf = [None] * NA
    g_wgi = [None] * NB
    g_wgo = [None] * NB
    d_ws = [None] * NB
    d_bs = [None] * NB
    d_vg = [None] * NB
    ffn_pairs = None
    mix_pairs = None

    def finish_mixer(layer, quad_mo, quad_in):
        g_mo = sum_slots(quad_mo, "rs_final_sum_mix_out")
        g_in = sum_slots(quad_in, "rs_final_sum_mix_in").T
        if layer % 2 == 0:
            g_wao[layer // 2], g_wai[layer // 2] = g_mo, g_in
        else:
            g_wgo[layer // 2], g_wgi[layer // 2] = g_mo, g_in

    for i in reversed(range(L)):
        st = stash[i]
        sh1, sc1, g1, sh2, sc2, g2 = (mod_me[i, k] for k in range(6))
        jm = i // 2
        dy, dg2 = gate_bwd(dx, st["f"], g2, "gate_bwd")
        if mix_pairs is None:
            dact = matmul(dy, st["w_fo_full"], name="ffn_down_dx", tb=True)
        else:
            dact, quad_mo = matmul(dy, st["w_fo_full"], name="ffn_down_dx", tb=True,
                                   riders=[chip_exchange(mix_pairs[1])])
        dw_fo = matmul(st["act"], dy, name="ffn_down_dw", ta=True, out_dtype=bf16)
        g8_fo = dw_fo.reshape(N_DEV, FRP, D)
        da3, dwg, dwu, dbg, dbu, got_fo = conv_glu_bwd(st["a3"], conv_w_full[i], conv_b_full[i], dact,
                                                       "conv_glu_bwd", riders=[pair_exchange(g8_fo)])
        pair_fo = pair_sum(g8_fo, got_fo, c_idx, "rs_pair_sum_ffn_out")
        d_conv_w[i] = jnp.concatenate([dwg, dwu], axis=1)
        d_conv_b[i] = jnp.concatenate([dbg, dbu], axis=1)
        above_fi = None
        if ffn_pairs is None:
            dw_fi_t = matmul(da3, st["h2"], name="ffn_up_dw", ta=True, a_split=True, out_dtype=bf16)
        else:
            dw_fi_t, quad = matmul(da3, st["h2"], name="ffn_up_dw", ta=True, a_split=True, out_dtype=bf16,
                                   riders=[chip_exchange(ffn_pairs[0])])
            g_wfo[i + 1] = sum_slots(quad, "rs_final_sum_ffn_out")[:FR]
            above_fi = ffn_pairs[1]
        g8_fi = dw_fi_t.reshape(N_DEV, FWP, D)
        if mix_pairs is None:
            dh2, got_fi = matmul(da3, st["w_fi_full"], name="ffn_up_dx", a_split=True,
                                 riders=[pair_exchange(g8_fi)])
        else:
            dh2, got_fi, quad_in = matmul(da3, st["w_fi_full"], name="ffn_up_dx", a_split=True,
                                          riders=[pair_exchange(g8_fi), chip_exchange(mix_pairs[2])])
            finish_mixer(mix_pairs[0], quad_mo, quad_in)
        ffn_pairs = (pair_fo, pair_sum(g8_fi, got_fi, c_idx, "rs_pair_sum_ffn_in"))
        dx, dsh2, dsc2, d_ffn_g[i] = norm_mod_bwd(st["x1"], ffn_norm_g[i][None], sc2, dh2, dx, "norm_mod_bwd")
        dy, dg1 = gate_bwd(dx, st["y"], g1, "gate_bwd")
        if i % 2 == 0:
            do = matmul(dy, st["w_mo"], name="mix_out_dx", tb=True)
            g8_mo = matmul(st["o"], dy, name="mix_out_dw", ta=True, out_dtype=bf16).reshape(N_DEV, DR, D)
            carried = [chip_exchange(ffn_pairs[0]), chip_exchange(ffn_pairs[1]), pair_exchange(g8_mo)]
            if above_fi is not None:
                carried.append(chip_exchange(above_fi))
            dq, dk, dv, cs, rs, quad_fo, quad_fi, got_mo, *quad_above = fox_attn_bwd(
                st["qkv"], st["o"], do, st["lse"], st["f_row"], st["f_col"], "fox_attn_bwd", riders=carried)
            g_wfo[i] = sum_slots(quad_fo, "rs_final_sum_ffn_out")[:FR]
            g_wfi[i] = unpad_ff(sum_slots(quad_fi, "rs_final_sum_ffn_in"), 0, 2).T
            if quad_above:
                g_wfi[i + 1] = unpad_ff(sum_slots(quad_above[0], "rs_final_sum_ffn_in"), 0, 2).T
            ffn_pairs = None
            dF = rs[:, :, 0] - cs[:, 0, :]
            dflog_t, d_bf[jm] = fox_gates_bwd(st["flog_t"], st["b_col"], dF, "fox_gates_bwd")
            dflog = jnp.pad(dflog_t.T, ((0, 0), (0, LANES - H))).astype(bf16)
            dqkv = jnp.concatenate([dq, dk, dv], axis=1)
            dw_qkv_t = matmul(dqkv, st["h"], name="fox_qkv_dw", ta=True, out_dtype=bf16)
            dw_f_t = matmul(dflog, st["h"], name="fox_flog_dw", ta=True, out_dtype=bf16)
            g8_in = jnp.concatenate([dw_qkv_t, dw_f_t[:H]], axis=0).reshape(N_DEV, QW, D)
            dh, got_in = matmul(dqkv, st["w_qkv_t"], name="fox_qkv_dx", riders=[pair_exchange(g8_in)])
            dh = matmul(dflog, st["w_f_t"], name="fox_flog_dx", resid=dh)
        else:
            if above_fi is not None:
                g_wfi[i + 1] = unpad_ff(reduce_scatter_tail(above_fi, "ffn_in"), 0, 2).T
            dgated = matmul(dy, st["w_mo"], name="mix_out_dx", tb=True)
            g8_mo =matmul(st["gated"], dy, name="mix_out_dw", ta=True, out_dtype=bf16).reshape(N_DEV, DR, D)
            dz, d_ws[jm], dbs_t, d_vg[jm] = gm_gate_bwd(st["z"], vg_full[jm], gm_w_s[jm], st["bs_t"], dgated, "gm_gate_bwd")
            d_bs[jm] = dbs_t.T
            dw_in_t, got_mo = matmul(dz, st["h"], name="gm_in_dw", ta=True, out_dtype=bf16,
                                     riders=[pair_exchange(g8_mo)])
            g8_in = dw_in_t.reshape(N_DEV, GW, D)
            dh, got_in = matmul(dz, st["w_gi_full"], name="gm_in_dx", riders=[pair_exchange(g8_in)])
        mix_pairs = (i, pair_sum(g8_mo, got_mo, c_idx, "rs_pair_sum_mix_out"),
                     pair_sum(g8_in, got_in, c_idx, "rs_pair_sum_mix_in"))
        dx, dsh1, dsc1, d_mix_g[i] = norm_mod_bwd(st["x_in"], mix_norm_g[i][None], sc1, dh, dx, "norm_mod_bwd")
        d_mod[i] = jnp.concatenate([dsh1, dsc1, dg1, dsh2, dsc2, dg2], axis=0)

    grad_x = dx[None]
    if ffn_pairs is not None:
        g_wfo[0] = reduce_scatter_tail(ffn_pairs[0], "ffn_out")[:FR]
        g_wfi[0] = unpad_ff(reduce_scatter_tail(ffn_pairs[1], "ffn_in"), 0, 2).T

    def gathered_sum(rows, tag, mult=SUBLANES):
        n = rows.shape[0]
        rows = jnp.pad(rows, ((0, _round_up(n, mult) - n), (0, 0)))
        every = all_gather(rows, "gather_small_grads_" + tag)
        return every, sum_slots(every, "sum_small_grads_" + tag)

    rows_d = jnp.concatenate([jnp.concatenate(d_mod, axis=0), jnp.concatenate(d_mix_g, axis=0),
                              jnp.concatenate(d_ffn_g, axis=0), jnp.concatenate(d_vg, axis=0), d_final_g], axis=0)
    every_d, sum_d = gathered_sum(rows_d, "d")
    r0 = L * 6
    grad_mod_b = sum_d[:r0].reshape(L, 6 * D)
    grad_mix_g, grad_ffn_g = sum_d[r0:r0 + L], sum_d[r0 + L:r0 + 2 * L]
    grad_vg_full = sum_d[r0 + 2 * L:r0 + 2 * L + NB]
    grad_final_g = sum_d[r0 + 2 * L + NB]
    grad_vg = lax.dynamic_slice_in_dim(grad_vg_full, me * DR, DR, axis=1)

    rows_f = jnp.concatenate([jnp.concatenate(d_conv_w, axis=0), jnp.concatenate(d_conv_b, axis=0)], axis=0)
    _, sum_f = gathered_sum(rows_f, "f")
    sum_f = unpad_ff(sum_f, 1, 2 * N_DEV)
    grad_conv_w = lax.dynamic_slice_in_dim(sum_f[:L * CONV_W].reshape(L, CONV_W, DFF2), me * FW, FW, axis=2)
    grad_conv_b = sum_f[L * CONV_W:L * CONV_W + L]

    rows_c = jnp.concatenate([jnp.stack(d_ws).reshape(NB * G * CHUNK, CHUNK), jnp.stack(d_bs).reshape(NB * G, CHUNK),
                              jnp.pad(jnp.stack(d_bf).reshape(NA, H), ((0, 0), (0, LANES - H)))], axis=0)
    _, sum_c = gathered_sum(rows_c, "c", mult=SLOT_ROWS)
    n_ws = NB * G * CHUNK
    grad_ws = sum_c[:n_ws].reshape(NB, G, CHUNK, CHUNK)
    grad_bs = sum_c[n_ws:n_ws + NB * G].reshape(NB, G, CHUNK)
    grad_bf = sum_c[n_ws + NB * G:n_ws + NB * G + NA, :H]

    dmod_all = every_d[:, :r0].reshape(N_DEV, L, 6 * D)
    dmod_loc = lax.dynamic_slice_in_dim(dmod_all, me * MW, MW, axis=2).transpose(1, 0, 2)
    dmod_loc = jnp.pad(dmod_loc, ((0, 0), (0, LANES - N_DEV), (0, 0)))
    c_t = jnp.pad(c_all.T, ((0, 0), (0, LANES - N_DEV)))
    grad_mod_w = mod_w_bwd(c_t, dmod_loc, "mod_w_bwd")

    first = adamw(mod_w, grad_mod_w, m_mod_w, v_mod_w, "adamw",
                  riders=[chip_exchange(mix_pairs[1]), chip_exchange(mix_pairs[2])])
    finish_mixer(mix_pairs[0], first[3], first[4])
    grad_attn_w_in, grad_attn_w_o = jnp.stack(g_wai), jnp.stack(g_wao)
    grad_gm_w_in, grad_gm_w_o = jnp.stack(g_wgi), jnp.stack(g_wgo)
    grad_ffn_w_in, grad_ffn_w_out = jnp.stack(g_wfi), jnp.stack(g_wfo)

    weights = [mod_w, mod_b, mix_norm_g, ffn_norm_g, attn_w_in, attn_b_f, attn_w_o, gm_w_in, gm_v_g, gm_w_s,
               gm_b_s, gm_w_o, ffn_w_in, ffn_conv_w, ffn_conv_b, ffn_w_out, final_g]
    grads = [grad_mod_w, grad_mod_b, grad_mix_g, grad_ffn_g, grad_attn_w_in, grad_bf, grad_attn_w_o,
             grad_gm_w_in, grad_vg, grad_ws, grad_bs, grad_gm_w_o, grad_ffn_w_in, grad_conv_w, grad_conv_b,
             grad_ffn_w_out, grad_final_g]
    ms = [m_mod_w, m_mod_b, m_mix_norm_g, m_ffn_norm_g, m_attn_w_in, m_attn_b_f, m_attn_w_o, m_gm_w_in, m_gm_v_g,
          m_gm_w_s, m_gm_b_s, m_gm_w_o, m_ffn_w_in, m_ffn_conv_w, m_ffn_conv_b, m_ffn_w_out, m_final_g]
    vs = [v_mod_w, v_mod_b, v_mix_norm_g, v_ffn_norm_g, v_attn_w_in, v_attn_b_f, v_attn_w_o, v_gm_w_in, v_gm_v_g,
          v_gm_w_s, v_gm_b_s, v_gm_w_o, v_ffn_w_in, v_ffn_conv_w, v_ffn_conv_b, v_ffn_w_out, v_final_g]
    deltas, new_ms, new_vs = [], [], []
    for k, (w, g, m_, v_) in enumerate(zip(weights, grads, ms, vs)):
        d_, mn_, vn_ = first[:3] if k == 0 else adamw(w, g, m_, v_, "adamw")
        deltas.append(d_)
        new_ms.append(mn_)
        new_vs.append(vn_)

    return (loss, grad_x, *grads, *deltas, *new_ms, *new_vs)
```
